```python
import math
import jax, jax.numpy as jnp
from jax import lax
import numpy as np

D_MODEL = 4096
BATCH = 2
SEQ = 8192
DEPTH = 1

D_MIX = D_MODEL
D_ATT = D_MIX // 2
D_SSM = D_MIX - D_ATT
HEAD_DIM = 128
N_ATT_HEADS = D_ATT // HEAD_DIM
Q_BLOCK = 128
SSM_GROUP = 16
N_SSM_GROUPS = D_SSM // SSM_GROUP
SSM_STATE = 64
SSM_CHUNK = 128
D_IN = 3 * D_ATT + N_ATT_HEADS + D_SSM
N_EXPERT_GROUPS = 8
EXPERTS_PER_GROUP = 8
N_EXPERTS = N_EXPERT_GROUPS * EXPERTS_PER_GROUP
TOP_K = 2
D_EXPERT = D_MODEL // 8
MOE_BLOCK = 128
N_MOD = 6
DEEPNORM_ALPHA = (2.0 * DEPTH) ** 0.25
DEEPNORM_BETA = (8.0 * DEPTH) ** -0.25
LN_EPS = 1e-5
RMS_EPS = 1e-6

kernel_name = 'hymba_fox_s5_hmoe_deepnorm_adaln'


def layer_norm(x, g, b):
    xf = x.astype(jnp.float32)
    mu = jnp.mean(xf, axis=-1, keepdims=True)
    var = jnp.mean(jnp.square(xf - mu), axis=-1, keepdims=True)
    return ((xf - mu) * lax.rsqrt(var + LN_EPS) * g.astype(jnp.float32) + b.astype(jnp.float32)).astype(x.dtype)


def rms_norm(x, g):
    xf = x.astype(jnp.float32)
    return (xf * lax.rsqrt(jnp.mean(xf * xf, axis=-1, keepdims=True) + RMS_EPS) * g.astype(jnp.float32)).astype(x.dtype)


def forgetting_attention(q, k, v, f_logit, b_forget):
    bsz, seq = q.shape[0], q.shape[1]
    log_f = jax.nn.log_sigmoid((f_logit + b_forget).astype(jnp.float32))
    cum = jnp.cumsum(log_f, axis=1).transpose(0, 2, 1)
    qh = q.transpose(0, 2, 1, 3)
    kh = k.transpose(0, 2, 1, 3)
    vh = v.transpose(0, 2, 1, 3)
    key_pos = jnp.arange(seq)
    scale = HEAD_DIM ** -0.5

    def query_block(i):
        start = i * Q_BLOCK
        qb = lax.dynamic_slice_in_dim(qh, start, Q_BLOCK, axis=2)
        cb = lax.dynamic_slice_in_dim(cum, start, Q_BLOCK, axis=2)
        s = jnp.einsum('bhqd,bhkd->bhqk', qb, kh).astype(jnp.float32) * scale
        s = s + cb[..., :, None] - cum[..., None, :]
        q_pos = start + jnp.arange(Q_BLOCK)
        s = jnp.where(key_pos[None, :] <= q_pos[:, None], s, -jnp.inf)
        p = jax.nn.softmax(s, axis=-1).astype(vh.dtype)
        return jnp.einsum('bhqk,bhkd->bqhd', p, vh)

    out = lax.map(query_block, jnp.arange(seq // Q_BLOCK))
    return out.transpose(1, 0, 2, 3, 4).reshape(bsz, seq, N_ATT_HEADS * HEAD_DIM)


def _ssm_combine(e1, e2):
    ar1, ai1, br1, bi1 = e1
    ar2, ai2, br2, bi2 = e2
    return (ar2 * ar1 - ai2 * ai1, ar2 * ai1 + ai2 * ar1,
            ar2 * br1 - ai2 * bi1 + br2, ar2 * bi1 + ai2 * br1 + bi2)


def s5_ssm(u, lam_re, lam_im, log_dt, b_re, b_im, c_re, c_im, d_skip, w_glu, b_glu):
    bsz, seq = u.shape[0], u.shape[1]
    uf = u.astype(jnp.float32).reshape(bsz, seq, N_SSM_GROUPS, SSM_GROUP)
    dt = jnp.exp(log_dt.astype(jnp.float32))[:, None]
    lr = lam_re.astype(jnp.float32)
    li = lam_im.astype(jnp.float32)
    mag = jnp.exp(lr * dt)
    a_re = mag * jnp.cos(li * dt)
    a_im = mag * jnp.sin(li * dt)
    den = lr * lr + li * li
    z_re = ((a_re - 1.0) * lr + a_im * li) / den
    z_im = (a_im * lr - (a_re - 1.0) * li) / den
    br = b_re.astype(jnp.float32)
    bi = b_im.astype(jnp.float32)
    bb_re = z_re[..., None] * br - z_im[..., None] * bi
    bb_im = z_re[..., None] * bi + z_im[..., None] * br
    cr = c_re.astype(jnp.float32)
    ci = c_im.astype(jnp.float32)
    n_chunks = seq // SSM_CHUNK
    u_chunks = uf.reshape(bsz, n_chunks, SSM_CHUNK, N_SSM_GROUPS, SSM_GROUP).transpose(1, 0, 2, 3, 4)
    state_shape = (bsz, SSM_CHUNK, N_SSM_GROUPS, SSM_STATE)
    a_re_b = jnp.broadcast_to(a_re, state_shape)
    a_im_b = jnp.broadcast_to(a_im, state_shape)

    def chunk_step(carry, uc):
        h_re0, h_im0 = carry
        bu_re = jnp.einsum('btgc,gnc->btgn', uc, bb_re)
        bu_im = jnp.einsum('btgc,gnc->btgn', uc, bb_im)
        p_re, p_im, l_re, l_im = lax.associative_scan(_ssm_combine, (a_re_b, a_im_b, bu_re, bu_im), axis=1)
        h_re = l_re + p_re * h_re0[:, None] - p_im * h_im0[:, None]
        h_im = l_im + p_re * h_im0[:, None] + p_im * h_re0[:, None]
        y = jnp.einsum('btgn,gcn->btgc', h_re, cr) - jnp.einsum('btgn,gcn->btgc', h_im, ci)
        return (h_re[:, -1], h_im[:, -1]), y

    init = (jnp.zeros((bsz, N_SSM_GROUPS, SSM_STATE), jnp.float32),
            jnp.zeros((bsz, N_SSM_GROUPS, SSM_STATE), jnp.float32))
    _, ys = lax.scan(chunk_step, init, u_chunks)
    y = ys.transpose(1, 0, 2, 3, 4).reshape(bsz, seq, N_SSM_GROUPS, SSM_GROUP) + d_skip.astype(jnp.float32) * uf
    y = jax.nn.gelu(y.reshape(bsz, seq, D_SSM).astype(u.dtype))
    return y * jax.nn.sigmoid(y @ w_glu + b_glu)


def hierarchical_moe(h, w_rg, b_rg, w_re, b_re, w_gate, w_up, w_down):
    bsz, seq, d = h.shape
    n_tok = bsz * seq
    hf = h.reshape(n_tok, d)
    g_logits = (hf @ w_rg + b_rg).astype(jnp.float32)
    g_prob = jax.nn.softmax(g_logits, axis=-1)
    g_top, g_sel = lax.top_k(g_logits, 1)
    p_group = jnp.take_along_axis(g_prob, g_sel, axis=1)[:, 0]
    e_logits = (hf @ w_re + b_re).astype(jnp.float32).reshape(n_tok, N_EXPERT_GROUPS, EXPERTS_PER_GROUP)
    e_logits_g = jnp.take_along_axis(e_logits, g_sel[:, :, None], axis=1)[:, 0]
    e_top, e_idx = lax.top_k(e_logits_g, TOP_K)
    weights = p_group[:, None] * jax.nn.softmax(e_top, axis=-1)
    expert_id = g_sel * EXPERTS_PER_GROUP + e_idx
    n_assign = n_tok * TOP_K
    n_blocks = -(-(n_assign + N_EXPERTS * (MOE_BLOCK - 1)) // MOE_BLOCK)
    n_rows = n_blocks * MOE_BLOCK
    eid = expert_id.reshape(n_assign).astype(jnp.int32)
    tok = jnp.repeat(jnp.arange(n_tok, dtype=jnp.int32), TOP_K)
    wts = weights.reshape(n_assign)
    order = jnp.argsort(eid)
    eid_s, tok_s, w_s = eid[order], tok[order], wts[order]
    counts = jnp.zeros((N_EXPERTS,), jnp.int32).at[eid].add(1)
    starts = jnp.cumsum(counts) - counts
    padded = ((counts + MOE_BLOCK - 1) // MOE_BLOCK) * MOE_BLOCK
    pends = jnp.cumsum(padded)
    pstarts = pends - padded
    dest = pstarts[eid_s] + (jnp.arange(n_assign, dtype=jnp.int32) - starts[eid_s])
    tok_buf = jnp.zeros((n_rows,), jnp.int32).at[dest].set(tok_s)
    w_buf = jnp.zeros((n_rows,), wts.dtype).at[dest].set(w_s)
    block_expert = jnp.clip(jnp.searchsorted(pends, jnp.arange(n_blocks, dtype=jnp.int32) * MOE_BLOCK, side='right'), 0, N_EXPERTS - 1)

    def run_block(args):
        tb, eb = args
        xb = hf[tb]
        return (jax.nn.silu(xb @ w_gate[eb]) * (xb @ w_up[eb])) @ w_down[eb]

    ys = lax.map(run_block, (tok_buf.reshape(n_blocks, MOE_BLOCK), block_expert))
    contrib = (ys.reshape(n_rows, d) * w_buf[:, None]).astype(h.dtype)
    out = jnp.zeros((n_tok, d), h.dtype).at[tok_buf].add(contrib)
    return out.reshape(bsz, seq, d)


def hybrid_layer(x, c, w_ada, b_ada, w_in, b_forget, ssm_lambda_re, ssm_lambda_im, ssm_log_dt,
                 ssm_b_re, ssm_b_im, ssm_c_re, ssm_c_im, ssm_d, w_glu, b_glu, g_attn, g_ssm, w_out,
                 ln1_g, ln1_b, w_router_group, b_router_group, w_router_expert, b_router_expert,
                 w_gate, w_up, w_down, ln2_g, ln2_b):
    bsz, seq, _ = x.shape
    mod = (jax.nn.silu(c) @ w_ada + b_ada)[:, None, :]
    shift1, scale1, gate1, shift2, scale2, gate2 = jnp.split(mod, N_MOD, axis=-1)
    h = x * (1.0 + scale1) + shift1
    proj = h @ w_in
    q, k, v, f_logit, u = jnp.split(proj, [D_ATT, 2 * D_ATT, 3 * D_ATT, 3 * D_ATT + N_ATT_HEADS], axis=-1)
    hs = (bsz, seq, N_ATT_HEADS, HEAD_DIM)
    attn = forgetting_attention(q.reshape(hs), k.reshape(hs), v.reshape(hs), f_logit, b_forget)
    ssm = s5_ssm(u, ssm_lambda_re, ssm_lambda_im, ssm_log_dt, ssm_b_re, ssm_b_im,
                 ssm_c_re, ssm_c_im, ssm_d, w_glu, b_glu)
    mixed = jnp.concatenate([rms_norm(attn, g_attn), rms_norm(ssm, g_ssm)], axis=-1) @ w_out
    x = layer_norm(DEEPNORM_ALPHA * x + (1.0 + gate1) * mixed, ln1_g, ln1_b)
    h2 = x * (1.0 + scale2) + shift2
    moe = hierarchical_moe(h2, w_router_group, b_router_group, w_router_expert, b_router_expert, w_gate, w_up, w_down)
    return layer_norm(DEEPNORM_ALPHA * x + (1.0 + gate2) * moe, ln2_g, ln2_b)


def setup_inputs(seed: int = 0) -> dict:
    key = jax.random.key(seed)
    ks = jax.random.split(key, 40)
    f32 = jnp.float32

    def nrm(k, shape, scale):
        return jax.random.normal(k, shape, f32) * scale

    L, G, N, C = DEPTH, N_SSM_GROUPS, SSM_STATE, SSM_GROUP
    lam_im_base = math.pi * jnp.arange(N, dtype=f32)
    return {
        'x': nrm(ks[0], (BATCH, SEQ, D_MODEL), 1.0),
        'c': nrm(ks[1], (BATCH, D_MODEL), 1.0),
        'w_ada': nrm(ks[2], (L, D_MODEL, N_MOD * D_MODEL), 0.01),
        'b_ada': nrm(ks[3], (L, N_MOD * D_MODEL), 0.01),
        'w_in': nrm(ks[4], (L, D_MODEL, D_IN), D_MODEL ** -0.5),
        'b_forget': jnp.linspace(1.0, 6.0, N_ATT_HEADS, dtype=f32)[None, :] + nrm(ks[5], (L, N_ATT_HEADS), 0.01),
        'ssm_lambda_re': -0.5 * jnp.exp(nrm(ks[6], (L, G, N), 0.05)),
        'ssm_lambda_im': lam_im_base[None, None, :] + nrm(ks[7], (L, G, N), 0.01),
        'ssm_log_dt': jax.random.uniform(ks[8], (L, G), f32, math.log(1e-3), math.log(1e-1)),
        'ssm_b_re': nrm(ks[9], (L, G, N, C), (2.0 * C) ** -0.5),
        'ssm_b_im': nrm(ks[10], (L, G, N, C), (2.0 * C) ** -0.5),
        'ssm_c_re': nrm(ks[11], (L, G, C, N), (2.0 * N) ** -0.5),
        'ssm_c_im': nrm(ks[12], (L, G, C, N), (2.0 * N) ** -0.5),
        'ssm_d': nrm(ks[13], (L, G, C), 1.0),
        'w_glu': nrm(ks[14], (L, D_SSM, D_SSM), D_SSM ** -0.5),
        'b_glu': nrm(ks[15], (L, D_SSM), 0.01),
        'g_attn': 1.0 + nrm(ks[16], (L, D_ATT), 0.01),
        'g_ssm': 1.0 + nrm(ks[17], (L, D_SSM), 0.01),
        'w_out': nrm(ks[18], (L, D_MIX, D_MODEL), DEEPNORM_BETA * D_MIX ** -0.5),
        'ln1_g': 1.0 + nrm(ks[19], (L, D_MODEL), 0.01),
        'ln1_b': nrm(ks[20], (L, D_MODEL), 0.01),
        'w_router_group': nrm(ks[21], (L, D_MODEL, N_EXPERT_GROUPS), D_MODEL ** -0.5),
        'b_router_group': nrm(ks[22], (L, N_EXPERT_GROUPS), 0.01),
        'w_router_expert': nrm(ks[23], (L, D_MODEL, N_EXPERTS), D_MODEL ** -0.5),
        'b_router_expert': nrm(ks[24], (L, N_EXPERTS), 0.01),
        'w_gate': nrm(ks[25], (L, N_EXPERTS, D_MODEL, D_EXPERT), D_MODEL ** -0.5),
        'w_up': nrm(ks[26], (L, N_EXPERTS, D_MODEL, D_EXPERT), D_MODEL ** -0.5),
        'w_down': nrm(ks[27], (L, N_EXPERTS, D_EXPERT, D_MODEL), DEEPNORM_BETA * D_EXPERT ** -0.5),
        'ln2_g': 1.0 + nrm(ks[28], (L, D_MODEL), 0.01),
        'ln2_b': nrm(ks[29], (L, D_MODEL), 0.01),
    }


def reference(x, c, w_ada, b_ada, w_in, b_forget, ssm_lambda_re, ssm_lambda_im, ssm_log_dt,
              ssm_b_re, ssm_b_im, ssm_c_re, ssm_c_im, ssm_d, w_glu, b_glu, g_attn, g_ssm, w_out,
              ln1_g, ln1_b, w_router_group, b_router_group, w_router_expert, b_router_expert,
              w_gate, w_up, w_down, ln2_g, ln2_b):
    for l in range(DEPTH):
        x = hybrid_layer(x, c, w_ada[l], b_ada[l], w_in[l], b_forget[l], ssm_lambda_re[l], ssm_lambda_im[l],
                         ssm_log_dt[l], ssm_b_re[l], ssm_b_im[l], ssm_c_re[l], ssm_c_im[l], ssm_d[l],
                         w_glu[l], b_glu[l], g_attn[l], g_ssm[l], w_out[l], ln1_g[l], ln1_b[l],
                         w_router_group[l], b_router_group[l], w_router_expert[l], b_router_expert[l],
                         w_gate[l], w_up[l], w_down[l], ln2_g[l], ln2_b[l])
    return x
```

```python
import functools
import math

import jax
import jax.numpy as jnp
from jax import lax
from jax.experimental import pallas as pl
from jax.experimental.pallas import tpu as pltpu

F32 = jnp.float32
BF16 = jnp.bfloat16

LANES = 128
HEAD_DIM = 128
SSM_GROUP = 16
SSM_STATE = 64
GROUPS_PER_SLAB = LANES // SSM_GROUP
SLAB_STATE = GROUPS_PER_SLAB * SSM_STATE
SSM_CHUNK = 16
N_EXPERT_GROUPS = 8
EXPERTS_PER_GROUP = 8
TOP_K = 2
MOE_ROWS = 256
LN_EPS = 1e-5
RMS_EPS = 1e-6
NEG_BIG = -1e30
MIB = 1024 * 1024

_NT = (((1,), (1,)), ((), ()))


def _params(semantics, vmem_mib):
    return pltpu.CompilerParams(dimension_semantics=semantics, vmem_limit_bytes=vmem_mib * MIB)


def _dot(a, b):
    return jnp.dot(a, b, preferred_element_type=F32)


def _ada_kernel(c_ref, w_ref, b_ref, o_ref):
    s = jax.nn.silu(c_ref[...]).astype(BF16)
    o_ref[...] = _dot(s, w_ref[...].astype(BF16)) + b_ref[...]


def ada_mod(c, w_ada, b_ada):
    bsz, d = c.shape
    n = w_ada.shape[1]
    rows = 8
    assert bsz <= rows
    cp = jnp.zeros((rows, d), F32).at[:bsz].set(c)
    tn = 512
    out = pl.pallas_call(
        _ada_kernel,
        grid=(n // tn,),
        in_specs=[pl.BlockSpec((rows, d), lambda j: (0, 0)),
                  pl.BlockSpec((d, tn), lambda j: (0, j)),
                  pl.BlockSpec((1, tn), lambda j: (0, j))],
        out_specs=pl.BlockSpec((rows, tn), lambda j: (0, j)),
        out_shape=jax.ShapeDtypeStruct((rows, n), F32),
        compiler_params=_params(("arbitrary",), 40),
        name="ada_mod",
    )(cp, w_ada, b_ada.reshape(1, n))
    return out[:bsz]


def _inproj_kernel(x_ref, sc_ref, sh_ref, w_ref, wf_ref, qkv_ref, u_ref, f_ref, h_scr, *, n_qkv_tiles):
    j = pl.program_id(2)

    @pl.when(j == 0)
    def _():
        hb = (x_ref[0] * (1.0 + sc_ref[0]) + sh_ref[0]).astype(BF16)
        h_scr[...] = hb
        f_ref[0] = _dot(hb, wf_ref[...])

    acc = _dot(h_scr[...], w_ref[...])

    @pl.when(j < n_qkv_tiles)
    def _():
        qkv_ref[0] = acc.astype(BF16)

    @pl.when(j >= n_qkv_tiles)
    def _():
        u_ref[0] = acc


def in_proj(x, scale, shift, w_main, w_f, n_qkv):
    bsz, seq, d = x.shape
    n_all = w_main.shape[1]
    n_u = n_all - n_qkv
    tm, tn = 512, 1024
    nq = n_qkv // tn
    grid = (bsz, seq // tm, n_all // tn)
    return pl.pallas_call(
        functools.partial(_inproj_kernel, n_qkv_tiles=nq),
        grid=grid,
        in_specs=[pl.BlockSpec((1, tm, d), lambda b, i, j: (b, i, 0)),
                  pl.BlockSpec((1, 1, d), lambda b, i, j: (b, 0, 0)),
                  pl.BlockSpec((1, 1, d), lambda b, i, j: (b, 0, 0)),
                  pl.BlockSpec((d, tn), lambda b, i, j: (0, j)),
                  pl.BlockSpec((d, LANES), lambda b, i, j: (0, 0))],
        out_specs=[pl.BlockSpec((1, tm, tn), lambda b, i, j: (b, i, jnp.minimum(j, nq - 1))),
                   pl.BlockSpec((1, tm, tn), lambda b, i, j: (b, i, jnp.maximum(j - nq, 0))),
                   pl.BlockSpec((1, tm, LANES), lambda b, i, j: (b, i, 0))],
        out_shape=[jax.ShapeDtypeStruct((bsz, seq, n_qkv), BF16),
                   jax.ShapeDtypeStruct((bsz, seq, n_u), F32),
                   jax.ShapeDtypeStruct((bsz, seq, LANES), F32)],
        scratch_shapes=[pltpu.VMEM((tm, d), BF16)],
        compiler_params=_params(("arbitrary", "arbitrary", "arbitrary"), 52),
        name="in_proj",
    )(x, scale, shift, w_main, w_f)


def _split3(x):
    p1 = x.astype(BF16)
    r1 = x - p1.astype(F32)
    p2 = r1.astype(BF16)
    p3 = (r1 - p2.astype(F32)).astype(BF16)
    return p1, p2, p3


def _cum_kernel(f_ref, b_ref, o_ref, carry, *, n_heads):
    i = pl.program_id(1)

    @pl.when(i == 0)
    def _():
        carry[...] = jnp.zeros_like(carry)

    tc = f_ref.shape[1]
    lf = jax.nn.log_sigmoid(f_ref[0] + b_ref[...])
    row = lax.broadcasted_iota(jnp.int32, (tc, tc), 0)
    col = lax.broadcasted_iota(jnp.int32, (tc, tc), 1)
    tri = (col <= row).astype(BF16)
    p1, p2, p3 = _split3(lf)
    cs = _dot(tri, p1) + _dot(tri, p2) + _dot(tri, p3) + carry[...]
    carry[...] = cs[tc - 1:tc, :]
    o_ref[0] = jnp.transpose(cs)[:n_heads, :]


def forget_cum(f, b_forget):
    bsz, seq, _ = f.shape
    n_heads = b_forget.shape[0]
    tc = 256
    bpad = jnp.zeros((1, LANES), F32).at[0, :n_heads].set(b_forget)
    return pl.pallas_call(
        functools.partial(_cum_kernel, n_heads=n_heads),
        grid=(bsz, seq // tc),
        in_specs=[pl.BlockSpec((1, tc, LANES), lambda b, i: (b, i, 0)),
                  pl.BlockSpec((1, LANES), lambda b, i: (0, 0))],
        out_specs=pl.BlockSpec((1, n_heads, tc), lambda b, i: (b, 0, i)),
        out_shape=jax.ShapeDtypeStruct((bsz, n_heads, seq), F32),
        scratch_shapes=[pltpu.VMEM((1, LANES), F32)],
        compiler_params=_params(("arbitrary", "arbitrary"), 16),
        name="forget_cum",
    )(f, bpad)


def _attn_kernel(q_ref, k_ref, v_ref, c_ref, o_ref, m_scr, l_scr, acc_scr, *, scale):
    t = q_ref.shape[1]
    qi = pl.program_id(2)
    q = q_ref[0]
    q0 = pl.multiple_of(qi * t, t)
    cq_row = c_ref[0, 0, :, pl.ds(q0, t)]
    cq = jnp.transpose(jnp.broadcast_to(cq_row, (LANES, t)))[:, :1]

    m_scr[...] = jnp.full_like(m_scr, NEG_BIG)
    l_scr[...] = jnp.zeros_like(l_scr)
    acc_scr[...] = jnp.zeros_like(acc_scr)

    def step(kb, masked):
        k0 = pl.multiple_of(kb * t, t)
        kt = k_ref[0, pl.ds(k0, t), :]
        vt = v_ref[0, pl.ds(k0, t), :]
        s = lax.dot_general(q, kt, _NT, preferred_element_type=F32) * scale
        s = s + cq - c_ref[0, 0, :, pl.ds(k0, t)]
        if masked:
            qpos = lax.broadcasted_iota(jnp.int32, (t, t), 0)
            kpos = lax.broadcasted_iota(jnp.int32, (t, t), 1)
            s = jnp.where(kpos <= qpos, s, NEG_BIG)
        m_prev = m_scr[...]
        m_new = jnp.maximum(m_prev, jnp.max(s, axis=1, keepdims=True))
        alpha = jnp.exp(m_prev - m_new)
        p = jnp.exp(s - m_new)
        l_scr[...] = alpha * l_scr[...] + jnp.sum(p, axis=1, keepdims=True)
        acc_scr[...] = alpha * acc_scr[...] + _dot(p.astype(BF16), vt)
        m_scr[...] = m_new

    def body(kb, carry):
        step(kb, False)
        return carry

    lax.fori_loop(0, qi, body, 0)
    step(qi, True)
    o_ref[0] = acc_scr[...] / l_scr[...]


def attention(qkv, cum, n_heads):
    bsz, seq, _ = qkv.shape
    t = 512
    cum4 = cum.reshape(bsz, n_heads, 1, seq)
    return pl.pallas_call(
        functools.partial(_attn_kernel, scale=HEAD_DIM ** -0.5),
        grid=(bsz, n_heads, seq // t),
        in_specs=[pl.BlockSpec((1, t, HEAD_DIM), lambda b, h, i: (b, i, h)),
                  pl.BlockSpec((1, seq, HEAD_DIM), lambda b, h, i: (b, 0, n_heads + h)),
                  pl.BlockSpec((1, seq, HEAD_DIM), lambda b, h, i: (b, 0, 2 * n_heads + h)),
                  pl.BlockSpec((1, 1, 1, seq), lambda b, h, i: (b, h, 0, 0))],
        out_specs=pl.BlockSpec((1, t, HEAD_DIM), lambda b, h, i: (b, i, h)),
        out_shape=jax.ShapeDtypeStruct((bsz, seq, n_heads * HEAD_DIM), F32),
        scratch_shapes=[pltpu.VMEM((t, 1), F32), pltpu.VMEM((t, 1), F32), pltpu.VMEM((t, HEAD_DIM), F32)],
        compiler_params=_params(("arbitrary", "arbitrary", "arbitrary"), 32),
        name="attention",
    )(qkv, qkv, qkv, cum4)


def _blockdiag(p):
    g, c, n = p.shape
    ns = g // GROUPS_PER_SLAB
    eye = jnp.eye(GROUPS_PER_SLAB, dtype=p.dtype)
    out = p.reshape(ns, GROUPS_PER_SLAB, c, 1, n) * eye[None, :, None, :, None]
    return out.reshape(ns, GROUPS_PER_SLAB * c, GROUPS_PER_SLAB * n)


def _ssm_prep_kernel(lr_ref, li_ref, ldt_ref, bre_ref, bim_ref, cre_ref, cim_ref, w1_ref, ft_ref, al_ref):
    L = SSM_CHUNK
    lr = lr_ref[0]
    li = li_ref[0]
    dt = jnp.exp(ldt_ref[0])
    mag = jnp.exp(lr * dt)
    a_re = mag * jnp.cos(li * dt)
    a_im = mag * jnp.sin(li * dt)
    den = lr * lr + li * li
    z_re = ((a_re - 1.0) * lr + a_im * li) / den
    z_im = (a_im * lr - (a_re - 1.0) * li) / den
    br = bre_ref[0]
    bi = bim_ref[0]
    bb_re = z_re * br - z_im * bi
    bb_im = z_re * bi + z_im * br
    cr = cre_ref[0]
    ci = cim_ref[0]
    ft0 = jnp.concatenate([cr, -ci], axis=1)

    def power(d):
        m = jnp.exp(lr * dt * d)
        return m * jnp.cos(li * dt * d), m * jnp.sin(li * dt * d)

    w1_ref[0, :, :L * LANES] = jnp.zeros((L * LANES, L * LANES), BF16)
    for d in range(L):
        pr, pi = power(float(d))
        xe = jnp.concatenate([bb_re * pr - bb_im * pi, bb_re * pi + bb_im * pr], axis=1)
        j = L - 1 - d
        w1_ref[0, j * LANES:(j + 1) * LANES, L * LANES:] = xe.astype(BF16)
        m_d = lax.dot_general(xe, ft0, _NT, preferred_element_type=F32,
                              precision=lax.Precision.HIGHEST).astype(BF16)
        for jj in range(L - d):
            w1_ref[0, jj * LANES:(jj + 1) * LANES, (jj + d) * LANES:(jj + d + 1) * LANES] = m_d
        pr1, pi1 = power(float(d + 1))
        ft_ref[0, d * LANES:(d + 1) * LANES, :] = jnp.concatenate(
            [cr * pr1 - ci * pi1, -(cr * pi1 + ci * pr1)], axis=1).astype(BF16)
    prl, pil = power(float(L))
    al_ref[0] = jnp.concatenate([prl, pil], axis=1)


def ssm_prep(lam_re, lam_im, log_dt, b_re, b_im, c_re, c_im):
    g, n = lam_re.shape
    ns = g // GROUPS_PER_SLAB
    L = SSM_CHUNK
    rowvec = lambda a: a.reshape(ns, 1, SLAB_STATE)
    args = (rowvec(lam_re), rowvec(lam_im), rowvec(jnp.repeat(log_dt, n)),
            _blockdiag(b_re.transpose(0, 2, 1)), _blockdiag(b_im.transpose(0, 2, 1)),
            _blockdiag(c_re), _blockdiag(c_im))
    vec_spec = pl.BlockSpec((1, 1, SLAB_STATE), lambda s: (s, 0, 0))
    mat_spec = pl.BlockSpec((1, LANES, SLAB_STATE), lambda s: (s, 0, 0))
    return pl.pallas_call(
        _ssm_prep_kernel,
        grid=(ns,),
        in_specs=[vec_spec] * 3 + [mat_spec] * 4,
        out_specs=[pl.BlockSpec((1, L * LANES, L * LANES + 2 * SLAB_STATE), lambda s: (s, 0, 0)),
                   pl.BlockSpec((1, L * LANES, 2 * SLAB_STATE), lambda s: (s, 0, 0)),
                   pl.BlockSpec((1, 1, 2 * SLAB_STATE), lambda s: (s, 0, 0))],
        out_shape=[jax.ShapeDtypeStruct((ns, L * LANES, L * LANES + 2 * SLAB_STATE), BF16),
                   jax.ShapeDtypeStruct((ns, L * LANES, 2 * SLAB_STATE), BF16),
                   jax.ShapeDtypeStruct((ns, 1, 2 * SLAB_STATE), F32)],
        compiler_params=_params(("arbitrary",), 48),
        name="ssm_prep",
    )(*args)


def _ssm_kernel(u_ref, w1_ref, ft_ref, al_ref, d_ref, y_ref, uf_scr, e_scr, y_scr):
    L = SSM_CHUNK
    nch = uf_scr.shape[0]
    lc = L * LANES
    for j in range(L):
        uf_scr[:, j * LANES:(j + 1) * LANES] = u_ref[0, pl.ds(j, nch, stride=L), :].astype(BF16)
    uf = uf_scr[...]
    y_scr[...] = _dot(uf, w1_ref[0, :, :lc])
    e_scr[...] = _dot(uf, w1_ref[0, :, lc:])

    a_re = al_ref[0, :, :SLAB_STATE]
    a_im = al_ref[0, :, SLAB_STATE:]

    def body(k, h):
        h_re, h_im = h
        e = e_scr[pl.ds(k, 1), :]
        e_scr[pl.ds(k, 1), :] = jnp.concatenate([h_re, h_im], axis=1)
        return (a_re * h_re - a_im * h_im + e[:, :SLAB_STATE],
                a_re * h_im + a_im * h_re + e[:, SLAB_STATE:])

    zero = jnp.zeros((1, SLAB_STATE), F32)
    lax.fori_loop(0, nch, body, (zero, zero))
    y = y_scr[...] + lax.dot_general(e_scr[...].astype(BF16), ft_ref[0], _NT, preferred_element_type=F32)
    for i in range(L):
        yi = y[:, i * LANES:(i + 1) * LANES] + d_ref[0] * u_ref[0, pl.ds(i, nch, stride=L), :]
        y_ref[0, pl.ds(i, nch, stride=L), :] = jax.nn.gelu(yi)


def ssm_scan(u, w1, ft, al, d_skip):
    bsz, seq, c = u.shape
    ns = c // LANES
    L = SSM_CHUNK
    nch = seq // L
    lc = L * LANES
    once = pl.Buffered(1)
    return pl.pallas_call(
        _ssm_kernel,
        grid=(ns, bsz),
        in_specs=[pl.BlockSpec((1, seq, LANES), lambda s, b: (b, 0, s)),
                  pl.BlockSpec((1, lc, lc + 2 * SLAB_STATE), lambda s, b: (s, 0, 0), pipeline_mode=once),
                  pl.BlockSpec((1, lc, 2 * SLAB_STATE), lambda s, b: (s, 0, 0), pipeline_mode=once),
                  pl.BlockSpec((1, 1, 2 * SLAB_STATE), lambda s, b: (s, 0, 0)),
                  pl.BlockSpec((1, 1, LANES), lambda s, b: (s, 0, 0))],
        out_specs=pl.BlockSpec((1, seq, LANES), lambda s, b: (b, 0, s)),
        out_shape=jax.ShapeDtypeStruct((bsz, seq, c), F32),
        scratch_shapes=[pltpu.VMEM((nch, lc), BF16), pltpu.VMEM((nch, 2 * SLAB_STATE), F32),
                        pltpu.VMEM((nch, lc), F32)],
        compiler_params=_params(("arbitrary", "arbitrary"), 56),
        name="ssm_scan",
    )(u, w1, ft, al, d_skip.reshape(ns, 1, LANES))


def _glu_kernel(y_ref, w_ref, b_ref, g_ref, o_ref):
    y = y_ref[...]
    o = y * jax.nn.sigmoid(_dot(y.astype(BF16), w_ref[...]) + b_ref[...])
    ms = jnp.mean(o * o, axis=-1, keepdims=True)
    o_ref[...] = (o * lax.rsqrt(ms + RMS_EPS) * g_ref[...]).astype(BF16)


def glu_norm(y, w_glu, b_glu, g):
    t, c = y.shape
    tm = 512
    return pl.pallas_call(
        _glu_kernel,
        grid=(t // tm,),
        in_specs=[pl.BlockSpec((tm, c), lambda i: (i, 0)),
                  pl.BlockSpec((c, c), lambda i: (0, 0)),
                  pl.BlockSpec((1, c), lambda i: (0, 0)),
                  pl.BlockSpec((1, c), lambda i: (0, 0))],
        out_specs=pl.BlockSpec((tm, c), lambda i: (i, 0)),
        out_shape=jax.ShapeDtypeStruct((t, c), BF16),
        compiler_params=_params(("arbitrary",), 48),
        name="glu_norm",
    )(y, w_glu, b_glu.reshape(1, c), g.reshape(1, c))


def _outproj_kernel(attn_ref, ga_ref, ssm_ref, w_ref, x_ref, gate_ref, lng_ref, lnb_ref, sc2_ref, sh2_ref,
                    wrh_ref, wrl_ref, x1_ref, h2_ref, lg_ref, a_scr, *, alpha, n_att):
    j = pl.program_id(2)
    nj = pl.num_programs(2)
    tn = w_ref.shape[1]

    @pl.when(j == 0)
    def _():
        a = attn_ref[0]
        ms = jnp.mean(a * a, axis=-1, keepdims=True)
        a_scr[:, :n_att] = (a * lax.rsqrt(ms + RMS_EPS) * ga_ref[...]).astype(BF16)
        a_scr[:, n_att:] = ssm_ref[0]

    mixed = _dot(a_scr[...], w_ref[...])
    col = pl.multiple_of(j * tn, tn)
    x1_ref[0, :, pl.ds(col, tn)] = alpha * x_ref[0] + (1.0 + gate_ref[0]) * mixed

    @pl.when(j == nj - 1)
    def _():
        r = x1_ref[0]
        mu = jnp.mean(r, axis=-1, keepdims=True)
        var = jnp.mean(jnp.square(r - mu), axis=-1, keepdims=True)
        x1 = (r - mu) * lax.rsqrt(var + LN_EPS) * lng_ref[...] + lnb_ref[...]
        x1_ref[0] = x1
        h2 = x1 * (1.0 + sc2_ref[0]) + sh2_ref[0]
        h2_ref[0] = h2
        hi = h2.astype(BF16)
        lo = (h2 - hi.astype(F32)).astype(BF16)
        lg_ref[0] = _dot(hi, wrh_ref[...]) + _dot(hi, wrl_ref[...]) + _dot(lo, wrh_ref[...])


def out_proj(attn, g_attn, ssm_n, w_out, x, gate1, ln_g, ln_b, scale2, shift2, wr_hi, wr_lo, alpha):
    bsz, seq, d = x.shape
    n_att = attn.shape[-1]
    n_ssm = ssm_n.shape[-1]
    k = n_att + n_ssm
    tm, tn = 256, 512
    row = lambda a: a.reshape(1, -1)
    full = lambda n: pl.BlockSpec((1, n), lambda b, i, j: (0, 0))
    return pl.pallas_call(
        functools.partial(_outproj_kernel, alpha=alpha, n_att=n_att),
        grid=(bsz, seq // tm, d // tn),
        in_specs=[pl.BlockSpec((1, tm, n_att), lambda b, i, j: (b, i, 0)),
                  full(n_att),
                  pl.BlockSpec((1, tm, n_ssm), lambda b, i, j: (b, i, 0)),
                  pl.BlockSpec((k, tn), lambda b, i, j: (0, j)),
                  pl.BlockSpec((1, tm, tn), lambda b, i, j: (b, i, j)),
                  pl.BlockSpec((1, 1, tn), lambda b, i, j: (b, 0, j)),
                  full(d), full(d),
                  pl.BlockSpec((1, 1, d), lambda b, i, j: (b, 0, 0)),
                  pl.BlockSpec((1, 1, d), lambda b, i, j: (b, 0, 0)),
                  pl.BlockSpec((d, LANES), lambda b, i, j: (0, 0)),
                  pl.BlockSpec((d, LANES), lambda b, i, j: (0, 0))],
        out_specs=[pl.BlockSpec((1, tm, d), lambda b, i, j: (b, i, 0)),
                   pl.BlockSpec((1, tm, d), lambda b, i, j: (b, i, 0)),
                   pl.BlockSpec((1, tm, LANES), lambda b, i, j: (b, i, 0))],
        out_shape=[jax.ShapeDtypeStruct((bsz, seq, d), F32),
                   jax.ShapeDtypeStruct((bsz, seq, d), F32),
                   jax.ShapeDtypeStruct((bsz, seq, LANES), F32)],
        scratch_shapes=[pltpu.VMEM((tm, k), BF16)],
        compiler_params=_params(("arbitrary", "arbitrary", "arbitrary"), 52),
        name="out_proj",
    )(attn, row(g_attn), ssm_n, w_out, x, gate1, row(ln_g), row(ln_b), scale2, shift2, wr_hi, wr_lo)


def _moe_kernel(tok_ref, bexp_ref, nact_ref, h_hbm, wrow_ref, wg_ref, wu_ref, wd_ref, y_ref, xbuf, sem):
    i = pl.program_id(0)
    nact = nact_ref[0]
    rows = xbuf.shape[1]

    def gather(blk, slot, start):
        def body(r, carry):
            tok = tok_ref[blk * rows + r]
            cp = pltpu.make_async_copy(h_hbm.at[pl.ds(tok, 1)], xbuf.at[slot, pl.ds(r, 1)], sem.at[slot])
            if start:
                cp.start()
            else:
                cp.wait()
            return carry
        lax.fori_loop(0, rows, body, 0)

    @pl.when(jnp.logical_and(i == 0, nact > 0))
    def _():
        gather(0, 0, True)

    @pl.when(i + 1 < nact)
    def _():
        gather(i + 1, (i + 1) % 2, True)

    @pl.when(i < nact)
    def _():
        slot = i % 2
        gather(i, slot, False)
        xb = xbuf[slot].astype(BF16)
        g = _dot(xb, wg_ref[0])
        u = _dot(xb, wu_ref[0])
        act = (jax.nn.silu(g) * u).astype(BF16)
        y_ref[...] = _dot(act, wd_ref[0]) * wrow_ref[...]

    @pl.when(i >= nact)
    def _():
        y_ref[...] = jnp.zeros_like(y_ref)


def moe_experts(h2, tok_buf, w_buf, block_expert, n_active, w_gate, w_up, w_down):
    t, d = h2.shape
    n_rows = tok_buf.shape[0]
    rows = MOE_ROWS
    n_blocks = n_rows // rows
    de = w_gate.shape[-1]

    def wmap(i, tok, bexp, nact):
        return (bexp[jnp.minimum(i, jnp.maximum(nact[0] - 1, 0))], 0, 0)

    grid_spec = pltpu.PrefetchScalarGridSpec(
        num_scalar_prefetch=3,
        grid=(n_blocks,),
        in_specs=[pl.BlockSpec(memory_space=pl.ANY),
                  pl.BlockSpec((rows, 1), lambda i, *_: (i, 0)),
                  pl.BlockSpec((1, d, de), wmap),
                  pl.BlockSpec((1, d, de), wmap),
                  pl.BlockSpec((1, de, d), wmap)],
        out_specs=pl.BlockSpec((rows, d), lambda i, *_: (i, 0)),
        scratch_shapes=[pltpu.VMEM((2, rows, d), F32), pltpu.SemaphoreType.DMA((2,))],
    )
    return pl.pallas_call(
        _moe_kernel,
        grid_spec=grid_spec,
        out_shape=jax.ShapeDtypeStruct((n_rows, d), F32),
        compiler_params=_params(("arbitrary",), 56),
        name="moe_experts",
    )(tok_buf, block_expert, n_active, h2, w_buf.reshape(n_rows, 1), w_gate, w_up, w_down)


def _combine_kernel(dest_ref, ys_hbm, x1_ref, gate_ref, lng_ref, lnb_ref, o_ref, ybuf, sem, *, alpha):
    i = pl.program_id(0)
    n = pl.num_programs(0)
    tm = x1_ref.shape[0]
    n_copies = TOP_K * tm

    def gather(blk, slot, start):
        def body(r, carry):
            row = dest_ref[blk * n_copies + r]
            cp = pltpu.make_async_copy(ys_hbm.at[pl.ds(row, 1)], ybuf.at[slot, pl.ds(r, 1)], sem.at[slot])
            if start:
                cp.start()
            else:
                cp.wait()
            return carry
        lax.fori_loop(0, n_copies, body, 0)

    @pl.when(i == 0)
    def _():
        gather(0, 0, True)

    @pl.when(i + 1 < n)
    def _():
        gather(i + 1, (i + 1) % 2, True)

    slot = i % 2
    gather(i, slot, False)
    moe = ybuf[slot, :tm, :]
    for kk in range(1, TOP_K):
        moe = moe + ybuf[slot, kk * tm:(kk + 1) * tm, :]
    r = alpha * x1_ref[...] + (1.0 + gate_ref[0]) * moe
    mu = jnp.mean(r, axis=-1, keepdims=True)
    var = jnp.mean(jnp.square(r - mu), axis=-1, keepdims=True)
    o_ref[...] = (r - mu) * lax.rsqrt(var + LN_EPS) * lng_ref[...] + lnb_ref[...]


def moe_combine(ys, dest, x1, gate2, ln_g, ln_b, alpha, seq):
    t, d = x1.shape
    tm = 128
    tiles_per_seq = seq // tm
    grid_spec = pltpu.PrefetchScalarGridSpec(
        num_scalar_prefetch=1,
        grid=(t // tm,),
        in_specs=[pl.BlockSpec(memory_space=pl.ANY),
                  pl.BlockSpec((tm, d), lambda i, *_: (i, 0)),
                  pl.BlockSpec((1, 1, d), lambda i, *_: (i // tiles_per_seq, 0, 0)),
                  pl.BlockSpec((1, d), lambda i, *_: (0, 0)),
                  pl.BlockSpec((1, d), lambda i, *_: (0, 0))],
        out_specs=pl.BlockSpec((tm, d), lambda i, *_: (i, 0)),
        scratch_shapes=[pltpu.VMEM((2, TOP_K * tm, d), F32), pltpu.SemaphoreType.DMA((2,))],
    )
    return pl.pallas_call(
        functools.partial(_combine_kernel, alpha=alpha),
        grid_spec=grid_spec,
        out_shape=jax.ShapeDtypeStruct((t, d), F32),
        compiler_params=_params(("arbitrary",), 32),
        name="moe_combine",
    )(dest, ys, x1, gate2, ln_g.reshape(1, d), ln_b.reshape(1, d))


def route(logits, b_rg, b_re, n_rows, tm_combine):
    n_tok = logits.shape[0]
    n_experts = N_EXPERT_GROUPS * EXPERTS_PER_GROUP
    g_logits = logits[:, :N_EXPERT_GROUPS] + b_rg
    e_logits = (logits[:, N_EXPERT_GROUPS:N_EXPERT_GROUPS + n_experts] + b_re).reshape(
        n_tok, N_EXPERT_GROUPS, EXPERTS_PER_GROUP)
    g_prob = jax.nn.softmax(g_logits, axis=-1)
    _, g_sel = lax.top_k(g_logits, 1)
    p_group = jnp.take_along_axis(g_prob, g_sel, axis=1)[:, 0]
    e_logits_g = jnp.take_along_axis(e_logits, g_sel[:, :, None], axis=1)[:, 0]
    e_top, e_idx = lax.top_k(e_logits_g, TOP_K)
    weights = p_group[:, None] * jax.nn.softmax(e_top, axis=-1)
    eid = (g_sel * EXPERTS_PER_GROUP + e_idx).astype(jnp.int32)

    n_assign = n_tok * TOP_K
    eid_f = eid.reshape(n_assign)
    onehot = (eid_f[:, None] == jnp.arange(n_experts, dtype=jnp.int32)[None, :]).astype(jnp.int32)
    csum = jnp.cumsum(onehot, axis=0)
    counts = csum[-1]
    rank = jnp.take_along_axis(csum, eid_f[:, None], axis=1)[:, 0] - 1
    padded = ((counts + MOE_ROWS - 1) // MOE_ROWS) * MOE_ROWS
    pends = jnp.cumsum(padded)
    pstarts = pends - padded
    dest = pstarts[eid_f] + rank
    tok = jnp.repeat(jnp.arange(n_tok, dtype=jnp.int32), TOP_K)
    tok_buf = jnp.zeros((n_rows,), jnp.int32).at[dest].set(tok)
    w_buf = jnp.zeros((n_rows,), F32).at[dest].set(weights.reshape(n_assign))
    n_blocks = n_rows // MOE_ROWS
    block_expert = jnp.clip(
        jnp.searchsorted(pends, jnp.arange(n_blocks, dtype=jnp.int32) * MOE_ROWS, side='right'),
        0, n_experts - 1).astype(jnp.int32)
    n_active = (pends[-1] // MOE_ROWS).astype(jnp.int32).reshape(1)
    dest_tiles = dest.reshape(n_tok // tm_combine, tm_combine, TOP_K).transpose(0, 2, 1).reshape(n_assign)
    return tok_buf, w_buf, block_expert, n_active, dest_tiles.astype(jnp.int32)


def _layer(x, c, w_ada, b_ada, w_in, b_forget, lam_re, lam_im, log_dt, b_re, b_im, c_re, c_im, d_skip,
           w_glu, b_glu, g_attn, g_ssm, w_out, ln1_g, ln1_b, w_rg, b_rg, w_re, b_re_r,
           w_gate, w_up, w_down, ln2_g, ln2_b, alpha):
    bsz, seq, d = x.shape
    n_tok = bsz * seq
    n_heads = b_forget.shape[0]
    d_att = n_heads * HEAD_DIM
    d_ssm = d_skip.shape[0] * d_skip.shape[1]

    mod = ada_mod(c, w_ada, b_ada).reshape(bsz, 1, -1)
    shift1, scale1, gate1, shift2, scale2, gate2 = jnp.split(mod, 6, axis=-1)

    n_qkv = 3 * d_att
    w_main = jnp.concatenate([w_in[:, :n_qkv], w_in[:, n_qkv + n_heads:]], axis=1).astype(BF16)
    w_f = jnp.zeros((d, LANES), BF16).at[:, :n_heads].set(w_in[:, n_qkv:n_qkv + n_heads].astype(BF16))
    qkv, u, f = in_proj(x, scale1, shift1, w_main, w_f, n_qkv)

    cum = forget_cum(f, b_forget)
    attn = attention(qkv, cum, n_heads)

    w1, ft, al = ssm_prep(lam_re, lam_im, log_dt, b_re, b_im, c_re, c_im)
    y = ssm_scan(u, w1, ft, al, d_skip)
    ssm_n = glu_norm(y.reshape(n_tok, d_ssm), w_glu.astype(BF16), b_glu, g_ssm).reshape(bsz, seq, d_ssm)

    n_experts = N_EXPERT_GROUPS * EXPERTS_PER_GROUP
    w_r = jnp.zeros((d, LANES), F32).at[:, :N_EXPERT_GROUPS].set(w_rg)
    w_r = w_r.at[:, N_EXPERT_GROUPS:N_EXPERT_GROUPS + n_experts].set(w_re)
    wr_hi = w_r.astype(BF16)
    wr_lo = (w_r - wr_hi.astype(F32)).astype(BF16)
    x1, h2, logits = out_proj(attn, g_attn, ssm_n, w_out.astype(BF16), x, gate1, ln1_g, ln1_b,
                              scale2, shift2, wr_hi, wr_lo, alpha)

    n_assign = n_tok * TOP_K
    n_blocks = -(-(n_assign + n_experts * (MOE_ROWS - 1)) // MOE_ROWS)
    n_rows = n_blocks * MOE_ROWS
    tok_buf, w_buf, block_expert, n_active, dest = route(logits.reshape(n_tok, LANES), b_rg, b_re_r, n_rows, 128)
    ys = moe_experts(h2.reshape(n_tok, d), tok_buf, w_buf, block_expert, n_active,
                     w_gate.astype(BF16), w_up.astype(BF16), w_down.astype(BF16))
    out = moe_combine(ys, dest, x1.reshape(n_tok, d), gate2, ln2_g, ln2_b, alpha, seq)
    return out.reshape(bsz, seq, d)


def kernel(x, c, w_ada, b_ada, w_in, b_forget, ssm_lambda_re, ssm_lambda_im, ssm_log_dt, ssm_b_re, ssm_b_im,
           ssm_c_re, ssm_c_im, ssm_d, w_glu, b_glu, g_attn, g_ssm, w_out, ln1_g, ln1_b, w_router_group,
           b_router_group, w_router_expert, b_router_expert, w_gate, w_up, w_down, ln2_g, ln2_b):
    depth = w_ada.shape[0]
    alpha = (2.0 * depth) ** 0.25
    for l in range(depth):
        x = _layer(x, c, w_ada[l], b_ada[l], w_in[l], b_forget[l], ssm_lambda_re[l], ssm_lambda_im[l],
                   ssm_log_dt[l], ssm_b_re[l], ssm_b_im[l], ssm_c_re[l], ssm_c_im[l], ssm_d[l],
                   w_glu[l], b_glu[l], g_attn[l], g_ssm[l], w_out[l], ln1_g[l], ln1_b[l],
                   w_router_group[l], b_router_group[l], w_router_expert[l], b_router_expert[l],
                   w_gate[l], w_up[l], w_down[l], ln2_g[l], ln2_b[l], alpha)
    return x
```

```python
import functools
import math

import jax
import jax.numpy as jnp
from jax import lax
from jax.experimental import pallas as pl
from jax.experimental.pallas import tpu as pltpu

F32 = jnp.float32
BF16 = jnp.bfloat16

LANES = 128
HEAD_DIM = 128
SSM_GROUP = 16
SSM_STATE = 64
GROUPS_PER_SLAB = LANES // SSM_GROUP
SLAB_STATE = GROUPS_PER_SLAB * SSM_STATE
SSM_CHUNK = 16
N_EXPERT_GROUPS = 8
EXPERTS_PER_GROUP = 8
TOP_K = 2
MOE_ROWS = 256
ROUTE_EID, ROUTE_RANK, ROUTE_W = 0, TOP_K, 2 * TOP_K
LN_EPS = 1e-5
RMS_EPS = 1e-6
NEG_BIG = -1e30
LOG2E = math.log2(math.e)
MIB = 1024 * 1024

_NT = (((1,), (1,)), ((), ()))


def _params(semantics, vmem_mib):
    return pltpu.CompilerParams(dimension_semantics=semantics, vmem_limit_bytes=vmem_mib * MIB)


def _dot(a, b):
    return jnp.dot(a, b, preferred_element_type=F32)


def _ada_kernel(c_ref, w_ref, b_ref, o_ref):
    s = jax.nn.silu(c_ref[...]).astype(BF16)
    o_ref[...] = _dot(s, w_ref[...].astype(BF16)) + b_ref[...]


def ada_mod(c, w_ada, b_ada):
    bsz, d = c.shape
    n = w_ada.shape[1]
    rows = 8
    assert bsz <= rows
    cp = jnp.zeros((rows, d), F32).at[:bsz].set(c)
    tn = 512
    out = pl.pallas_call(
        _ada_kernel,
        grid=(n // tn,),
        in_specs=[pl.BlockSpec((rows, d), lambda j: (0, 0)),
                  pl.BlockSpec((d, tn), lambda j: (0, j)),
                  pl.BlockSpec((1, tn), lambda j: (0, j))],
        out_specs=pl.BlockSpec((rows, tn), lambda j: (0, j)),
        out_shape=jax.ShapeDtypeStruct((rows, n), F32),
        compiler_params=_params(("arbitrary",), 40),
        name="ada_mod",
    )(cp, w_ada, b_ada.reshape(1, n))
    return out[:bsz]


def _inproj_kernel(x_ref, sc_ref, sh_ref, w_ref, wf_ref, qkv_ref, u_ref, f_ref, h_scr, *,
                   n_q_tiles, n_qkv_tiles, q_scale):
    j = pl.program_id(2)

    @pl.when(j == 0)
    def _():
        hb = (x_ref[0] * (1.0 + sc_ref[0]) + sh_ref[0]).astype(BF16)
        h_scr[...] = hb
        f_ref[0] = _dot(hb, wf_ref[...])

    acc = _dot(h_scr[...], w_ref[...])

    @pl.when(j < n_q_tiles)
    def _():
        qkv_ref[0] = (acc * q_scale).astype(BF16)

    @pl.when(jnp.logical_and(j >= n_q_tiles, j < n_qkv_tiles))
    def _():
        qkv_ref[0] = acc.astype(BF16)

    @pl.when(j >= n_qkv_tiles)
    def _():
        u_ref[0] = acc


def in_proj(x, scale, shift, w_main, w_f, n_qkv):
    bsz, seq, d = x.shape
    n_all = w_main.shape[1]
    n_u = n_all - n_qkv
    tm, tn = 512, 1024
    nq = n_qkv // tn
    grid = (bsz, seq // tm, n_all // tn)
    return pl.pallas_call(
        functools.partial(_inproj_kernel, n_q_tiles=n_qkv // 3 // tn, n_qkv_tiles=nq,
                          q_scale=HEAD_DIM ** -0.5 * LOG2E),
        grid=grid,
        in_specs=[pl.BlockSpec((1, tm, d), lambda b, i, j: (b, i, 0)),
                  pl.BlockSpec((1, 1, d), lambda b, i, j: (b, 0, 0)),
                  pl.BlockSpec((1, 1, d), lambda b, i, j: (b, 0, 0)),
                  pl.BlockSpec((d, tn), lambda b, i, j: (0, j)),
                  pl.BlockSpec((d, LANES), lambda b, i, j: (0, 0))],
        out_specs=[pl.BlockSpec((1, tm, tn), lambda b, i, j: (b, i, jnp.minimum(j, nq - 1))),
                   pl.BlockSpec((1, tm, tn), lambda b, i, j: (b, i, jnp.maximum(j - nq, 0))),
                   pl.BlockSpec((1, tm, LANES), lambda b, i, j: (b, i, 0))],
        out_shape=[jax.ShapeDtypeStruct((bsz, seq, n_qkv), BF16),
                   jax.ShapeDtypeStruct((bsz, seq, n_u), F32),
                   jax.ShapeDtypeStruct((bsz, seq, LANES), F32)],
        scratch_shapes=[pltpu.VMEM((tm, d), BF16)],
        compiler_params=_params(("arbitrary", "arbitrary", "arbitrary"), 52),
        name="in_proj",
    )(x, scale, shift, w_main, w_f)


def _split3(x):
    p1 = x.astype(BF16)
    r1 = x - p1.astype(F32)
    p2 = r1.astype(BF16)
    p3 = (r1 - p2.astype(F32)).astype(BF16)
    return p1, p2, p3


N_PIECES = 3


def _cum_kernel(f_ref, b_ref, qa_ref, ka_ref, carry, *, n_heads):
    i = pl.program_id(1)

    @pl.when(i == 0)
    def _():
        carry[...] = jnp.zeros_like(carry)

    tc = f_ref.shape[1]
    lf = jax.nn.log_sigmoid(f_ref[0] + b_ref[...])
    row = lax.broadcasted_iota(jnp.int32, (tc, tc), 0)
    col = lax.broadcasted_iota(jnp.int32, (tc, tc), 1)
    tri = (col <= row).astype(BF16)
    p1, p2, p3 = _split3(lf)
    cs = _dot(tri, p1) + _dot(tri, p2) + _dot(tri, p3) + carry[...]
    carry[...] = cs[tc - 1:tc, :]

    pieces = jnp.concatenate(_split3(cs * LOG2E), axis=1)
    r = lax.broadcasted_iota(jnp.int32, (N_PIECES * LANES, LANES), 0)
    c = lax.broadcasted_iota(jnp.int32, (N_PIECES * LANES, LANES), 1)
    lane = lax.broadcasted_iota(jnp.int32, (tc, LANES), 1)
    ones_q = jnp.where(jnp.logical_and(lane >= N_PIECES, lane < 2 * N_PIECES), 1.0, 0.0)
    ones_k = jnp.where(lane < N_PIECES, 1.0, 0.0)
    for h in range(n_heads):
        sel_q = (r == c * LANES + h).astype(BF16)
        sel_k = (r == (c - N_PIECES) * LANES + h).astype(BF16)
        qa_ref[0, h] = (_dot(pieces, sel_q) + ones_q).astype(BF16)
        ka_ref[0, h] = (ones_k - _dot(pieces, sel_k)).astype(BF16)


def forget_cum(f, b_forget):
    bsz, seq, _ = f.shape
    n_heads = b_forget.shape[0]
    tc = 256
    bpad = jnp.zeros((1, LANES), F32).at[0, :n_heads].set(b_forget)
    out_spec = pl.BlockSpec((1, n_heads, tc, LANES), lambda b, i: (b, 0, i, 0))
    out_shape = jax.ShapeDtypeStruct((bsz, n_heads, seq, LANES), BF16)
    return pl.pallas_call(
        functools.partial(_cum_kernel, n_heads=n_heads),
        grid=(bsz, seq // tc),
        in_specs=[pl.BlockSpec((1, tc, LANES), lambda b, i: (b, i, 0)),
                  pl.BlockSpec((1, LANES), lambda b, i: (0, 0))],
        out_specs=[out_spec, out_spec],
        out_shape=[out_shape, out_shape],
        scratch_shapes=[pltpu.VMEM((1, LANES), F32)],
        compiler_params=_params(("arbitrary", "arbitrary"), 24),
        name="forget_cum",
    )(f, bpad)


ATT_TILE = 512
ATT_SPLIT = 2


def _attn_kernel(q_ref, qa_ref, k_ref, ka_ref, v_ref, o_ref, m_scr, acc_scr, s_scr):
    t = q_ref.shape[1]
    rows = t // ATT_SPLIT
    qi = pl.program_id(2)
    q = jnp.concatenate([q_ref[0], qa_ref[0, 0]], axis=1)
    ones = jnp.ones((t, HEAD_DIM), BF16)

    m_scr[...] = jnp.full_like(m_scr, NEG_BIG)
    acc_scr[...] = jnp.zeros_like(acc_scr)

    def scores(kb, slot):
        k0 = pl.multiple_of(kb * t, t)
        kt = jnp.concatenate([k_ref[0, pl.ds(k0, t), :], ka_ref[0, 0, pl.ds(k0, t), :]], axis=1)
        s_scr[slot] = lax.dot_general(q, kt, _NT, preferred_element_type=F32)

    def update(kb, slot, masked):
        k0 = pl.multiple_of(kb * t, t)
        vt = jnp.concatenate([v_ref[0, pl.ds(k0, t), :], ones], axis=1)
        for g in range(ATT_SPLIT):
            rs = slice(g * rows, (g + 1) * rows)
            s = s_scr[slot, rs, :]
            if masked:
                qpos = lax.broadcasted_iota(jnp.int32, (rows, t), 0) + g * rows
                kpos = lax.broadcasted_iota(jnp.int32, (rows, t), 1)
                s = jnp.where(kpos <= qpos, s, NEG_BIG)
            m_prev = m_scr[rs]
            m_new = jnp.maximum(m_prev, jnp.max(s, axis=1, keepdims=True))
            alpha = jnp.exp2(m_prev - m_new)
            p = jnp.exp2(s - m_new)
            acc_scr[rs] = alpha * acc_scr[rs] + _dot(p.astype(BF16), vt)
            m_scr[rs] = m_new

    def body(i, carry):
        kb = 2 * i
        scores(kb + 1, 1)
        update(kb, 0, False)
        scores(kb + 2, 0)
        update(kb + 1, 1, False)
        return carry

    scores(0, 0)
    lax.fori_loop(0, qi // 2, body, 0)

    @pl.when(qi % 2 == 0)
    def _():
        update(qi, 0, True)

    @pl.when(qi % 2 == 1)
    def _():
        scores(qi, 1)
        update(qi - 1, 0, False)
        update(qi, 1, True)

    o_ref[0] = acc_scr[:, :HEAD_DIM] / acc_scr[:, HEAD_DIM:]


def attention(qkv, q_aug, k_aug, n_heads):
    bsz, seq, _ = qkv.shape
    t = ATT_TILE
    return pl.pallas_call(
        _attn_kernel,
        grid=(bsz, n_heads, seq // t),
        in_specs=[pl.BlockSpec((1, t, HEAD_DIM), lambda b, h, i: (b, i, h)),
                  pl.BlockSpec((1, 1, t, LANES), lambda b, h, i: (b, h, i, 0)),
                  pl.BlockSpec((1, seq, HEAD_DIM), lambda b, h, i: (b, 0, n_heads + h)),
                  pl.BlockSpec((1, 1, seq, LANES), lambda b, h, i: (b, h, 0, 0)),
                  pl.BlockSpec((1, seq, HEAD_DIM), lambda b, h, i: (b, 0, 2 * n_heads + h))],
        out_specs=pl.BlockSpec((1, t, HEAD_DIM), lambda b, h, i: (b, i, h)),
        out_shape=jax.ShapeDtypeStruct((bsz, seq, n_heads * HEAD_DIM), F32),
        scratch_shapes=[pltpu.VMEM((t, 1), F32), pltpu.VMEM((t, 2 * HEAD_DIM), F32),
                        pltpu.VMEM((2, t, t), F32)],
        compiler_params=_params(("arbitrary", "arbitrary", "arbitrary"), 32),
        name="attention",
    )(qkv, q_aug, qkv, k_aug, qkv)


def _blockdiag(p):
    g, c, n = p.shape
    ns = g // GROUPS_PER_SLAB
    eye = jnp.eye(GROUPS_PER_SLAB, dtype=p.dtype)
    out = p.reshape(ns, GROUPS_PER_SLAB, c, 1, n) * eye[None, :, None, :, None]
    return out.reshape(ns, GROUPS_PER_SLAB * c, GROUPS_PER_SLAB * n)


def _ssm_prep_kernel(lr_ref, li_ref, ldt_ref, bre_ref, bim_ref, cre_ref, cim_ref, w1_ref, ft_ref, al_ref):
    L = SSM_CHUNK
    lr = lr_ref[0]
    li = li_ref[0]
    dt = jnp.exp(ldt_ref[0])
    mag = jnp.exp(lr * dt)
    a_re = mag * jnp.cos(li * dt)
    a_im = mag * jnp.sin(li * dt)
    den = lr * lr + li * li
    z_re = ((a_re - 1.0) * lr + a_im * li) / den
    z_im = (a_im * lr - (a_re - 1.0) * li) / den
    br = bre_ref[0]
    bi = bim_ref[0]
    bb_re = z_re * br - z_im * bi
    bb_im = z_re * bi + z_im * br
    cr = cre_ref[0]
    ci = cim_ref[0]
    ft0 = jnp.concatenate([cr, -ci], axis=1)

    def power(d):
        m = jnp.exp(lr * dt * d)
        return m * jnp.cos(li * dt * d), m * jnp.sin(li * dt * d)

    w1_ref[0, :, :L * LANES] = jnp.zeros((L * LANES, L * LANES), BF16)
    for d in range(L):
        pr, pi = power(float(d))
        xe = jnp.concatenate([bb_re * pr - bb_im * pi, bb_re * pi + bb_im * pr], axis=1)
        j = L - 1 - d
        w1_ref[0, j * LANES:(j + 1) * LANES, L * LANES:] = xe.astype(BF16)
        m_d = lax.dot_general(xe, ft0, _NT, preferred_element_type=F32,
                              precision=lax.Precision.HIGHEST).astype(BF16)
        for jj in range(L - d):
            w1_ref[0, jj * LANES:(jj + 1) * LANES, (jj + d) * LANES:(jj + d + 1) * LANES] = m_d
        pr1, pi1 = power(float(d + 1))
        ft_ref[0, d * LANES:(d + 1) * LANES, :] = jnp.concatenate(
            [cr * pr1 - ci * pi1, -(cr * pi1 + ci * pr1)], axis=1).astype(BF16)
    prl, pil = power(float(L))
    al_ref[0] = jnp.concatenate([prl, pil], axis=1)


def ssm_prep(lam_re, lam_im, log_dt, b_re, b_im, c_re, c_im):
    g, n = lam_re.shape
    ns = g // GROUPS_PER_SLAB
    L = SSM_CHUNK
    rowvec = lambda a: a.reshape(ns, 1, SLAB_STATE)
    args = (rowvec(lam_re), rowvec(lam_im), rowvec(jnp.repeat(log_dt, n)),
            _blockdiag(b_re.transpose(0, 2, 1)), _blockdiag(b_im.transpose(0, 2, 1)),
            _blockdiag(c_re), _blockdiag(c_im))
    vec_spec = pl.BlockSpec((1, 1, SLAB_STATE), lambda s: (s, 0, 0))
    mat_spec = pl.BlockSpec((1, LANES, SLAB_STATE), lambda s: (s, 0, 0))
    return pl.pallas_call(
        _ssm_prep_kernel,
        grid=(ns,),
        in_specs=[vec_spec] * 3 + [mat_spec] * 4,
        out_specs=[pl.BlockSpec((1, L * LANES, L * LANES + 2 * SLAB_STATE), lambda s: (s, 0, 0)),
                   pl.BlockSpec((1, L * LANES, 2 * SLAB_STATE), lambda s: (s, 0, 0)),
                   pl.BlockSpec((1, 1, 2 * SLAB_STATE), lambda s: (s, 0, 0))],
        out_shape=[jax.ShapeDtypeStruct((ns, L * LANES, L * LANES + 2 * SLAB_STATE), BF16),
                   jax.ShapeDtypeStruct((ns, L * LANES, 2 * SLAB_STATE), BF16),
                   jax.ShapeDtypeStruct((ns, 1, 2 * SLAB_STATE), F32)],
        compiler_params=_params(("arbitrary",), 48),
        name="ssm_prep",
    )(*args)


def _ssm_kernel(u_ref, w1_ref, ft_ref, al_ref, d_ref, y_ref, uf_scr, e_scr, y_scr):
    L = SSM_CHUNK
    nch = uf_scr.shape[0]
    lc = L * LANES
    for j in range(L):
        uf_scr[:, j * LANES:(j + 1) * LANES] = u_ref[0, pl.ds(j, nch, stride=L), :].astype(BF16)
    uf = uf_scr[...]
    y_scr[...] = _dot(uf, w1_ref[0, :, :lc])
    e_scr[...] = _dot(uf, w1_ref[0, :, lc:])

    a_re = al_ref[0, :, :SLAB_STATE]
    a_im = al_ref[0, :, SLAB_STATE:]

    def body(k, h):
        h_re, h_im = h
        e = e_scr[pl.ds(k, 1), :]
        e_scr[pl.ds(k, 1), :] = jnp.concatenate([h_re, h_im], axis=1)
        return (a_re * h_re - a_im * h_im + e[:, :SLAB_STATE],
                a_re * h_im + a_im * h_re + e[:, SLAB_STATE:])

    zero = jnp.zeros((1, SLAB_STATE), F32)
    lax.fori_loop(0, nch, body, (zero, zero))
    y = y_scr[...] + lax.dot_general(e_scr[...].astype(BF16), ft_ref[0], _NT, preferred_element_type=F32)
    for i in range(L):
        yi = y[:, i * LANES:(i + 1) * LANES] + d_ref[0] * u_ref[0, pl.ds(i, nch, stride=L), :]
        y_ref[0, pl.ds(i, nch, stride=L), :] = jax.nn.gelu(yi)


def ssm_scan(u, w1, ft, al, d_skip):
    bsz, seq, c = u.shape
    ns = c // LANES
    L = SSM_CHUNK
    nch = seq // L
    lc = L * LANES
    once = pl.Buffered(1)
    return pl.pallas_call(
        _ssm_kernel,
        grid=(ns, bsz),
        in_specs=[pl.BlockSpec((1, seq, LANES), lambda s, b: (b, 0, s)),
                  pl.BlockSpec((1, lc, lc + 2 * SLAB_STATE), lambda s, b: (s, 0, 0), pipeline_mode=once),
                  pl.BlockSpec((1, lc, 2 * SLAB_STATE), lambda s, b: (s, 0, 0), pipeline_mode=once),
                  pl.BlockSpec((1, 1, 2 * SLAB_STATE), lambda s, b: (s, 0, 0)),
                  pl.BlockSpec((1, 1, LANES), lambda s, b: (s, 0, 0))],
        out_specs=pl.BlockSpec((1, seq, LANES), lambda s, b: (b, 0, s)),
        out_shape=jax.ShapeDtypeStruct((bsz, seq, c), F32),
        scratch_shapes=[pltpu.VMEM((nch, lc), BF16), pltpu.VMEM((nch, 2 * SLAB_STATE), F32),
                        pltpu.VMEM((nch, lc), F32)],
        compiler_params=_params(("arbitrary", "arbitrary"), 56),
        name="ssm_scan",
    )(u, w1, ft, al, d_skip.reshape(ns, 1, LANES))


def _glu_kernel(y_ref, w_ref, b_ref, g_ref, o_ref):
    y = y_ref[...]
    o = y * jax.nn.sigmoid(_dot(y.astype(BF16), w_ref[...]) + b_ref[...])
    ms = jnp.mean(o * o, axis=-1, keepdims=True)
    o_ref[...] = (o * lax.rsqrt(ms + RMS_EPS) * g_ref[...]).astype(BF16)


def glu_norm(y, w_glu, b_glu, g):
    t, c = y.shape
    tm = 512
    return pl.pallas_call(
        _glu_kernel,
        grid=(t // tm,),
        in_specs=[pl.BlockSpec((tm, c), lambda i: (i, 0)),
                  pl.BlockSpec((c, c), lambda i: (0, 0)),
                  pl.BlockSpec((1, c), lambda i: (0, 0)),
                  pl.BlockSpec((1, c), lambda i: (0, 0))],
        out_specs=pl.BlockSpec((tm, c), lambda i: (i, 0)),
        out_shape=jax.ShapeDtypeStruct((t, c), BF16),
        compiler_params=_params(("arbitrary",), 48),
        name="glu_norm",
    )(y, w_glu, b_glu.reshape(1, c), g.reshape(1, c))


def _outproj_kernel(attn_ref, ga_ref, ssm_ref, w_ref, x_ref, gate_ref, lng_ref, lnb_ref, sc2_ref, sh2_ref,
                    wrh_ref, wrl_ref, x1_ref, h2_ref, lg_ref, a_scr, *, alpha, n_att):
    j = pl.program_id(2)
    nj = pl.num_programs(2)
    tn = w_ref.shape[1]

    @pl.when(j == 0)
    def _():
        a = attn_ref[0]
        ms = jnp.mean(a * a, axis=-1, keepdims=True)
        a_scr[:, :n_att] = (a * lax.rsqrt(ms + RMS_EPS) * ga_ref[...]).astype(BF16)
        a_scr[:, n_att:] = ssm_ref[0]

    mixed = _dot(a_scr[...], w_ref[...])
    col = pl.multiple_of(j * tn, tn)
    x1_ref[0, :, pl.ds(col, tn)] = alpha * x_ref[0] + (1.0 + gate_ref[0]) * mixed

    @pl.when(j == nj - 1)
    def _():
        r = x1_ref[0]
        mu = jnp.mean(r, axis=-1, keepdims=True)
        var = jnp.mean(jnp.square(r - mu), axis=-1, keepdims=True)
        x1 = (r - mu) * lax.rsqrt(var + LN_EPS) * lng_ref[...] + lnb_ref[...]
        x1_ref[0] = x1
        h2 = x1 * (1.0 + sc2_ref[0]) + sh2_ref[0]
        h2_ref[0] = h2
        hi = h2.astype(BF16)
        lo = (h2 - hi.astype(F32)).astype(BF16)
        lg_ref[0] = _dot(hi, wrh_ref[...]) + _dot(hi, wrl_ref[...]) + _dot(lo, wrh_ref[...])


def out_proj(attn, g_attn, ssm_n, w_out, x, gate1, ln_g, ln_b, scale2, shift2, wr_hi, wr_lo, alpha):
    bsz, seq, d = x.shape
    n_att = attn.shape[-1]
    n_ssm = ssm_n.shape[-1]
    k = n_att + n_ssm
    tm, tn = 256, 512
    row = lambda a: a.reshape(1, -1)
    full = lambda n: pl.BlockSpec((1, n), lambda b, i, j: (0, 0))
    return pl.pallas_call(
        functools.partial(_outproj_kernel, alpha=alpha, n_att=n_att),
        grid=(bsz, seq // tm, d // tn),
        in_specs=[pl.BlockSpec((1, tm, n_att), lambda b, i, j: (b, i, 0)),
                  full(n_att),
                  pl.BlockSpec((1, tm, n_ssm), lambda b, i, j: (b, i, 0)),
                  pl.BlockSpec((k, tn), lambda b, i, j: (0, j)),
                  pl.BlockSpec((1, tm, tn), lambda b, i, j: (b, i, j)),
                  pl.BlockSpec((1, 1, tn), lambda b, i, j: (b, 0, j)),
                  full(d), full(d),
                  pl.BlockSpec((1, 1, d), lambda b, i, j: (b, 0, 0)),
                  pl.BlockSpec((1, 1, d), lambda b, i, j: (b, 0, 0)),
                  pl.BlockSpec((d, LANES), lambda b, i, j: (0, 0)),
                  pl.BlockSpec((d, LANES), lambda b, i, j: (0, 0))],
        out_specs=[pl.BlockSpec((1, tm, d), lambda b, i, j: (b, i, 0)),
                   pl.BlockSpec((1, tm, d), lambda b, i, j: (b, i, 0)),
                   pl.BlockSpec((1, tm, LANES), lambda b, i, j: (b, i, 0))],
        out_shape=[jax.ShapeDtypeStruct((bsz, seq, d), F32),
                   jax.ShapeDtypeStruct((bsz, seq, d), F32),
                   jax.ShapeDtypeStruct((bsz, seq, LANES), F32)],
        scratch_shapes=[pltpu.VMEM((tm, k), BF16)],
        compiler_params=_params(("arbitrary", "arbitrary", "arbitrary"), 52),
        name="out_proj",
    )(attn, row(g_attn), ssm_n, w_out, x, gate1, row(ln_g), row(ln_b), scale2, shift2, wr_hi, wr_lo)


def _moe_kernel(tok_ref, bexp_ref, nact_ref, h_hbm, wg_ref, wu_ref, wd_ref, y_ref, xbuf, sem):
    i = pl.program_id(0)
    nact = nact_ref[0]
    rows = xbuf.shape[1]

    def gather(blk, slot, start):
        def body(r, carry):
            tok = tok_ref[blk * rows + r]
            cp = pltpu.make_async_copy(h_hbm.at[pl.ds(tok, 1)], xbuf.at[slot, pl.ds(r, 1)], sem.at[slot])
            if start:
                cp.start()
            else:
                cp.wait()
            return carry
        lax.fori_loop(0, rows, body, 0)

    @pl.when(jnp.logical_and(i == 0, nact > 0))
    def _():
        gather(0, 0, True)

    @pl.when(i + 1 < nact)
    def _():
        gather(i + 1, (i + 1) % 2, True)

    @pl.when(i < nact)
    def _():
        slot = i % 2
        gather(i, slot, False)
        xb = xbuf[slot].astype(BF16)
        g = _dot(xb, wg_ref[0])
        u = _dot(xb, wu_ref[0])
        act = (jax.nn.silu(g) * u).astype(BF16)
        y_ref[...] = _dot(act, wd_ref[0])

    @pl.when(i >= nact)
    def _():
        y_ref[...] = jnp.zeros_like(y_ref)


def moe_experts(h2, tok_buf, block_expert, n_active, w_gate, w_up, w_down):
    t, d = h2.shape
    n_rows = tok_buf.shape[0]
    rows = MOE_ROWS
    n_blocks = n_rows // rows
    de = w_gate.shape[-1]

    def wmap(i, tok, bexp, nact):
        return (bexp[jnp.minimum(i, jnp.maximum(nact[0] - 1, 0))], 0, 0)

    grid_spec = pltpu.PrefetchScalarGridSpec(
        num_scalar_prefetch=3,
        grid=(n_blocks,),
        in_specs=[pl.BlockSpec(memory_space=pl.ANY),
                  pl.BlockSpec((1, d, de), wmap),
                  pl.BlockSpec((1, d, de), wmap),
                  pl.BlockSpec((1, de, d), wmap)],
        out_specs=pl.BlockSpec((rows, d), lambda i, *_: (i, 0)),
        scratch_shapes=[pltpu.VMEM((2, rows, d), F32), pltpu.SemaphoreType.DMA((2,))],
    )
    return pl.pallas_call(
        _moe_kernel,
        grid_spec=grid_spec,
        out_shape=jax.ShapeDtypeStruct((n_rows, d), F32),
        compiler_params=_params(("arbitrary",), 56),
        name="moe_experts",
    )(tok_buf, block_expert, n_active, h2, w_gate, w_up, w_down)


def _combine_kernel(dest_ref, ys_hbm, rt_ref, x1_ref, gate_ref, lng_ref, lnb_ref, o_ref, ybuf, sem, *, alpha):
    i = pl.program_id(0)
    n = pl.num_programs(0)
    tm = x1_ref.shape[0]
    n_copies = TOP_K * tm

    def gather(blk, slot, start):
        def body(r, carry):
            row = dest_ref[blk * n_copies + r]
            cp = pltpu.make_async_copy(ys_hbm.at[pl.ds(row, 1)], ybuf.at[slot, pl.ds(r, 1)], sem.at[slot])
            if start:
                cp.start()
            else:
                cp.wait()
            return carry
        lax.fori_loop(0, n_copies, body, 0)

    @pl.when(i == 0)
    def _():
        gather(0, 0, True)

    @pl.when(i + 1 < n)
    def _():
        gather(i + 1, (i + 1) % 2, True)

    slot = i % 2
    gather(i, slot, False)
    moe = rt_ref[:, ROUTE_W:ROUTE_W + 1] * ybuf[slot, :tm, :]
    for kk in range(1, TOP_K):
        moe = moe + rt_ref[:, ROUTE_W + kk:ROUTE_W + kk + 1] * ybuf[slot, kk * tm:(kk + 1) * tm, :]
    r = alpha * x1_ref[...] + (1.0 + gate_ref[0]) * moe
    mu = jnp.mean(r, axis=-1, keepdims=True)
    var = jnp.mean(jnp.square(r - mu), axis=-1, keepdims=True)
    o_ref[...] = (r - mu) * lax.rsqrt(var + LN_EPS) * lng_ref[...] + lnb_ref[...]


def moe_combine(ys, dest, table, x1, gate2, ln_g, ln_b, alpha, seq):
    t, d = x1.shape
    tm = 128
    tiles_per_seq = seq // tm
    grid_spec = pltpu.PrefetchScalarGridSpec(
        num_scalar_prefetch=1,
        grid=(t // tm,),
        in_specs=[pl.BlockSpec(memory_space=pl.ANY),
                  pl.BlockSpec((tm, LANES), lambda i, *_: (i, 0)),
                  pl.BlockSpec((tm, d), lambda i, *_: (i, 0)),
                  pl.BlockSpec((1, 1, d), lambda i, *_: (i // tiles_per_seq, 0, 0)),
                  pl.BlockSpec((1, d), lambda i, *_: (0, 0)),
                  pl.BlockSpec((1, d), lambda i, *_: (0, 0))],
        out_specs=pl.BlockSpec((tm, d), lambda i, *_: (i, 0)),
        scratch_shapes=[pltpu.VMEM((2, TOP_K * tm, d), F32), pltpu.SemaphoreType.DMA((2,))],
    )
    return pl.pallas_call(
        functools.partial(_combine_kernel, alpha=alpha),
        grid_spec=grid_spec,
        out_shape=jax.ShapeDtypeStruct((t, d), F32),
        compiler_params=_params(("arbitrary",), 32),
        name="moe_combine",
    )(dest, ys, table, x1, gate2, ln_g.reshape(1, d), ln_b.reshape(1, d))


def _route_kernel(lg_ref, b_ref, o_ref, cnt_ref, carry):
    i = pl.program_id(0)

    @pl.when(i == 0)
    def _():
        carry[...] = jnp.zeros_like(carry)

    tm = lg_ref.shape[0]
    x = lg_ref[...] + b_ref[...]
    lane = lax.broadcasted_iota(jnp.int32, (tm, LANES), 1)
    ninf = -jnp.inf

    def top(v):
        vmax = jnp.max(v, axis=1, keepdims=True)
        return vmax, jnp.min(jnp.where(v == vmax, lane, LANES), axis=1, keepdims=True)

    gmask = lane < N_EXPERT_GROUPS
    gmax, g_sel = top(jnp.where(gmask, x, ninf))
    p_group = 1.0 / jnp.sum(jnp.where(gmask, jnp.exp(x - gmax), 0.0), axis=1, keepdims=True)

    lo = N_EXPERT_GROUPS + g_sel * EXPERTS_PER_GROUP
    cur = jnp.where(jnp.logical_and(lane >= lo, lane < lo + EXPERTS_PER_GROUP), x, ninf)
    vals, idxs = [], []
    for _ in range(TOP_K):
        v, ix = top(cur)
        vals.append(v)
        idxs.append(ix)
        cur = jnp.where(lane == ix, ninf, cur)
    exps = [jnp.exp(v - vals[0]) for v in vals]
    den = functools.reduce(lambda a, c: a + c, exps)

    member = functools.reduce(jnp.logical_or, [lane == ix for ix in idxs])
    mf = jnp.where(member, 1.0, 0.0)
    r_i = lax.broadcasted_iota(jnp.int32, (tm, tm), 0)
    c_i = lax.broadcasted_iota(jnp.int32, (tm, tm), 1)
    before = _dot((c_i < r_i).astype(BF16), mf.astype(BF16)) + carry[...]
    carry[...] = carry[...] + jnp.sum(mf, axis=0, keepdims=True)
    cnt_ref[...] = carry[...]

    out = jnp.zeros((tm, LANES), F32)
    for k in range(TOP_K):
        rank = jnp.sum(jnp.where(lane == idxs[k], before, 0.0), axis=1, keepdims=True)
        out = jnp.where(lane == ROUTE_EID + k, (idxs[k] - N_EXPERT_GROUPS).astype(F32), out)
        out = jnp.where(lane == ROUTE_RANK + k, rank, out)
        out = jnp.where(lane == ROUTE_W + k, p_group * (exps[k] / den), out)
    o_ref[...] = out


def route(logits, b_rg, b_re, n_rows, tm_combine):
    n_tok = logits.shape[0]
    n_experts = N_EXPERT_GROUPS * EXPERTS_PER_GROUP
    bias = jnp.zeros((1, LANES), F32).at[0, :N_EXPERT_GROUPS].set(b_rg)
    bias = bias.at[0, N_EXPERT_GROUPS:N_EXPERT_GROUPS + n_experts].set(b_re)
    tm = 512
    table, cnt = pl.pallas_call(
        _route_kernel,
        grid=(n_tok // tm,),
        in_specs=[pl.BlockSpec((tm, LANES), lambda i: (i, 0)),
                  pl.BlockSpec((1, LANES), lambda i: (0, 0))],
        out_specs=[pl.BlockSpec((tm, LANES), lambda i: (i, 0)),
                   pl.BlockSpec((1, LANES), lambda i: (0, 0))],
        out_shape=[jax.ShapeDtypeStruct((n_tok, LANES), F32), jax.ShapeDtypeStruct((1, LANES), F32)],
        scratch_shapes=[pltpu.VMEM((1, LANES), F32)],
        compiler_params=_params(("arbitrary",), 16),
        name="route",
    )(logits, bias)

    eid = table[:, ROUTE_EID:ROUTE_EID + TOP_K].astype(jnp.int32)
    rank = table[:, ROUTE_RANK:ROUTE_RANK + TOP_K].astype(jnp.int32)
    counts = cnt[0, N_EXPERT_GROUPS:N_EXPERT_GROUPS + n_experts].astype(jnp.int32)
    padded = ((counts + MOE_ROWS - 1) // MOE_ROWS) * MOE_ROWS
    pends = jnp.cumsum(padded)
    pstarts = pends - padded
    onehot = eid[:, :, None] == jnp.arange(n_experts, dtype=jnp.int32)
    dest = jnp.sum(jnp.where(onehot, pstarts, 0), axis=-1) + rank
    tok = jnp.broadcast_to(jnp.arange(n_tok, dtype=jnp.int32)[:, None], (n_tok, TOP_K))
    tok_buf = jnp.zeros((n_rows,), jnp.int32).at[dest.reshape(-1)].set(tok.reshape(-1), unique_indices=True)
    n_blocks = n_rows // MOE_ROWS
    block_expert = jnp.clip(
        jnp.searchsorted(pends, jnp.arange(n_blocks, dtype=jnp.int32) * MOE_ROWS, side='right'),
        0, n_experts - 1).astype(jnp.int32)
    n_active = (pends[-1] // MOE_ROWS).astype(jnp.int32).reshape(1)
    dest_tiles = dest.reshape(n_tok // tm_combine, tm_combine, TOP_K).transpose(0, 2, 1).reshape(-1)
    return table, tok_buf, block_expert, n_active, dest_tiles.astype(jnp.int32)


def _layer(x, c, w_ada, b_ada, w_in, b_forget, lam_re, lam_im, log_dt, b_re, b_im, c_re, c_im, d_skip,
           w_glu, b_glu, g_attn, g_ssm, w_out, ln1_g, ln1_b, w_rg, b_rg, w_re, b_re_r,
           w_gate, w_up, w_down, ln2_g, ln2_b, alpha):
    bsz, seq, d = x.shape
    n_tok = bsz * seq
    n_heads = b_forget.shape[0]
    d_att = n_heads * HEAD_DIM
    d_ssm = d_skip.shape[0] * d_skip.shape[1]

    mod = ada_mod(c, w_ada, b_ada).reshape(bsz, 1, -1)
    shift1, scale1, gate1, shift2, scale2, gate2 = jnp.split(mod, 6, axis=-1)

    n_qkv = 3 * d_att
    w_main = jnp.concatenate([w_in[:, :n_qkv], w_in[:, n_qkv + n_heads:]], axis=1).astype(BF16)
    w_f = jnp.zeros((d, LANES), BF16).at[:, :n_heads].set(w_in[:, n_qkv:n_qkv + n_heads].astype(BF16))
    qkv, u, f = in_proj(x, scale1, shift1, w_main, w_f, n_qkv)

    q_aug, k_aug = forget_cum(f, b_forget)
    attn = attention(qkv, q_aug, k_aug, n_heads)

    w1, ft, al = ssm_prep(lam_re, lam_im, log_dt, b_re, b_im, c_re, c_im)
    y = ssm_scan(u, w1, ft, al, d_skip)
    ssm_n = glu_norm(y.reshape(n_tok, d_ssm), w_glu.astype(BF16), b_glu, g_ssm).reshape(bsz, seq, d_ssm)

    n_experts = N_EXPERT_GROUPS * EXPERTS_PER_GROUP
    w_r = jnp.zeros((d, LANES), F32).at[:, :N_EXPERT_GROUPS].set(w_rg)
    w_r = w_r.at[:, N_EXPERT_GROUPS:N_EXPERT_GROUPS + n_experts].set(w_re)
    wr_hi = w_r.astype(BF16)
    wr_lo = (w_r - wr_hi.astype(F32)).astype(BF16)
    x1, h2, logits = out_proj(attn, g_attn, ssm_n, w_out.astype(BF16), x, gate1, ln1_g, ln1_b,
                              scale2, shift2, wr_hi, wr_lo, alpha)

    n_assign = n_tok * TOP_K
    n_blocks = -(-(n_assign + n_experts * (MOE_ROWS - 1)) // MOE_ROWS)
    n_rows = n_blocks * MOE_ROWS
    table, tok_buf, block_expert, n_active, dest = route(logits.reshape(n_tok, LANES), b_rg, b_re_r, n_rows, 128)
    ys = moe_experts(h2.reshape(n_tok, d), tok_buf, block_expert, n_active,
                     w_gate.astype(BF16), w_up.astype(BF16), w_down.astype(BF16))
    out = moe_combine(ys, dest, table, x1.reshape(n_tok, d), gate2, ln2_g, ln2_b, alpha, seq)
    return out.reshape(bsz, seq, d)


def kernel(x, c, w_ada, b_ada, w_in, b_forget, ssm_lambda_re, ssm_lambda_im, ssm_log_dt, ssm_b_re, ssm_b_im,
           ssm_c_re, ssm_c_im, ssm_d, w_glu, b_glu, g_attn, g_ssm, w_out, ln1_g, ln1_b, w_router_group,
           b_router_group, w_router_expert, b_router_expert, w_gate, w_up, w_down, ln2_g, ln2_b):
    depth = w_ada.shape[0]
    alpha = (2.0 * depth) ** 0.25
    for l in range(depth):
        x = _layer(x, c, w_ada[l], b_ada[l], w_in[l], b_forget[l], ssm_lambda_re[l], ssm_lambda_im[l],
                   ssm_log_dt[l], ssm_b_re[l], ssm_b_im[l], ssm_c_re[l], ssm_c_im[l], ssm_d[l],
                   w_glu[l], b_glu[l], g_attn[l], g_ssm[l], w_out[l], ln1_g[l], ln1_b[l],
                   w_router_group[l], b_router_group[l], w_router_expert[l], b_router_expert[l],
                   w_gate[l], w_up[l], w_down[l], ln2_g[l], ln2_b[l], alpha)
    return x
```

```python
import functools
import math

import jax
import jax.numpy as jnp
from jax import lax
from jax.experimental import pallas as pl
from jax.experimental.pallas import tpu as pltpu

F32 = jnp.float32
BF16 = jnp.bfloat16

LANES = 128
HEAD_DIM = 128
SSM_GROUP = 16
SSM_STATE = 64
GROUPS_PER_SLAB = LANES // SSM_GROUP
SLAB_STATE = GROUPS_PER_SLAB * SSM_STATE
SSM_CHUNK = 16
N_EXPERT_GROUPS = 8
EXPERTS_PER_GROUP = 8
TOP_K = 2
MOE_ROWS = 256
GATHER_UNROLL = 8
ROUTE_EID, ROUTE_RANK, ROUTE_W = 0, TOP_K, 2 * TOP_K
LN_EPS = 1e-5
RMS_EPS = 1e-6
NEG_BIG = -1e30
LOG2E = math.log2(math.e)
MIB = 1024 * 1024

_NT = (((1,), (1,)), ((), ()))


def _params(semantics, vmem_mib):
    return pltpu.CompilerParams(dimension_semantics=semantics, vmem_limit_bytes=vmem_mib * MIB)


def _dot(a, b):
    return jnp.dot(a, b, preferred_element_type=F32)


def _ada_kernel(c_ref, w_ref, b_ref, o_ref):
    s = jax.nn.silu(c_ref[...]).astype(BF16)
    o_ref[...] = _dot(s, w_ref[...].astype(BF16)) + b_ref[...]


def ada_mod(c, w_ada, b_ada):
    bsz, d = c.shape
    n = w_ada.shape[1]
    rows = 8
    assert bsz <= rows
    cp = jnp.zeros((rows, d), F32).at[:bsz].set(c)
    tn = 512
    out = pl.pallas_call(
        _ada_kernel,
        grid=(n // tn,),
        in_specs=[pl.BlockSpec((rows, d), lambda j: (0, 0)),
                  pl.BlockSpec((d, tn), lambda j: (0, j)),
                  pl.BlockSpec((1, tn), lambda j: (0, j))],
        out_specs=pl.BlockSpec((rows, tn), lambda j: (0, j)),
        out_shape=jax.ShapeDtypeStruct((rows, n), F32),
        compiler_params=_params(("arbitrary",), 40),
        name="ada_mod",
    )(cp, w_ada, b_ada.reshape(1, n))
    return out[:bsz]


def _inproj_kernel(x_ref, sc_ref, sh_ref, w_ref, wf_ref, qkv_ref, u_ref, f_ref, h_scr, *,
                   n_q_tiles, n_qkv_tiles, q_scale):
    j = pl.program_id(2)

    @pl.when(j == 0)
    def _():
        hb = (x_ref[0] * (1.0 + sc_ref[0]) + sh_ref[0]).astype(BF16)
        h_scr[...] = hb
        f_ref[0] = _dot(hb, wf_ref[...])

    acc = _dot(h_scr[...], w_ref[...])

    @pl.when(j < n_q_tiles)
    def _():
        qkv_ref[0] = (acc * q_scale).astype(BF16)

    @pl.when(jnp.logical_and(j >= n_q_tiles, j < n_qkv_tiles))
    def _():
        qkv_ref[0] = acc.astype(BF16)

    @pl.when(j >= n_qkv_tiles)
    def _():
        u_ref[0] = acc


def in_proj(x, scale, shift, w_main, w_f, n_qkv):
    bsz, seq, d = x.shape
    n_all = w_main.shape[1]
    n_u = n_all - n_qkv
    tm, tn = 512, 1024
    nq = n_qkv // tn
    grid = (bsz, seq // tm, n_all // tn)
    return pl.pallas_call(
        functools.partial(_inproj_kernel, n_q_tiles=n_qkv // 3 // tn, n_qkv_tiles=nq,
                          q_scale=HEAD_DIM ** -0.5 * LOG2E),
        grid=grid,
        in_specs=[pl.BlockSpec((1, tm, d), lambda b, i, j: (b, i, 0)),
                  pl.BlockSpec((1, 1, d), lambda b, i, j: (b, 0, 0)),
                  pl.BlockSpec((1, 1, d), lambda b, i, j: (b, 0, 0)),
                  pl.BlockSpec((d, tn), lambda b, i, j: (0, j)),
                  pl.BlockSpec((d, LANES), lambda b, i, j: (0, 0))],
        out_specs=[pl.BlockSpec((1, tm, tn), lambda b, i, j: (b, i, jnp.minimum(j, nq - 1))),
                   pl.BlockSpec((1, tm, tn), lambda b, i, j: (b, i, jnp.maximum(j - nq, 0))),
                   pl.BlockSpec((1, tm, LANES), lambda b, i, j: (b, i, 0))],
        out_shape=[jax.ShapeDtypeStruct((bsz, seq, n_qkv), BF16),
                   jax.ShapeDtypeStruct((bsz, seq, n_u), F32),
                   jax.ShapeDtypeStruct((bsz, seq, LANES), F32)],
        scratch_shapes=[pltpu.VMEM((tm, d), BF16)],
        compiler_params=_params(("arbitrary", "arbitrary", "arbitrary"), 52),
        name="in_proj",
    )(x, scale, shift, w_main, w_f)


def _split3(x):
    p1 = x.astype(BF16)
    r1 = x - p1.astype(F32)
    p2 = r1.astype(BF16)
    p3 = (r1 - p2.astype(F32)).astype(BF16)
    return p1, p2, p3


N_PIECES = 3


def _cum_kernel(f_ref, b_ref, qa_ref, ka_ref, carry, *, n_heads):
    i = pl.program_id(1)

    @pl.when(i == 0)
    def _():
        carry[...] = jnp.zeros_like(carry)

    tc = f_ref.shape[1]
    lf = jax.nn.log_sigmoid(f_ref[0] + b_ref[...])
    row = lax.broadcasted_iota(jnp.int32, (tc, tc), 0)
    col = lax.broadcasted_iota(jnp.int32, (tc, tc), 1)
    tri = (col <= row).astype(BF16)
    p1, p2, p3 = _split3(lf)
    cs = _dot(tri, p1) + _dot(tri, p2) + _dot(tri, p3) + carry[...]
    carry[...] = cs[tc - 1:tc, :]

    pieces = jnp.concatenate(_split3(cs * LOG2E), axis=1)
    r = lax.broadcasted_iota(jnp.int32, (N_PIECES * LANES, LANES), 0)
    c = lax.broadcasted_iota(jnp.int32, (N_PIECES * LANES, LANES), 1)
    lane = lax.broadcasted_iota(jnp.int32, (tc, LANES), 1)
    ones_q = jnp.where(jnp.logical_and(lane >= N_PIECES, lane < 2 * N_PIECES), 1.0, 0.0)
    ones_k = jnp.where(lane < N_PIECES, 1.0, 0.0)
    for h in range(n_heads):
        sel_q = (r == c * LANES + h).astype(BF16)
        sel_k = (r == (c - N_PIECES) * LANES + h).astype(BF16)
        qa_ref[0, h] = (_dot(pieces, sel_q) + ones_q).astype(BF16)
        ka_ref[0, h] = (ones_k - _dot(pieces, sel_k)).astype(BF16)


def forget_cum(f, b_forget):
    bsz, seq, _ = f.shape
    n_heads = b_forget.shape[0]
    tc = 256
    bpad = jnp.zeros((1, LANES), F32).at[0, :n_heads].set(b_forget)
    out_spec = pl.BlockSpec((1, n_heads, tc, LANES), lambda b, i: (b, 0, i, 0))
    out_shape = jax.ShapeDtypeStruct((bsz, n_heads, seq, LANES), BF16)
    return pl.pallas_call(
        functools.partial(_cum_kernel, n_heads=n_heads),
        grid=(bsz, seq // tc),
        in_specs=[pl.BlockSpec((1, tc, LANES), lambda b, i: (b, i, 0)),
                  pl.BlockSpec((1, LANES), lambda b, i: (0, 0))],
        out_specs=[out_spec, out_spec],
        out_shape=[out_shape, out_shape],
        scratch_shapes=[pltpu.VMEM((1, LANES), F32)],
        compiler_params=_params(("arbitrary", "arbitrary"), 24),
        name="forget_cum",
    )(f, bpad)


ATT_TILE = 512
ATT_SPLIT = 2


def _attn_kernel(*refs, n_cast):
    q_ref, qa_ref, k_ref, ka_ref, v_ref = refs[:5]
    src_refs = refs[5:5 + n_cast]
    o_ref = refs[5 + n_cast]
    dst_refs = refs[6 + n_cast:6 + 2 * n_cast]
    m_scr, acc_scr, s_scr = refs[6 + 2 * n_cast:]
    for src, dst in zip(src_refs, dst_refs):
        dst[...] = src[...].astype(BF16)

    t = q_ref.shape[1]
    rows = t // ATT_SPLIT
    qi = pl.program_id(2)
    q = jnp.concatenate([q_ref[0], qa_ref[0, 0]], axis=1)
    ones = jnp.ones((t, HEAD_DIM), BF16)

    m_scr[...] = jnp.full_like(m_scr, NEG_BIG)
    acc_scr[...] = jnp.zeros_like(acc_scr)

    def scores(kb, slot):
        k0 = pl.multiple_of(kb * t, t)
        kt = jnp.concatenate([k_ref[0, pl.ds(k0, t), :], ka_ref[0, 0, pl.ds(k0, t), :]], axis=1)
        s_scr[slot] = lax.dot_general(q, kt, _NT, preferred_element_type=F32)

    def update(kb, slot, masked):
        k0 = pl.multiple_of(kb * t, t)
        vt = jnp.concatenate([v_ref[0, pl.ds(k0, t), :], ones], axis=1)
        for g in range(ATT_SPLIT):
            rs = slice(g * rows, (g + 1) * rows)
            s = s_scr[slot, rs, :]
            if masked:
                qpos = lax.broadcasted_iota(jnp.int32, (rows, t), 0) + g * rows
                kpos = lax.broadcasted_iota(jnp.int32, (rows, t), 1)
                s = jnp.where(kpos <= qpos, s, NEG_BIG)
            m_prev = m_scr[rs]
            m_new = jnp.maximum(m_prev, jnp.max(s, axis=1, keepdims=True))
            alpha = jnp.exp2(m_prev - m_new)
            p = jnp.exp2(s - m_new)
            acc_scr[rs] = alpha * acc_scr[rs] + _dot(p.astype(BF16), vt)
            m_scr[rs] = m_new

    def body(i, carry):
        kb = 2 * i
        scores(kb + 1, 1)
        update(kb, 0, False)
        scores(kb + 2, 0)
        update(kb + 1, 1, False)
        return carry

    scores(0, 0)
    lax.fori_loop(0, qi // 2, body, 0)

    @pl.when(qi % 2 == 0)
    def _():
        update(qi, 0, True)

    @pl.when(qi % 2 == 1)
    def _():
        scores(qi, 1)
        update(qi - 1, 0, False)
        update(qi, 1, True)

    o_ref[0] = acc_scr[:, :HEAD_DIM] / acc_scr[:, HEAD_DIM:]


BF16_ROWS = 16


def _cast_chunks(w, n_steps):
    cols = w.shape[-1]
    total_rows = w.size // cols
    n_chunks = n_steps
    while total_rows % (n_chunks * BF16_ROWS):
        n_chunks //= 2
    return w.reshape(n_chunks, total_rows // n_chunks, cols)


def attention(qkv, q_aug, k_aug, n_heads, cast_weights):
    bsz, seq, _ = qkv.shape
    t = ATT_TILE
    nq = seq // t
    n_steps = bsz * n_heads * nq
    srcs = [_cast_chunks(w, n_steps) for w in cast_weights]

    def chunk_spec(a):
        per = n_steps // a.shape[0]
        return pl.BlockSpec((1,) + a.shape[1:], lambda b, h, i: (((b * n_heads + h) * nq + i) // per, 0, 0))

    outs = pl.pallas_call(
        functools.partial(_attn_kernel, n_cast=len(srcs)),
        grid=(bsz, n_heads, nq),
        in_specs=[pl.BlockSpec((1, t, HEAD_DIM), lambda b, h, i: (b, i, h)),
                  pl.BlockSpec((1, 1, t, LANES), lambda b, h, i: (b, h, i, 0)),
                  pl.BlockSpec((1, seq, HEAD_DIM), lambda b, h, i: (b, 0, n_heads + h)),
                  pl.BlockSpec((1, 1, seq, LANES), lambda b, h, i: (b, h, 0, 0)),
                  pl.BlockSpec((1, seq, HEAD_DIM), lambda b, h, i: (b, 0, 2 * n_heads + h))]
                 + [chunk_spec(a) for a in srcs],
        out_specs=[pl.BlockSpec((1, t, HEAD_DIM), lambda b, h, i: (b, i, h))] + [chunk_spec(a) for a in srcs],
        out_shape=[jax.ShapeDtypeStruct((bsz, seq, n_heads * HEAD_DIM), F32)]
                  + [jax.ShapeDtypeStruct(a.shape, BF16) for a in srcs],
        scratch_shapes=[pltpu.VMEM((t, 1), F32), pltpu.VMEM((t, 2 * HEAD_DIM), F32),
                        pltpu.VMEM((2, t, t), F32)],
        compiler_params=_params(("arbitrary", "arbitrary", "arbitrary"), 48),
        name="attention",
    )(qkv, q_aug, qkv, k_aug, qkv, *srcs)
    return outs[0], [o.reshape(w.shape) for o, w in zip(outs[1:], cast_weights)]


def _blockdiag(p):
    g, c, n = p.shape
    ns = g // GROUPS_PER_SLAB
    eye = jnp.eye(GROUPS_PER_SLAB, dtype=p.dtype)
    out = p.reshape(ns, GROUPS_PER_SLAB, c, 1, n) * eye[None, :, None, :, None]
    return out.reshape(ns, GROUPS_PER_SLAB * c, GROUPS_PER_SLAB * n)


def _ssm_prep_kernel(lr_ref, li_ref, ldt_ref, bre_ref, bim_ref, cre_ref, cim_ref, w1_ref, ft_ref, al_ref):
    L = SSM_CHUNK
    lr = lr_ref[0]
    li = li_ref[0]
    dt = jnp.exp(ldt_ref[0])
    mag = jnp.exp(lr * dt)
    a_re = mag * jnp.cos(li * dt)
    a_im = mag * jnp.sin(li * dt)
    den = lr * lr + li * li
    z_re = ((a_re - 1.0) * lr + a_im * li) / den
    z_im = (a_im * lr - (a_re - 1.0) * li) / den
    br = bre_ref[0]
    bi = bim_ref[0]
    bb_re = z_re * br - z_im * bi
    bb_im = z_re * bi + z_im * br
    cr = cre_ref[0]
    ci = cim_ref[0]
    ft0 = jnp.concatenate([cr, -ci], axis=1)

    def power(d):
        m = jnp.exp(lr * dt * d)
        return m * jnp.cos(li * dt * d), m * jnp.sin(li * dt * d)

    w1_ref[0, :, :L * LANES] = jnp.zeros((L * LANES, L * LANES), BF16)
    for d in range(L):
        pr, pi = power(float(d))
        xe = jnp.concatenate([bb_re * pr - bb_im * pi, bb_re * pi + bb_im * pr], axis=1)
        j = L - 1 - d
        w1_ref[0, j * LANES:(j + 1) * LANES, L * LANES:] = xe.astype(BF16)
        m_d = lax.dot_general(xe, ft0, _NT, preferred_element_type=F32,
                              precision=lax.Precision.HIGHEST).astype(BF16)
        for jj in range(L - d):
            w1_ref[0, jj * LANES:(jj + 1) * LANES, (jj + d) * LANES:(jj + d + 1) * LANES] = m_d
        pr1, pi1 = power(float(d + 1))
        ft_ref[0, d * LANES:(d + 1) * LANES, :] = jnp.concatenate(
            [cr * pr1 - ci * pi1, -(cr * pi1 + ci * pr1)], axis=1).astype(BF16)
    prl, pil = power(float(L))
    al_ref[0] = jnp.concatenate([prl, pil], axis=1)


def ssm_prep(lam_re, lam_im, log_dt, b_re, b_im, c_re, c_im):
    g, n = lam_re.shape
    ns = g // GROUPS_PER_SLAB
    L = SSM_CHUNK
    rowvec = lambda a: a.reshape(ns, 1, SLAB_STATE)
    args = (rowvec(lam_re), rowvec(lam_im), rowvec(jnp.repeat(log_dt, n)),
            _blockdiag(b_re.transpose(0, 2, 1)), _blockdiag(b_im.transpose(0, 2, 1)),
            _blockdiag(c_re), _blockdiag(c_im))
    vec_spec = pl.BlockSpec((1, 1, SLAB_STATE), lambda s: (s, 0, 0))
    mat_spec = pl.BlockSpec((1, LANES, SLAB_STATE), lambda s: (s, 0, 0))
    return pl.pallas_call(
        _ssm_prep_kernel,
        grid=(ns,),
        in_specs=[vec_spec] * 3 + [mat_spec] * 4,
        out_specs=[pl.BlockSpec((1, L * LANES, L * LANES + 2 * SLAB_STATE), lambda s: (s, 0, 0)),
                   pl.BlockSpec((1, L * LANES, 2 * SLAB_STATE), lambda s: (s, 0, 0)),
                   pl.BlockSpec((1, 1, 2 * SLAB_STATE), lambda s: (s, 0, 0))],
        out_shape=[jax.ShapeDtypeStruct((ns, L * LANES, L * LANES + 2 * SLAB_STATE), BF16),
                   jax.ShapeDtypeStruct((ns, L * LANES, 2 * SLAB_STATE), BF16),
                   jax.ShapeDtypeStruct((ns, 1, 2 * SLAB_STATE), F32)],
        compiler_params=_params(("arbitrary",), 48),
        name="ssm_prep",
    )(*args)


def _ssm_kernel(u_ref, w1_ref, ft_ref, al_ref, d_ref, y_ref, uf_scr, e_scr, y_scr):
    L = SSM_CHUNK
    nch = uf_scr.shape[0]
    lc = L * LANES
    for j in range(L):
        uf_scr[:, j * LANES:(j + 1) * LANES] = u_ref[0, pl.ds(j, nch, stride=L), :].astype(BF16)
    uf = uf_scr[...]
    y_scr[...] = _dot(uf, w1_ref[0, :, :lc])
    e_scr[...] = _dot(uf, w1_ref[0, :, lc:])

    a_re = al_ref[0, :, :SLAB_STATE]
    a_im = al_ref[0, :, SLAB_STATE:]

    def body(k, h):
        h_re, h_im = h
        e = e_scr[pl.ds(k, 1), :]
        e_scr[pl.ds(k, 1), :] = jnp.concatenate([h_re, h_im], axis=1)
        return (a_re * h_re - a_im * h_im + e[:, :SLAB_STATE],
                a_re * h_im + a_im * h_re + e[:, SLAB_STATE:])

    zero = jnp.zeros((1, SLAB_STATE), F32)
    lax.fori_loop(0, nch, body, (zero, zero))
    y = y_scr[...] + lax.dot_general(e_scr[...].astype(BF16), ft_ref[0], _NT, preferred_element_type=F32)
    for i in range(L):
        yi = y[:, i * LANES:(i + 1) * LANES] + d_ref[0] * u_ref[0, pl.ds(i, nch, stride=L), :]
        y_ref[0, pl.ds(i, nch, stride=L), :] = jax.nn.gelu(yi)


def ssm_scan(u, w1, ft, al, d_skip):
    bsz, seq, c = u.shape
    ns = c // LANES
    L = SSM_CHUNK
    nch = seq // L
    lc = L * LANES
    once = pl.Buffered(1)
    return pl.pallas_call(
        _ssm_kernel,
        grid=(ns, bsz),
        in_specs=[pl.BlockSpec((1, seq, LANES), lambda s, b: (b, 0, s)),
                  pl.BlockSpec((1, lc, lc + 2 * SLAB_STATE), lambda s, b: (s, 0, 0), pipeline_mode=once),
                  pl.BlockSpec((1, lc, 2 * SLAB_STATE), lambda s, b: (s, 0, 0), pipeline_mode=once),
                  pl.BlockSpec((1, 1, 2 * SLAB_STATE), lambda s, b: (s, 0, 0)),
                  pl.BlockSpec((1, 1, LANES), lambda s, b: (s, 0, 0))],
        out_specs=pl.BlockSpec((1, seq, LANES), lambda s, b: (b, 0, s)),
        out_shape=jax.ShapeDtypeStruct((bsz, seq, c), F32),
        scratch_shapes=[pltpu.VMEM((nch, lc), BF16), pltpu.VMEM((nch, 2 * SLAB_STATE), F32),
                        pltpu.VMEM((nch, lc), F32)],
        compiler_params=_params(("arbitrary", "arbitrary"), 56),
        name="ssm_scan",
    )(u, w1, ft, al, d_skip.reshape(ns, 1, LANES))


def _glu_kernel(y_ref, w_ref, b_ref, g_ref, o_ref):
    y = y_ref[...]
    o = y * jax.nn.sigmoid(_dot(y.astype(BF16), w_ref[...]) + b_ref[...])
    ms = jnp.mean(o * o, axis=-1, keepdims=True)
    o_ref[...] = (o * lax.rsqrt(ms + RMS_EPS) * g_ref[...]).astype(BF16)


def glu_norm(y, w_glu, b_glu, g):
    t, c = y.shape
    tm = 512
    return pl.pallas_call(
        _glu_kernel,
        grid=(t // tm,),
        in_specs=[pl.BlockSpec((tm, c), lambda i: (i, 0)),
                  pl.BlockSpec((c, c), lambda i: (0, 0)),
                  pl.BlockSpec((1, c), lambda i: (0, 0)),
                  pl.BlockSpec((1, c), lambda i: (0, 0))],
        out_specs=pl.BlockSpec((tm, c), lambda i: (i, 0)),
        out_shape=jax.ShapeDtypeStruct((t, c), BF16),
        compiler_params=_params(("arbitrary",), 48),
        name="glu_norm",
    )(y, w_glu, b_glu.reshape(1, c), g.reshape(1, c))


def _outproj_kernel(attn_ref, ga_ref, ssm_ref, w_ref, x_ref, gate_ref, lng_ref, lnb_ref, sc2_ref, sh2_ref,
                    wrh_ref, wrl_ref, x1_ref, h2_ref, lg_ref, a_scr, *, alpha, n_att):
    j = pl.program_id(2)
    nj = pl.num_programs(2)
    tn = w_ref.shape[1]

    @pl.when(j == 0)
    def _():
        a = attn_ref[0]
        ms = jnp.mean(a * a, axis=-1, keepdims=True)
        a_scr[:, :n_att] = (a * lax.rsqrt(ms + RMS_EPS) * ga_ref[...]).astype(BF16)
        a_scr[:, n_att:] = ssm_ref[0]

    mixed = _dot(a_scr[...], w_ref[...])
    col = pl.multiple_of(j * tn, tn)
    x1_ref[0, :, pl.ds(col, tn)] = alpha * x_ref[0] + (1.0 + gate_ref[0]) * mixed

    @pl.when(j == nj - 1)
    def _():
        r = x1_ref[0]
        mu = jnp.mean(r, axis=-1, keepdims=True)
        var = jnp.mean(jnp.square(r - mu), axis=-1, keepdims=True)
        x1 = (r - mu) * lax.rsqrt(var + LN_EPS) * lng_ref[...] + lnb_ref[...]
        x1_ref[0] = x1
        h2 = x1 * (1.0 + sc2_ref[0]) + sh2_ref[0]
        h2_ref[0] = h2
        hi = h2.astype(BF16)
        lo = (h2 - hi.astype(F32)).astype(BF16)
        lg_ref[0] = _dot(hi, wrh_ref[...]) + _dot(hi, wrl_ref[...]) + _dot(lo, wrh_ref[...])


def out_proj(attn, g_attn, ssm_n, w_out, x, gate1, ln_g, ln_b, scale2, shift2, wr_hi, wr_lo, alpha):
    bsz, seq, d = x.shape
    n_att = attn.shape[-1]
    n_ssm = ssm_n.shape[-1]
    k = n_att + n_ssm
    tm, tn = 256, 512
    row = lambda a: a.reshape(1, -1)
    full = lambda n: pl.BlockSpec((1, n), lambda b, i, j: (0, 0))
    return pl.pallas_call(
        functools.partial(_outproj_kernel, alpha=alpha, n_att=n_att),
        grid=(bsz, seq // tm, d // tn),
        in_specs=[pl.BlockSpec((1, tm, n_att), lambda b, i, j: (b, i, 0)),
                  full(n_att),
                  pl.BlockSpec((1, tm, n_ssm), lambda b, i, j: (b, i, 0)),
                  pl.BlockSpec((k, tn), lambda b, i, j: (0, j)),
                  pl.BlockSpec((1, tm, tn), lambda b, i, j: (b, i, j)),
                  pl.BlockSpec((1, 1, tn), lambda b, i, j: (b, 0, j)),
                  full(d), full(d),
                  pl.BlockSpec((1, 1, d), lambda b, i, j: (b, 0, 0)),
                  pl.BlockSpec((1, 1, d), lambda b, i, j: (b, 0, 0)),
                  pl.BlockSpec((d, LANES), lambda b, i, j: (0, 0)),
                  pl.BlockSpec((d, LANES), lambda b, i, j: (0, 0))],
        out_specs=[pl.BlockSpec((1, tm, d), lambda b, i, j: (b, i, 0)),
                   pl.BlockSpec((1, tm, d), lambda b, i, j: (b, i, 0)),
                   pl.BlockSpec((1, tm, LANES), lambda b, i, j: (b, i, 0))],
        out_shape=[jax.ShapeDtypeStruct((bsz, seq, d), F32),
                   jax.ShapeDtypeStruct((bsz, seq, d), F32),
                   jax.ShapeDtypeStruct((bsz, seq, LANES), F32)],
        scratch_shapes=[pltpu.VMEM((tm, k), BF16)],
        compiler_params=_params(("arbitrary", "arbitrary", "arbitrary"), 52),
        name="out_proj",
    )(attn, row(g_attn), ssm_n, w_out, x, gate1, row(ln_g), row(ln_b), scale2, shift2, wr_hi, wr_lo)


def _moe_kernel(tok_ref, bexp_ref, nact_ref, h_hbm, wg_ref, wu_ref, wd_ref, y_ref, xbuf, sem):
    i = pl.program_id(0)
    nact = nact_ref[0]
    rows = xbuf.shape[1]

    def gather(blk, slot, start):
        if not start:
            pltpu.make_async_copy(h_hbm.at[pl.ds(0, rows)], xbuf.at[slot], sem.at[slot]).wait()
            return

        def body(r, carry):
            tok = tok_ref[blk * rows + r]
            pltpu.make_async_copy(h_hbm.at[pl.ds(tok, 1)], xbuf.at[slot, pl.ds(r, 1)], sem.at[slot]).start()
            return carry
        lax.fori_loop(0, rows, body, 0, unroll=GATHER_UNROLL)

    @pl.when(jnp.logical_and(i == 0, nact > 0))
    def _():
        gather(0, 0, True)

    @pl.when(i + 1 < nact)
    def _():
        gather(i + 1, (i + 1) % 2, True)

    @pl.when(i < nact)
    def _():
        slot = i % 2
        gather(i, slot, False)
        xb = xbuf[slot].astype(BF16)
        g = _dot(xb, wg_ref[0])
        u = _dot(xb, wu_ref[0])
        act = (jax.nn.silu(g) * u).astype(BF16)
        y_ref[...] = _dot(act, wd_ref[0])

    @pl.when(i >= nact)
    def _():
        y_ref[...] = jnp.zeros_like(y_ref)


def moe_experts(h2, tok_buf, block_expert, n_active, w_gate, w_up, w_down):
    t, d = h2.shape
    n_rows = tok_buf.shape[0]
    rows = MOE_ROWS
    n_blocks = n_rows // rows
    de = w_gate.shape[-1]

    def wmap(i, tok, bexp, nact):
        return (bexp[jnp.minimum(i, jnp.maximum(nact[0] - 1, 0))], 0, 0)

    grid_spec = pltpu.PrefetchScalarGridSpec(
        num_scalar_prefetch=3,
        grid=(n_blocks,),
        in_specs=[pl.BlockSpec(memory_space=pl.ANY),
                  pl.BlockSpec((1, d, de), wmap),
                  pl.BlockSpec((1, d, de), wmap),
                  pl.BlockSpec((1, de, d), wmap)],
        out_specs=pl.BlockSpec((rows, d), lambda i, *_: (i, 0)),
        scratch_shapes=[pltpu.VMEM((2, rows, d), F32), pltpu.SemaphoreType.DMA((2,))],
    )
    return pl.pallas_call(
        _moe_kernel,
        grid_spec=grid_spec,
        out_shape=jax.ShapeDtypeStruct((n_rows, d), F32),
        compiler_params=_params(("arbitrary",), 56),
        name="moe_experts",
    )(tok_buf, block_expert, n_active, h2, w_gate, w_up, w_down)


def _combine_kernel(dest_ref, ys_hbm, rt_ref, x1_ref, gate_ref, lng_ref, lnb_ref, o_ref, ybuf, sem, *, alpha):
    i = pl.program_id(0)
    n = pl.num_programs(0)
    tm = x1_ref.shape[0]
    n_copies = TOP_K * tm

    def gather(blk, slot, start):
        if not start:
            pltpu.make_async_copy(ys_hbm.at[pl.ds(0, n_copies)], ybuf.at[slot], sem.at[slot]).wait()
            return

        def body(r, carry):
            row = dest_ref[blk * n_copies + r]
            pltpu.make_async_copy(ys_hbm.at[pl.ds(row, 1)], ybuf.at[slot, pl.ds(r, 1)], sem.at[slot]).start()
            return carry
        lax.fori_loop(0, n_copies, body, 0, unroll=GATHER_UNROLL)

    @pl.when(i == 0)
    def _():
        gather(0, 0, True)

    @pl.when(i + 1 < n)
    def _():
        gather(i + 1, (i + 1) % 2, True)

    slot = i % 2
    gather(i, slot, False)
    moe = rt_ref[:, ROUTE_W:ROUTE_W + 1] * ybuf[slot, :tm, :]
    for kk in range(1, TOP_K):
        moe = moe + rt_ref[:, ROUTE_W + kk:ROUTE_W + kk + 1] * ybuf[slot, kk * tm:(kk + 1) * tm, :]
    r = alpha * x1_ref[...] + (1.0 + gate_ref[0]) * moe
    mu = jnp.mean(r, axis=-1, keepdims=True)
    var = jnp.mean(jnp.square(r - mu), axis=-1, keepdims=True)
    o_ref[...] = (r - mu) * lax.rsqrt(var + LN_EPS) * lng_ref[...] + lnb_ref[...]


def moe_combine(ys, dest, table, x1, gate2, ln_g, ln_b, alpha, seq):
    t, d = x1.shape
    tm = 128
    tiles_per_seq = seq // tm
    grid_spec = pltpu.PrefetchScalarGridSpec(
        num_scalar_prefetch=1,
        grid=(t // tm,),
        in_specs=[pl.BlockSpec(memory_space=pl.ANY),
                  pl.BlockSpec((tm, LANES), lambda i, *_: (i, 0)),
                  pl.BlockSpec((tm, d), lambda i, *_: (i, 0)),
                  pl.BlockSpec((1, 1, d), lambda i, *_: (i // tiles_per_seq, 0, 0)),
                  pl.BlockSpec((1, d), lambda i, *_: (0, 0)),
                  pl.BlockSpec((1, d), lambda i, *_: (0, 0))],
        out_specs=pl.BlockSpec((tm, d), lambda i, *_: (i, 0)),
        scratch_shapes=[pltpu.VMEM((2, TOP_K * tm, d), F32), pltpu.SemaphoreType.DMA((2,))],
    )
    return pl.pallas_call(
        functools.partial(_combine_kernel, alpha=alpha),
        grid_spec=grid_spec,
        out_shape=jax.ShapeDtypeStruct((t, d), F32),
        compiler_params=_params(("arbitrary",), 32),
        name="moe_combine",
    )(dest, ys, table, x1, gate2, ln_g.reshape(1, d), ln_b.reshape(1, d))


def _route_kernel(lg_ref, b_ref, o_ref, cnt_ref, carry):
    i = pl.program_id(0)

    @pl.when(i == 0)
    def _():
        carry[...] = jnp.zeros_like(carry)

    tm = lg_ref.shape[0]
    x = lg_ref[...] + b_ref[...]
    lane = lax.broadcasted_iota(jnp.int32, (tm, LANES), 1)
    ninf = -jnp.inf

    def top(v):
        vmax = jnp.max(v, axis=1, keepdims=True)
        return vmax, jnp.min(jnp.where(v == vmax, lane, LANES), axis=1, keepdims=True)

    gmask = lane < N_EXPERT_GROUPS
    gmax, g_sel = top(jnp.where(gmask, x, ninf))
    p_group = 1.0 / jnp.sum(jnp.where(gmask, jnp.exp(x - gmax), 0.0), axis=1, keepdims=True)

    lo = N_EXPERT_GROUPS + g_sel * EXPERTS_PER_GROUP
    cur = jnp.where(jnp.logical_and(lane >= lo, lane < lo + EXPERTS_PER_GROUP), x, ninf)
    vals, idxs = [], []
    for _ in range(TOP_K):
        v, ix = top(cur)
        vals.append(v)
        idxs.append(ix)
        cur = jnp.where(lane == ix, ninf, cur)
    exps = [jnp.exp(v - vals[0]) for v in vals]
    den = functools.reduce(lambda a, c: a + c, exps)

    member = functools.reduce(jnp.logical_or, [lane == ix for ix in idxs])
    mf = jnp.where(member, 1.0, 0.0)
    r_i = lax.broadcasted_iota(jnp.int32, (tm, tm), 0)
    c_i = lax.broadcasted_iota(jnp.int32, (tm, tm), 1)
    before = _dot((c_i < r_i).astype(BF16), mf.astype(BF16)) + carry[...]
    carry[...] = carry[...] + jnp.sum(mf, axis=0, keepdims=True)
    cnt_ref[...] = carry[...]

    out = jnp.zeros((tm, LANES), F32)
    for k in range(TOP_K):
        rank = jnp.sum(jnp.where(lane == idxs[k], before, 0.0), axis=1, keepdims=True)
        out = jnp.where(lane == ROUTE_EID + k, (idxs[k] - N_EXPERT_GROUPS).astype(F32), out)
        out = jnp.where(lane == ROUTE_RANK + k, rank, out)
        out = jnp.where(lane == ROUTE_W + k, p_group * (exps[k] / den), out)
    o_ref[...] = out


def route(logits, b_rg, b_re, n_rows, tm_combine):
    n_tok = logits.shape[0]
    n_experts = N_EXPERT_GROUPS * EXPERTS_PER_GROUP
    bias = jnp.zeros((1, LANES), F32).at[0, :N_EXPERT_GROUPS].set(b_rg)
    bias = bias.at[0, N_EXPERT_GROUPS:N_EXPERT_GROUPS + n_experts].set(b_re)
    tm = 512
    table, cnt = pl.pallas_call(
        _route_kernel,
        grid=(n_tok // tm,),
        in_specs=[pl.BlockSpec((tm, LANES), lambda i: (i, 0)),
                  pl.BlockSpec((1, LANES), lambda i: (0, 0))],
        out_specs=[pl.BlockSpec((tm, LANES), lambda i: (i, 0)),
                   pl.BlockSpec((1, LANES), lambda i: (0, 0))],
        out_shape=[jax.ShapeDtypeStruct((n_tok, LANES), F32), jax.ShapeDtypeStruct((1, LANES), F32)],
        scratch_shapes=[pltpu.VMEM((1, LANES), F32)],
        compiler_params=_params(("arbitrary",), 16),
        name="route",
    )(logits, bias)

    eid = table[:, ROUTE_EID:ROUTE_EID + TOP_K].astype(jnp.int32)
    rank = table[:, ROUTE_RANK:ROUTE_RANK + TOP_K].astype(jnp.int32)
    counts = cnt[0, N_EXPERT_GROUPS:N_EXPERT_GROUPS + n_experts].astype(jnp.int32)
    padded = ((counts + MOE_ROWS - 1) // MOE_ROWS) * MOE_ROWS
    pends = jnp.cumsum(padded)
    pstarts = pends - padded
    onehot = eid[:, :, None] == jnp.arange(n_experts, dtype=jnp.int32)
    dest = jnp.sum(jnp.where(onehot, pstarts, 0), axis=-1) + rank
    tok = jnp.broadcast_to(jnp.arange(n_tok, dtype=jnp.int32)[:, None], (n_tok, TOP_K))
    tok_buf = jnp.zeros((n_rows,), jnp.int32).at[dest.reshape(-1)].set(tok.reshape(-1), unique_indices=True)
    n_blocks = n_rows // MOE_ROWS
    block_expert = jnp.clip(
        jnp.searchsorted(pends, jnp.arange(n_blocks, dtype=jnp.int32) * MOE_ROWS, side='right'),
        0, n_experts - 1).astype(jnp.int32)
    n_active = (pends[-1] // MOE_ROWS).astype(jnp.int32).reshape(1)
    dest_tiles = dest.reshape(n_tok // tm_combine, tm_combine, TOP_K).transpose(0, 2, 1).reshape(-1)
    return table, tok_buf, block_expert, n_active, dest_tiles.astype(jnp.int32)


def _layer(x, c, w_ada, b_ada, w_in, b_forget, lam_re, lam_im, log_dt, b_re, b_im, c_re, c_im, d_skip,
           w_glu, b_glu, g_attn, g_ssm, w_out, ln1_g, ln1_b, w_rg, b_rg, w_re, b_re_r,
           w_gate, w_up, w_down, ln2_g, ln2_b, alpha):
    bsz, seq, d = x.shape
    n_tok = bsz * seq
    n_heads = b_forget.shape[0]
    d_att = n_heads * HEAD_DIM
    d_ssm = d_skip.shape[0] * d_skip.shape[1]

    mod = ada_mod(c, w_ada, b_ada).reshape(bsz, 1, -1)
    shift1, scale1, gate1, shift2, scale2, gate2 = jnp.split(mod, 6, axis=-1)

    n_qkv = 3 * d_att
    w_main = jnp.concatenate([w_in[:, :n_qkv], w_in[:, n_qkv + n_heads:]], axis=1).astype(BF16)
    w_f = jnp.zeros((d, LANES), BF16).at[:, :n_heads].set(w_in[:, n_qkv:n_qkv + n_heads].astype(BF16))
    qkv, u, f = in_proj(x, scale1, shift1, w_main, w_f, n_qkv)

    q_aug, k_aug = forget_cum(f, b_forget)
    attn, (w_glu_b, w_out_b, w_gate_b, w_up_b, w_down_b) = attention(
        qkv, q_aug, k_aug, n_heads, [w_glu, w_out, w_gate, w_up, w_down])

    w1, ft, al = ssm_prep(lam_re, lam_im, log_dt, b_re, b_im, c_re, c_im)
    y = ssm_scan(u, w1, ft, al, d_skip)
    ssm_n = glu_norm(y.reshape(n_tok, d_ssm), w_glu_b, b_glu, g_ssm).reshape(bsz, seq, d_ssm)

    n_experts = N_EXPERT_GROUPS * EXPERTS_PER_GROUP
    w_r = jnp.zeros((d, LANES), F32).at[:, :N_EXPERT_GROUPS].set(w_rg)
    w_r = w_r.at[:, N_EXPERT_GROUPS:N_EXPERT_GROUPS + n_experts].set(w_re)
    wr_hi = w_r.astype(BF16)
    wr_lo = (w_r - wr_hi.astype(F32)).astype(BF16)
    x1, h2, logits = out_proj(attn, g_attn, ssm_n, w_out_b, x, gate1, ln1_g, ln1_b,
                              scale2, shift2, wr_hi, wr_lo, alpha)

    n_assign = n_tok * TOP_K
    n_blocks = -(-(n_assign + n_experts * (MOE_ROWS - 1)) // MOE_ROWS)
    n_rows = n_blocks * MOE_ROWS
    table, tok_buf, block_expert, n_active, dest = route(logits.reshape(n_tok, LANES), b_rg, b_re_r, n_rows, 128)
    ys = moe_experts(h2.reshape(n_tok, d), tok_buf, block_expert, n_active,
                     w_gate_b, w_up_b, w_down_b)
    out = moe_combine(ys, dest, table, x1.reshape(n_tok, d), gate2, ln2_g, ln2_b, alpha, seq)
    return out.reshape(bsz, seq, d)


def kernel(x, c, w_ada, b_ada, w_in, b_forget, ssm_lambda_re, ssm_lambda_im, ssm_log_dt, ssm_b_re, ssm_b_im,
           ssm_c_re, ssm_c_im, ssm_d, w_glu, b_glu, g_attn, g_ssm, w_out, ln1_g, ln1_b, w_router_group,
           b_router_group, w_router_expert, b_router_expert, w_gate, w_up, w_down, ln2_g, ln2_b):
    depth = w_ada.shape[0]
    alpha = (2.0 * depth) ** 0.25
    for l in range(depth):
        x = _layer(x, c, w_ada[l], b_ada[l], w_in[l], b_forget[l], ssm_lambda_re[l], ssm_lambda_im[l],
                   ssm_log_dt[l], ssm_b_re[l], ssm_b_im[l], ssm_c_re[l], ssm_c_im[l], ssm_d[l],
                   w_glu[l], b_glu[l], g_attn[l], g_ssm[l], w_out[l], ln1_g[l], ln1_b[l],
                   w_router_group[l], b_router_group[l], w_router_expert[l], b_router_expert[l],
                   w_gate[l], w_up[l], w_down[l], ln2_g[l], ln2_b[l], alpha)
    return x
```

```python
import functools
import math

import jax
import jax.numpy as jnp
from jax import lax
from jax.experimental import pallas as pl
from jax.experimental.pallas import tpu as pltpu

F32 = jnp.float32
BF16 = jnp.bfloat16

LANES = 128
HEAD_DIM = 128
SSM_GROUP = 16
SSM_STATE = 64
GROUPS_PER_SLAB = LANES // SSM_GROUP
SLAB_STATE = GROUPS_PER_SLAB * SSM_STATE
SSM_CHUNK = 16
N_EXPERT_GROUPS = 8
EXPERTS_PER_GROUP = 8
TOP_K = 2
MOE_ROWS = 256
GATHER_UNROLL = 8
ROW_CHUNK = 128
ROUTE_EID, ROUTE_RANK, ROUTE_W = 0, TOP_K, 2 * TOP_K
LN_EPS = 1e-5
RMS_EPS = 1e-6
NEG_BIG = -1e30
LOG2E = math.log2(math.e)
MIB = 1024 * 1024

_NT = (((1,), (1,)), ((), ()))


def _params(semantics, vmem_mib):
    return pltpu.CompilerParams(dimension_semantics=semantics, vmem_limit_bytes=vmem_mib * MIB)


def _dot(a, b):
    return jnp.dot(a, b, preferred_element_type=F32)


def _ada_kernel(c_ref, w_ref, b_ref, o_ref):
    s = jax.nn.silu(c_ref[...]).astype(BF16)
    o_ref[...] = _dot(s, w_ref[...].astype(BF16)) + b_ref[...]


def ada_mod(c, w_ada, b_ada):
    bsz, d = c.shape
    n = w_ada.shape[1]
    rows = 8
    assert bsz <= rows
    cp = jnp.zeros((rows, d), F32).at[:bsz].set(c)
    tn = 512
    out = pl.pallas_call(
        _ada_kernel,
        grid=(n // tn,),
        in_specs=[pl.BlockSpec((rows, d), lambda j: (0, 0)),
                  pl.BlockSpec((d, tn), lambda j: (0, j)),
                  pl.BlockSpec((1, tn), lambda j: (0, j))],
        out_specs=pl.BlockSpec((rows, tn), lambda j: (0, j)),
        out_shape=jax.ShapeDtypeStruct((rows, n), F32),
        compiler_params=_params(("arbitrary",), 40),
        name="ada_mod",
    )(cp, w_ada, b_ada.reshape(1, n))
    return out[:bsz]


def _inproj_kernel(x_ref, sc_ref, sh_ref, w_ref, wf_ref, qkv_ref, u_ref, f_ref, h_scr, *,
                   n_q_tiles, n_qkv_tiles, q_scale):
    j = pl.program_id(2)

    @pl.when(j == 0)
    def _():
        hb = (x_ref[0] * (1.0 + sc_ref[0]) + sh_ref[0]).astype(BF16)
        h_scr[...] = hb
        f_ref[0] = _dot(hb, wf_ref[...])

    acc = _dot(h_scr[...], w_ref[...])

    @pl.when(j < n_q_tiles)
    def _():
        qkv_ref[0] = (acc * q_scale).astype(BF16)

    @pl.when(jnp.logical_and(j >= n_q_tiles, j < n_qkv_tiles))
    def _():
        qkv_ref[0] = acc.astype(BF16)

    @pl.when(j >= n_qkv_tiles)
    def _():
        u_ref[0] = acc


def in_proj(x, scale, shift, w_main, w_f, n_qkv):
    bsz, seq, d = x.shape
    n_all = w_main.shape[1]
    n_u = n_all - n_qkv
    tm, tn = 512, 1024
    nq = n_qkv // tn
    grid = (bsz, seq // tm, n_all // tn)
    return pl.pallas_call(
        functools.partial(_inproj_kernel, n_q_tiles=n_qkv // 3 // tn, n_qkv_tiles=nq,
                          q_scale=HEAD_DIM ** -0.5 * LOG2E),
        grid=grid,
        in_specs=[pl.BlockSpec((1, tm, d), lambda b, i, j: (b, i, 0)),
                  pl.BlockSpec((1, 1, d), lambda b, i, j: (b, 0, 0)),
                  pl.BlockSpec((1, 1, d), lambda b, i, j: (b, 0, 0)),
                  pl.BlockSpec((d, tn), lambda b, i, j: (0, j)),
                  pl.BlockSpec((d, LANES), lambda b, i, j: (0, 0))],
        out_specs=[pl.BlockSpec((1, tm, tn), lambda b, i, j: (b, i, jnp.minimum(j, nq - 1))),
                   pl.BlockSpec((1, tm, tn), lambda b, i, j: (b, i, jnp.maximum(j - nq, 0))),
                   pl.BlockSpec((1, tm, LANES), lambda b, i, j: (b, i, 0))],
        out_shape=[jax.ShapeDtypeStruct((bsz, seq, n_qkv), BF16),
                   jax.ShapeDtypeStruct((bsz, seq, n_u), F32),
                   jax.ShapeDtypeStruct((bsz, seq, LANES), F32)],
        scratch_shapes=[pltpu.VMEM((tm, d), BF16)],
        compiler_params=_params(("arbitrary", "arbitrary", "arbitrary"), 52),
        name="in_proj",
    )(x, scale, shift, w_main, w_f)


def _split3(x):
    p1 = x.astype(BF16)
    r1 = x - p1.astype(F32)
    p2 = r1.astype(BF16)
    p3 = (r1 - p2.astype(F32)).astype(BF16)
    return p1, p2, p3


N_PIECES = 3


def _cum_kernel(f_ref, b_ref, qa_ref, ka_ref, carry, *, n_heads):
    i = pl.program_id(1)

    @pl.when(i == 0)
    def _():
        carry[...] = jnp.zeros_like(carry)

    tc = f_ref.shape[1]
    lf = jax.nn.log_sigmoid(f_ref[0] + b_ref[...])
    row = lax.broadcasted_iota(jnp.int32, (tc, tc), 0)
    col = lax.broadcasted_iota(jnp.int32, (tc, tc), 1)
    tri = (col <= row).astype(BF16)
    p1, p2, p3 = _split3(lf)
    cs = _dot(tri, p1) + _dot(tri, p2) + _dot(tri, p3) + carry[...]
    carry[...] = cs[tc - 1:tc, :]

    pieces = jnp.concatenate(_split3(cs * LOG2E), axis=1)
    r = lax.broadcasted_iota(jnp.int32, (N_PIECES * LANES, LANES), 0)
    c = lax.broadcasted_iota(jnp.int32, (N_PIECES * LANES, LANES), 1)
    lane = lax.broadcasted_iota(jnp.int32, (tc, LANES), 1)
    ones_q = jnp.where(jnp.logical_and(lane >= N_PIECES, lane < 2 * N_PIECES), 1.0, 0.0)
    ones_k = jnp.where(lane < N_PIECES, 1.0, 0.0)
    for h in range(n_heads):
        sel_q = (r == c * LANES + h).astype(BF16)
        sel_k = (r == (c - N_PIECES) * LANES + h).astype(BF16)
        qa_ref[0, h] = (_dot(pieces, sel_q) + ones_q).astype(BF16)
        ka_ref[0, h] = (ones_k - _dot(pieces, sel_k)).astype(BF16)


def forget_cum(f, b_forget):
    bsz, seq, _ = f.shape
    n_heads = b_forget.shape[0]
    tc = 256
    bpad = jnp.zeros((1, LANES), F32).at[0, :n_heads].set(b_forget)
    out_spec = pl.BlockSpec((1, n_heads, tc, LANES), lambda b, i: (b, 0, i, 0))
    out_shape = jax.ShapeDtypeStruct((bsz, n_heads, seq, LANES), BF16)
    return pl.pallas_call(
        functools.partial(_cum_kernel, n_heads=n_heads),
        grid=(bsz, seq // tc),
        in_specs=[pl.BlockSpec((1, tc, LANES), lambda b, i: (b, i, 0)),
                  pl.BlockSpec((1, LANES), lambda b, i: (0, 0))],
        out_specs=[out_spec, out_spec],
        out_shape=[out_shape, out_shape],
        scratch_shapes=[pltpu.VMEM((1, LANES), F32)],
        compiler_params=_params(("arbitrary", "arbitrary"), 24),
        name="forget_cum",
    )(f, bpad)


ATT_TILE = 512
ATT_SPLIT = 2


def _attn_kernel(*refs, n_cast):
    q_ref, qa_ref, k_ref, ka_ref, v_ref = refs[:5]
    src_refs = refs[5:5 + n_cast]
    o_ref = refs[5 + n_cast]
    dst_refs = refs[6 + n_cast:6 + 2 * n_cast]
    m_scr, acc_scr, s_scr = refs[6 + 2 * n_cast:]
    for src, dst in zip(src_refs, dst_refs):
        dst[...] = src[...].astype(BF16)

    t = q_ref.shape[1]
    rows = t // ATT_SPLIT
    qi = pl.program_id(2)
    q = jnp.concatenate([q_ref[0], qa_ref[0, 0]], axis=1)
    ones = jnp.ones((t, HEAD_DIM), BF16)

    m_scr[...] = jnp.full_like(m_scr, NEG_BIG)
    acc_scr[...] = jnp.zeros_like(acc_scr)

    def scores(kb, slot):
        k0 = pl.multiple_of(kb * t, t)
        kt = jnp.concatenate([k_ref[0, pl.ds(k0, t), :], ka_ref[0, 0, pl.ds(k0, t), :]], axis=1)
        s_scr[slot] = lax.dot_general(q, kt, _NT, preferred_element_type=F32)

    def update(kb, slot, masked):
        k0 = pl.multiple_of(kb * t, t)
        vt = jnp.concatenate([v_ref[0, pl.ds(k0, t), :], ones], axis=1)
        for g in range(ATT_SPLIT):
            rs = slice(g * rows, (g + 1) * rows)
            s = s_scr[slot, rs, :]
            if masked:
                qpos = lax.broadcasted_iota(jnp.int32, (rows, t), 0) + g * rows
                kpos = lax.broadcasted_iota(jnp.int32, (rows, t), 1)
                s = jnp.where(kpos <= qpos, s, NEG_BIG)
            m_prev = m_scr[rs]
            m_new = jnp.maximum(m_prev, jnp.max(s, axis=1, keepdims=True))
            alpha = jnp.exp2(m_prev - m_new)
            p = jnp.exp2(s - m_new)
            acc_scr[rs] = alpha * acc_scr[rs] + _dot(p.astype(BF16), vt)
            m_scr[rs] = m_new

    def body(i, carry):
        kb = 2 * i
        scores(kb + 1, 1)
        update(kb, 0, False)
        scores(kb + 2, 0)
        update(kb + 1, 1, False)
        return carry

    scores(0, 0)
    lax.fori_loop(0, qi // 2, body, 0)

    @pl.when(qi % 2 == 0)
    def _():
        update(qi, 0, True)

    @pl.when(qi % 2 == 1)
    def _():
        scores(qi, 1)
        update(qi - 1, 0, False)
        update(qi, 1, True)

    o_ref[0] = acc_scr[:, :HEAD_DIM] / acc_scr[:, HEAD_DIM:]


BF16_ROWS = 16


def _cast_chunks(w, n_steps):
    cols = w.shape[-1]
    total_rows = w.size // cols
    n_chunks = n_steps
    while total_rows % (n_chunks * BF16_ROWS):
        n_chunks //= 2
    return w.reshape(n_chunks, total_rows // n_chunks, cols)


def attention(qkv, q_aug, k_aug, n_heads, cast_weights):
    bsz, seq, _ = qkv.shape
    t = ATT_TILE
    nq = seq // t
    n_steps = bsz * n_heads * nq
    srcs = [_cast_chunks(w, n_steps) for w in cast_weights]

    def chunk_spec(a):
        per = n_steps // a.shape[0]
        return pl.BlockSpec((1,) + a.shape[1:], lambda b, h, i: (((b * n_heads + h) * nq + i) // per, 0, 0))

    outs = pl.pallas_call(
        functools.partial(_attn_kernel, n_cast=len(srcs)),
        grid=(bsz, n_heads, nq),
        in_specs=[pl.BlockSpec((1, t, HEAD_DIM), lambda b, h, i: (b, i, h)),
                  pl.BlockSpec((1, 1, t, LANES), lambda b, h, i: (b, h, i, 0)),
                  pl.BlockSpec((1, seq, HEAD_DIM), lambda b, h, i: (b, 0, n_heads + h)),
                  pl.BlockSpec((1, 1, seq, LANES), lambda b, h, i: (b, h, 0, 0)),
                  pl.BlockSpec((1, seq, HEAD_DIM), lambda b, h, i: (b, 0, 2 * n_heads + h))]
                 + [chunk_spec(a) for a in srcs],
        out_specs=[pl.BlockSpec((1, t, HEAD_DIM), lambda b, h, i: (b, i, h))] + [chunk_spec(a) for a in srcs],
        out_shape=[jax.ShapeDtypeStruct((bsz, seq, n_heads * HEAD_DIM), F32)]
                  + [jax.ShapeDtypeStruct(a.shape, BF16) for a in srcs],
        scratch_shapes=[pltpu.VMEM((t, 1), F32), pltpu.VMEM((t, 2 * HEAD_DIM), F32),
                        pltpu.VMEM((2, t, t), F32)],
        compiler_params=_params(("arbitrary", "arbitrary", "arbitrary"), 48),
        name="attention",
    )(qkv, q_aug, qkv, k_aug, qkv, *srcs)
    return outs[0], [o.reshape(w.shape) for o, w in zip(outs[1:], cast_weights)]


def _blockdiag(p):
    g, c, n = p.shape
    ns = g // GROUPS_PER_SLAB
    eye = jnp.eye(GROUPS_PER_SLAB, dtype=p.dtype)
    out = p.reshape(ns, GROUPS_PER_SLAB, c, 1, n) * eye[None, :, None, :, None]
    return out.reshape(ns, GROUPS_PER_SLAB * c, GROUPS_PER_SLAB * n)


def _ssm_prep_kernel(lr_ref, li_ref, ldt_ref, bre_ref, bim_ref, cre_ref, cim_ref, w1_ref, ft_ref, al_ref):
    L = SSM_CHUNK
    lr = lr_ref[0]
    li = li_ref[0]
    dt = jnp.exp(ldt_ref[0])
    mag = jnp.exp(lr * dt)
    a_re = mag * jnp.cos(li * dt)
    a_im = mag * jnp.sin(li * dt)
    den = lr * lr + li * li
    z_re = ((a_re - 1.0) * lr + a_im * li) / den
    z_im = (a_im * lr - (a_re - 1.0) * li) / den
    br = bre_ref[0]
    bi = bim_ref[0]
    bb_re = z_re * br - z_im * bi
    bb_im = z_re * bi + z_im * br
    cr = cre_ref[0]
    ci = cim_ref[0]
    ft0 = jnp.concatenate([cr, -ci], axis=1)

    def power(d):
        m = jnp.exp(lr * dt * d)
        return m * jnp.cos(li * dt * d), m * jnp.sin(li * dt * d)

    w1_ref[0, :, :L * LANES] = jnp.zeros((L * LANES, L * LANES), BF16)
    for d in range(L):
        pr, pi = power(float(d))
        xe = jnp.concatenate([bb_re * pr - bb_im * pi, bb_re * pi + bb_im * pr], axis=1)
        j = L - 1 - d
        w1_ref[0, j * LANES:(j + 1) * LANES, L * LANES:] = xe.astype(BF16)
        m_d = lax.dot_general(xe, ft0, _NT, preferred_element_type=F32,
                              precision=lax.Precision.HIGHEST).astype(BF16)
        for jj in range(L - d):
            w1_ref[0, jj * LANES:(jj + 1) * LANES, (jj + d) * LANES:(jj + d + 1) * LANES] = m_d
        pr1, pi1 = power(float(d + 1))
        ft_ref[0, d * LANES:(d + 1) * LANES, :] = jnp.concatenate(
            [cr * pr1 - ci * pi1, -(cr * pi1 + ci * pr1)], axis=1).astype(BF16)
    prl, pil = power(float(L))
    al_ref[0] = jnp.concatenate([prl, pil], axis=1)


def ssm_prep(lam_re, lam_im, log_dt, b_re, b_im, c_re, c_im):
    g, n = lam_re.shape
    ns = g // GROUPS_PER_SLAB
    L = SSM_CHUNK
    rowvec = lambda a: a.reshape(ns, 1, SLAB_STATE)
    args = (rowvec(lam_re), rowvec(lam_im), rowvec(jnp.repeat(log_dt, n)),
            _blockdiag(b_re.transpose(0, 2, 1)), _blockdiag(b_im.transpose(0, 2, 1)),
            _blockdiag(c_re), _blockdiag(c_im))
    vec_spec = pl.BlockSpec((1, 1, SLAB_STATE), lambda s: (s, 0, 0))
    mat_spec = pl.BlockSpec((1, LANES, SLAB_STATE), lambda s: (s, 0, 0))
    return pl.pallas_call(
        _ssm_prep_kernel,
        grid=(ns,),
        in_specs=[vec_spec] * 3 + [mat_spec] * 4,
        out_specs=[pl.BlockSpec((1, L * LANES, L * LANES + 2 * SLAB_STATE), lambda s: (s, 0, 0)),
                   pl.BlockSpec((1, L * LANES, 2 * SLAB_STATE), lambda s: (s, 0, 0)),
                   pl.BlockSpec((1, 1, 2 * SLAB_STATE), lambda s: (s, 0, 0))],
        out_shape=[jax.ShapeDtypeStruct((ns, L * LANES, L * LANES + 2 * SLAB_STATE), BF16),
                   jax.ShapeDtypeStruct((ns, L * LANES, 2 * SLAB_STATE), BF16),
                   jax.ShapeDtypeStruct((ns, 1, 2 * SLAB_STATE), F32)],
        compiler_params=_params(("arbitrary",), 48),
        name="ssm_prep",
    )(*args)


def _ssm_kernel(u_ref, w1_ref, ft_ref, al_ref, d_ref, y_ref, uf_scr, e_scr, y_scr):
    L = SSM_CHUNK
    nch = uf_scr.shape[0]
    lc = L * LANES
    for j in range(L):
        uf_scr[:, j * LANES:(j + 1) * LANES] = u_ref[0, pl.ds(j, nch, stride=L), :].astype(BF16)
    uf = uf_scr[...]
    y_scr[...] = _dot(uf, w1_ref[0, :, :lc])
    e_scr[...] = _dot(uf, w1_ref[0, :, lc:])

    a_re = al_ref[0, :, :SLAB_STATE]
    a_im = al_ref[0, :, SLAB_STATE:]

    def body(k, h):
        h_re, h_im = h
        e = e_scr[pl.ds(k, 1), :]
        e_scr[pl.ds(k, 1), :] = jnp.concatenate([h_re, h_im], axis=1)
        return (a_re * h_re - a_im * h_im + e[:, :SLAB_STATE],
                a_re * h_im + a_im * h_re + e[:, SLAB_STATE:])

    zero = jnp.zeros((1, SLAB_STATE), F32)
    lax.fori_loop(0, nch, body, (zero, zero))
    y = y_scr[...] + lax.dot_general(e_scr[...].astype(BF16), ft_ref[0], _NT, preferred_element_type=F32)
    for i in range(L):
        yi = y[:, i * LANES:(i + 1) * LANES] + d_ref[0] * u_ref[0, pl.ds(i, nch, stride=L), :]
        y_ref[0, pl.ds(i, nch, stride=L), :] = jax.nn.gelu(yi)


def ssm_scan(u, w1, ft, al, d_skip):
    bsz, seq, c = u.shape
    ns = c // LANES
    L = SSM_CHUNK
    nch = seq // L
    lc = L * LANES
    once = pl.Buffered(1)
    return pl.pallas_call(
        _ssm_kernel,
        grid=(ns, bsz),
        in_specs=[pl.BlockSpec((1, seq, LANES), lambda s, b: (b, 0, s)),
                  pl.BlockSpec((1, lc, lc + 2 * SLAB_STATE), lambda s, b: (s, 0, 0), pipeline_mode=once),
                  pl.BlockSpec((1, lc, 2 * SLAB_STATE), lambda s, b: (s, 0, 0), pipeline_mode=once),
                  pl.BlockSpec((1, 1, 2 * SLAB_STATE), lambda s, b: (s, 0, 0)),
                  pl.BlockSpec((1, 1, LANES), lambda s, b: (s, 0, 0))],
        out_specs=pl.BlockSpec((1, seq, LANES), lambda s, b: (b, 0, s)),
        out_shape=jax.ShapeDtypeStruct((bsz, seq, c), F32),
        scratch_shapes=[pltpu.VMEM((nch, lc), BF16), pltpu.VMEM((nch, 2 * SLAB_STATE), F32),
                        pltpu.VMEM((nch, lc), F32)],
        compiler_params=_params(("arbitrary", "arbitrary"), 56),
        name="ssm_scan",
    )(u, w1, ft, al, d_skip.reshape(ns, 1, LANES))


def _glu_kernel(y_ref, w_ref, b_ref, g_ref, o_ref):
    y = y_ref[...]
    o = y * jax.nn.sigmoid(_dot(y.astype(BF16), w_ref[...]) + b_ref[...])
    ms = jnp.mean(o * o, axis=-1, keepdims=True)
    o_ref[...] = (o * lax.rsqrt(ms + RMS_EPS) * g_ref[...]).astype(BF16)


def glu_norm(y, w_glu, b_glu, g):
    t, c = y.shape
    tm = 512
    return pl.pallas_call(
        _glu_kernel,
        grid=(t // tm,),
        in_specs=[pl.BlockSpec((tm, c), lambda i: (i, 0)),
                  pl.BlockSpec((c, c), lambda i: (0, 0)),
                  pl.BlockSpec((1, c), lambda i: (0, 0)),
                  pl.BlockSpec((1, c), lambda i: (0, 0))],
        out_specs=pl.BlockSpec((tm, c), lambda i: (i, 0)),
        out_shape=jax.ShapeDtypeStruct((t, c), BF16),
        compiler_params=_params(("arbitrary",), 48),
        name="glu_norm",
    )(y, w_glu, b_glu.reshape(1, c), g.reshape(1, c))


def _outproj_kernel(attn_ref, ga_ref, ssm_ref, w_ref, x_ref, gate_ref, lng_ref, lnb_ref, sc2_ref, sh2_ref,
                    wrh_ref, wrl_ref, x1_ref, h2_ref, lg_ref, a_scr, *, alpha, n_att):
    j = pl.program_id(2)
    nj = pl.num_programs(2)
    tn = w_ref.shape[1]

    tm = x1_ref.shape[1]
    n_chunks = tm // ROW_CHUNK

    @pl.when(j == 0)
    def _():
        def norm(c, carry):
            rs = pl.ds(pl.multiple_of(c * ROW_CHUNK, ROW_CHUNK), ROW_CHUNK)
            a = attn_ref[0, rs, :]
            ms = jnp.mean(a * a, axis=-1, keepdims=True)
            a_scr[rs, :n_att] = (a * lax.rsqrt(ms + RMS_EPS) * ga_ref[...]).astype(BF16)
            return carry
        lax.fori_loop(0, n_chunks, norm, 0)
        a_scr[:, n_att:] = ssm_ref[0]

    mixed = _dot(a_scr[...], w_ref[...])
    col = pl.multiple_of(j * tn, tn)
    x1_ref[0, :, pl.ds(col, tn)] = alpha * x_ref[0] + (1.0 + gate_ref[0]) * mixed

    @pl.when(j == nj - 1)
    def _():
        def finish(c, carry):
            rs = pl.ds(pl.multiple_of(c * ROW_CHUNK, ROW_CHUNK), ROW_CHUNK)
            r = x1_ref[0, rs, :]
            mu = jnp.mean(r, axis=-1, keepdims=True)
            var = jnp.mean(jnp.square(r - mu), axis=-1, keepdims=True)
            x1 = (r - mu) * lax.rsqrt(var + LN_EPS) * lng_ref[...] + lnb_ref[...]
            x1_ref[0, rs, :] = x1
            h2 = x1 * (1.0 + sc2_ref[0]) + sh2_ref[0]
            hi = h2.astype(BF16)
            hi_f = hi.astype(F32)
            lo = (h2 - hi_f).astype(BF16)
            lg_ref[0, rs, :] = _dot(hi, wrh_ref[...]) + _dot(hi, wrl_ref[...]) + _dot(lo, wrh_ref[...])
            h2_ref[0, rs, :] = pack_bf16_pairs(hi_f)
            return carry
        lax.fori_loop(0, n_chunks, finish, 0)


def pack_bf16_pairs(x):
    n = x.shape[-1] // 2
    bits = lax.bitcast_convert_type(x, jnp.uint32)
    return bits[:, n:] | (bits[:, :n] >> 16)


def unpack_bf16_pairs(p):
    lo = lax.bitcast_convert_type(p << 16, F32).astype(BF16)
    hi = lax.bitcast_convert_type(p & jnp.uint32(0xFFFF0000), F32).astype(BF16)
    return lo, hi


def out_proj(attn, g_attn, ssm_n, w_out, x, gate1, ln_g, ln_b, scale2, shift2, wr_hi, wr_lo, alpha):
    bsz, seq, d = x.shape
    n_att = attn.shape[-1]
    n_ssm = ssm_n.shape[-1]
    k = n_att + n_ssm
    tm, tn = 512, 256
    row = lambda a: a.reshape(1, -1)
    full = lambda n: pl.BlockSpec((1, n), lambda b, i, j: (0, 0))
    return pl.pallas_call(
        functools.partial(_outproj_kernel, alpha=alpha, n_att=n_att),
        grid=(bsz, seq // tm, d // tn),
        in_specs=[pl.BlockSpec((1, tm, n_att), lambda b, i, j: (b, i, 0)),
                  full(n_att),
                  pl.BlockSpec((1, tm, n_ssm), lambda b, i, j: (b, i, 0)),
                  pl.BlockSpec((k, tn), lambda b, i, j: (0, j)),
                  pl.BlockSpec((1, tm, tn), lambda b, i, j: (b, i, j)),
                  pl.BlockSpec((1, 1, tn), lambda b, i, j: (b, 0, j)),
                  full(d), full(d),
                  pl.BlockSpec((1, 1, d), lambda b, i, j: (b, 0, 0)),
                  pl.BlockSpec((1, 1, d), lambda b, i, j: (b, 0, 0)),
                  pl.BlockSpec((d, LANES), lambda b, i, j: (0, 0)),
                  pl.BlockSpec((d, LANES), lambda b, i, j: (0, 0))],
        out_specs=[pl.BlockSpec((1, tm, d), lambda b, i, j: (b, i, 0)),
                   pl.BlockSpec((1, tm, d // 2), lambda b, i, j: (b, i, 0)),
                   pl.BlockSpec((1, tm, LANES), lambda b, i, j: (b, i, 0))],
        out_shape=[jax.ShapeDtypeStruct((bsz, seq, d), F32),
                   jax.ShapeDtypeStruct((bsz, seq, d // 2), jnp.uint32),
                   jax.ShapeDtypeStruct((bsz, seq, LANES), F32)],
        scratch_shapes=[pltpu.VMEM((tm, k), BF16)],
        compiler_params=_params(("arbitrary", "arbitrary", "arbitrary"), 56),
        name="out_proj",
    )(attn, row(g_attn), ssm_n, w_out, x, gate1, row(ln_g), row(ln_b), scale2, shift2, wr_hi, wr_lo)


def _moe_kernel(tok_ref, bexp_ref, nact_ref, h_hbm, wg_ref, wu_ref, wd_ref, y_ref, xbuf0, xbuf1, sem):
    i = pl.program_id(0)
    nact = nact_ref[0]
    bufs = (xbuf0, xbuf1)
    rows = xbuf0.shape[0]

    def row_copy(blk, r, slot):
        tok = tok_ref[blk * rows + r]
        return pltpu.make_async_copy(h_hbm.at[pl.ds(tok, 1)], bufs[slot].at[pl.ds(r, 1)], sem.at[slot])

    def wait(slot):
        pltpu.make_async_copy(h_hbm.at[pl.ds(0, rows)], bufs[slot], sem.at[slot]).wait()

    @pl.when(jnp.logical_and(i == 0, nact > 0))
    def _():
        def body(r, carry):
            row_copy(0, r, 0).start()
            return carry
        lax.fori_loop(0, rows, body, 0, unroll=GATHER_UNROLL)

    for slot in range(2):
        @pl.when(jnp.logical_and(i < nact, i % 2 == slot))
        def _():
            wait(slot)
            nxt = jnp.minimum(i + 1, nact - 1)
            for r in range(rows):
                row_copy(nxt, r, 1 - slot).start()
            x_lo, x_hi = unpack_bf16_pairs(bufs[slot][...])
            half = x_lo.shape[1]
            g = _dot(x_lo, wg_ref[0, :half, :]) + _dot(x_hi, wg_ref[0, half:, :])
            u = _dot(x_lo, wu_ref[0, :half, :]) + _dot(x_hi, wu_ref[0, half:, :])
            act = (jax.nn.silu(g) * u).astype(BF16)
            y_ref[...] = _dot(act, wd_ref[0])

            @pl.when(i + 1 >= nact)
            def _():
                wait(1 - slot)

    @pl.when(i >= nact)
    def _():
        y_ref[...] = jnp.zeros_like(y_ref)


def moe_experts(h2, tok_buf, block_expert, n_active, w_gate, w_up, w_down):
    t, dp = h2.shape
    d = 2 * dp
    n_rows = tok_buf.shape[0]
    rows = MOE_ROWS
    n_blocks = n_rows // rows
    de = w_gate.shape[-1]

    def wmap(i, tok, bexp, nact):
        return (bexp[jnp.minimum(i, jnp.maximum(nact[0] - 1, 0))], 0, 0)

    grid_spec = pltpu.PrefetchScalarGridSpec(
        num_scalar_prefetch=3,
        grid=(n_blocks,),
        in_specs=[pl.BlockSpec(memory_space=pl.ANY),
                  pl.BlockSpec((1, d, de), wmap),
                  pl.BlockSpec((1, d, de), wmap),
                  pl.BlockSpec((1, de, d), wmap)],
        out_specs=pl.BlockSpec((rows, d), lambda i, *_: (i, 0)),
        scratch_shapes=[pltpu.VMEM((rows, dp), jnp.uint32), pltpu.VMEM((rows, dp), jnp.uint32),
                        pltpu.SemaphoreType.DMA((2,))],
    )
    return pl.pallas_call(
        _moe_kernel,
        grid_spec=grid_spec,
        out_shape=jax.ShapeDtypeStruct((n_rows, d), F32),
        compiler_params=_params(("arbitrary",), 56),
        name="moe_experts",
    )(tok_buf, block_expert, n_active, h2, w_gate, w_up, w_down)


def _combine_kernel(dest_ref, ys_hbm, rt_ref, x1_ref, gate_ref, lng_ref, lnb_ref, o_ref, ybuf, sem, *, alpha):
    i = pl.program_id(0)
    n = pl.num_programs(0)
    tm = x1_ref.shape[0]
    n_copies = TOP_K * tm

    def gather(blk, slot, start):
        if not start:
            pltpu.make_async_copy(ys_hbm.at[pl.ds(0, n_copies)], ybuf.at[slot], sem.at[slot]).wait()
            return

        def body(r, carry):
            row = dest_ref[blk * n_copies + r]
            pltpu.make_async_copy(ys_hbm.at[pl.ds(row, 1)], ybuf.at[slot, pl.ds(r, 1)], sem.at[slot]).start()
            return carry
        lax.fori_loop(0, n_copies, body, 0, unroll=GATHER_UNROLL)

    @pl.when(i == 0)
    def _():
        gather(0, 0, True)

    @pl.when(i + 1 < n)
    def _():
        gather(i + 1, (i + 1) % 2, True)

    slot = i % 2
    gather(i, slot, False)
    moe = rt_ref[:, ROUTE_W:ROUTE_W + 1] * ybuf[slot, :tm, :]
    for kk in range(1, TOP_K):
        moe = moe + rt_ref[:, ROUTE_W + kk:ROUTE_W + kk + 1] * ybuf[slot, kk * tm:(kk + 1) * tm, :]
    r = alpha * x1_ref[...] + (1.0 + gate_ref[0]) * moe
    mu = jnp.mean(r, axis=-1, keepdims=True)
    var = jnp.mean(jnp.square(r - mu), axis=-1, keepdims=True)
    o_ref[...] = (r - mu) * lax.rsqrt(var + LN_EPS) * lng_ref[...] + lnb_ref[...]


def moe_combine(ys, dest, table, x1, gate2, ln_g, ln_b, alpha, seq):
    t, d = x1.shape
    tm = 128
    tiles_per_seq = seq // tm
    grid_spec = pltpu.PrefetchScalarGridSpec(
        num_scalar_prefetch=1,
        grid=(t // tm,),
        in_specs=[pl.BlockSpec(memory_space=pl.ANY),
                  pl.BlockSpec((tm, LANES), lambda i, *_: (i, 0)),
                  pl.BlockSpec((tm, d), lambda i, *_: (i, 0)),
                  pl.BlockSpec((1, 1, d), lambda i, *_: (i // tiles_per_seq, 0, 0)),
                  pl.BlockSpec((1, d), lambda i, *_: (0, 0)),
                  pl.BlockSpec((1, d), lambda i, *_: (0, 0))],
        out_specs=pl.BlockSpec((tm, d), lambda i, *_: (i, 0)),
        scratch_shapes=[pltpu.VMEM((2, TOP_K * tm, d), F32), pltpu.SemaphoreType.DMA((2,))],
    )
    return pl.pallas_call(
        functools.partial(_combine_kernel, alpha=alpha),
        grid_spec=grid_spec,
        out_shape=jax.ShapeDtypeStruct((t, d), F32),
        compiler_params=_params(("arbitrary",), 32),
        name="moe_combine",
    )(dest, ys, table, x1, gate2, ln_g.reshape(1, d), ln_b.reshape(1, d))


def _route_kernel(lg_ref, b_ref, o_ref, cnt_ref, carry):
    i = pl.program_id(0)

    @pl.when(i == 0)
    def _():
        carry[...] = jnp.zeros_like(carry)

    tm = lg_ref.shape[0]
    x = lg_ref[...] + b_ref[...]
    lane = lax.broadcasted_iota(jnp.int32, (tm, LANES), 1)
    ninf = -jnp.inf

    def top(v):
        vmax = jnp.max(v, axis=1, keepdims=True)
        return vmax, jnp.min(jnp.where(v == vmax, lane, LANES), axis=1, keepdims=True)

    gmask = lane < N_EXPERT_GROUPS
    gmax, g_sel = top(jnp.where(gmask, x, ninf))
    p_group = 1.0 / jnp.sum(jnp.where(gmask, jnp.exp(x - gmax), 0.0), axis=1, keepdims=True)

    lo = N_EXPERT_GROUPS + g_sel * EXPERTS_PER_GROUP
    cur = jnp.where(jnp.logical_and(lane >= lo, lane < lo + EXPERTS_PER_GROUP), x, ninf)
    vals, idxs = [], []
    for _ in range(TOP_K):
        v, ix = top(cur)
        vals.append(v)
        idxs.append(ix)
        cur = jnp.where(lane == ix, ninf, cur)
    exps = [jnp.exp(v - vals[0]) for v in vals]
    den = functools.reduce(lambda a, c: a + c, exps)

    member = functools.reduce(jnp.logical_or, [lane == ix for ix in idxs])
    mf = jnp.where(member, 1.0, 0.0)
    r_i = lax.broadcasted_iota(jnp.int32, (tm, tm), 0)
    c_i = lax.broadcasted_iota(jnp.int32, (tm, tm), 1)
    before = _dot((c_i < r_i).astype(BF16), mf.astype(BF16)) + carry[...]
    carry[...] = carry[...] + jnp.sum(mf, axis=0, keepdims=True)
    cnt_ref[...] = carry[...]

    out = jnp.zeros((tm, LANES), F32)
    for k in range(TOP_K):
        rank = jnp.sum(jnp.where(lane == idxs[k], before, 0.0), axis=1, keepdims=True)
        out = jnp.where(lane == ROUTE_EID + k, (idxs[k] - N_EXPERT_GROUPS).astype(F32), out)
        out = jnp.where(lane == ROUTE_RANK + k, rank, out)
        out = jnp.where(lane == ROUTE_W + k, p_group * (exps[k] / den), out)
    o_ref[...] = out


def route(logits, b_rg, b_re, n_rows, tm_combine):
    n_tok = logits.shape[0]
    n_experts = N_EXPERT_GROUPS * EXPERTS_PER_GROUP
    bias = jnp.zeros((1, LANES), F32).at[0, :N_EXPERT_GROUPS].set(b_rg)
    bias = bias.at[0, N_EXPERT_GROUPS:N_EXPERT_GROUPS + n_experts].set(b_re)
    tm = 512
    table, cnt = pl.pallas_call(
        _route_kernel,
        grid=(n_tok // tm,),
        in_specs=[pl.BlockSpec((tm, LANES), lambda i: (i, 0)),
                  pl.BlockSpec((1, LANES), lambda i: (0, 0))],
        out_specs=[pl.BlockSpec((tm, LANES), lambda i: (i, 0)),
                   pl.BlockSpec((1, LANES), lambda i: (0, 0))],
        out_shape=[jax.ShapeDtypeStruct((n_tok, LANES), F32), jax.ShapeDtypeStruct((1, LANES), F32)],
        scratch_shapes=[pltpu.VMEM((1, LANES), F32)],
        compiler_params=_params(("arbitrary",), 16),
        name="route",
    )(logits, bias)

    eid = table[:, ROUTE_EID:ROUTE_EID + TOP_K].astype(jnp.int32)
    rank = table[:, ROUTE_RANK:ROUTE_RANK + TOP_K].astype(jnp.int32)
    counts = cnt[0, N_EXPERT_GROUPS:N_EXPERT_GROUPS + n_experts].astype(jnp.int32)
    padded = ((counts + MOE_ROWS - 1) // MOE_ROWS) * MOE_ROWS
    pends = jnp.cumsum(padded)
    pstarts = pends - padded
    onehot = eid[:, :, None] == jnp.arange(n_experts, dtype=jnp.int32)
    dest = jnp.sum(jnp.where(onehot, pstarts, 0), axis=-1) + rank
    tok = jnp.broadcast_to(jnp.arange(n_tok, dtype=jnp.int32)[:, None], (n_tok, TOP_K))
    tok_buf = jnp.zeros((n_rows,), jnp.int32).at[dest.reshape(-1)].set(tok.reshape(-1), unique_indices=True)
    n_blocks = n_rows // MOE_ROWS
    block_expert = jnp.clip(
        jnp.searchsorted(pends, jnp.arange(n_blocks, dtype=jnp.int32) * MOE_ROWS, side='right'),
        0, n_experts - 1).astype(jnp.int32)
    n_active = (pends[-1] // MOE_ROWS).astype(jnp.int32).reshape(1)
    dest_tiles = dest.reshape(n_tok // tm_combine, tm_combine, TOP_K).transpose(0, 2, 1).reshape(-1)
    return table, tok_buf, block_expert, n_active, dest_tiles.astype(jnp.int32)


def _layer(x, c, w_ada, b_ada, w_in, b_forget, lam_re, lam_im, log_dt, b_re, b_im, c_re, c_im, d_skip,
           w_glu, b_glu, g_attn, g_ssm, w_out, ln1_g, ln1_b, w_rg, b_rg, w_re, b_re_r,
           w_gate, w_up, w_down, ln2_g, ln2_b, alpha):
    bsz, seq, d = x.shape
    n_tok = bsz * seq
    n_heads = b_forget.shape[0]
    d_att = n_heads * HEAD_DIM
    d_ssm = d_skip.shape[0] * d_skip.shape[1]

    mod = ada_mod(c, w_ada, b_ada).reshape(bsz, 1, -1)
    shift1, scale1, gate1, shift2, scale2, gate2 = jnp.split(mod, 6, axis=-1)

    n_qkv = 3 * d_att
    w_main = jnp.concatenate([w_in[:, :n_qkv], w_in[:, n_qkv + n_heads:]], axis=1).astype(BF16)
    w_f = jnp.zeros((d, LANES), BF16).at[:, :n_heads].set(w_in[:, n_qkv:n_qkv + n_heads].astype(BF16))
    qkv, u, f = in_proj(x, scale1, shift1, w_main, w_f, n_qkv)

    q_aug, k_aug = forget_cum(f, b_forget)
    attn, (w_glu_b, w_out_b, w_gate_b, w_up_b, w_down_b) = attention(
        qkv, q_aug, k_aug, n_heads, [w_glu, w_out, w_gate, w_up, w_down])

    w1, ft, al = ssm_prep(lam_re, lam_im, log_dt, b_re, b_im, c_re, c_im)
    y = ssm_scan(u, w1, ft, al, d_skip)
    ssm_n = glu_norm(y.reshape(n_tok, d_ssm), w_glu_b, b_glu, g_ssm).reshape(bsz, seq, d_ssm)

    n_experts = N_EXPERT_GROUPS * EXPERTS_PER_GROUP
    w_r = jnp.zeros((d, LANES), F32).at[:, :N_EXPERT_GROUPS].set(w_rg)
    w_r = w_r.at[:, N_EXPERT_GROUPS:N_EXPERT_GROUPS + n_experts].set(w_re)
    wr_hi = w_r.astype(BF16)
    wr_lo = (w_r - wr_hi.astype(F32)).astype(BF16)
    x1, h2, logits = out_proj(attn, g_attn, ssm_n, w_out_b, x, gate1, ln1_g, ln1_b,
                              scale2, shift2, wr_hi, wr_lo, alpha)

    n_assign = n_tok * TOP_K
    n_blocks = -(-(n_assign + n_experts * (MOE_ROWS - 1)) // MOE_ROWS)
    n_rows = n_blocks * MOE_ROWS
    table, tok_buf, block_expert, n_active, dest = route(logits.reshape(n_tok, LANES), b_rg, b_re_r, n_rows, 128)
    ys = moe_experts(h2.reshape(n_tok, d // 2), tok_buf, block_expert, n_active,
                     w_gate_b, w_up_b, w_down_b)
    out = moe_combine(ys, dest, table, x1.reshape(n_tok, d), gate2, ln2_g, ln2_b, alpha, seq)
    return out.reshape(bsz, seq, d)


def kernel(x, c, w_ada, b_ada, w_in, b_forget, ssm_lambda_re, ssm_lambda_im, ssm_log_dt, ssm_b_re, ssm_b_im,
           ssm_c_re, ssm_c_im, ssm_d, w_glu, b_glu, g_attn, g_ssm, w_out, ln1_g, ln1_b, w_router_group,
           b_router_group, w_router_expert, b_router_expert, w_gate, w_up, w_down, ln2_g, ln2_b):
    depth = w_ada.shape[0]
    alpha = (2.0 * depth) ** 0.25
    for l in range(depth):
        x = _layer(x, c, w_ada[l], b_ada[l], w_in[l], b_forget[l], ssm_lambda_re[l], ssm_lambda_im[l],
                   ssm_log_dt[l], ssm_b_re[l], ssm_b_im[l], ssm_c_re[l], ssm_c_im[l], ssm_d[l],
                   w_glu[l], b_glu[l], g_attn[l], g_ssm[l], w_out[l], ln1_g[l], ln1_b[l],
                   w_router_group[l], b_router_group[l], w_router_expert[l], b_router_expert[l],
                   w_gate[l], w_up[l], w_down[l], ln2_g[l], ln2_b[l], alpha)
    return x
```

```python
import functools
import math

import jax
import jax.numpy as jnp
from jax import lax
from jax.experimental import pallas as pl
from jax.experimental.pallas import tpu as pltpu

F32 = jnp.float32
BF16 = jnp.bfloat16

LANES = 128
HEAD_DIM = 128
SSM_GROUP = 16
SSM_STATE = 64
GROUPS_PER_SLAB = LANES // SSM_GROUP
SLAB_STATE = GROUPS_PER_SLAB * SSM_STATE
SSM_CHUNK = 16
N_EXPERT_GROUPS = 8
EXPERTS_PER_GROUP = 8
TOP_K = 2
MOE_ROWS = 256
GATHER_UNROLL = 8
ROW_CHUNK = 128
ROUTE_EID, ROUTE_RANK, ROUTE_W = 0, TOP_K, 2 * TOP_K
LN_EPS = 1e-5
RMS_EPS = 1e-6
NEG_BIG = -1e30
LOG2E = math.log2(math.e)
MIB = 1024 * 1024

_NT = (((1,), (1,)), ((), ()))


def _params(semantics, vmem_mib):
    return pltpu.CompilerParams(dimension_semantics=semantics, vmem_limit_bytes=vmem_mib * MIB)


def _dot(a, b):
    return jnp.dot(a, b, preferred_element_type=F32)


def _ada_kernel(c_ref, w_ref, b_ref, o_ref):
    s = jax.nn.silu(c_ref[...]).astype(BF16)
    o_ref[...] = _dot(s, w_ref[...].astype(BF16)) + b_ref[...]


def ada_mod(c, w_ada, b_ada):
    bsz, d = c.shape
    n = w_ada.shape[1]
    rows = 8
    assert bsz <= rows
    cp = jnp.zeros((rows, d), F32).at[:bsz].set(c)
    tn = 512
    out = pl.pallas_call(
        _ada_kernel,
        grid=(n // tn,),
        in_specs=[pl.BlockSpec((rows, d), lambda j: (0, 0)),
                  pl.BlockSpec((d, tn), lambda j: (0, j)),
                  pl.BlockSpec((1, tn), lambda j: (0, j))],
        out_specs=pl.BlockSpec((rows, tn), lambda j: (0, j)),
        out_shape=jax.ShapeDtypeStruct((rows, n), F32),
        compiler_params=_params(("arbitrary",), 40),
        name="ada_mod",
    )(cp, w_ada, b_ada.reshape(1, n))
    return out[:bsz]


def _inproj_kernel(x_ref, sc_ref, sh_ref, w_ref, wf_ref, qkv_ref, u_ref, f_ref, h_scr, *,
                   n_q_tiles, n_qkv_tiles, q_scale):
    j = pl.program_id(2)

    @pl.when(j == 0)
    def _():
        hb = (x_ref[0] * (1.0 + sc_ref[0]) + sh_ref[0]).astype(BF16)
        h_scr[...] = hb
        f_ref[0] = _dot(hb, wf_ref[...])

    acc = _dot(h_scr[...], w_ref[...])

    @pl.when(j < n_q_tiles)
    def _():
        qkv_ref[0] = (acc * q_scale).astype(BF16)

    @pl.when(jnp.logical_and(j >= n_q_tiles, j < n_qkv_tiles))
    def _():
        qkv_ref[0] = acc.astype(BF16)

    @pl.when(j >= n_qkv_tiles)
    def _():
        u_ref[0] = acc


def in_proj(x, scale, shift, w_main, w_f, n_qkv):
    bsz, seq, d = x.shape
    n_all = w_main.shape[1]
    n_u = n_all - n_qkv
    tm, tn = 512, 1024
    nq = n_qkv // tn
    grid = (bsz, seq // tm, n_all // tn)
    return pl.pallas_call(
        functools.partial(_inproj_kernel, n_q_tiles=n_qkv // 3 // tn, n_qkv_tiles=nq,
                          q_scale=HEAD_DIM ** -0.5 * LOG2E),
        grid=grid,
        in_specs=[pl.BlockSpec((1, tm, d), lambda b, i, j: (b, i, 0)),
                  pl.BlockSpec((1, 1, d), lambda b, i, j: (b, 0, 0)),
                  pl.BlockSpec((1, 1, d), lambda b, i, j: (b, 0, 0)),
                  pl.BlockSpec((d, tn), lambda b, i, j: (0, j)),
                  pl.BlockSpec((d, LANES), lambda b, i, j: (0, 0))],
        out_specs=[pl.BlockSpec((1, tm, tn), lambda b, i, j: (b, i, jnp.minimum(j, nq - 1))),
                   pl.BlockSpec((1, tm, tn), lambda b, i, j: (b, i, jnp.maximum(j - nq, 0))),
                   pl.BlockSpec((1, tm, LANES), lambda b, i, j: (b, i, 0))],
        out_shape=[jax.ShapeDtypeStruct((bsz, seq, n_qkv), BF16),
                   jax.ShapeDtypeStruct((bsz, seq, n_u), F32),
                   jax.ShapeDtypeStruct((bsz, seq, LANES), F32)],
        scratch_shapes=[pltpu.VMEM((tm, d), BF16)],
        compiler_params=_params(("arbitrary", "arbitrary", "arbitrary"), 52),
        name="in_proj",
    )(x, scale, shift, w_main, w_f)


def _split3(x):
    p1 = x.astype(BF16)
    r1 = x - p1.astype(F32)
    p2 = r1.astype(BF16)
    p3 = (r1 - p2.astype(F32)).astype(BF16)
    return p1, p2, p3


N_PIECES = 3


def _cum_kernel(f_ref, b_ref, qa_ref, ka_ref, carry, *, n_heads):
    i = pl.program_id(1)

    @pl.when(i == 0)
    def _():
        carry[...] = jnp.zeros_like(carry)

    tc = f_ref.shape[1]
    lf = jax.nn.log_sigmoid(f_ref[0] + b_ref[...])
    row = lax.broadcasted_iota(jnp.int32, (tc, tc), 0)
    col = lax.broadcasted_iota(jnp.int32, (tc, tc), 1)
    tri = (col <= row).astype(BF16)
    p1, p2, p3 = _split3(lf)
    cs = _dot(tri, p1) + _dot(tri, p2) + _dot(tri, p3) + carry[...]
    carry[...] = cs[tc - 1:tc, :]

    pieces = jnp.concatenate(_split3(cs * LOG2E), axis=1)
    r = lax.broadcasted_iota(jnp.int32, (N_PIECES * LANES, LANES), 0)
    c = lax.broadcasted_iota(jnp.int32, (N_PIECES * LANES, LANES), 1)
    lane = lax.broadcasted_iota(jnp.int32, (tc, LANES), 1)
    ones_q = jnp.where(jnp.logical_and(lane >= N_PIECES, lane < 2 * N_PIECES), 1.0, 0.0)
    ones_k = jnp.where(lane < N_PIECES, 1.0, 0.0)
    for h in range(n_heads):
        sel_q = (r == c * LANES + h).astype(BF16)
        sel_k = (r == (c - N_PIECES) * LANES + h).astype(BF16)
        qa_ref[0, h] = (_dot(pieces, sel_q) + ones_q).astype(BF16)
        ka_ref[0, h] = (ones_k - _dot(pieces, sel_k)).astype(BF16)


def forget_cum(f, b_forget):
    bsz, seq, _ = f.shape
    n_heads = b_forget.shape[0]
    tc = 256
    bpad = jnp.zeros((1, LANES), F32).at[0, :n_heads].set(b_forget)
    out_spec = pl.BlockSpec((1, n_heads, tc, LANES), lambda b, i: (b, 0, i, 0))
    out_shape = jax.ShapeDtypeStruct((bsz, n_heads, seq, LANES), BF16)
    return pl.pallas_call(
        functools.partial(_cum_kernel, n_heads=n_heads),
        grid=(bsz, seq // tc),
        in_specs=[pl.BlockSpec((1, tc, LANES), lambda b, i: (b, i, 0)),
                  pl.BlockSpec((1, LANES), lambda b, i: (0, 0))],
        out_specs=[out_spec, out_spec],
        out_shape=[out_shape, out_shape],
        scratch_shapes=[pltpu.VMEM((1, LANES), F32)],
        compiler_params=_params(("arbitrary", "arbitrary"), 24),
        name="forget_cum",
    )(f, bpad)


ATT_QROWS = 1024
ATT_KEYS = 512


def _attn_kernel(*refs, n_cast):
    q_ref, qa_ref, k_ref, ka_ref, v_ref = refs[:5]
    src_refs = refs[5:5 + n_cast]
    o_ref = refs[5 + n_cast]
    dst_refs = refs[6 + n_cast:6 + 2 * n_cast]
    m_scr, acc_scr, s_scr = refs[6 + 2 * n_cast:]
    for src, dst in zip(src_refs, dst_refs):
        dst[...] = src[...].astype(BF16)

    tq = q_ref.shape[1]
    tk = ATT_KEYS
    n_groups = tq // tk
    qi = pl.program_id(2)
    q = jnp.concatenate([q_ref[0], qa_ref[0, 0]], axis=1)
    ones = jnp.ones((tk, HEAD_DIM), BF16)

    m_scr[...] = jnp.full_like(m_scr, NEG_BIG)
    acc_scr[...] = jnp.zeros_like(acc_scr)

    def scores(kb, slot, first_group=0):
        k0 = pl.multiple_of(kb * tk, tk)
        kt = jnp.concatenate([k_ref[0, pl.ds(k0, tk), :], ka_ref[0, 0, pl.ds(k0, tk), :]], axis=1)
        r0 = first_group * tk
        s_scr[slot, r0:, :] = lax.dot_general(q[r0:], kt, _NT, preferred_element_type=F32)

    def update(kb, slot, diag_group=None):
        k0 = pl.multiple_of(kb * tk, tk)
        vt = jnp.concatenate([v_ref[0, pl.ds(k0, tk), :], ones], axis=1)
        for g in range(n_groups):
            if diag_group is not None and g < diag_group:
                continue
            rs = slice(g * tk, (g + 1) * tk)
            s = s_scr[slot, rs, :]
            if g == diag_group:
                qpos = lax.broadcasted_iota(jnp.int32, (tk, tk), 0)
                kpos = lax.broadcasted_iota(jnp.int32, (tk, tk), 1)
                s = jnp.where(kpos <= qpos, s, NEG_BIG)
            m_prev = m_scr[rs]
            m_new = jnp.maximum(m_prev, jnp.max(s, axis=1, keepdims=True))
            alpha = jnp.exp2(m_prev - m_new)
            p = jnp.exp2(s - m_new)
            acc_scr[rs] = alpha * acc_scr[rs] + _dot(p.astype(BF16), vt)
            m_scr[rs] = m_new

    def body(i, carry):
        kb = n_groups * i
        for g in range(n_groups):
            scores(kb + g + 1, (g + 1) % 2)
            update(kb + g, g % 2)
        return carry

    assert n_groups % 2 == 0
    scores(0, 0)
    lax.fori_loop(0, qi, body, 0)
    kb = n_groups * qi
    for g in range(n_groups):
        if g + 1 < n_groups:
            scores(kb + g + 1, (g + 1) % 2, first_group=g + 1)
        update(kb + g, g % 2, diag_group=g)

    o_ref[0] = acc_scr[:, :HEAD_DIM] / acc_scr[:, HEAD_DIM:]


BF16_ROWS = 16


def _cast_chunks(w, n_steps):
    cols = w.shape[-1]
    total_rows = w.size // cols
    n_chunks = n_steps
    while total_rows % (n_chunks * BF16_ROWS):
        n_chunks //= 2
    return w.reshape(n_chunks, total_rows // n_chunks, cols)


def attention(qkv, q_aug, k_aug, n_heads, cast_weights):
    bsz, seq, _ = qkv.shape
    t = min(ATT_QROWS, seq)
    nq = seq // t
    n_steps = bsz * n_heads * nq
    srcs = [_cast_chunks(w, n_steps) for w in cast_weights]

    def chunk_spec(a):
        per = n_steps // a.shape[0]
        return pl.BlockSpec((1,) + a.shape[1:], lambda b, h, i: (((b * n_heads + h) * nq + i) // per, 0, 0))

    outs = pl.pallas_call(
        functools.partial(_attn_kernel, n_cast=len(srcs)),
        grid=(bsz, n_heads, nq),
        in_specs=[pl.BlockSpec((1, t, HEAD_DIM), lambda b, h, i: (b, i, h)),
                  pl.BlockSpec((1, 1, t, LANES), lambda b, h, i: (b, h, i, 0)),
                  pl.BlockSpec((1, seq, HEAD_DIM), lambda b, h, i: (b, 0, n_heads + h)),
                  pl.BlockSpec((1, 1, seq, LANES), lambda b, h, i: (b, h, 0, 0)),
                  pl.BlockSpec((1, seq, HEAD_DIM), lambda b, h, i: (b, 0, 2 * n_heads + h))]
                 + [chunk_spec(a) for a in srcs],
        out_specs=[pl.BlockSpec((1, t, HEAD_DIM), lambda b, h, i: (b, i, h))] + [chunk_spec(a) for a in srcs],
        out_shape=[jax.ShapeDtypeStruct((bsz, seq, n_heads * HEAD_DIM), F32)]
                  + [jax.ShapeDtypeStruct(a.shape, BF16) for a in srcs],
        scratch_shapes=[pltpu.VMEM((t, 1), F32), pltpu.VMEM((t, 2 * HEAD_DIM), F32),
                        pltpu.VMEM((2, t, ATT_KEYS), F32)],
        compiler_params=_params(("arbitrary", "arbitrary", "arbitrary"), 56),
        name="attention",
    )(qkv, q_aug, qkv, k_aug, qkv, *srcs)
    return outs[0], [o.reshape(w.shape) for o, w in zip(outs[1:], cast_weights)]


def _blockdiag(p):
    g, c, n = p.shape
    ns = g // GROUPS_PER_SLAB
    eye = jnp.eye(GROUPS_PER_SLAB, dtype=p.dtype)
    out = p.reshape(ns, GROUPS_PER_SLAB, c, 1, n) * eye[None, :, None, :, None]
    return out.reshape(ns, GROUPS_PER_SLAB * c, GROUPS_PER_SLAB * n)


def _ssm_prep_kernel(lr_ref, li_ref, ldt_ref, bre_ref, bim_ref, cre_ref, cim_ref, w1_ref, ft_ref, al_ref):
    L = SSM_CHUNK
    lr = lr_ref[0]
    li = li_ref[0]
    dt = jnp.exp(ldt_ref[0])
    mag = jnp.exp(lr * dt)
    a_re = mag * jnp.cos(li * dt)
    a_im = mag * jnp.sin(li * dt)
    den = lr * lr + li * li
    z_re = ((a_re - 1.0) * lr + a_im * li) / den
    z_im = (a_im * lr - (a_re - 1.0) * li) / den
    br = bre_ref[0]
    bi = bim_ref[0]
    bb_re = z_re * br - z_im * bi
    bb_im = z_re * bi + z_im * br
    cr = cre_ref[0]
    ci = cim_ref[0]
    ft0 = jnp.concatenate([cr, -ci], axis=1)

    def power(d):
        m = jnp.exp(lr * dt * d)
        return m * jnp.cos(li * dt * d), m * jnp.sin(li * dt * d)

    w1_ref[0, :, :L * LANES] = jnp.zeros((L * LANES, L * LANES), BF16)
    for d in range(L):
        pr, pi = power(float(d))
        xe = jnp.concatenate([bb_re * pr - bb_im * pi, bb_re * pi + bb_im * pr], axis=1)
        j = L - 1 - d
        w1_ref[0, j * LANES:(j + 1) * LANES, L * LANES:] = xe.astype(BF16)
        m_d = lax.dot_general(xe, ft0, _NT, preferred_element_type=F32,
                              precision=lax.Precision.HIGHEST).astype(BF16)
        for jj in range(L - d):
            w1_ref[0, jj * LANES:(jj + 1) * LANES, (jj + d) * LANES:(jj + d + 1) * LANES] = m_d
        pr1, pi1 = power(float(d + 1))
        ft_ref[0, d * LANES:(d + 1) * LANES, :] = jnp.concatenate(
            [cr * pr1 - ci * pi1, -(cr * pi1 + ci * pr1)], axis=1).astype(BF16)
    prl, pil = power(float(L))
    al_ref[0] = jnp.concatenate([prl, pil], axis=1)


def ssm_prep(lam_re, lam_im, log_dt, b_re, b_im, c_re, c_im):
    g, n = lam_re.shape
    ns = g // GROUPS_PER_SLAB
    L = SSM_CHUNK
    rowvec = lambda a: a.reshape(ns, 1, SLAB_STATE)
    args = (rowvec(lam_re), rowvec(lam_im), rowvec(jnp.repeat(log_dt, n)),
            _blockdiag(b_re.transpose(0, 2, 1)), _blockdiag(b_im.transpose(0, 2, 1)),
            _blockdiag(c_re), _blockdiag(c_im))
    vec_spec = pl.BlockSpec((1, 1, SLAB_STATE), lambda s: (s, 0, 0))
    mat_spec = pl.BlockSpec((1, LANES, SLAB_STATE), lambda s: (s, 0, 0))
    return pl.pallas_call(
        _ssm_prep_kernel,
        grid=(ns,),
        in_specs=[vec_spec] * 3 + [mat_spec] * 4,
        out_specs=[pl.BlockSpec((1, L * LANES, L * LANES + 2 * SLAB_STATE), lambda s: (s, 0, 0)),
                   pl.BlockSpec((1, L * LANES, 2 * SLAB_STATE), lambda s: (s, 0, 0)),
                   pl.BlockSpec((1, 1, 2 * SLAB_STATE), lambda s: (s, 0, 0))],
        out_shape=[jax.ShapeDtypeStruct((ns, L * LANES, L * LANES + 2 * SLAB_STATE), BF16),
                   jax.ShapeDtypeStruct((ns, L * LANES, 2 * SLAB_STATE), BF16),
                   jax.ShapeDtypeStruct((ns, 1, 2 * SLAB_STATE), F32)],
        compiler_params=_params(("arbitrary",), 48),
        name="ssm_prep",
    )(*args)


def _ssm_kernel(u_ref, w1_ref, ft_ref, al_ref, d_ref, y_ref, uf_scr, e_scr, y_scr):
    L = SSM_CHUNK
    nch = uf_scr.shape[0]
    lc = L * LANES
    for j in range(L):
        uf_scr[:, j * LANES:(j + 1) * LANES] = u_ref[0, pl.ds(j, nch, stride=L), :].astype(BF16)
    uf = uf_scr[...]
    y_scr[...] = _dot(uf, w1_ref[0, :, :lc])
    e_scr[...] = _dot(uf, w1_ref[0, :, lc:])

    a_re = al_ref[0, :, :SLAB_STATE]
    a_im = al_ref[0, :, SLAB_STATE:]

    def body(k, h):
        h_re, h_im = h
        e = e_scr[pl.ds(k, 1), :]
        e_scr[pl.ds(k, 1), :] = jnp.concatenate([h_re, h_im], axis=1)
        return (a_re * h_re - a_im * h_im + e[:, :SLAB_STATE],
                a_re * h_im + a_im * h_re + e[:, SLAB_STATE:])

    zero = jnp.zeros((1, SLAB_STATE), F32)
    lax.fori_loop(0, nch, body, (zero, zero))
    y = y_scr[...] + lax.dot_general(e_scr[...].astype(BF16), ft_ref[0], _NT, preferred_element_type=F32)
    for i in range(L):
        yi = y[:, i * LANES:(i + 1) * LANES] + d_ref[0] * u_ref[0, pl.ds(i, nch, stride=L), :]
        y_ref[0, pl.ds(i, nch, stride=L), :] = jax.nn.gelu(yi)


def ssm_scan(u, w1, ft, al, d_skip):
    bsz, seq, c = u.shape
    ns = c // LANES
    L = SSM_CHUNK
    nch = seq // L
    lc = L * LANES
    once = pl.Buffered(1)
    return pl.pallas_call(
        _ssm_kernel,
        grid=(ns, bsz),
        in_specs=[pl.BlockSpec((1, seq, LANES), lambda s, b: (b, 0, s)),
                  pl.BlockSpec((1, lc, lc + 2 * SLAB_STATE), lambda s, b: (s, 0, 0), pipeline_mode=once),
                  pl.BlockSpec((1, lc, 2 * SLAB_STATE), lambda s, b: (s, 0, 0), pipeline_mode=once),
                  pl.BlockSpec((1, 1, 2 * SLAB_STATE), lambda s, b: (s, 0, 0)),
                  pl.BlockSpec((1, 1, LANES), lambda s, b: (s, 0, 0))],
        out_specs=pl.BlockSpec((1, seq, LANES), lambda s, b: (b, 0, s)),
        out_shape=jax.ShapeDtypeStruct((bsz, seq, c), F32),
        scratch_shapes=[pltpu.VMEM((nch, lc), BF16), pltpu.VMEM((nch, 2 * SLAB_STATE), F32),
                        pltpu.VMEM((nch, lc), F32)],
        compiler_params=_params(("arbitrary", "arbitrary"), 56),
        name="ssm_scan",
    )(u, w1, ft, al, d_skip.reshape(ns, 1, LANES))


def _glu_kernel(y_ref, w_ref, b_ref, g_ref, o_ref):
    y = y_ref[...]
    o = y * jax.nn.sigmoid(_dot(y.astype(BF16), w_ref[...]) + b_ref[...])
    ms = jnp.mean(o * o, axis=-1, keepdims=True)
    o_ref[...] = (o * lax.rsqrt(ms + RMS_EPS) * g_ref[...]).astype(BF16)


def glu_norm(y, w_glu, b_glu, g):
    t, c = y.shape
    tm = 512
    return pl.pallas_call(
        _glu_kernel,
        grid=(t // tm,),
        in_specs=[pl.BlockSpec((tm, c), lambda i: (i, 0)),
                  pl.BlockSpec((c, c), lambda i: (0, 0)),
                  pl.BlockSpec((1, c), lambda i: (0, 0)),
                  pl.BlockSpec((1, c), lambda i: (0, 0))],
        out_specs=pl.BlockSpec((tm, c), lambda i: (i, 0)),
        out_shape=jax.ShapeDtypeStruct((t, c), BF16),
        compiler_params=_params(("arbitrary",), 48),
        name="glu_norm",
    )(y, w_glu, b_glu.reshape(1, c), g.reshape(1, c))


def _outproj_kernel(attn_ref, ga_ref, ssm_ref, w_ref, x_ref, gate_ref, lng_ref, lnb_ref, sc2_ref, sh2_ref,
                    wrh_ref, wrl_ref, x1_ref, h2_ref, lg_ref, a_scr, *, alpha, n_att):
    j = pl.program_id(2)
    nj = pl.num_programs(2)
    tn = w_ref.shape[1]

    tm = x1_ref.shape[1]
    n_chunks = tm // ROW_CHUNK

    @pl.when(j == 0)
    def _():
        def norm(c, carry):
            rs = pl.ds(pl.multiple_of(c * ROW_CHUNK, ROW_CHUNK), ROW_CHUNK)
            a = attn_ref[0, rs, :]
            ms = jnp.mean(a * a, axis=-1, keepdims=True)
            a_scr[rs, :n_att] = (a * lax.rsqrt(ms + RMS_EPS) * ga_ref[...]).astype(BF16)
            return carry
        lax.fori_loop(0, n_chunks, norm, 0)
        a_scr[:, n_att:] = ssm_ref[0]

    mixed = _dot(a_scr[...], w_ref[...])
    col = pl.multiple_of(j * tn, tn)
    x1_ref[0, :, pl.ds(col, tn)] = alpha * x_ref[0] + (1.0 + gate_ref[0]) * mixed

    @pl.when(j == nj - 1)
    def _():
        def finish(c, carry):
            rs = pl.ds(pl.multiple_of(c * ROW_CHUNK, ROW_CHUNK), ROW_CHUNK)
            r = x1_ref[0, rs, :]
            mu = jnp.mean(r, axis=-1, keepdims=True)
            var = jnp.mean(jnp.square(r - mu), axis=-1, keepdims=True)
            x1 = (r - mu) * lax.rsqrt(var + LN_EPS) * lng_ref[...] + lnb_ref[...]
            x1_ref[0, rs, :] = x1
            h2 = x1 * (1.0 + sc2_ref[0]) + sh2_ref[0]
            hi = h2.astype(BF16)
            hi_f = hi.astype(F32)
            lo = (h2 - hi_f).astype(BF16)
            lg_ref[0, rs, :] = _dot(hi, wrh_ref[...]) + _dot(hi, wrl_ref[...]) + _dot(lo, wrh_ref[...])
            h2_ref[0, rs, :] = pack_bf16_pairs(hi_f)
            return carry
        lax.fori_loop(0, n_chunks, finish, 0)


def pack_bf16_pairs(x):
    n = x.shape[-1] // 2
    bits = lax.bitcast_convert_type(x, jnp.uint32)
    return bits[:, n:] | (bits[:, :n] >> 16)


def unpack_bf16_pairs(p):
    lo = lax.bitcast_convert_type(p << 16, F32).astype(BF16)
    hi = lax.bitcast_convert_type(p & jnp.uint32(0xFFFF0000), F32).astype(BF16)
    return lo, hi


def out_proj(attn, g_attn, ssm_n, w_out, x, gate1, ln_g, ln_b, scale2, shift2, wr_hi, wr_lo, alpha):
    bsz, seq, d = x.shape
    n_att = attn.shape[-1]
    n_ssm = ssm_n.shape[-1]
    k = n_att + n_ssm
    tm, tn = 512, 256
    row = lambda a: a.reshape(1, -1)
    full = lambda n: pl.BlockSpec((1, n), lambda b, i, j: (0, 0))
    return pl.pallas_call(
        functools.partial(_outproj_kernel, alpha=alpha, n_att=n_att),
        grid=(bsz, seq // tm, d // tn),
        in_specs=[pl.BlockSpec((1, tm, n_att), lambda b, i, j: (b, i, 0)),
                  full(n_att),
                  pl.BlockSpec((1, tm, n_ssm), lambda b, i, j: (b, i, 0)),
                  pl.BlockSpec((k, tn), lambda b, i, j: (0, j)),
                  pl.BlockSpec((1, tm, tn), lambda b, i, j: (b, i, j)),
                  pl.BlockSpec((1, 1, tn), lambda b, i, j: (b, 0, j)),
                  full(d), full(d),
                  pl.BlockSpec((1, 1, d), lambda b, i, j: (b, 0, 0)),
                  pl.BlockSpec((1, 1, d), lambda b, i, j: (b, 0, 0)),
                  pl.BlockSpec((d, LANES), lambda b, i, j: (0, 0)),
                  pl.BlockSpec((d, LANES), lambda b, i, j: (0, 0))],
        out_specs=[pl.BlockSpec((1, tm, d), lambda b, i, j: (b, i, 0)),
                   pl.BlockSpec((1, tm, d // 2), lambda b, i, j: (b, i, 0)),
                   pl.BlockSpec((1, tm, LANES), lambda b, i, j: (b, i, 0))],
        out_shape=[jax.ShapeDtypeStruct((bsz, seq, d), F32),
                   jax.ShapeDtypeStruct((bsz, seq, d // 2), jnp.uint32),
                   jax.ShapeDtypeStruct((bsz, seq, LANES), F32)],
        scratch_shapes=[pltpu.VMEM((tm, k), BF16)],
        compiler_params=_params(("arbitrary", "arbitrary", "arbitrary"), 56),
        name="out_proj",
    )(attn, row(g_attn), ssm_n, w_out, x, gate1, row(ln_g), row(ln_b), scale2, shift2, wr_hi, wr_lo)


def _moe_kernel(tok_ref, bexp_ref, nact_ref, h_hbm, wg_ref, wu_ref, wd_ref, y_ref, xbuf, sem):
    i = pl.program_id(0)
    nact = nact_ref[0]
    rows = xbuf.shape[1]

    def start_gather(blk, slot):
        def body(r, carry):
            tok = tok_ref[blk * rows + r]
            pltpu.make_async_copy(h_hbm.at[pl.ds(tok, 1)], xbuf.at[slot, pl.ds(r, 1)], sem.at[slot]).start()
            return carry
        lax.fori_loop(0, rows, body, 0, unroll=GATHER_UNROLL)

    def wait_gather(slot):
        pltpu.make_async_copy(h_hbm.at[pl.ds(0, rows)], xbuf.at[slot], sem.at[slot]).wait()

    @pl.when(jnp.logical_and(i == 0, nact > 0))
    def _():
        start_gather(0, 0)

    @pl.when(i + 1 < nact)
    def _():
        start_gather(i + 1, (i + 1) % 2)

    @pl.when(i < nact)
    def _():
        slot = i % 2
        wait_gather(slot)
        x_lo, x_hi = unpack_bf16_pairs(xbuf[slot])
        half = x_lo.shape[1]
        g = _dot(x_lo, wg_ref[0, :half, :]) + _dot(x_hi, wg_ref[0, half:, :])
        u = _dot(x_lo, wu_ref[0, :half, :]) + _dot(x_hi, wu_ref[0, half:, :])
        act = (jax.nn.silu(g) * u).astype(BF16)
        y_ref[...] = _dot(act, wd_ref[0])

    @pl.when(i >= nact)
    def _():
        y_ref[...] = jnp.zeros_like(y_ref)


def moe_experts(h2, tok_buf, block_expert, n_active, w_gate, w_up, w_down):
    t, dp = h2.shape
    d = 2 * dp
    n_rows = tok_buf.shape[0]
    rows = MOE_ROWS
    n_blocks = n_rows // rows
    de = w_gate.shape[-1]

    def wmap(i, tok, bexp, nact):
        return (bexp[jnp.minimum(i, jnp.maximum(nact[0] - 1, 0))], 0, 0)

    grid_spec = pltpu.PrefetchScalarGridSpec(
        num_scalar_prefetch=3,
        grid=(n_blocks,),
        in_specs=[pl.BlockSpec(memory_space=pl.ANY),
                  pl.BlockSpec((1, d, de), wmap),
                  pl.BlockSpec((1, d, de), wmap),
                  pl.BlockSpec((1, de, d), wmap)],
        out_specs=pl.BlockSpec((rows, d), lambda i, *_: (i, 0)),
        scratch_shapes=[pltpu.VMEM((2, rows, dp), jnp.uint32), pltpu.SemaphoreType.DMA((2,))],
    )
    return pl.pallas_call(
        _moe_kernel,
        grid_spec=grid_spec,
        out_shape=jax.ShapeDtypeStruct((n_rows, d), F32),
        compiler_params=_params(("arbitrary",), 56),
        name="moe_experts",
    )(tok_buf, block_expert, n_active, h2, w_gate, w_up, w_down)


def _combine_kernel(dest_ref, ys_hbm, rt_ref, x1_ref, gate_ref, lng_ref, lnb_ref, o_ref, ybuf, sem, *, alpha):
    i = pl.program_id(0)
    n = pl.num_programs(0)
    tm = x1_ref.shape[0]
    n_copies = TOP_K * tm

    def gather(blk, slot, start):
        if not start:
            pltpu.make_async_copy(ys_hbm.at[pl.ds(0, n_copies)], ybuf.at[slot], sem.at[slot]).wait()
            return

        def body(r, carry):
            row = dest_ref[blk * n_copies + r]
            pltpu.make_async_copy(ys_hbm.at[pl.ds(row, 1)], ybuf.at[slot, pl.ds(r, 1)], sem.at[slot]).start()
            return carry
        lax.fori_loop(0, n_copies, body, 0, unroll=GATHER_UNROLL)

    @pl.when(i == 0)
    def _():
        gather(0, 0, True)

    @pl.when(i + 1 < n)
    def _():
        gather(i + 1, (i + 1) % 2, True)

    slot = i % 2
    gather(i, slot, False)
    moe = rt_ref[:, ROUTE_W:ROUTE_W + 1] * ybuf[slot, :tm, :]
    for kk in range(1, TOP_K):
        moe = moe + rt_ref[:, ROUTE_W + kk:ROUTE_W + kk + 1] * ybuf[slot, kk * tm:(kk + 1) * tm, :]
    r = alpha * x1_ref[...] + (1.0 + gate_ref[0]) * moe
    mu = jnp.mean(r, axis=-1, keepdims=True)
    var = jnp.mean(jnp.square(r - mu), axis=-1, keepdims=True)
    o_ref[...] = (r - mu) * lax.rsqrt(var + LN_EPS) * lng_ref[...] + lnb_ref[...]


def moe_combine(ys, dest, table, x1, gate2, ln_g, ln_b, alpha, seq):
    t, d = x1.shape
    tm = 128
    tiles_per_seq = seq // tm
    grid_spec = pltpu.PrefetchScalarGridSpec(
        num_scalar_prefetch=1,
        grid=(t // tm,),
        in_specs=[pl.BlockSpec(memory_space=pl.ANY),
                  pl.BlockSpec((tm, LANES), lambda i, *_: (i, 0)),
                  pl.BlockSpec((tm, d), lambda i, *_: (i, 0)),
                  pl.BlockSpec((1, 1, d), lambda i, *_: (i // tiles_per_seq, 0, 0)),
                  pl.BlockSpec((1, d), lambda i, *_: (0, 0)),
                  pl.BlockSpec((1, d), lambda i, *_: (0, 0))],
        out_specs=pl.BlockSpec((tm, d), lambda i, *_: (i, 0)),
        scratch_shapes=[pltpu.VMEM((2, TOP_K * tm, d), F32), pltpu.SemaphoreType.DMA((2,))],
    )
    return pl.pallas_call(
        functools.partial(_combine_kernel, alpha=alpha),
        grid_spec=grid_spec,
        out_shape=jax.ShapeDtypeStruct((t, d), F32),
        compiler_params=_params(("arbitrary",), 32),
        name="moe_combine",
    )(dest, ys, table, x1, gate2, ln_g.reshape(1, d), ln_b.reshape(1, d))


def _route_kernel(lg_ref, b_ref, o_ref, cnt_ref, carry):
    i = pl.program_id(0)

    @pl.when(i == 0)
    def _():
        carry[...] = jnp.zeros_like(carry)

    tm = lg_ref.shape[0]
    x = lg_ref[...] + b_ref[...]
    lane = lax.broadcasted_iota(jnp.int32, (tm, LANES), 1)
    ninf = -jnp.inf

    def top(v):
        vmax = jnp.max(v, axis=1, keepdims=True)
        return vmax, jnp.min(jnp.where(v == vmax, lane, LANES), axis=1, keepdims=True)

    gmask = lane < N_EXPERT_GROUPS
    gmax, g_sel = top(jnp.where(gmask, x, ninf))
    p_group = 1.0 / jnp.sum(jnp.where(gmask, jnp.exp(x - gmax), 0.0), axis=1, keepdims=True)

    lo = N_EXPERT_GROUPS + g_sel * EXPERTS_PER_GROUP
    cur = jnp.where(jnp.logical_and(lane >= lo, lane < lo + EXPERTS_PER_GROUP), x, ninf)
    vals, idxs = [], []
    for _ in range(TOP_K):
        v, ix = top(cur)
        vals.append(v)
        idxs.append(ix)
        cur = jnp.where(lane == ix, ninf, cur)
    exps = [jnp.exp(v - vals[0]) for v in vals]
    den = functools.reduce(lambda a, c: a + c, exps)

    member = functools.reduce(jnp.logical_or, [lane == ix for ix in idxs])
    mf = jnp.where(member, 1.0, 0.0)
    r_i = lax.broadcasted_iota(jnp.int32, (tm, tm), 0)
    c_i = lax.broadcasted_iota(jnp.int32, (tm, tm), 1)
    before = _dot((c_i < r_i).astype(BF16), mf.astype(BF16)) + carry[...]
    carry[...] = carry[...] + jnp.sum(mf, axis=0, keepdims=True)
    cnt_ref[...] = carry[...]

    out = jnp.zeros((tm, LANES), F32)
    for k in range(TOP_K):
        rank = jnp.sum(jnp.where(lane == idxs[k], before, 0.0), axis=1, keepdims=True)
        out = jnp.where(lane == ROUTE_EID + k, (idxs[k] - N_EXPERT_GROUPS).astype(F32), out)
        out = jnp.where(lane == ROUTE_RANK + k, rank, out)
        out = jnp.where(lane == ROUTE_W + k, p_group * (exps[k] / den), out)
    o_ref[...] = out


def route(logits, b_rg, b_re, n_rows, tm_combine):
    n_tok = logits.shape[0]
    n_experts = N_EXPERT_GROUPS * EXPERTS_PER_GROUP
    bias = jnp.zeros((1, LANES), F32).at[0, :N_EXPERT_GROUPS].set(b_rg)
    bias = bias.at[0, N_EXPERT_GROUPS:N_EXPERT_GROUPS + n_experts].set(b_re)
    tm = 512
    table, cnt = pl.pallas_call(
        _route_kernel,
        grid=(n_tok // tm,),
        in_specs=[pl.BlockSpec((tm, LANES), lambda i: (i, 0)),
                  pl.BlockSpec((1, LANES), lambda i: (0, 0))],
        out_specs=[pl.BlockSpec((tm, LANES), lambda i: (i, 0)),
                   pl.BlockSpec((1, LANES), lambda i: (0, 0))],
        out_shape=[jax.ShapeDtypeStruct((n_tok, LANES), F32), jax.ShapeDtypeStruct((1, LANES), F32)],
        scratch_shapes=[pltpu.VMEM((1, LANES), F32)],
        compiler_params=_params(("arbitrary",), 16),
        name="route",
    )(logits, bias)

    eid = table[:, ROUTE_EID:ROUTE_EID + TOP_K].astype(jnp.int32)
    rank = table[:, ROUTE_RANK:ROUTE_RANK + TOP_K].astype(jnp.int32)
    counts = cnt[0, N_EXPERT_GROUPS:N_EXPERT_GROUPS + n_experts].astype(jnp.int32)
    padded = ((counts + MOE_ROWS - 1) // MOE_ROWS) * MOE_ROWS
    pends = jnp.cumsum(padded)
    pstarts = pends - padded
    onehot = eid[:, :, None] == jnp.arange(n_experts, dtype=jnp.int32)
    dest = jnp.sum(jnp.where(onehot, pstarts, 0), axis=-1) + rank
    tok = jnp.broadcast_to(jnp.arange(n_tok, dtype=jnp.int32)[:, None], (n_tok, TOP_K))
    tok_buf = jnp.zeros((n_rows,), jnp.int32).at[dest.reshape(-1)].set(tok.reshape(-1), unique_indices=True)
    n_blocks = n_rows // MOE_ROWS
    block_expert = jnp.clip(
        jnp.searchsorted(pends, jnp.arange(n_blocks, dtype=jnp.int32) * MOE_ROWS, side='right'),
        0, n_experts - 1).astype(jnp.int32)
    n_active = (pends[-1] // MOE_ROWS).astype(jnp.int32).reshape(1)
    dest_tiles = dest.reshape(n_tok // tm_combine, tm_combine, TOP_K).transpose(0, 2, 1).reshape(-1)
    return table, tok_buf, block_expert, n_active, dest_tiles.astype(jnp.int32)


def _layer(x, c, w_ada, b_ada, w_in, b_forget, lam_re, lam_im, log_dt, b_re, b_im, c_re, c_im, d_skip,
           w_glu, b_glu, g_attn, g_ssm, w_out, ln1_g, ln1_b, w_rg, b_rg, w_re, b_re_r,
           w_gate, w_up, w_down, ln2_g, ln2_b, alpha):
    bsz, seq, d = x.shape
    n_tok = bsz * seq
    n_heads = b_forget.shape[0]
    d_att = n_heads * HEAD_DIM
    d_ssm = d_skip.shape[0] * d_skip.shape[1]

    mod = ada_mod(c, w_ada, b_ada).reshape(bsz, 1, -1)
    shift1, scale1, gate1, shift2, scale2, gate2 = jnp.split(mod, 6, axis=-1)

    n_qkv = 3 * d_att
    w_main = jnp.concatenate([w_in[:, :n_qkv], w_in[:, n_qkv + n_heads:]], axis=1).astype(BF16)
    w_f = jnp.zeros((d, LANES), BF16).at[:, :n_heads].set(w_in[:, n_qkv:n_qkv + n_heads].astype(BF16))
    qkv, u, f = in_proj(x, scale1, shift1, w_main, w_f, n_qkv)

    q_aug, k_aug = forget_cum(f, b_forget)
    attn, (w_glu_b, w_out_b, w_gate_b, w_up_b, w_down_b) = attention(
        qkv, q_aug, k_aug, n_heads, [w_glu, w_out, w_gate, w_up, w_down])

    w1, ft, al = ssm_prep(lam_re, lam_im, log_dt, b_re, b_im, c_re, c_im)
    y = ssm_scan(u, w1, ft, al, d_skip)
    ssm_n = glu_norm(y.reshape(n_tok, d_ssm), w_glu_b, b_glu, g_ssm).reshape(bsz, seq, d_ssm)

    n_experts = N_EXPERT_GROUPS * EXPERTS_PER_GROUP
    w_r = jnp.zeros((d, LANES), F32).at[:, :N_EXPERT_GROUPS].set(w_rg)
    w_r = w_r.at[:, N_EXPERT_GROUPS:N_EXPERT_GROUPS + n_experts].set(w_re)
    wr_hi = w_r.astype(BF16)
    wr_lo = (w_r - wr_hi.astype(F32)).astype(BF16)
    x1, h2, logits = out_proj(attn, g_attn, ssm_n, w_out_b, x, gate1, ln1_g, ln1_b,
                              scale2, shift2, wr_hi, wr_lo, alpha)

    n_assign = n_tok * TOP_K
    n_blocks = -(-(n_assign + n_experts * (MOE_ROWS - 1)) // MOE_ROWS)
    n_rows = n_blocks * MOE_ROWS
    table, tok_buf, block_expert, n_active, dest = route(logits.reshape(n_tok, LANES), b_rg, b_re_r, n_rows, 128)
    ys = moe_experts(h2.reshape(n_tok, d // 2), tok_buf, block_expert, n_active,
                     w_gate_b, w_up_b, w_down_b)
    out = moe_combine(ys, dest, table, x1.reshape(n_tok, d), gate2, ln2_g, ln2_b, alpha, seq)
    return out.reshape(bsz, seq, d)


def kernel(x, c, w_ada, b_ada, w_in, b_forget, ssm_lambda_re, ssm_lambda_im, ssm_log_dt, ssm_b_re, ssm_b_im,
           ssm_c_re, ssm_c_im, ssm_d, w_glu, b_glu, g_attn, g_ssm, w_out, ln1_g, ln1_b, w_router_group,
           b_router_group, w_router_expert, b_router_expert, w_gate, w_up, w_down, ln2_g, ln2_b):
    depth = w_ada.shape[0]
    alpha = (2.0 * depth) ** 0.25
    for l in range(depth):
        x = _layer(x, c, w_ada[l], b_ada[l], w_in[l], b_forget[l], ssm_lambda_re[l], ssm_lambda_im[l],
                   ssm_log_dt[l], ssm_b_re[l], ssm_b_im[l], ssm_c_re[l], ssm_c_im[l], ssm_d[l],
                   w_glu[l], b_glu[l], g_attn[l], g_ssm[l], w_out[l], ln1_g[l], ln1_b[l],
                   w_router_group[l], b_router_group[l], w_router_expert[l], b_router_expert[l],
                   w_gate[l], w_up[l], w_down[l], ln2_g[l], ln2_b[l], alpha)
    return x
```

```python
import functools
import math

import jax
import jax.numpy as jnp
from jax import lax
from jax.experimental import pallas as pl
from jax.experimental.pallas import tpu as pltpu

F32 = jnp.float32
BF16 = jnp.bfloat16

LANES = 128
HEAD_DIM = 128
SSM_GROUP = 16
SSM_STATE = 64
GROUPS_PER_SLAB = LANES // SSM_GROUP
SLAB_STATE = GROUPS_PER_SLAB * SSM_STATE
SSM_CHUNK = 16
SSM_COLS = 512
N_EXPERT_GROUPS = 8
EXPERTS_PER_GROUP = 8
TOP_K = 2
MOE_ROWS = 256
GATHER_UNROLL = 8
ROW_CHUNK = 128
ROUTE_EID, ROUTE_RANK, ROUTE_W = 0, TOP_K, 2 * TOP_K
LN_EPS = 1e-5
RMS_EPS = 1e-6
NEG_BIG = -1e30
LOG2E = math.log2(math.e)
MIB = 1024 * 1024

_NT = (((1,), (1,)), ((), ()))


def _params(semantics, vmem_mib):
    return pltpu.CompilerParams(dimension_semantics=semantics, vmem_limit_bytes=vmem_mib * MIB)


def _dot(a, b):
    return jnp.dot(a, b, preferred_element_type=F32)


def _ada_kernel(c_ref, w_ref, b_ref, o_ref):
    s = jax.nn.silu(c_ref[...]).astype(BF16)
    o_ref[...] = _dot(s, w_ref[...].astype(BF16)) + b_ref[...]


def ada_mod(c, w_ada, b_ada):
    bsz, d = c.shape
    n = w_ada.shape[1]
    rows = 8
    assert bsz <= rows
    cp = jnp.zeros((rows, d), F32).at[:bsz].set(c)
    tn = 512
    out = pl.pallas_call(
        _ada_kernel,
        grid=(n // tn,),
        in_specs=[pl.BlockSpec((rows, d), lambda j: (0, 0)),
                  pl.BlockSpec((d, tn), lambda j: (0, j)),
                  pl.BlockSpec((1, tn), lambda j: (0, j))],
        out_specs=pl.BlockSpec((rows, tn), lambda j: (0, j)),
        out_shape=jax.ShapeDtypeStruct((rows, n), F32),
        compiler_params=_params(("arbitrary",), 40),
        name="ada_mod",
    )(cp, w_ada, b_ada.reshape(1, n))
    return out[:bsz]


def _inproj_kernel(x_ref, sc_ref, sh_ref, wqkv_ref, wu_ref, wf_ref, qkv_ref, u_ref, f_ref, h_scr, *,
                   n_q_tiles, n_qkv_tiles, q_scale):
    j = pl.program_id(2)

    @pl.when(j == 0)
    def _():
        hb = (x_ref[0] * (1.0 + sc_ref[0]) + sh_ref[0]).astype(BF16)
        h_scr[...] = hb
        f_ref[0] = _dot(hb, wf_ref[...])

    @pl.when(j < n_q_tiles)
    def _():
        qkv_ref[0] = (_dot(h_scr[...], wqkv_ref[...]) * q_scale).astype(BF16)

    @pl.when(jnp.logical_and(j >= n_q_tiles, j < n_qkv_tiles))
    def _():
        qkv_ref[0] = _dot(h_scr[...], wqkv_ref[...]).astype(BF16)

    @pl.when(j >= n_qkv_tiles)
    def _():
        u_ref[0] = _dot(h_scr[...], wu_ref[...])


def in_proj(x, scale, shift, w_qkv, w_u, w_f):
    bsz, seq, d = x.shape
    n_qkv = w_qkv.shape[1]
    n_u = w_u.shape[1]
    n_all = n_qkv + n_u
    tm, tn, tn_u = 512, 1024, 512
    nq = n_qkv // tn
    grid = (bsz, seq // tm, nq + n_u // tn_u)
    return pl.pallas_call(
        functools.partial(_inproj_kernel, n_q_tiles=n_qkv // 3 // tn, n_qkv_tiles=nq,
                          q_scale=HEAD_DIM ** -0.5 * LOG2E),
        grid=grid,
        in_specs=[pl.BlockSpec((1, tm, d), lambda b, i, j: (b, i, 0)),
                  pl.BlockSpec((1, 1, d), lambda b, i, j: (b, 0, 0)),
                  pl.BlockSpec((1, 1, d), lambda b, i, j: (b, 0, 0)),
                  pl.BlockSpec((d, tn), lambda b, i, j: (0, jnp.minimum(j, nq - 1))),
                  pl.BlockSpec((d, tn_u), lambda b, i, j: (0, jnp.maximum(j - nq, 0))),
                  pl.BlockSpec((d, LANES), lambda b, i, j: (0, 0))],
        out_specs=[pl.BlockSpec((1, tm, tn), lambda b, i, j: (b, i, jnp.minimum(j, nq - 1))),
                   pl.BlockSpec((1, tm, tn_u), lambda b, i, j: (b, i, jnp.maximum(j - nq, 0))),
                   pl.BlockSpec((1, tm, LANES), lambda b, i, j: (b, i, 0))],
        out_shape=[jax.ShapeDtypeStruct((bsz, seq, n_qkv), BF16),
                   jax.ShapeDtypeStruct((bsz, seq, n_u), F32),
                   jax.ShapeDtypeStruct((bsz, seq, LANES), F32)],
        scratch_shapes=[pltpu.VMEM((tm, d), BF16)],
        compiler_params=_params(("arbitrary", "arbitrary", "arbitrary"), 56),
        name="in_proj",
    )(x, scale, shift, w_qkv, w_u, w_f)


def _split3(x):
    p1 = x.astype(BF16)
    r1 = x - p1.astype(F32)
    p2 = r1.astype(BF16)
    p3 = (r1 - p2.astype(F32)).astype(BF16)
    return p1, p2, p3


N_PIECES = 3


def _cum_kernel(f_ref, b_ref, qa_ref, ka_ref, carry, *, n_heads):
    i = pl.program_id(1)

    @pl.when(i == 0)
    def _():
        carry[...] = jnp.zeros_like(carry)

    tc = f_ref.shape[1]
    lf = jax.nn.log_sigmoid(f_ref[0] + b_ref[...])
    row = lax.broadcasted_iota(jnp.int32, (tc, tc), 0)
    col = lax.broadcasted_iota(jnp.int32, (tc, tc), 1)
    tri = (col <= row).astype(BF16)
    p1, p2, p3 = _split3(lf)
    cs = _dot(tri, p1) + _dot(tri, p2) + _dot(tri, p3) + carry[...]
    carry[...] = cs[tc - 1:tc, :]

    pieces = jnp.concatenate(_split3(cs * LOG2E), axis=1)
    r = lax.broadcasted_iota(jnp.int32, (N_PIECES * LANES, LANES), 0)
    c = lax.broadcasted_iota(jnp.int32, (N_PIECES * LANES, LANES), 1)
    lane = lax.broadcasted_iota(jnp.int32, (tc, LANES), 1)
    ones_q = jnp.where(jnp.logical_and(lane >= N_PIECES, lane < 2 * N_PIECES), 1.0, 0.0)
    ones_k = jnp.where(lane < N_PIECES, 1.0, 0.0)
    for h in range(n_heads):
        sel_q = (r == c * LANES + h).astype(BF16)
        sel_k = (r == (c - N_PIECES) * LANES + h).astype(BF16)
        qa_ref[0, h] = (_dot(pieces, sel_q) + ones_q).astype(BF16)
        ka_ref[0, h] = (ones_k - _dot(pieces, sel_k)).astype(BF16)


def forget_cum(f, b_forget):
    bsz, seq, _ = f.shape
    n_heads = b_forget.shape[0]
    tc = 256
    bpad = jnp.zeros((1, LANES), F32).at[0, :n_heads].set(b_forget)
    out_spec = pl.BlockSpec((1, n_heads, tc, LANES), lambda b, i: (b, 0, i, 0))
    out_shape = jax.ShapeDtypeStruct((bsz, n_heads, seq, LANES), BF16)
    return pl.pallas_call(
        functools.partial(_cum_kernel, n_heads=n_heads),
        grid=(bsz, seq // tc),
        in_specs=[pl.BlockSpec((1, tc, LANES), lambda b, i: (b, i, 0)),
                  pl.BlockSpec((1, LANES), lambda b, i: (0, 0))],
        out_specs=[out_spec, out_spec],
        out_shape=[out_shape, out_shape],
        scratch_shapes=[pltpu.VMEM((1, LANES), F32)],
        compiler_params=_params(("arbitrary", "arbitrary"), 24),
        name="forget_cum",
    )(f, bpad)


ATT_QROWS = 1024
ATT_KEYS = 512


def _attn_kernel(*refs, n_cast):
    q_ref, qa_ref, k_ref, ka_ref, v_ref = refs[:5]
    src_refs = refs[5:5 + n_cast]
    o_ref = refs[5 + n_cast]
    dst_refs = refs[6 + n_cast:6 + 2 * n_cast]
    m_scr, acc_scr, s_scr = refs[6 + 2 * n_cast:]
    for src, dst in zip(src_refs, dst_refs):
        dst[...] = src[...].astype(BF16)

    tq = q_ref.shape[1]
    tk = ATT_KEYS
    n_groups = tq // tk
    qi = pl.program_id(2)
    q = jnp.concatenate([q_ref[0], qa_ref[0, 0]], axis=1)
    ones = jnp.ones((tk, HEAD_DIM), BF16)

    m_scr[...] = jnp.full_like(m_scr, NEG_BIG)
    acc_scr[...] = jnp.zeros_like(acc_scr)

    def scores(kb, slot, first_group=0):
        k0 = pl.multiple_of(kb * tk, tk)
        kt = jnp.concatenate([k_ref[0, pl.ds(k0, tk), :], ka_ref[0, 0, pl.ds(k0, tk), :]], axis=1)
        r0 = first_group * tk
        s_scr[slot, r0:, :] = lax.dot_general(q[r0:], kt, _NT, preferred_element_type=F32)

    def update(kb, slot, diag_group=None):
        k0 = pl.multiple_of(kb * tk, tk)
        vt = jnp.concatenate([v_ref[0, pl.ds(k0, tk), :], ones], axis=1)
        for g in range(n_groups):
            if diag_group is not None and g < diag_group:
                continue
            rs = slice(g * tk, (g + 1) * tk)
            s = s_scr[slot, rs, :]
            if g == diag_group:
                qpos = lax.broadcasted_iota(jnp.int32, (tk, tk), 0)
                kpos = lax.broadcasted_iota(jnp.int32, (tk, tk), 1)
                s = jnp.where(kpos <= qpos, s, NEG_BIG)
            m_prev = m_scr[rs]
            m_new = jnp.maximum(m_prev, jnp.max(s, axis=1, keepdims=True))
            alpha = jnp.exp2(m_prev - m_new)
            p = jnp.exp2(s - m_new)
            acc_scr[rs] = alpha * acc_scr[rs] + _dot(p.astype(BF16), vt)
            m_scr[rs] = m_new

    def body(i, carry):
        kb = n_groups * i
        for g in range(n_groups):
            scores(kb + g + 1, (g + 1) % 2)
            update(kb + g, g % 2)
        return carry

    assert n_groups % 2 == 0
    scores(0, 0)
    lax.fori_loop(0, qi, body, 0)
    kb = n_groups * qi
    for g in range(n_groups):
        if g + 1 < n_groups:
            scores(kb + g + 1, (g + 1) % 2, first_group=g + 1)
        update(kb + g, g % 2, diag_group=g)

    o_ref[0] = acc_scr[:, :HEAD_DIM] / acc_scr[:, HEAD_DIM:]


BF16_ROWS = 16


def _cast_chunks(w, n_steps):
    cols = w.shape[-1]
    total_rows = w.size // cols
    n_chunks = n_steps
    while total_rows % (n_chunks * BF16_ROWS):
        n_chunks //= 2
    return w.reshape(n_chunks, total_rows // n_chunks, cols)


def attention(qkv, q_aug, k_aug, n_heads, cast_weights):
    bsz, seq, _ = qkv.shape
    t = min(ATT_QROWS, seq)
    nq = seq // t
    n_steps = bsz * n_heads * nq
    srcs = [_cast_chunks(w, n_steps) for w in cast_weights]

    def chunk_spec(a):
        per = n_steps // a.shape[0]
        return pl.BlockSpec((1,) + a.shape[1:], lambda b, h, i: (((b * n_heads + h) * nq + i) // per, 0, 0))

    outs = pl.pallas_call(
        functools.partial(_attn_kernel, n_cast=len(srcs)),
        grid=(bsz, n_heads, nq),
        in_specs=[pl.BlockSpec((1, t, HEAD_DIM), lambda b, h, i: (b, i, h)),
                  pl.BlockSpec((1, 1, t, LANES), lambda b, h, i: (b, h, i, 0)),
                  pl.BlockSpec((1, seq, HEAD_DIM), lambda b, h, i: (b, 0, n_heads + h)),
                  pl.BlockSpec((1, 1, seq, LANES), lambda b, h, i: (b, h, 0, 0)),
                  pl.BlockSpec((1, seq, HEAD_DIM), lambda b, h, i: (b, 0, 2 * n_heads + h))]
                 + [chunk_spec(a) for a in srcs],
        out_specs=[pl.BlockSpec((1, t, HEAD_DIM), lambda b, h, i: (b, i, h))] + [chunk_spec(a) for a in srcs],
        out_shape=[jax.ShapeDtypeStruct((bsz, seq, n_heads * HEAD_DIM), F32)]
                  + [jax.ShapeDtypeStruct(a.shape, BF16) for a in srcs],
        scratch_shapes=[pltpu.VMEM((t, 1), F32), pltpu.VMEM((t, 2 * HEAD_DIM), F32),
                        pltpu.VMEM((2, t, ATT_KEYS), F32)],
        compiler_params=_params(("arbitrary", "arbitrary", "arbitrary"), 56),
        name="attention",
    )(qkv, q_aug, qkv, k_aug, qkv, *srcs)
    return outs[0], [o.reshape(w.shape) for o, w in zip(outs[1:], cast_weights)]


def _blockdiag(p):
    g, c, n = p.shape
    ns = g // GROUPS_PER_SLAB
    eye = jnp.eye(GROUPS_PER_SLAB, dtype=p.dtype)
    out = p.reshape(ns, GROUPS_PER_SLAB, c, 1, n) * eye[None, :, None, :, None]
    return out.reshape(ns, GROUPS_PER_SLAB * c, GROUPS_PER_SLAB * n)


def _ssm_prep_kernel(lr_ref, li_ref, ldt_ref, bre_ref, bim_ref, cre_ref, cim_ref, w1_ref, ft_ref, al_ref):
    L = SSM_CHUNK
    lr = lr_ref[0]
    li = li_ref[0]
    dt = jnp.exp(ldt_ref[0])
    mag = jnp.exp(lr * dt)
    a_re = mag * jnp.cos(li * dt)
    a_im = mag * jnp.sin(li * dt)
    den = lr * lr + li * li
    z_re = ((a_re - 1.0) * lr + a_im * li) / den
    z_im = (a_im * lr - (a_re - 1.0) * li) / den
    br = bre_ref[0]
    bi = bim_ref[0]
    bb_re = z_re * br - z_im * bi
    bb_im = z_re * bi + z_im * br
    cr = cre_ref[0]
    ci = cim_ref[0]
    ft0 = jnp.concatenate([cr, -ci], axis=1)

    def power(d):
        m = jnp.exp(lr * dt * d)
        return m * jnp.cos(li * dt * d), m * jnp.sin(li * dt * d)

    w1_ref[0, :, :L * LANES] = jnp.zeros((L * LANES, L * LANES), BF16)
    for d in range(L):
        pr, pi = power(float(d))
        xe = jnp.concatenate([bb_re * pr - bb_im * pi, bb_re * pi + bb_im * pr], axis=1)
        j = L - 1 - d
        w1_ref[0, j * LANES:(j + 1) * LANES, L * LANES:] = xe.astype(BF16)
        m_d = lax.dot_general(xe, ft0, _NT, preferred_element_type=F32,
                              precision=lax.Precision.HIGHEST).astype(BF16)
        for jj in range(L - d):
            w1_ref[0, jj * LANES:(jj + 1) * LANES, (jj + d) * LANES:(jj + d + 1) * LANES] = m_d
        pr1, pi1 = power(float(d + 1))
        ft_ref[0, d * LANES:(d + 1) * LANES, :] = jnp.concatenate(
            [cr * pr1 - ci * pi1, -(cr * pi1 + ci * pr1)], axis=1).astype(BF16)
    prl, pil = power(float(L))
    al_ref[0] = jnp.concatenate([prl, pil], axis=1)


def ssm_prep(lam_re, lam_im, log_dt, b_re, b_im, c_re, c_im):
    g, n = lam_re.shape
    ns = g // GROUPS_PER_SLAB
    L = SSM_CHUNK
    rowvec = lambda a: a.reshape(ns, 1, SLAB_STATE)
    args = (rowvec(lam_re), rowvec(lam_im), rowvec(jnp.repeat(log_dt, n)),
            _blockdiag(b_re.transpose(0, 2, 1)), _blockdiag(b_im.transpose(0, 2, 1)),
            _blockdiag(c_re), _blockdiag(c_im))
    vec_spec = pl.BlockSpec((1, 1, SLAB_STATE), lambda s: (s, 0, 0))
    mat_spec = pl.BlockSpec((1, LANES, SLAB_STATE), lambda s: (s, 0, 0))
    return pl.pallas_call(
        _ssm_prep_kernel,
        grid=(ns,),
        in_specs=[vec_spec] * 3 + [mat_spec] * 4,
        out_specs=[pl.BlockSpec((1, L * LANES, L * LANES + 2 * SLAB_STATE), lambda s: (s, 0, 0)),
                   pl.BlockSpec((1, L * LANES, 2 * SLAB_STATE), lambda s: (s, 0, 0)),
                   pl.BlockSpec((1, 1, 2 * SLAB_STATE), lambda s: (s, 0, 0))],
        out_shape=[jax.ShapeDtypeStruct((ns, L * LANES, L * LANES + 2 * SLAB_STATE), BF16),
                   jax.ShapeDtypeStruct((ns, L * LANES, 2 * SLAB_STATE), BF16),
                   jax.ShapeDtypeStruct((ns, 1, 2 * SLAB_STATE), F32)],
        compiler_params=_params(("arbitrary",), 48),
        name="ssm_prep",
    )(*args)


def _ssm_kernel(u_ref, w1_ref, ft_ref, al_ref, d_ref, y_ref, uf_scr, e_scr, y_scr):
    L = SSM_CHUNK
    nch = uf_scr.shape[0]
    lc = L * LANES
    for j in range(L):
        uf_scr[:, j * LANES:(j + 1) * LANES] = u_ref[0, pl.ds(j, nch, stride=L), :].astype(BF16)
    uf = uf_scr[...]
    e_scr[...] = _dot(uf, w1_ref[0, :, lc:])

    a_re = al_ref[0, :, :SLAB_STATE]
    a_im = al_ref[0, :, SLAB_STATE:]
    h_re = jnp.zeros((1, SLAB_STATE), F32)
    h_im = jnp.zeros((1, SLAB_STATE), F32)
    for k in range(nch):
        e = e_scr[k:k + 1, :]
        e_scr[k:k + 1, :] = jnp.concatenate([h_re, h_im], axis=1)
        h_re, h_im = (a_re * h_re - a_im * h_im + e[:, :SLAB_STATE],
                      a_re * h_im + a_im * h_re + e[:, SLAB_STATE:])

    for c in range(lc // SSM_COLS):
        k_hi = (c + 1) * SSM_COLS
        cols = slice(c * SSM_COLS, k_hi)
        y_scr[:, cols] = _dot(uf[:, :k_hi], w1_ref[0, :k_hi, cols])
    y = y_scr[...] + lax.dot_general(e_scr[...].astype(BF16), ft_ref[0], _NT, preferred_element_type=F32)
    for i in range(L):
        yi = y[:, i * LANES:(i + 1) * LANES] + d_ref[0] * u_ref[0, pl.ds(i, nch, stride=L), :]
        y_ref[0, pl.ds(i, nch, stride=L), :] = jax.nn.gelu(yi)


def ssm_scan(u, w1, ft, al, d_skip):
    bsz, seq, c = u.shape
    ns = c // LANES
    L = SSM_CHUNK
    nch = seq // L
    lc = L * LANES
    once = pl.Buffered(1)
    return pl.pallas_call(
        _ssm_kernel,
        grid=(ns, bsz),
        in_specs=[pl.BlockSpec((1, seq, LANES), lambda s, b: (b, 0, s)),
                  pl.BlockSpec((1, lc, lc + 2 * SLAB_STATE), lambda s, b: (s, 0, 0), pipeline_mode=once),
                  pl.BlockSpec((1, lc, 2 * SLAB_STATE), lambda s, b: (s, 0, 0), pipeline_mode=once),
                  pl.BlockSpec((1, 1, 2 * SLAB_STATE), lambda s, b: (s, 0, 0)),
                  pl.BlockSpec((1, 1, LANES), lambda s, b: (s, 0, 0))],
        out_specs=pl.BlockSpec((1, seq, LANES), lambda s, b: (b, 0, s)),
        out_shape=jax.ShapeDtypeStruct((bsz, seq, c), F32),
        scratch_shapes=[pltpu.VMEM((nch, lc), BF16), pltpu.VMEM((nch, 2 * SLAB_STATE), F32),
                        pltpu.VMEM((nch, lc), F32)],
        compiler_params=_params(("arbitrary", "arbitrary"), 56),
        name="ssm_scan",
    )(u, w1, ft, al, d_skip.reshape(ns, 1, LANES))


def _glu_kernel(y_ref, w_ref, b_ref, g_ref, o_ref):
    y = y_ref[...]
    o = y * jax.nn.sigmoid(_dot(y.astype(BF16), w_ref[...]) + b_ref[...])
    ms = jnp.mean(o * o, axis=-1, keepdims=True)
    o_ref[...] = (o * lax.rsqrt(ms + RMS_EPS) * g_ref[...]).astype(BF16)


def glu_norm(y, w_glu, b_glu, g):
    t, c = y.shape
    tm = 512
    return pl.pallas_call(
        _glu_kernel,
        grid=(t // tm,),
        in_specs=[pl.BlockSpec((tm, c), lambda i: (i, 0)),
                  pl.BlockSpec((c, c), lambda i: (0, 0)),
                  pl.BlockSpec((1, c), lambda i: (0, 0)),
                  pl.BlockSpec((1, c), lambda i: (0, 0))],
        out_specs=pl.BlockSpec((tm, c), lambda i: (i, 0)),
        out_shape=jax.ShapeDtypeStruct((t, c), BF16),
        compiler_params=_params(("arbitrary",), 48),
        name="glu_norm",
    )(y, w_glu, b_glu.reshape(1, c), g.reshape(1, c))


def _outproj_kernel(attn_ref, ga_ref, ssm_ref, w_ref, x_ref, gate_ref, lng_ref, lnb_ref, sc2_ref, sh2_ref,
                    wrh_ref, wrl_ref, x1_ref, h2_ref, lg_ref, a_scr, *, alpha, n_att):
    j = pl.program_id(2)
    nj = pl.num_programs(2)
    tn = w_ref.shape[1]

    tm = x1_ref.shape[1]
    n_chunks = tm // ROW_CHUNK

    @pl.when(j == 0)
    def _():
        def norm(c, carry):
            rs = pl.ds(pl.multiple_of(c * ROW_CHUNK, ROW_CHUNK), ROW_CHUNK)
            a = attn_ref[0, rs, :]
            ms = jnp.mean(a * a, axis=-1, keepdims=True)
            a_scr[rs, :n_att] = (a * lax.rsqrt(ms + RMS_EPS) * ga_ref[...]).astype(BF16)
            return carry
        lax.fori_loop(0, n_chunks, norm, 0)
        a_scr[:, n_att:] = ssm_ref[0]

    mixed = _dot(a_scr[...], w_ref[...])
    col = pl.multiple_of(j * tn, tn)
    x1_ref[0, :, pl.ds(col, tn)] = alpha * x_ref[0] + (1.0 + gate_ref[0]) * mixed

    @pl.when(j == nj - 1)
    def _():
        def finish(c, carry):
            rs = pl.ds(pl.multiple_of(c * ROW_CHUNK, ROW_CHUNK), ROW_CHUNK)
            r = x1_ref[0, rs, :]
            mu = jnp.mean(r, axis=-1, keepdims=True)
            var = jnp.mean(jnp.square(r - mu), axis=-1, keepdims=True)
            x1 = (r - mu) * lax.rsqrt(var + LN_EPS) * lng_ref[...] + lnb_ref[...]
            x1_ref[0, rs, :] = x1
            h2 = x1 * (1.0 + sc2_ref[0]) + sh2_ref[0]
            hi = h2.astype(BF16)
            hi_f = hi.astype(F32)
            lo = (h2 - hi_f).astype(BF16)
            lg_ref[0, rs, :] = _dot(hi, wrh_ref[...]) + _dot(hi, wrl_ref[...]) + _dot(lo, wrh_ref[...])
            h2_ref[0, rs, :] = pack_bf16_pairs(hi_f)
            return carry
        lax.fori_loop(0, n_chunks, finish, 0)


def pack_bf16_pairs(x):
    n = x.shape[-1] // 2
    bits = lax.bitcast_convert_type(x, jnp.uint32)
    return bits[:, n:] | (bits[:, :n] >> 16)


def unpack_bf16_pairs(p):
    lo = lax.bitcast_convert_type(p << 16, F32).astype(BF16)
    hi = lax.bitcast_convert_type(p & jnp.uint32(0xFFFF0000), F32).astype(BF16)
    return lo, hi


def out_proj(attn, g_attn, ssm_n, w_out, x, gate1, ln_g, ln_b, scale2, shift2, wr_hi, wr_lo, alpha):
    bsz, seq, d = x.shape
    n_att = attn.shape[-1]
    n_ssm = ssm_n.shape[-1]
    k = n_att + n_ssm
    tm, tn = 512, 256
    row = lambda a: a.reshape(1, -1)
    full = lambda n: pl.BlockSpec((1, n), lambda b, i, j: (0, 0))
    return pl.pallas_call(
        functools.partial(_outproj_kernel, alpha=alpha, n_att=n_att),
        grid=(bsz, seq // tm, d // tn),
        in_specs=[pl.BlockSpec((1, tm, n_att), lambda b, i, j: (b, i, 0)),
                  full(n_att),
                  pl.BlockSpec((1, tm, n_ssm), lambda b, i, j: (b, i, 0)),
                  pl.BlockSpec((k, tn), lambda b, i, j: (0, j)),
                  pl.BlockSpec((1, tm, tn), lambda b, i, j: (b, i, j)),
                  pl.BlockSpec((1, 1, tn), lambda b, i, j: (b, 0, j)),
                  full(d), full(d),
                  pl.BlockSpec((1, 1, d), lambda b, i, j: (b, 0, 0)),
                  pl.BlockSpec((1, 1, d), lambda b, i, j: (b, 0, 0)),
                  pl.BlockSpec((d, LANES), lambda b, i, j: (0, 0)),
                  pl.BlockSpec((d, LANES), lambda b, i, j: (0, 0))],
        out_specs=[pl.BlockSpec((1, tm, d), lambda b, i, j: (b, i, 0)),
                   pl.BlockSpec((1, tm, d // 2), lambda b, i, j: (b, i, 0)),
                   pl.BlockSpec((1, tm, LANES), lambda b, i, j: (b, i, 0))],
        out_shape=[jax.ShapeDtypeStruct((bsz, seq, d), F32),
                   jax.ShapeDtypeStruct((bsz, seq, d // 2), jnp.uint32),
                   jax.ShapeDtypeStruct((bsz, seq, LANES), F32)],
        scratch_shapes=[pltpu.VMEM((tm, k), BF16)],
        compiler_params=_params(("arbitrary", "arbitrary", "arbitrary"), 56),
        name="out_proj",
    )(attn, row(g_attn), ssm_n, w_out, x, gate1, row(ln_g), row(ln_b), scale2, shift2, wr_hi, wr_lo)


def _moe_kernel(tok_ref, bexp_ref, nact_ref, h_hbm, wg_ref, wu_ref, wd_ref, y_ref, xbuf, sem):
    i = pl.program_id(0)
    nact = nact_ref[0]
    rows = xbuf.shape[1]

    def start_gather(blk, slot):
        def body(r, carry):
            tok = tok_ref[blk * rows + r]
            pltpu.make_async_copy(h_hbm.at[pl.ds(tok, 1)], xbuf.at[slot, pl.ds(r, 1)], sem.at[slot]).start()
            return carry
        lax.fori_loop(0, rows, body, 0, unroll=GATHER_UNROLL)

    def wait_gather(slot):
        pltpu.make_async_copy(h_hbm.at[pl.ds(0, rows)], xbuf.at[slot], sem.at[slot]).wait()

    @pl.when(jnp.logical_and(i == 0, nact > 0))
    def _():
        start_gather(0, 0)

    @pl.when(i + 1 < nact)
    def _():
        start_gather(i + 1, (i + 1) % 2)

    @pl.when(i < nact)
    def _():
        slot = i % 2
        wait_gather(slot)
        x_lo, x_hi = unpack_bf16_pairs(xbuf[slot])
        half = x_lo.shape[1]
        g = _dot(x_lo, wg_ref[0, :half, :]) + _dot(x_hi, wg_ref[0, half:, :])
        u = _dot(x_lo, wu_ref[0, :half, :]) + _dot(x_hi, wu_ref[0, half:, :])
        act = (jax.nn.silu(g) * u).astype(BF16)
        y_ref[...] = _dot(act, wd_ref[0])

    @pl.when(i >= nact)
    def _():
        y_ref[...] = jnp.zeros_like(y_ref)


def moe_experts(h2, tok_buf, block_expert, n_active, w_gate, w_up, w_down):
    t, dp = h2.shape
    d = 2 * dp
    n_rows = tok_buf.shape[0]
    rows = MOE_ROWS
    n_blocks = n_rows // rows
    de = w_gate.shape[-1]

    def wmap(i, tok, bexp, nact):
        return (bexp[jnp.minimum(i, jnp.maximum(nact[0] - 1, 0))], 0, 0)

    grid_spec = pltpu.PrefetchScalarGridSpec(
        num_scalar_prefetch=3,
        grid=(n_blocks,),
        in_specs=[pl.BlockSpec(memory_space=pl.ANY),
                  pl.BlockSpec((1, d, de), wmap),
                  pl.BlockSpec((1, d, de), wmap),
                  pl.BlockSpec((1, de, d), wmap)],
        out_specs=pl.BlockSpec((rows, d), lambda i, *_: (i, 0)),
        scratch_shapes=[pltpu.VMEM((2, rows, dp), jnp.uint32), pltpu.SemaphoreType.DMA((2,))],
    )
    return pl.pallas_call(
        _moe_kernel,
        grid_spec=grid_spec,
        out_shape=jax.ShapeDtypeStruct((n_rows, d), F32),
        compiler_params=_params(("arbitrary",), 56),
        name="moe_experts",
    )(tok_buf, block_expert, n_active, h2, w_gate, w_up, w_down)


def _combine_kernel(dest_ref, ys_hbm, rt_ref, x1_ref, gate_ref, lng_ref, lnb_ref, o_ref, ybuf, sem, *, alpha):
    i = pl.program_id(0)
    n = pl.num_programs(0)
    tm = x1_ref.shape[0]
    n_copies = TOP_K * tm

    def gather(blk, slot, start):
        if not start:
            pltpu.make_async_copy(ys_hbm.at[pl.ds(0, n_copies)], ybuf.at[slot], sem.at[slot]).wait()
            return

        def body(r, carry):
            row = dest_ref[blk * n_copies + r]
            pltpu.make_async_copy(ys_hbm.at[pl.ds(row, 1)], ybuf.at[slot, pl.ds(r, 1)], sem.at[slot]).start()
            return carry
        lax.fori_loop(0, n_copies, body, 0, unroll=GATHER_UNROLL)

    @pl.when(i == 0)
    def _():
        gather(0, 0, True)

    @pl.when(i + 1 < n)
    def _():
        gather(i + 1, (i + 1) % 2, True)

    slot = i % 2
    gather(i, slot, False)
    moe = rt_ref[:, ROUTE_W:ROUTE_W + 1] * ybuf[slot, :tm, :]
    for kk in range(1, TOP_K):
        moe = moe + rt_ref[:, ROUTE_W + kk:ROUTE_W + kk + 1] * ybuf[slot, kk * tm:(kk + 1) * tm, :]
    r = alpha * x1_ref[...] + (1.0 + gate_ref[0]) * moe
    mu = jnp.mean(r, axis=-1, keepdims=True)
    var = jnp.mean(jnp.square(r - mu), axis=-1, keepdims=True)
    o_ref[...] = (r - mu) * lax.rsqrt(var + LN_EPS) * lng_ref[...] + lnb_ref[...]


def moe_combine(ys, dest, table, x1, gate2, ln_g, ln_b, alpha, seq):
    t, d = x1.shape
    tm = 128
    tiles_per_seq = seq // tm
    grid_spec = pltpu.PrefetchScalarGridSpec(
        num_scalar_prefetch=1,
        grid=(t // tm,),
        in_specs=[pl.BlockSpec(memory_space=pl.ANY),
                  pl.BlockSpec((tm, LANES), lambda i, *_: (i, 0)),
                  pl.BlockSpec((tm, d), lambda i, *_: (i, 0)),
                  pl.BlockSpec((1, 1, d), lambda i, *_: (i // tiles_per_seq, 0, 0)),
                  pl.BlockSpec((1, d), lambda i, *_: (0, 0)),
                  pl.BlockSpec((1, d), lambda i, *_: (0, 0))],
        out_specs=pl.BlockSpec((tm, d), lambda i, *_: (i, 0)),
        scratch_shapes=[pltpu.VMEM((2, TOP_K * tm, d), F32), pltpu.SemaphoreType.DMA((2,))],
    )
    return pl.pallas_call(
        functools.partial(_combine_kernel, alpha=alpha),
        grid_spec=grid_spec,
        out_shape=jax.ShapeDtypeStruct((t, d), F32),
        compiler_params=_params(("arbitrary",), 32),
        name="moe_combine",
    )(dest, ys, table, x1, gate2, ln_g.reshape(1, d), ln_b.reshape(1, d))


def _route_kernel(lg_ref, b_ref, o_ref, cnt_ref, carry):
    i = pl.program_id(0)

    @pl.when(i == 0)
    def _():
        carry[...] = jnp.zeros_like(carry)

    tm = lg_ref.shape[0]
    x = lg_ref[...] + b_ref[...]
    lane = lax.broadcasted_iota(jnp.int32, (tm, LANES), 1)
    ninf = -jnp.inf

    def top(v):
        vmax = jnp.max(v, axis=1, keepdims=True)
        return vmax, jnp.min(jnp.where(v == vmax, lane, LANES), axis=1, keepdims=True)

    gmask = lane < N_EXPERT_GROUPS
    gmax, g_sel = top(jnp.where(gmask, x, ninf))
    p_group = 1.0 / jnp.sum(jnp.where(gmask, jnp.exp(x - gmax), 0.0), axis=1, keepdims=True)

    lo = N_EXPERT_GROUPS + g_sel * EXPERTS_PER_GROUP
    cur = jnp.where(jnp.logical_and(lane >= lo, lane < lo + EXPERTS_PER_GROUP), x, ninf)
    vals, idxs = [], []
    for _ in range(TOP_K):
        v, ix = top(cur)
        vals.append(v)
        idxs.append(ix)
        cur = jnp.where(lane == ix, ninf, cur)
    exps = [jnp.exp(v - vals[0]) for v in vals]
    den = functools.reduce(lambda a, c: a + c, exps)

    member = functools.reduce(jnp.logical_or, [lane == ix for ix in idxs])
    mf = jnp.where(member, 1.0, 0.0)
    r_i = lax.broadcasted_iota(jnp.int32, (tm, tm), 0)
    c_i = lax.broadcasted_iota(jnp.int32, (tm, tm), 1)
    before = _dot((c_i < r_i).astype(BF16), mf.astype(BF16)) + carry[...]
    carry[...] = carry[...] + jnp.sum(mf, axis=0, keepdims=True)
    cnt_ref[...] = carry[...]

    out = jnp.zeros((tm, LANES), F32)
    for k in range(TOP_K):
        rank = jnp.sum(jnp.where(lane == idxs[k], before, 0.0), axis=1, keepdims=True)
        out = jnp.where(lane == ROUTE_EID + k, (idxs[k] - N_EXPERT_GROUPS).astype(F32), out)
        out = jnp.where(lane == ROUTE_RANK + k, rank, out)
        out = jnp.where(lane == ROUTE_W + k, p_group * (exps[k] / den), out)
    o_ref[...] = out


def route(logits, b_rg, b_re, n_rows, tm_combine):
    n_tok = logits.shape[0]
    n_experts = N_EXPERT_GROUPS * EXPERTS_PER_GROUP
    bias = jnp.zeros((1, LANES), F32).at[0, :N_EXPERT_GROUPS].set(b_rg)
    bias = bias.at[0, N_EXPERT_GROUPS:N_EXPERT_GROUPS + n_experts].set(b_re)
    tm = 512
    table, cnt = pl.pallas_call(
        _route_kernel,
        grid=(n_tok // tm,),
        in_specs=[pl.BlockSpec((tm, LANES), lambda i: (i, 0)),
                  pl.BlockSpec((1, LANES), lambda i: (0, 0))],
        out_specs=[pl.BlockSpec((tm, LANES), lambda i: (i, 0)),
                   pl.BlockSpec((1, LANES), lambda i: (0, 0))],
        out_shape=[jax.ShapeDtypeStruct((n_tok, LANES), F32), jax.ShapeDtypeStruct((1, LANES), F32)],
        scratch_shapes=[pltpu.VMEM((1, LANES), F32)],
        compiler_params=_params(("arbitrary",), 16),
        name="route",
    )(logits, bias)

    eid = table[:, ROUTE_EID:ROUTE_EID + TOP_K].astype(jnp.int32)
    rank = table[:, ROUTE_RANK:ROUTE_RANK + TOP_K].astype(jnp.int32)
    counts = cnt[0, N_EXPERT_GROUPS:N_EXPERT_GROUPS + n_experts].astype(jnp.int32)
    padded = ((counts + MOE_ROWS - 1) // MOE_ROWS) * MOE_ROWS
    pends = jnp.cumsum(padded)
    pstarts = pends - padded
    onehot = eid[:, :, None] == jnp.arange(n_experts, dtype=jnp.int32)
    dest = jnp.sum(jnp.where(onehot, pstarts, 0), axis=-1) + rank
    tok = jnp.broadcast_to(jnp.arange(n_tok, dtype=jnp.int32)[:, None], (n_tok, TOP_K))
    tok_buf = jnp.zeros((n_rows,), jnp.int32).at[dest.reshape(-1)].set(tok.reshape(-1), unique_indices=True)
    n_blocks = n_rows // MOE_ROWS
    block_expert = jnp.clip(
        jnp.searchsorted(pends, jnp.arange(n_blocks, dtype=jnp.int32) * MOE_ROWS, side='right'),
        0, n_experts - 1).astype(jnp.int32)
    n_active = (pends[-1] // MOE_ROWS).astype(jnp.int32).reshape(1)
    dest_tiles = dest.reshape(n_tok // tm_combine, tm_combine, TOP_K).transpose(0, 2, 1).reshape(-1)
    return table, tok_buf, block_expert, n_active, dest_tiles.astype(jnp.int32)


def _layer(x, c, w_ada, b_ada, w_in, b_forget, lam_re, lam_im, log_dt, b_re, b_im, c_re, c_im, d_skip,
           w_glu, b_glu, g_attn, g_ssm, w_out, ln1_g, ln1_b, w_rg, b_rg, w_re, b_re_r,
           w_gate, w_up, w_down, ln2_g, ln2_b, alpha):
    bsz, seq, d = x.shape
    n_tok = bsz * seq
    n_heads = b_forget.shape[0]
    d_att = n_heads * HEAD_DIM
    d_ssm = d_skip.shape[0] * d_skip.shape[1]

    mod = ada_mod(c, w_ada, b_ada).reshape(bsz, 1, -1)
    shift1, scale1, gate1, shift2, scale2, gate2 = jnp.split(mod, 6, axis=-1)

    n_qkv = 3 * d_att
    w_qkv = w_in[:, :n_qkv].astype(BF16)
    w_u = w_in[:, n_qkv + n_heads:].astype(BF16)
    w_f = jnp.zeros((d, LANES), BF16).at[:, :n_heads].set(w_in[:, n_qkv:n_qkv + n_heads].astype(BF16))
    qkv, u, f = in_proj(x, scale1, shift1, w_qkv, w_u, w_f)

    q_aug, k_aug = forget_cum(f, b_forget)
    attn, (w_glu_b, w_out_b, w_gate_b, w_up_b, w_down_b) = attention(
        qkv, q_aug, k_aug, n_heads, [w_glu, w_out, w_gate, w_up, w_down])

    w1, ft, al = ssm_prep(lam_re, lam_im, log_dt, b_re, b_im, c_re, c_im)
    y = ssm_scan(u, w1, ft, al, d_skip)
    ssm_n = glu_norm(y.reshape(n_tok, d_ssm), w_glu_b, b_glu, g_ssm).reshape(bsz, seq, d_ssm)

    n_experts = N_EXPERT_GROUPS * EXPERTS_PER_GROUP
    w_r = jnp.zeros((d, LANES), F32).at[:, :N_EXPERT_GROUPS].set(w_rg)
    w_r = w_r.at[:, N_EXPERT_GROUPS:N_EXPERT_GROUPS + n_experts].set(w_re)
    wr_hi = w_r.astype(BF16)
    wr_lo = (w_r - wr_hi.astype(F32)).astype(BF16)
    x1, h2, logits = out_proj(attn, g_attn, ssm_n, w_out_b, x, gate1, ln1_g, ln1_b,
                              scale2, shift2, wr_hi, wr_lo, alpha)

    n_assign = n_tok * TOP_K
    n_blocks = -(-(n_assign + n_experts * (MOE_ROWS - 1)) // MOE_ROWS)
    n_rows = n_blocks * MOE_ROWS
    table, tok_buf, block_expert, n_active, dest = route(logits.reshape(n_tok, LANES), b_rg, b_re_r, n_rows, 128)
    ys = moe_experts(h2.reshape(n_tok, d // 2), tok_buf, block_expert, n_active,
                     w_gate_b, w_up_b, w_down_b)
    out = moe_combine(ys, dest, table, x1.reshape(n_tok, d), gate2, ln2_g, ln2_b, alpha, seq)
    return out.reshape(bsz, seq, d)


def kernel(x, c, w_ada, b_ada, w_in, b_forget, ssm_lambda_re, ssm_lambda_im, ssm_log_dt, ssm_b_re, ssm_b_im,
           ssm_c_re, ssm_c_im, ssm_d, w_glu, b_glu, g_attn, g_ssm, w_out, ln1_g, ln1_b, w_router_group,
           b_router_group, w_router_expert, b_router_expert, w_gate, w_up, w_down, ln2_g, ln2_b):
    depth = w_ada.shape[0]
    alpha = (2.0 * depth) ** 0.25
    for l in range(depth):
        x = _layer(x, c, w_ada[l], b_ada[l], w_in[l], b_forget[l], ssm_lambda_re[l], ssm_lambda_im[l],
                   ssm_log_dt[l], ssm_b_re[l], ssm_b_im[l], ssm_c_re[l], ssm_c_im[l], ssm_d[l],
                   w_glu[l], b_glu[l], g_attn[l], g_ssm[l], w_out[l], ln1_g[l], ln1_b[l],
                   w_router_group[l], b_router_group[l], w_router_expert[l], b_router_expert[l],
                   w_gate[l], w_up[l], w_down[l], ln2_g[l], ln2_b[l], alpha)
    return x
```

```python
import functools
import math

import jax
import jax.numpy as jnp
from jax import lax
from jax.experimental import pallas as pl
from jax.experimental.pallas import tpu as pltpu

F32 = jnp.float32
BF16 = jnp.bfloat16

LANES = 128
HEAD_DIM = 128
SSM_GROUP = 16
SSM_STATE = 64
GROUPS_PER_SLAB = LANES // SSM_GROUP
SLAB_STATE = GROUPS_PER_SLAB * SSM_STATE
SSM_CHUNK = 16
SSM_COLS = 512
N_EXPERT_GROUPS = 8
EXPERTS_PER_GROUP = 8
TOP_K = 2
MOE_ROWS = 512
GATHER_UNROLL = 8
ROW_CHUNK = 128
ROUTE_EID, ROUTE_RANK, ROUTE_W = 0, TOP_K, 2 * TOP_K
LN_EPS = 1e-5
RMS_EPS = 1e-6
NEG_BIG = -1e30
LOG2E = math.log2(math.e)
MIB = 1024 * 1024

_NT = (((1,), (1,)), ((), ()))


def _params(semantics, vmem_mib):
    return pltpu.CompilerParams(dimension_semantics=semantics, vmem_limit_bytes=vmem_mib * MIB)


def _dot(a, b):
    return jnp.dot(a, b, preferred_element_type=F32)


def _dot_nt(a, b):
    return lax.dot_general(a, b, _NT, preferred_element_type=F32)


def _split2(x):
    hi = x.astype(BF16)
    return hi, (x - hi.astype(F32)).astype(BF16)


def _ada_kernel(c_ref, w_ref, b_ref, o_ref):
    s = jax.nn.silu(c_ref[...]).astype(BF16)
    o_ref[...] = _dot(s, w_ref[...].astype(BF16)) + b_ref[...]


def ada_mod(c, w_ada, b_ada):
    bsz, d = c.shape
    n = w_ada.shape[1]
    rows = 8
    assert bsz <= rows
    cp = jnp.zeros((rows, d), F32).at[:bsz].set(c)
    tn = 512
    out = pl.pallas_call(
        _ada_kernel,
        grid=(n // tn,),
        in_specs=[pl.BlockSpec((rows, d), lambda j: (0, 0)),
                  pl.BlockSpec((d, tn), lambda j: (0, j)),
                  pl.BlockSpec((1, tn), lambda j: (0, j))],
        out_specs=pl.BlockSpec((rows, tn), lambda j: (0, j)),
        out_shape=jax.ShapeDtypeStruct((rows, n), F32),
        compiler_params=_params(("arbitrary",), 40),
        name="ada_mod",
    )(cp, w_ada, b_ada.reshape(1, n))
    return out[:bsz]


def _cast_cols_kernel(a_ref, b_ref, o_ref, *, n_a):
    @pl.when(pl.program_id(0) < n_a)
    def _():
        o_ref[...] = a_ref[...].astype(BF16)

    @pl.when(pl.program_id(0) >= n_a)
    def _():
        o_ref[...] = b_ref[...].astype(BF16)


def cast_concat_cols(a, n_a_cols, b):
    d, nb = b.shape
    tn = 512
    n_a = n_a_cols // tn
    return pl.pallas_call(
        functools.partial(_cast_cols_kernel, n_a=n_a),
        grid=(n_a + nb // tn,),
        in_specs=[pl.BlockSpec((d, tn), lambda j: (0, jnp.minimum(j, n_a - 1))),
                  pl.BlockSpec((d, tn), lambda j: (0, jnp.maximum(j - n_a, 0)))],
        out_specs=pl.BlockSpec((d, tn), lambda j: (0, j)),
        out_shape=jax.ShapeDtypeStruct((d, n_a_cols + nb), BF16),
        compiler_params=_params(("arbitrary",), 48),
        name="cast_w_in",
    )(a, b)


def _inproj_kernel(x_ref, sc_ref, sh_ref, w_ref, wf_ref, qkv_ref, u_ref, f_ref, h_scr, *,
                   n_q_tiles, n_qkv_tiles, q_scale):
    j = pl.program_id(2)

    @pl.when(j == 0)
    def _():
        hb = (x_ref[0] * (1.0 + sc_ref[0]) + sh_ref[0]).astype(BF16)
        h_scr[...] = hb
        f_ref[0] = _dot(hb, wf_ref[...])

    acc = _dot(h_scr[...], w_ref[...])

    @pl.when(j < n_q_tiles)
    def _():
        qkv_ref[0] = (acc * q_scale).astype(BF16)

    @pl.when(jnp.logical_and(j >= n_q_tiles, j < n_qkv_tiles))
    def _():
        qkv_ref[0] = acc.astype(BF16)

    @pl.when(j >= n_qkv_tiles)
    def _():
        u_ref[0] = acc


def in_proj(x, scale, shift, w_main, w_f, n_qkv):
    bsz, seq, d = x.shape
    n_all = w_main.shape[1]
    n_u = n_all - n_qkv
    tm, tn = 512, 1024
    nq = n_qkv // tn
    grid = (bsz, seq // tm, n_all // tn)
    return pl.pallas_call(
        functools.partial(_inproj_kernel, n_q_tiles=n_qkv // 3 // tn, n_qkv_tiles=nq,
                          q_scale=HEAD_DIM ** -0.5 * LOG2E),
        grid=grid,
        in_specs=[pl.BlockSpec((1, tm, d), lambda b, i, j: (b, i, 0)),
                  pl.BlockSpec((1, 1, d), lambda b, i, j: (b, 0, 0)),
                  pl.BlockSpec((1, 1, d), lambda b, i, j: (b, 0, 0)),
                  pl.BlockSpec((d, tn), lambda b, i, j: (0, j)),
                  pl.BlockSpec((d, LANES), lambda b, i, j: (0, 0))],
        out_specs=[pl.BlockSpec((1, tm, tn), lambda b, i, j: (b, i, jnp.minimum(j, nq - 1))),
                   pl.BlockSpec((1, tm, tn), lambda b, i, j: (b, i, jnp.maximum(j - nq, 0))),
                   pl.BlockSpec((1, tm, LANES), lambda b, i, j: (b, i, 0))],
        out_shape=[jax.ShapeDtypeStruct((bsz, seq, n_qkv), BF16),
                   jax.ShapeDtypeStruct((bsz, seq, n_u), F32),
                   jax.ShapeDtypeStruct((bsz, seq, LANES), F32)],
        scratch_shapes=[pltpu.VMEM((tm, d), BF16)],
        compiler_params=_params(("arbitrary", "arbitrary", "arbitrary"), 52),
        name="in_proj",
    )(x, scale, shift, w_main, w_f)


def _split3(x):
    p1 = x.astype(BF16)
    r1 = x - p1.astype(F32)
    p2 = r1.astype(BF16)
    p3 = (r1 - p2.astype(F32)).astype(BF16)
    return p1, p2, p3


N_PIECES = 3


def _cum_kernel(f_ref, b_ref, qa_ref, ka_ref, carry, *, n_heads):
    i = pl.program_id(1)

    @pl.when(i == 0)
    def _():
        carry[...] = jnp.zeros_like(carry)

    tc = f_ref.shape[1]
    lf = jax.nn.log_sigmoid(f_ref[0] + b_ref[...])
    row = lax.broadcasted_iota(jnp.int32, (tc, tc), 0)
    col = lax.broadcasted_iota(jnp.int32, (tc, tc), 1)
    tri = (col <= row).astype(BF16)
    p1, p2, p3 = _split3(lf)
    cs = _dot(tri, p1) + _dot(tri, p2) + _dot(tri, p3) + carry[...]
    carry[...] = cs[tc - 1:tc, :]

    pieces = jnp.concatenate(_split3(cs * LOG2E), axis=1)
    r = lax.broadcasted_iota(jnp.int32, (N_PIECES * LANES, LANES), 0)
    c = lax.broadcasted_iota(jnp.int32, (N_PIECES * LANES, LANES), 1)
    lane = lax.broadcasted_iota(jnp.int32, (tc, LANES), 1)
    ones_q = jnp.where(jnp.logical_and(lane >= N_PIECES, lane < 2 * N_PIECES), 1.0, 0.0)
    ones_k = jnp.where(lane < N_PIECES, 1.0, 0.0)
    for h in range(n_heads):
        sel_q = (r == c * LANES + h).astype(BF16)
        sel_k = (r == (c - N_PIECES) * LANES + h).astype(BF16)
        qa_ref[0, h] = (_dot(pieces, sel_q) + ones_q).astype(BF16)
        ka_ref[0, h] = (ones_k - _dot(pieces, sel_k)).astype(BF16)


def forget_cum(f, b_forget):
    bsz, seq, _ = f.shape
    n_heads = b_forget.shape[0]
    tc = 256
    bpad = jnp.zeros((1, LANES), F32).at[0, :n_heads].set(b_forget)
    out_spec = pl.BlockSpec((1, n_heads, tc, LANES), lambda b, i: (b, 0, i, 0))
    out_shape = jax.ShapeDtypeStruct((bsz, n_heads, seq, LANES), BF16)
    return pl.pallas_call(
        functools.partial(_cum_kernel, n_heads=n_heads),
        grid=(bsz, seq // tc),
        in_specs=[pl.BlockSpec((1, tc, LANES), lambda b, i: (b, i, 0)),
                  pl.BlockSpec((1, LANES), lambda b, i: (0, 0))],
        out_specs=[out_spec, out_spec],
        out_shape=[out_shape, out_shape],
        scratch_shapes=[pltpu.VMEM((1, LANES), F32)],
        compiler_params=_params(("arbitrary", "arbitrary"), 24),
        name="forget_cum",
    )(f, bpad)


ATT_QROWS = 1024
ATT_KEYS = 512


def _attn_kernel(*refs, n_cast):
    q_ref, qa_ref, k_ref, ka_ref, v_ref = refs[:5]
    src_refs = refs[5:5 + n_cast]
    o_ref = refs[5 + n_cast]
    dst_refs = refs[6 + n_cast:6 + 2 * n_cast]
    m_scr, acc_scr, s_scr = refs[6 + 2 * n_cast:]
    for src, dst in zip(src_refs, dst_refs):
        dst[...] = src[...].astype(BF16)

    tq = q_ref.shape[1]
    tk = ATT_KEYS
    n_groups = tq // tk
    qi = pl.program_id(2)
    q = jnp.concatenate([q_ref[0], qa_ref[0, 0]], axis=1)
    ones = jnp.ones((tk, HEAD_DIM), BF16)

    m_scr[...] = jnp.full_like(m_scr, NEG_BIG)
    acc_scr[...] = jnp.zeros_like(acc_scr)

    def scores(kb, slot, first_group=0):
        k0 = pl.multiple_of(kb * tk, tk)
        kt = jnp.concatenate([k_ref[0, pl.ds(k0, tk), :], ka_ref[0, 0, pl.ds(k0, tk), :]], axis=1)
        r0 = first_group * tk
        s_scr[slot, r0:, :] = lax.dot_general(q[r0:], kt, _NT, preferred_element_type=F32)

    def update(kb, slot, diag_group=None):
        k0 = pl.multiple_of(kb * tk, tk)
        vt = jnp.concatenate([v_ref[0, pl.ds(k0, tk), :], ones], axis=1)
        for g in range(n_groups):
            if diag_group is not None and g < diag_group:
                continue
            rs = slice(g * tk, (g + 1) * tk)
            s = s_scr[slot, rs, :]
            if g == diag_group:
                qpos = lax.broadcasted_iota(jnp.int32, (tk, tk), 0)
                kpos = lax.broadcasted_iota(jnp.int32, (tk, tk), 1)
                s = jnp.where(kpos <= qpos, s, NEG_BIG)
            m_prev = m_scr[rs]
            m_new = jnp.maximum(m_prev, jnp.max(s, axis=1, keepdims=True))
            alpha = jnp.exp2(m_prev - m_new)
            p = jnp.exp2(s - m_new)
            acc_scr[rs] = alpha * acc_scr[rs] + _dot(p.astype(BF16), vt)
            m_scr[rs] = m_new

    def body(i, carry):
        kb = n_groups * i
        for g in range(n_groups):
            scores(kb + g + 1, (g + 1) % 2)
            update(kb + g, g % 2)
        return carry

    assert n_groups % 2 == 0
    scores(0, 0)
    lax.fori_loop(0, qi, body, 0)
    kb = n_groups * qi
    for g in range(n_groups):
        if g + 1 < n_groups:
            scores(kb + g + 1, (g + 1) % 2, first_group=g + 1)
        update(kb + g, g % 2, diag_group=g)

    o_ref[0] = acc_scr[:, :HEAD_DIM] / acc_scr[:, HEAD_DIM:]


BF16_ROWS = 16


def _cast_chunks(w, n_steps):
    cols = w.shape[-1]
    total_rows = w.size // cols
    n_chunks = n_steps
    while total_rows % (n_chunks * BF16_ROWS):
        n_chunks //= 2
    return w.reshape(n_chunks, total_rows // n_chunks, cols)


def attention(qkv, q_aug, k_aug, n_heads, cast_weights):
    bsz, seq, _ = qkv.shape
    t = min(ATT_QROWS, seq)
    nq = seq // t
    n_steps = bsz * n_heads * nq
    srcs = [_cast_chunks(w, n_steps) for w in cast_weights]

    def chunk_spec(a):
        per = n_steps // a.shape[0]
        return pl.BlockSpec((1,) + a.shape[1:], lambda b, h, i: (((b * n_heads + h) * nq + i) // per, 0, 0))

    outs = pl.pallas_call(
        functools.partial(_attn_kernel, n_cast=len(srcs)),
        grid=(bsz, n_heads, nq),
        in_specs=[pl.BlockSpec((1, t, HEAD_DIM), lambda b, h, i: (b, i, h)),
                  pl.BlockSpec((1, 1, t, LANES), lambda b, h, i: (b, h, i, 0)),
                  pl.BlockSpec((1, seq, HEAD_DIM), lambda b, h, i: (b, 0, n_heads + h)),
                  pl.BlockSpec((1, 1, seq, LANES), lambda b, h, i: (b, h, 0, 0)),
                  pl.BlockSpec((1, seq, HEAD_DIM), lambda b, h, i: (b, 0, 2 * n_heads + h))]
                 + [chunk_spec(a) for a in srcs],
        out_specs=[pl.BlockSpec((1, t, HEAD_DIM), lambda b, h, i: (b, i, h))] + [chunk_spec(a) for a in srcs],
        out_shape=[jax.ShapeDtypeStruct((bsz, seq, n_heads * HEAD_DIM), F32)]
                  + [jax.ShapeDtypeStruct(a.shape, BF16) for a in srcs],
        scratch_shapes=[pltpu.VMEM((t, 1), F32), pltpu.VMEM((t, 2 * HEAD_DIM), F32),
                        pltpu.VMEM((2, t, ATT_KEYS), F32)],
        compiler_params=_params(("arbitrary", "arbitrary", "arbitrary"), 56),
        name="attention",
    )(qkv, q_aug, qkv, k_aug, qkv, *srcs)
    return outs[0], [o.reshape(w.shape) for o, w in zip(outs[1:], cast_weights)]


def _blockdiag(p):
    g, c, n = p.shape
    ns = g // GROUPS_PER_SLAB
    eye = jnp.eye(GROUPS_PER_SLAB, dtype=p.dtype)
    out = p.reshape(ns, GROUPS_PER_SLAB, c, 1, n) * eye[None, :, None, :, None]
    return out.reshape(ns, GROUPS_PER_SLAB * c, GROUPS_PER_SLAB * n)


def _ssm_prep_kernel(lr_ref, li_ref, ldt_ref, bre_ref, bim_ref, cre_ref, cim_ref, w1_ref, ft_ref, al_ref):
    L = SSM_CHUNK
    lr = lr_ref[0]
    li = li_ref[0]
    dt = jnp.exp(ldt_ref[0])
    mag = jnp.exp(lr * dt)
    a_re = mag * jnp.cos(li * dt)
    a_im = mag * jnp.sin(li * dt)
    den = lr * lr + li * li
    z_re = ((a_re - 1.0) * lr + a_im * li) / den
    z_im = (a_im * lr - (a_re - 1.0) * li) / den
    br = bre_ref[0]
    bi = bim_ref[0]
    bb_re = z_re * br - z_im * bi
    bb_im = z_re * bi + z_im * br
    cr = cre_ref[0]
    ci = cim_ref[0]
    ft0_hi, ft0_lo = _split2(jnp.concatenate([cr, -ci], axis=1))

    def power(d):
        m = jnp.exp(lr * dt * d)
        return m * jnp.cos(li * dt * d), m * jnp.sin(li * dt * d)

    w1_ref[0, :, :L * LANES] = jnp.zeros((L * LANES, L * LANES), BF16)
    for d in range(L):
        pr, pi = power(float(d))
        xe = jnp.concatenate([bb_re * pr - bb_im * pi, bb_re * pi + bb_im * pr], axis=1)
        j = L - 1 - d
        w1_ref[0, j * LANES:(j + 1) * LANES, L * LANES:] = xe.astype(BF16)
        xe_hi, xe_lo = _split2(xe)
        m_d = (_dot_nt(xe_hi, ft0_hi) + _dot_nt(xe_hi, ft0_lo) + _dot_nt(xe_lo, ft0_hi)).astype(BF16)
        for jj in range(L - d):
            w1_ref[0, jj * LANES:(jj + 1) * LANES, (jj + d) * LANES:(jj + d + 1) * LANES] = m_d
        pr1, pi1 = power(float(d + 1))
        ft_ref[0, d * LANES:(d + 1) * LANES, :] = jnp.concatenate(
            [cr * pr1 - ci * pi1, -(cr * pi1 + ci * pr1)], axis=1).astype(BF16)
    prl, pil = power(float(L))
    al_ref[0] = jnp.concatenate([prl, pil], axis=1)


def ssm_prep(lam_re, lam_im, log_dt, b_re, b_im, c_re, c_im):
    g, n = lam_re.shape
    ns = g // GROUPS_PER_SLAB
    L = SSM_CHUNK
    rowvec = lambda a: a.reshape(ns, 1, SLAB_STATE)
    args = (rowvec(lam_re), rowvec(lam_im), rowvec(jnp.repeat(log_dt, n)),
            _blockdiag(b_re.transpose(0, 2, 1)), _blockdiag(b_im.transpose(0, 2, 1)),
            _blockdiag(c_re), _blockdiag(c_im))
    vec_spec = pl.BlockSpec((1, 1, SLAB_STATE), lambda s: (s, 0, 0))
    mat_spec = pl.BlockSpec((1, LANES, SLAB_STATE), lambda s: (s, 0, 0))
    return pl.pallas_call(
        _ssm_prep_kernel,
        grid=(ns,),
        in_specs=[vec_spec] * 3 + [mat_spec] * 4,
        out_specs=[pl.BlockSpec((1, L * LANES, L * LANES + 2 * SLAB_STATE), lambda s: (s, 0, 0)),
                   pl.BlockSpec((1, L * LANES, 2 * SLAB_STATE), lambda s: (s, 0, 0)),
                   pl.BlockSpec((1, 1, 2 * SLAB_STATE), lambda s: (s, 0, 0))],
        out_shape=[jax.ShapeDtypeStruct((ns, L * LANES, L * LANES + 2 * SLAB_STATE), BF16),
                   jax.ShapeDtypeStruct((ns, L * LANES, 2 * SLAB_STATE), BF16),
                   jax.ShapeDtypeStruct((ns, 1, 2 * SLAB_STATE), F32)],
        compiler_params=_params(("arbitrary",), 48),
        name="ssm_prep",
    )(*args)


def _ssm_kernel(u_ref, w1_ref, ft_ref, al_ref, d_ref, y_ref, uf_scr, e_scr, y_scr):
    L = SSM_CHUNK
    nch = uf_scr.shape[0]
    lc = L * LANES
    for j in range(L):
        uf_scr[:, j * LANES:(j + 1) * LANES] = u_ref[0, pl.ds(j, nch, stride=L), :].astype(BF16)
    uf = uf_scr[...]
    e_scr[...] = _dot(uf, w1_ref[0, :, lc:])

    a_re = al_ref[0, :, :SLAB_STATE]
    a_im = al_ref[0, :, SLAB_STATE:]
    h_re = jnp.zeros((1, SLAB_STATE), F32)
    h_im = jnp.zeros((1, SLAB_STATE), F32)
    for k in range(nch):
        e = e_scr[k:k + 1, :]
        e_scr[k:k + 1, :] = jnp.concatenate([h_re, h_im], axis=1)
        h_re, h_im = (a_re * h_re - a_im * h_im + e[:, :SLAB_STATE],
                      a_re * h_im + a_im * h_re + e[:, SLAB_STATE:])

    for c in range(lc // SSM_COLS):
        k_hi = (c + 1) * SSM_COLS
        cols = slice(c * SSM_COLS, k_hi)
        y_scr[:, cols] = _dot(uf[:, :k_hi], w1_ref[0, :k_hi, cols])
    y = y_scr[...] + lax.dot_general(e_scr[...].astype(BF16), ft_ref[0], _NT, preferred_element_type=F32)
    for i in range(L):
        yi = y[:, i * LANES:(i + 1) * LANES] + d_ref[0] * u_ref[0, pl.ds(i, nch, stride=L), :]
        y_ref[0, pl.ds(i, nch, stride=L), :] = jax.nn.gelu(yi)


def ssm_scan(u, w1, ft, al, d_skip):
    bsz, seq, c = u.shape
    ns = c // LANES
    L = SSM_CHUNK
    nch = seq // L
    lc = L * LANES
    once = pl.Buffered(1)
    return pl.pallas_call(
        _ssm_kernel,
        grid=(ns, bsz),
        in_specs=[pl.BlockSpec((1, seq, LANES), lambda s, b: (b, 0, s)),
                  pl.BlockSpec((1, lc, lc + 2 * SLAB_STATE), lambda s, b: (s, 0, 0), pipeline_mode=once),
                  pl.BlockSpec((1, lc, 2 * SLAB_STATE), lambda s, b: (s, 0, 0), pipeline_mode=once),
                  pl.BlockSpec((1, 1, 2 * SLAB_STATE), lambda s, b: (s, 0, 0)),
                  pl.BlockSpec((1, 1, LANES), lambda s, b: (s, 0, 0))],
        out_specs=pl.BlockSpec((1, seq, LANES), lambda s, b: (b, 0, s)),
        out_shape=jax.ShapeDtypeStruct((bsz, seq, c), F32),
        scratch_shapes=[pltpu.VMEM((nch, lc), BF16), pltpu.VMEM((nch, 2 * SLAB_STATE), F32),
                        pltpu.VMEM((nch, lc), F32)],
        compiler_params=_params(("arbitrary", "arbitrary"), 56),
        name="ssm_scan",
    )(u, w1, ft, al, d_skip.reshape(ns, 1, LANES))


def _glu_kernel(y_ref, w_ref, b_ref, g_ref, o_ref):
    y = y_ref[...]
    o = y * jax.nn.sigmoid(_dot(y.astype(BF16), w_ref[...]) + b_ref[...])
    ms = jnp.mean(o * o, axis=-1, keepdims=True)
    o_ref[...] = (o * lax.rsqrt(ms + RMS_EPS) * g_ref[...]).astype(BF16)


def glu_norm(y, w_glu, b_glu, g):
    t, c = y.shape
    tm = 512
    return pl.pallas_call(
        _glu_kernel,
        grid=(t // tm,),
        in_specs=[pl.BlockSpec((tm, c), lambda i: (i, 0)),
                  pl.BlockSpec((c, c), lambda i: (0, 0)),
                  pl.BlockSpec((1, c), lambda i: (0, 0)),
                  pl.BlockSpec((1, c), lambda i: (0, 0))],
        out_specs=pl.BlockSpec((tm, c), lambda i: (i, 0)),
        out_shape=jax.ShapeDtypeStruct((t, c), BF16),
        compiler_params=_params(("arbitrary",), 48),
        name="glu_norm",
    )(y, w_glu, b_glu.reshape(1, c), g.reshape(1, c))


def _outproj_kernel(attn_ref, ga_ref, ssm_ref, w_ref, x_ref, gate_ref, lng_ref, lnb_ref, sc2_ref, sh2_ref,
                    wrh_ref, wrl_ref, x1_ref, h2_ref, lg_ref, a_scr, *, alpha, n_att):
    j = pl.program_id(2)
    nj = pl.num_programs(2)
    tn = w_ref.shape[1]

    tm = x1_ref.shape[1]
    n_chunks = tm // ROW_CHUNK

    @pl.when(j == 0)
    def _():
        def norm(c, carry):
            rs = pl.ds(pl.multiple_of(c * ROW_CHUNK, ROW_CHUNK), ROW_CHUNK)
            a = attn_ref[0, rs, :]
            ms = jnp.mean(a * a, axis=-1, keepdims=True)
            a_scr[rs, :n_att] = (a * lax.rsqrt(ms + RMS_EPS) * ga_ref[...]).astype(BF16)
            return carry
        lax.fori_loop(0, n_chunks, norm, 0)
        a_scr[:, n_att:] = ssm_ref[0]

    mixed = _dot(a_scr[...], w_ref[...])
    col = pl.multiple_of(j * tn, tn)
    x1_ref[0, :, pl.ds(col, tn)] = alpha * x_ref[0] + (1.0 + gate_ref[0]) * mixed

    @pl.when(j == nj - 1)
    def _():
        def finish(c, carry):
            rs = pl.ds(pl.multiple_of(c * ROW_CHUNK, ROW_CHUNK), ROW_CHUNK)
            r = x1_ref[0, rs, :]
            mu = jnp.mean(r, axis=-1, keepdims=True)
            var = jnp.mean(jnp.square(r - mu), axis=-1, keepdims=True)
            x1 = (r - mu) * lax.rsqrt(var + LN_EPS) * lng_ref[...] + lnb_ref[...]
            x1_ref[0, rs, :] = x1
            h2 = x1 * (1.0 + sc2_ref[0]) + sh2_ref[0]
            hi = h2.astype(BF16)
            hi_f = hi.astype(F32)
            lo = (h2 - hi_f).astype(BF16)
            lg_ref[0, rs, :] = _dot(hi, wrh_ref[...]) + _dot(hi, wrl_ref[...]) + _dot(lo, wrh_ref[...])
            h2_ref[0, rs, :] = pack_bf16_pairs(hi_f)
            return carry
        lax.fori_loop(0, n_chunks, finish, 0)


def pack_bf16_pairs(x):
    n = x.shape[-1] // 2
    bits = lax.bitcast_convert_type(x, jnp.uint32)
    return bits[:, n:] | (bits[:, :n] >> 16)


def unpack_bf16_pairs(p):
    lo = lax.bitcast_convert_type(p << 16, F32).astype(BF16)
    hi = lax.bitcast_convert_type(p & jnp.uint32(0xFFFF0000), F32).astype(BF16)
    return lo, hi


def out_proj(attn, g_attn, ssm_n, w_out, x, gate1, ln_g, ln_b, scale2, shift2, wr_hi, wr_lo, alpha):
    bsz, seq, d = x.shape
    n_att = attn.shape[-1]
    n_ssm = ssm_n.shape[-1]
    k = n_att + n_ssm
    tm, tn = 512, 256
    row = lambda a: a.reshape(1, -1)
    full = lambda n: pl.BlockSpec((1, n), lambda b, i, j: (0, 0))
    return pl.pallas_call(
        functools.partial(_outproj_kernel, alpha=alpha, n_att=n_att),
        grid=(bsz, seq // tm, d // tn),
        in_specs=[pl.BlockSpec((1, tm, n_att), lambda b, i, j: (b, i, 0)),
                  full(n_att),
                  pl.BlockSpec((1, tm, n_ssm), lambda b, i, j: (b, i, 0)),
                  pl.BlockSpec((k, tn), lambda b, i, j: (0, j)),
                  pl.BlockSpec((1, tm, tn), lambda b, i, j: (b, i, j)),
                  pl.BlockSpec((1, 1, tn), lambda b, i, j: (b, 0, j)),
                  full(d), full(d),
                  pl.BlockSpec((1, 1, d), lambda b, i, j: (b, 0, 0)),
                  pl.BlockSpec((1, 1, d), lambda b, i, j: (b, 0, 0)),
                  pl.BlockSpec((d, LANES), lambda b, i, j: (0, 0)),
                  pl.BlockSpec((d, LANES), lambda b, i, j: (0, 0))],
        out_specs=[pl.BlockSpec((1, tm, d), lambda b, i, j: (b, i, 0)),
                   pl.BlockSpec((1, tm, d // 2), lambda b, i, j: (b, i, 0)),
                   pl.BlockSpec((1, tm, LANES), lambda b, i, j: (b, i, 0))],
        out_shape=[jax.ShapeDtypeStruct((bsz, seq, d), F32),
                   jax.ShapeDtypeStruct((bsz, seq, d // 2), jnp.uint32),
                   jax.ShapeDtypeStruct((bsz, seq, LANES), F32)],
        scratch_shapes=[pltpu.VMEM((tm, k), BF16)],
        compiler_params=_params(("arbitrary", "arbitrary", "arbitrary"), 56),
        name="out_proj",
    )(attn, row(g_attn), ssm_n, w_out, x, gate1, row(ln_g), row(ln_b), scale2, shift2, wr_hi, wr_lo)


def _moe_kernel(tok_ref, bexp_ref, nact_ref, h_hbm, wg_ref, wu_ref, wd_ref, y_ref, xbuf, sem):
    i = pl.program_id(0)
    nact = nact_ref[0]
    rows = xbuf.shape[1]

    def start_gather(blk, slot):
        def body(r, carry):
            tok = tok_ref[blk * rows + r]
            pltpu.make_async_copy(h_hbm.at[pl.ds(tok, 1)], xbuf.at[slot, pl.ds(r, 1)], sem.at[slot]).start()
            return carry
        lax.fori_loop(0, rows, body, 0, unroll=GATHER_UNROLL)

    def wait_gather(slot):
        pltpu.make_async_copy(h_hbm.at[pl.ds(0, rows)], xbuf.at[slot], sem.at[slot]).wait()

    @pl.when(jnp.logical_and(i == 0, nact > 0))
    def _():
        start_gather(0, 0)

    @pl.when(i + 1 < nact)
    def _():
        start_gather(i + 1, (i + 1) % 2)

    @pl.when(i < nact)
    def _():
        slot = i % 2
        wait_gather(slot)
        x_lo, x_hi = unpack_bf16_pairs(xbuf[slot])
        half = x_lo.shape[1]
        g = _dot(x_lo, wg_ref[0, :half, :]) + _dot(x_hi, wg_ref[0, half:, :])
        u = _dot(x_lo, wu_ref[0, :half, :]) + _dot(x_hi, wu_ref[0, half:, :])
        act = (jax.nn.silu(g) * u).astype(BF16)
        y_ref[...] = _dot(act, wd_ref[0])

    @pl.when(i >= nact)
    def _():
        y_ref[...] = jnp.zeros_like(y_ref)


def moe_experts(h2, tok_buf, block_expert, n_active, w_gate, w_up, w_down):
    t, dp = h2.shape
    d = 2 * dp
    n_rows = tok_buf.shape[0]
    rows = MOE_ROWS
    n_blocks = n_rows // rows
    de = w_gate.shape[-1]

    def wmap(i, tok, bexp, nact):
        return (bexp[jnp.minimum(i, jnp.maximum(nact[0] - 1, 0))], 0, 0)

    grid_spec = pltpu.PrefetchScalarGridSpec(
        num_scalar_prefetch=3,
        grid=(n_blocks,),
        in_specs=[pl.BlockSpec(memory_space=pl.ANY),
                  pl.BlockSpec((1, d, de), wmap),
                  pl.BlockSpec((1, d, de), wmap),
                  pl.BlockSpec((1, de, d), wmap)],
        out_specs=pl.BlockSpec((rows, d), lambda i, *_: (i, 0)),
        scratch_shapes=[pltpu.VMEM((2, rows, dp), jnp.uint32), pltpu.SemaphoreType.DMA((2,))],
    )
    return pl.pallas_call(
        _moe_kernel,
        grid_spec=grid_spec,
        out_shape=jax.ShapeDtypeStruct((n_rows, d), F32),
        compiler_params=_params(("arbitrary",), 60),
        name="moe_experts",
    )(tok_buf, block_expert, n_active, h2, w_gate, w_up, w_down)


def _combine_kernel(dest_ref, ys_hbm, rt_ref, x1_ref, gate_ref, lng_ref, lnb_ref, o_ref, ybuf, sem, *, alpha):
    i = pl.program_id(0)
    n = pl.num_programs(0)
    tm = x1_ref.shape[0]
    n_copies = TOP_K * tm

    def gather(blk, slot, start):
        if not start:
            pltpu.make_async_copy(ys_hbm.at[pl.ds(0, n_copies)], ybuf.at[slot], sem.at[slot]).wait()
            return

        def body(r, carry):
            row = dest_ref[blk * n_copies + r]
            pltpu.make_async_copy(ys_hbm.at[pl.ds(row, 1)], ybuf.at[slot, pl.ds(r, 1)], sem.at[slot]).start()
            return carry
        lax.fori_loop(0, n_copies, body, 0, unroll=GATHER_UNROLL)

    @pl.when(i == 0)
    def _():
        gather(0, 0, True)

    @pl.when(i + 1 < n)
    def _():
        gather(i + 1, (i + 1) % 2, True)

    slot = i % 2
    gather(i, slot, False)
    moe = rt_ref[:, ROUTE_W:ROUTE_W + 1] * ybuf[slot, :tm, :]
    for kk in range(1, TOP_K):
        moe = moe + rt_ref[:, ROUTE_W + kk:ROUTE_W + kk + 1] * ybuf[slot, kk * tm:(kk + 1) * tm, :]
    r = alpha * x1_ref[...] + (1.0 + gate_ref[0]) * moe
    mu = jnp.mean(r, axis=-1, keepdims=True)
    var = jnp.mean(jnp.square(r - mu), axis=-1, keepdims=True)
    o_ref[...] = (r - mu) * lax.rsqrt(var + LN_EPS) * lng_ref[...] + lnb_ref[...]


def moe_combine(ys, dest, table, x1, gate2, ln_g, ln_b, alpha, seq):
    t, d = x1.shape
    tm = 128
    tiles_per_seq = seq // tm
    grid_spec = pltpu.PrefetchScalarGridSpec(
        num_scalar_prefetch=1,
        grid=(t // tm,),
        in_specs=[pl.BlockSpec(memory_space=pl.ANY),
                  pl.BlockSpec((tm, LANES), lambda i, *_: (i, 0)),
                  pl.BlockSpec((tm, d), lambda i, *_: (i, 0)),
                  pl.BlockSpec((1, 1, d), lambda i, *_: (i // tiles_per_seq, 0, 0)),
                  pl.BlockSpec((1, d), lambda i, *_: (0, 0)),
                  pl.BlockSpec((1, d), lambda i, *_: (0, 0))],
        out_specs=pl.BlockSpec((tm, d), lambda i, *_: (i, 0)),
        scratch_shapes=[pltpu.VMEM((2, TOP_K * tm, d), F32), pltpu.SemaphoreType.DMA((2,))],
    )
    return pl.pallas_call(
        functools.partial(_combine_kernel, alpha=alpha),
        grid_spec=grid_spec,
        out_shape=jax.ShapeDtypeStruct((t, d), F32),
        compiler_params=_params(("arbitrary",), 32),
        name="moe_combine",
    )(dest, ys, table, x1, gate2, ln_g.reshape(1, d), ln_b.reshape(1, d))


def _route_kernel(lg_ref, b_ref, o_ref, cnt_ref, carry):
    i = pl.program_id(0)

    @pl.when(i == 0)
    def _():
        carry[...] = jnp.zeros_like(carry)

    tm = lg_ref.shape[0]
    x = lg_ref[...] + b_ref[...]
    lane = lax.broadcasted_iota(jnp.int32, (tm, LANES), 1)
    ninf = -jnp.inf

    def top(v):
        vmax = jnp.max(v, axis=1, keepdims=True)
        return vmax, jnp.min(jnp.where(v == vmax, lane, LANES), axis=1, keepdims=True)

    gmask = lane < N_EXPERT_GROUPS
    gmax, g_sel = top(jnp.where(gmask, x, ninf))
    p_group = 1.0 / jnp.sum(jnp.where(gmask, jnp.exp(x - gmax), 0.0), axis=1, keepdims=True)

    lo = N_EXPERT_GROUPS + g_sel * EXPERTS_PER_GROUP
    cur = jnp.where(jnp.logical_and(lane >= lo, lane < lo + EXPERTS_PER_GROUP), x, ninf)
    vals, idxs = [], []
    for _ in range(TOP_K):
        v, ix = top(cur)
        vals.append(v)
        idxs.append(ix)
        cur = jnp.where(lane == ix, ninf, cur)
    exps = [jnp.exp(v - vals[0]) for v in vals]
    den = functools.reduce(lambda a, c: a + c, exps)

    member = functools.reduce(jnp.logical_or, [lane == ix for ix in idxs])
    mf = jnp.where(member, 1.0, 0.0)
    r_i = lax.broadcasted_iota(jnp.int32, (tm, tm), 0)
    c_i = lax.broadcasted_iota(jnp.int32, (tm, tm), 1)
    before = _dot((c_i < r_i).astype(BF16), mf.astype(BF16)) + carry[...]
    carry[...] = carry[...] + jnp.sum(mf, axis=0, keepdims=True)
    cnt_ref[...] = carry[...]

    out = jnp.zeros((tm, LANES), F32)
    for k in range(TOP_K):
        rank = jnp.sum(jnp.where(lane == idxs[k], before, 0.0), axis=1, keepdims=True)
        out = jnp.where(lane == ROUTE_EID + k, (idxs[k] - N_EXPERT_GROUPS).astype(F32), out)
        out = jnp.where(lane == ROUTE_RANK + k, rank, out)
        out = jnp.where(lane == ROUTE_W + k, p_group * (exps[k] / den), out)
    o_ref[...] = out


def route(logits, b_rg, b_re, n_rows, tm_combine):
    n_tok = logits.shape[0]
    n_experts = N_EXPERT_GROUPS * EXPERTS_PER_GROUP
    bias = jnp.zeros((1, LANES), F32).at[0, :N_EXPERT_GROUPS].set(b_rg)
    bias = bias.at[0, N_EXPERT_GROUPS:N_EXPERT_GROUPS + n_experts].set(b_re)
    tm = 512
    table, cnt = pl.pallas_call(
        _route_kernel,
        grid=(n_tok // tm,),
        in_specs=[pl.BlockSpec((tm, LANES), lambda i: (i, 0)),
                  pl.BlockSpec((1, LANES), lambda i: (0, 0))],
        out_specs=[pl.BlockSpec((tm, LANES), lambda i: (i, 0)),
                   pl.BlockSpec((1, LANES), lambda i: (0, 0))],
        out_shape=[jax.ShapeDtypeStruct((n_tok, LANES), F32), jax.ShapeDtypeStruct((1, LANES), F32)],
        scratch_shapes=[pltpu.VMEM((1, LANES), F32)],
        compiler_params=_params(("arbitrary",), 16),
        name="route",
    )(logits, bias)

    eid = table[:, ROUTE_EID:ROUTE_EID + TOP_K].astype(jnp.int32)
    rank = table[:, ROUTE_RANK:ROUTE_RANK + TOP_K].astype(jnp.int32)
    counts = cnt[0, N_EXPERT_GROUPS:N_EXPERT_GROUPS + n_experts].astype(jnp.int32)
    padded = ((counts + MOE_ROWS - 1) // MOE_ROWS) * MOE_ROWS
    pends = jnp.cumsum(padded)
    pstarts = pends - padded
    onehot = eid[:, :, None] == jnp.arange(n_experts, dtype=jnp.int32)
    dest = jnp.sum(jnp.where(onehot, pstarts, 0), axis=-1) + rank
    tok = jnp.broadcast_to(jnp.arange(n_tok, dtype=jnp.int32)[:, None], (n_tok, TOP_K))
    tok_buf = jnp.zeros((n_rows,), jnp.int32).at[dest.reshape(-1)].set(tok.reshape(-1), unique_indices=True)
    n_blocks = n_rows // MOE_ROWS
    block_expert = jnp.clip(
        jnp.searchsorted(pends, jnp.arange(n_blocks, dtype=jnp.int32) * MOE_ROWS, side='right'),
        0, n_experts - 1).astype(jnp.int32)
    n_active = (pends[-1] // MOE_ROWS).astype(jnp.int32).reshape(1)
    dest_tiles = dest.reshape(n_tok // tm_combine, tm_combine, TOP_K).transpose(0, 2, 1).reshape(-1)
    return table, tok_buf, block_expert, n_active, dest_tiles.astype(jnp.int32)


def _layer(x, c, w_ada, b_ada, w_in, b_forget, lam_re, lam_im, log_dt, b_re, b_im, c_re, c_im, d_skip,
           w_glu, b_glu, g_attn, g_ssm, w_out, ln1_g, ln1_b, w_rg, b_rg, w_re, b_re_r,
           w_gate, w_up, w_down, ln2_g, ln2_b, alpha):
    bsz, seq, d = x.shape
    n_tok = bsz * seq
    n_heads = b_forget.shape[0]
    d_att = n_heads * HEAD_DIM
    d_ssm = d_skip.shape[0] * d_skip.shape[1]

    mod = ada_mod(c, w_ada, b_ada).reshape(bsz, 1, -1)
    shift1, scale1, gate1, shift2, scale2, gate2 = jnp.split(mod, 6, axis=-1)

    n_qkv = 3 * d_att
    w_main = cast_concat_cols(w_in, n_qkv, w_in[:, n_qkv + n_heads:])
    w_f = jnp.zeros((d, LANES), BF16).at[:, :n_heads].set(w_in[:, n_qkv:n_qkv + n_heads].astype(BF16))
    qkv, u, f = in_proj(x, scale1, shift1, w_main, w_f, n_qkv)

    q_aug, k_aug = forget_cum(f, b_forget)
    attn, (w_glu_b, w_out_b, w_gate_b, w_up_b, w_down_b) = attention(
        qkv, q_aug, k_aug, n_heads, [w_glu, w_out, w_gate, w_up, w_down])

    w1, ft, al = ssm_prep(lam_re, lam_im, log_dt, b_re, b_im, c_re, c_im)
    y = ssm_scan(u, w1, ft, al, d_skip)
    ssm_n = glu_norm(y.reshape(n_tok, d_ssm), w_glu_b, b_glu, g_ssm).reshape(bsz, seq, d_ssm)

    n_experts = N_EXPERT_GROUPS * EXPERTS_PER_GROUP
    w_r = jnp.zeros((d, LANES), F32).at[:, :N_EXPERT_GROUPS].set(w_rg)
    w_r = w_r.at[:, N_EXPERT_GROUPS:N_EXPERT_GROUPS + n_experts].set(w_re)
    wr_hi = w_r.astype(BF16)
    wr_lo = (w_r - wr_hi.astype(F32)).astype(BF16)
    x1, h2, logits = out_proj(attn, g_attn, ssm_n, w_out_b, x, gate1, ln1_g, ln1_b,
                              scale2, shift2, wr_hi, wr_lo, alpha)

    n_assign = n_tok * TOP_K
    n_blocks = -(-(n_assign + n_experts * (MOE_ROWS - 1)) // MOE_ROWS)
    n_rows = n_blocks * MOE_ROWS
    table, tok_buf, block_expert, n_active, dest = route(logits.reshape(n_tok, LANES), b_rg, b_re_r, n_rows, 128)
    ys = moe_experts(h2.reshape(n_tok, d // 2), tok_buf, block_expert, n_active,
                     w_gate_b, w_up_b, w_down_b)
    out = moe_combine(ys, dest, table, x1.reshape(n_tok, d), gate2, ln2_g, ln2_b, alpha, seq)
    return out.reshape(bsz, seq, d)


def kernel(x, c, w_ada, b_ada, w_in, b_forget, ssm_lambda_re, ssm_lambda_im, ssm_log_dt, ssm_b_re, ssm_b_im,
           ssm_c_re, ssm_c_im, ssm_d, w_glu, b_glu, g_attn, g_ssm, w_out, ln1_g, ln1_b, w_router_group,
           b_router_group, w_router_expert, b_router_expert, w_gate, w_up, w_down, ln2_g, ln2_b):
    depth = w_ada.shape[0]
    alpha = (2.0 * depth) ** 0.25
    for l in range(depth):
        x = _layer(x, c, w_ada[l], b_ada[l], w_in[l], b_forget[l], ssm_lambda_re[l], ssm_lambda_im[l],
                   ssm_log_dt[l], ssm_b_re[l], ssm_b_im[l], ssm_c_re[l], ssm_c_im[l], ssm_d[l],
                   w_glu[l], b_glu[l], g_attn[l], g_ssm[l], w_out[l], ln1_g[l], ln1_b[l],
                   w_router_group[l], b_router_group[l], w_router_expert[l], b_router_expert[l],
                   w_gate[l], w_up[l], w_down[l], ln2_g[l], ln2_b[l], alpha)
    return x
```

```python
import functools
import math

import jax
import jax.numpy as jnp
from jax import lax
from jax.experimental import pallas as pl
from jax.experimental.pallas import tpu as pltpu

F32 = jnp.float32
BF16 = jnp.bfloat16

LANES = 128
HEAD_DIM = 128
SSM_GROUP = 16
SSM_STATE = 64
GROUPS_PER_SLAB = LANES // SSM_GROUP
SLAB_STATE = GROUPS_PER_SLAB * SSM_STATE
SSM_CHUNK = 16
SSM_COLS = 512
N_EXPERT_GROUPS = 8
EXPERTS_PER_GROUP = 8
TOP_K = 2
MOE_ROWS = 256
GATHER_UNROLL = 8
ROW_CHUNK = 128
ROUTE_EID, ROUTE_RANK, ROUTE_W = 0, TOP_K, 2 * TOP_K
LN_EPS = 1e-5
RMS_EPS = 1e-6
NEG_BIG = -1e30
LOG2E = math.log2(math.e)
MIB = 1024 * 1024

_NT = (((1,), (1,)), ((), ()))


def _params(semantics, vmem_mib):
    return pltpu.CompilerParams(dimension_semantics=semantics, vmem_limit_bytes=vmem_mib * MIB)


def _dot(a, b):
    return jnp.dot(a, b, preferred_element_type=F32)


def _dot_nt(a, b):
    return lax.dot_general(a, b, _NT, preferred_element_type=F32)


def _split2(x):
    hi = x.astype(BF16)
    return hi, (x - hi.astype(F32)).astype(BF16)


def _ada_kernel(c_ref, w_ref, b_ref, o_ref):
    s = jax.nn.silu(c_ref[...]).astype(BF16)
    o_ref[...] = _dot(s, w_ref[...].astype(BF16)) + b_ref[...]


def ada_mod(c, w_ada, b_ada):
    bsz, d = c.shape
    n = w_ada.shape[1]
    rows = 8
    assert bsz <= rows
    cp = jnp.zeros((rows, d), F32).at[:bsz].set(c)
    tn = 512
    out = pl.pallas_call(
        _ada_kernel,
        grid=(n // tn,),
        in_specs=[pl.BlockSpec((rows, d), lambda j: (0, 0)),
                  pl.BlockSpec((d, tn), lambda j: (0, j)),
                  pl.BlockSpec((1, tn), lambda j: (0, j))],
        out_specs=pl.BlockSpec((rows, tn), lambda j: (0, j)),
        out_shape=jax.ShapeDtypeStruct((rows, n), F32),
        compiler_params=_params(("arbitrary",), 40),
        name="ada_mod",
    )(cp, w_ada, b_ada.reshape(1, n))
    return out[:bsz]


def _inproj_kernel(x_ref, sc_ref, sh_ref, w_ref, wf_ref, qkv_ref, u_ref, f_ref, h_scr, *,
                   n_q_tiles, n_qkv_tiles, q_scale):
    j = pl.program_id(2)

    @pl.when(j == 0)
    def _():
        hb = (x_ref[0] * (1.0 + sc_ref[0]) + sh_ref[0]).astype(BF16)
        h_scr[...] = hb
        f_ref[0] = _dot(hb, wf_ref[...])

    acc = _dot(h_scr[...], w_ref[...])

    @pl.when(j < n_q_tiles)
    def _():
        qkv_ref[0] = (acc * q_scale).astype(BF16)

    @pl.when(jnp.logical_and(j >= n_q_tiles, j < n_qkv_tiles))
    def _():
        qkv_ref[0] = acc.astype(BF16)

    @pl.when(j >= n_qkv_tiles)
    def _():
        u_ref[0] = acc


def in_proj(x, scale, shift, w_main, w_f, n_qkv):
    bsz, seq, d = x.shape
    n_all = w_main.shape[1]
    n_u = n_all - n_qkv
    tm, tn = 512, 1024
    nq = n_qkv // tn
    grid = (bsz, seq // tm, n_all // tn)
    return pl.pallas_call(
        functools.partial(_inproj_kernel, n_q_tiles=n_qkv // 3 // tn, n_qkv_tiles=nq,
                          q_scale=HEAD_DIM ** -0.5 * LOG2E),
        grid=grid,
        in_specs=[pl.BlockSpec((1, tm, d), lambda b, i, j: (b, i, 0)),
                  pl.BlockSpec((1, 1, d), lambda b, i, j: (b, 0, 0)),
                  pl.BlockSpec((1, 1, d), lambda b, i, j: (b, 0, 0)),
                  pl.BlockSpec((d, tn), lambda b, i, j: (0, j)),
                  pl.BlockSpec((d, LANES), lambda b, i, j: (0, 0))],
        out_specs=[pl.BlockSpec((1, tm, tn), lambda b, i, j: (b, i, jnp.minimum(j, nq - 1))),
                   pl.BlockSpec((1, tm, tn), lambda b, i, j: (b, i, jnp.maximum(j - nq, 0))),
                   pl.BlockSpec((1, tm, LANES), lambda b, i, j: (b, i, 0))],
        out_shape=[jax.ShapeDtypeStruct((bsz, seq, n_qkv), BF16),
                   jax.ShapeDtypeStruct((bsz, seq, n_u), F32),
                   jax.ShapeDtypeStruct((bsz, seq, LANES), F32)],
        scratch_shapes=[pltpu.VMEM((tm, d), BF16)],
        compiler_params=_params(("arbitrary", "arbitrary", "arbitrary"), 52),
        name="in_proj",
    )(x, scale, shift, w_main, w_f)


def _split3(x):
    p1 = x.astype(BF16)
    r1 = x - p1.astype(F32)
    p2 = r1.astype(BF16)
    p3 = (r1 - p2.astype(F32)).astype(BF16)
    return p1, p2, p3


N_PIECES = 3


def _cum_kernel(f_ref, b_ref, qa_ref, ka_ref, carry, *, n_heads):
    i = pl.program_id(1)

    @pl.when(i == 0)
    def _():
        carry[...] = jnp.zeros_like(carry)

    tc = f_ref.shape[1]
    lf = jax.nn.log_sigmoid(f_ref[0] + b_ref[...])
    row = lax.broadcasted_iota(jnp.int32, (tc, tc), 0)
    col = lax.broadcasted_iota(jnp.int32, (tc, tc), 1)
    tri = (col <= row).astype(BF16)
    p1, p2, p3 = _split3(lf)
    cs = _dot(tri, p1) + _dot(tri, p2) + _dot(tri, p3) + carry[...]
    carry[...] = cs[tc - 1:tc, :]

    pieces = jnp.concatenate(_split3(cs * LOG2E), axis=1)
    r = lax.broadcasted_iota(jnp.int32, (N_PIECES * LANES, LANES), 0)
    c = lax.broadcasted_iota(jnp.int32, (N_PIECES * LANES, LANES), 1)
    lane = lax.broadcasted_iota(jnp.int32, (tc, LANES), 1)
    ones_q = jnp.where(jnp.logical_and(lane >= N_PIECES, lane < 2 * N_PIECES), 1.0, 0.0)
    ones_k = jnp.where(lane < N_PIECES, 1.0, 0.0)
    for h in range(n_heads):
        sel_q = (r == c * LANES + h).astype(BF16)
        sel_k = (r == (c - N_PIECES) * LANES + h).astype(BF16)
        qa_ref[0, h] = (_dot(pieces, sel_q) + ones_q).astype(BF16)
        ka_ref[0, h] = (ones_k - _dot(pieces, sel_k)).astype(BF16)


def forget_cum(f, b_forget):
    bsz, seq, _ = f.shape
    n_heads = b_forget.shape[0]
    tc = 256
    bpad = jnp.zeros((1, LANES), F32).at[0, :n_heads].set(b_forget)
    out_spec = pl.BlockSpec((1, n_heads, tc, LANES), lambda b, i: (b, 0, i, 0))
    out_shape = jax.ShapeDtypeStruct((bsz, n_heads, seq, LANES), BF16)
    return pl.pallas_call(
        functools.partial(_cum_kernel, n_heads=n_heads),
        grid=(bsz, seq // tc),
        in_specs=[pl.BlockSpec((1, tc, LANES), lambda b, i: (b, i, 0)),
                  pl.BlockSpec((1, LANES), lambda b, i: (0, 0))],
        out_specs=[out_spec, out_spec],
        out_shape=[out_shape, out_shape],
        scratch_shapes=[pltpu.VMEM((1, LANES), F32)],
        compiler_params=_params(("arbitrary", "arbitrary"), 24),
        name="forget_cum",
    )(f, bpad)


ATT_QROWS = 1024
ATT_KEYS = 512


def _attn_kernel(*refs, n_cast):
    q_ref, qa_ref, k_ref, ka_ref, v_ref = refs[:5]
    src_refs = refs[5:5 + n_cast]
    o_ref = refs[5 + n_cast]
    dst_refs = refs[6 + n_cast:6 + 2 * n_cast]
    m_scr, acc_scr, s_scr = refs[6 + 2 * n_cast:]
    for src, dst in zip(src_refs, dst_refs):
        dst[...] = src[...].astype(BF16)

    tq = q_ref.shape[1]
    tk = ATT_KEYS
    n_groups = tq // tk
    qi = pl.program_id(2)
    q = jnp.concatenate([q_ref[0], qa_ref[0, 0]], axis=1)
    ones = jnp.ones((tk, HEAD_DIM), BF16)

    m_scr[...] = jnp.full_like(m_scr, NEG_BIG)
    acc_scr[...] = jnp.zeros_like(acc_scr)

    def scores(kb, slot, first_group=0):
        k0 = pl.multiple_of(kb * tk, tk)
        kt = jnp.concatenate([k_ref[0, pl.ds(k0, tk), :], ka_ref[0, 0, pl.ds(k0, tk), :]], axis=1)
        r0 = first_group * tk
        s_scr[slot, r0:, :] = lax.dot_general(q[r0:], kt, _NT, preferred_element_type=F32)

    def update(kb, slot, diag_group=None):
        k0 = pl.multiple_of(kb * tk, tk)
        vt = jnp.concatenate([v_ref[0, pl.ds(k0, tk), :], ones], axis=1)
        for g in range(n_groups):
            if diag_group is not None and g < diag_group:
                continue
            rs = slice(g * tk, (g + 1) * tk)
            s = s_scr[slot, rs, :]
            if g == diag_group:
                qpos = lax.broadcasted_iota(jnp.int32, (tk, tk), 0)
                kpos = lax.broadcasted_iota(jnp.int32, (tk, tk), 1)
                s = jnp.where(kpos <= qpos, s, NEG_BIG)
            m_prev = m_scr[rs]
            m_new = jnp.maximum(m_prev, jnp.max(s, axis=1, keepdims=True))
            alpha = jnp.exp2(m_prev - m_new)
            p = jnp.exp2(s - m_new)
            acc_scr[rs] = alpha * acc_scr[rs] + _dot(p.astype(BF16), vt)
            m_scr[rs] = m_new

    def body(i, carry):
        kb = n_groups * i
        for g in range(n_groups):
            scores(kb + g + 1, (g + 1) % 2)
            update(kb + g, g % 2)
        return carry

    assert n_groups % 2 == 0
    scores(0, 0)
    lax.fori_loop(0, qi, body, 0)
    kb = n_groups * qi
    for g in range(n_groups):
        if g + 1 < n_groups:
            scores(kb + g + 1, (g + 1) % 2, first_group=g + 1)
        update(kb + g, g % 2, diag_group=g)

    o_ref[0] = acc_scr[:, :HEAD_DIM] / acc_scr[:, HEAD_DIM:]


BF16_ROWS = 16


def _cast_chunks(w, n_steps):
    cols = w.shape[-1]
    total_rows = w.size // cols
    n_chunks = n_steps
    while total_rows % (n_chunks * BF16_ROWS):
        n_chunks //= 2
    return w.reshape(n_chunks, total_rows // n_chunks, cols)


def attention(qkv, q_aug, k_aug, n_heads, cast_weights):
    bsz, seq, _ = qkv.shape
    t = min(ATT_QROWS, seq)
    nq = seq // t
    n_steps = bsz * n_heads * nq
    srcs = [_cast_chunks(w, n_steps) for w in cast_weights]

    def chunk_spec(a):
        per = n_steps // a.shape[0]
        return pl.BlockSpec((1,) + a.shape[1:], lambda b, h, i: (((b * n_heads + h) * nq + i) // per, 0, 0))

    outs = pl.pallas_call(
        functools.partial(_attn_kernel, n_cast=len(srcs)),
        grid=(bsz, n_heads, nq),
        in_specs=[pl.BlockSpec((1, t, HEAD_DIM), lambda b, h, i: (b, i, h)),
                  pl.BlockSpec((1, 1, t, LANES), lambda b, h, i: (b, h, i, 0)),
                  pl.BlockSpec((1, seq, HEAD_DIM), lambda b, h, i: (b, 0, n_heads + h)),
                  pl.BlockSpec((1, 1, seq, LANES), lambda b, h, i: (b, h, 0, 0)),
                  pl.BlockSpec((1, seq, HEAD_DIM), lambda b, h, i: (b, 0, 2 * n_heads + h))]
                 + [chunk_spec(a) for a in srcs],
        out_specs=[pl.BlockSpec((1, t, HEAD_DIM), lambda b, h, i: (b, i, h))] + [chunk_spec(a) for a in srcs],
        out_shape=[jax.ShapeDtypeStruct((bsz, seq, n_heads * HEAD_DIM), F32)]
                  + [jax.ShapeDtypeStruct(a.shape, BF16) for a in srcs],
        scratch_shapes=[pltpu.VMEM((t, 1), F32), pltpu.VMEM((t, 2 * HEAD_DIM), F32),
                        pltpu.VMEM((2, t, ATT_KEYS), F32)],
        compiler_params=_params(("arbitrary", "arbitrary", "arbitrary"), 56),
        name="attention",
    )(qkv, q_aug, qkv, k_aug, qkv, *srcs)
    return outs[0], [o.reshape(w.shape) for o, w in zip(outs[1:], cast_weights)]


def _blockdiag(p):
    g, c, n = p.shape
    ns = g // GROUPS_PER_SLAB
    eye = jnp.eye(GROUPS_PER_SLAB, dtype=p.dtype)
    out = p.reshape(ns, GROUPS_PER_SLAB, c, 1, n) * eye[None, :, None, :, None]
    return out.reshape(ns, GROUPS_PER_SLAB * c, GROUPS_PER_SLAB * n)


def _ssm_prep_kernel(lr_ref, li_ref, ldt_ref, bre_ref, bim_ref, cre_ref, cim_ref, w1_ref, ft_ref, al_ref):
    L = SSM_CHUNK
    lr = lr_ref[0]
    li = li_ref[0]
    dt = jnp.exp(ldt_ref[0])
    mag = jnp.exp(lr * dt)
    a_re = mag * jnp.cos(li * dt)
    a_im = mag * jnp.sin(li * dt)
    den = lr * lr + li * li
    z_re = ((a_re - 1.0) * lr + a_im * li) / den
    z_im = (a_im * lr - (a_re - 1.0) * li) / den
    br = bre_ref[0]
    bi = bim_ref[0]
    bb_re = z_re * br - z_im * bi
    bb_im = z_re * bi + z_im * br
    cr = cre_ref[0]
    ci = cim_ref[0]
    ft0_hi, ft0_lo = _split2(jnp.concatenate([cr, -ci], axis=1))

    def power(d):
        m = jnp.exp(lr * dt * d)
        return m * jnp.cos(li * dt * d), m * jnp.sin(li * dt * d)

    w1_ref[0, :, :L * LANES] = jnp.zeros((L * LANES, L * LANES), BF16)
    for d in range(L):
        pr, pi = power(float(d))
        xe = jnp.concatenate([bb_re * pr - bb_im * pi, bb_re * pi + bb_im * pr], axis=1)
        j = L - 1 - d
        w1_ref[0, j * LANES:(j + 1) * LANES, L * LANES:] = xe.astype(BF16)
        xe_hi, xe_lo = _split2(xe)
        m_d = (_dot_nt(xe_hi, ft0_hi) + _dot_nt(xe_hi, ft0_lo) + _dot_nt(xe_lo, ft0_hi)).astype(BF16)
        for jj in range(L - d):
            w1_ref[0, jj * LANES:(jj + 1) * LANES, (jj + d) * LANES:(jj + d + 1) * LANES] = m_d
        pr1, pi1 = power(float(d + 1))
        ft_ref[0, d * LANES:(d + 1) * LANES, :] = jnp.concatenate(
            [cr * pr1 - ci * pi1, -(cr * pi1 + ci * pr1)], axis=1).astype(BF16)
    prl, pil = power(float(L))
    al_ref[0] = jnp.concatenate([prl, pil], axis=1)


def ssm_prep(lam_re, lam_im, log_dt, b_re, b_im, c_re, c_im):
    g, n = lam_re.shape
    ns = g // GROUPS_PER_SLAB
    L = SSM_CHUNK
    rowvec = lambda a: a.reshape(ns, 1, SLAB_STATE)
    args = (rowvec(lam_re), rowvec(lam_im), rowvec(jnp.repeat(log_dt, n)),
            _blockdiag(b_re.transpose(0, 2, 1)), _blockdiag(b_im.transpose(0, 2, 1)),
            _blockdiag(c_re), _blockdiag(c_im))
    vec_spec = pl.BlockSpec((1, 1, SLAB_STATE), lambda s: (s, 0, 0))
    mat_spec = pl.BlockSpec((1, LANES, SLAB_STATE), lambda s: (s, 0, 0))
    return pl.pallas_call(
        _ssm_prep_kernel,
        grid=(ns,),
        in_specs=[vec_spec] * 3 + [mat_spec] * 4,
        out_specs=[pl.BlockSpec((1, L * LANES, L * LANES + 2 * SLAB_STATE), lambda s: (s, 0, 0)),
                   pl.BlockSpec((1, L * LANES, 2 * SLAB_STATE), lambda s: (s, 0, 0)),
                   pl.BlockSpec((1, 1, 2 * SLAB_STATE), lambda s: (s, 0, 0))],
        out_shape=[jax.ShapeDtypeStruct((ns, L * LANES, L * LANES + 2 * SLAB_STATE), BF16),
                   jax.ShapeDtypeStruct((ns, L * LANES, 2 * SLAB_STATE), BF16),
                   jax.ShapeDtypeStruct((ns, 1, 2 * SLAB_STATE), F32)],
        compiler_params=_params(("arbitrary",), 48),
        name="ssm_prep",
    )(*args)


def _ssm_kernel(u_ref, w1_ref, ft_ref, al_ref, d_ref, y_ref, uf_scr, e_scr, y_scr):
    L = SSM_CHUNK
    nch = uf_scr.shape[0]
    lc = L * LANES
    for j in range(L):
        uf_scr[:, j * LANES:(j + 1) * LANES] = u_ref[0, pl.ds(j, nch, stride=L), :].astype(BF16)
    uf = uf_scr[...]
    e_scr[...] = _dot(uf, w1_ref[0, :, lc:])

    a_re = al_ref[0, :, :SLAB_STATE]
    a_im = al_ref[0, :, SLAB_STATE:]
    h_re = jnp.zeros((1, SLAB_STATE), F32)
    h_im = jnp.zeros((1, SLAB_STATE), F32)
    for k in range(nch):
        e = e_scr[k:k + 1, :]
        e_scr[k:k + 1, :] = jnp.concatenate([h_re, h_im], axis=1)
        h_re, h_im = (a_re * h_re - a_im * h_im + e[:, :SLAB_STATE],
                      a_re * h_im + a_im * h_re + e[:, SLAB_STATE:])

    for c in range(lc // SSM_COLS):
        k_hi = (c + 1) * SSM_COLS
        cols = slice(c * SSM_COLS, k_hi)
        y_scr[:, cols] = _dot(uf[:, :k_hi], w1_ref[0, :k_hi, cols])
    y = y_scr[...] + lax.dot_general(e_scr[...].astype(BF16), ft_ref[0], _NT, preferred_element_type=F32)
    for i in range(L):
        yi = y[:, i * LANES:(i + 1) * LANES] + d_ref[0] * u_ref[0, pl.ds(i, nch, stride=L), :]
        y_ref[0, pl.ds(i, nch, stride=L), :] = jax.nn.gelu(yi)


def ssm_scan(u, w1, ft, al, d_skip):
    bsz, seq, c = u.shape
    ns = c // LANES
    L = SSM_CHUNK
    nch = seq // L
    lc = L * LANES
    once = pl.Buffered(1)
    return pl.pallas_call(
        _ssm_kernel,
        grid=(ns, bsz),
        in_specs=[pl.BlockSpec((1, seq, LANES), lambda s, b: (b, 0, s)),
                  pl.BlockSpec((1, lc, lc + 2 * SLAB_STATE), lambda s, b: (s, 0, 0), pipeline_mode=once),
                  pl.BlockSpec((1, lc, 2 * SLAB_STATE), lambda s, b: (s, 0, 0), pipeline_mode=once),
                  pl.BlockSpec((1, 1, 2 * SLAB_STATE), lambda s, b: (s, 0, 0)),
                  pl.BlockSpec((1, 1, LANES), lambda s, b: (s, 0, 0))],
        out_specs=pl.BlockSpec((1, seq, LANES), lambda s, b: (b, 0, s)),
        out_shape=jax.ShapeDtypeStruct((bsz, seq, c), F32),
        scratch_shapes=[pltpu.VMEM((nch, lc), BF16), pltpu.VMEM((nch, 2 * SLAB_STATE), F32),
                        pltpu.VMEM((nch, lc), F32)],
        compiler_params=_params(("arbitrary", "arbitrary"), 56),
        name="ssm_scan",
    )(u, w1, ft, al, d_skip.reshape(ns, 1, LANES))


def _glu_kernel(y_ref, w_ref, b_ref, g_ref, o_ref):
    y = y_ref[...]
    o = y * jax.nn.sigmoid(_dot(y.astype(BF16), w_ref[...]) + b_ref[...])
    ms = jnp.mean(o * o, axis=-1, keepdims=True)
    o_ref[...] = (o * lax.rsqrt(ms + RMS_EPS) * g_ref[...]).astype(BF16)


def glu_norm(y, w_glu, b_glu, g):
    t, c = y.shape
    tm = 512
    return pl.pallas_call(
        _glu_kernel,
        grid=(t // tm,),
        in_specs=[pl.BlockSpec((tm, c), lambda i: (i, 0)),
                  pl.BlockSpec((c, c), lambda i: (0, 0)),
                  pl.BlockSpec((1, c), lambda i: (0, 0)),
                  pl.BlockSpec((1, c), lambda i: (0, 0))],
        out_specs=pl.BlockSpec((tm, c), lambda i: (i, 0)),
        out_shape=jax.ShapeDtypeStruct((t, c), BF16),
        compiler_params=_params(("arbitrary",), 48),
        name="glu_norm",
    )(y, w_glu, b_glu.reshape(1, c), g.reshape(1, c))


def _outproj_kernel(attn_ref, ga_ref, ssm_ref, w_ref, x_ref, gate_ref, lng_ref, lnb_ref, sc2_ref, sh2_ref,
                    wrh_ref, wrl_ref, x1_ref, h2_ref, lg_ref, a_scr, *, alpha, n_att):
    j = pl.program_id(2)
    nj = pl.num_programs(2)
    tn = w_ref.shape[1]

    tm = x1_ref.shape[1]
    n_chunks = tm // ROW_CHUNK

    @pl.when(j == 0)
    def _():
        def norm(c, carry):
            rs = pl.ds(pl.multiple_of(c * ROW_CHUNK, ROW_CHUNK), ROW_CHUNK)
            a = attn_ref[0, rs, :]
            ms = jnp.mean(a * a, axis=-1, keepdims=True)
            a_scr[rs, :n_att] = (a * lax.rsqrt(ms + RMS_EPS) * ga_ref[...]).astype(BF16)
            return carry
        lax.fori_loop(0, n_chunks, norm, 0)
        a_scr[:, n_att:] = ssm_ref[0]

    mixed = _dot(a_scr[...], w_ref[...])
    col = pl.multiple_of(j * tn, tn)
    x1_ref[0, :, pl.ds(col, tn)] = alpha * x_ref[0] + (1.0 + gate_ref[0]) * mixed

    @pl.when(j == nj - 1)
    def _():
        def finish(c, carry):
            rs = pl.ds(pl.multiple_of(c * ROW_CHUNK, ROW_CHUNK), ROW_CHUNK)
            r = x1_ref[0, rs, :]
            mu = jnp.mean(r, axis=-1, keepdims=True)
            var = jnp.mean(jnp.square(r - mu), axis=-1, keepdims=True)
            x1 = (r - mu) * lax.rsqrt(var + LN_EPS) * lng_ref[...] + lnb_ref[...]
            x1_ref[0, rs, :] = x1
            h2 = x1 * (1.0 + sc2_ref[0]) + sh2_ref[0]
            hi = h2.astype(BF16)
            hi_f = hi.astype(F32)
            lo = (h2 - hi_f).astype(BF16)
            lg_ref[0, rs, :] = _dot(hi, wrh_ref[...]) + _dot(hi, wrl_ref[...]) + _dot(lo, wrh_ref[...])
            h2_ref[0, rs, :] = pack_bf16_pairs(hi_f)
            return carry
        lax.fori_loop(0, n_chunks, finish, 0)


def pack_bf16_pairs(x):
    n = x.shape[-1] // 2
    bits = lax.bitcast_convert_type(x, jnp.uint32)
    return bits[:, n:] | (bits[:, :n] >> 16)


def unpack_bf16_pairs(p):
    lo = lax.bitcast_convert_type(p << 16, F32).astype(BF16)
    hi = lax.bitcast_convert_type(p & jnp.uint32(0xFFFF0000), F32).astype(BF16)
    return lo, hi


def out_proj(attn, g_attn, ssm_n, w_out, x, gate1, ln_g, ln_b, scale2, shift2, wr_hi, wr_lo, alpha):
    bsz, seq, d = x.shape
    n_att = attn.shape[-1]
    n_ssm = ssm_n.shape[-1]
    k = n_att + n_ssm
    tm, tn = 512, 256
    row = lambda a: a.reshape(1, -1)
    full = lambda n: pl.BlockSpec((1, n), lambda b, i, j: (0, 0))
    return pl.pallas_call(
        functools.partial(_outproj_kernel, alpha=alpha, n_att=n_att),
        grid=(bsz, seq // tm, d // tn),
        in_specs=[pl.BlockSpec((1, tm, n_att), lambda b, i, j: (b, i, 0)),
                  full(n_att),
                  pl.BlockSpec((1, tm, n_ssm), lambda b, i, j: (b, i, 0)),
                  pl.BlockSpec((k, tn), lambda b, i, j: (0, j)),
                  pl.BlockSpec((1, tm, tn), lambda b, i, j: (b, i, j)),
                  pl.BlockSpec((1, 1, tn), lambda b, i, j: (b, 0, j)),
                  full(d), full(d),
                  pl.BlockSpec((1, 1, d), lambda b, i, j: (b, 0, 0)),
                  pl.BlockSpec((1, 1, d), lambda b, i, j: (b, 0, 0)),
                  pl.BlockSpec((d, LANES), lambda b, i, j: (0, 0)),
                  pl.BlockSpec((d, LANES), lambda b, i, j: (0, 0))],
        out_specs=[pl.BlockSpec((1, tm, d), lambda b, i, j: (b, i, 0)),
                   pl.BlockSpec((1, tm, d // 2), lambda b, i, j: (b, i, 0)),
                   pl.BlockSpec((1, tm, LANES), lambda b, i, j: (b, i, 0))],
        out_shape=[jax.ShapeDtypeStruct((bsz, seq, d), F32),
                   jax.ShapeDtypeStruct((bsz, seq, d // 2), jnp.uint32),
                   jax.ShapeDtypeStruct((bsz, seq, LANES), F32)],
        scratch_shapes=[pltpu.VMEM((tm, k), BF16)],
        compiler_params=_params(("arbitrary", "arbitrary", "arbitrary"), 56),
        name="out_proj",
    )(attn, row(g_attn), ssm_n, w_out, x, gate1, row(ln_g), row(ln_b), scale2, shift2, wr_hi, wr_lo)


def _moe_kernel(tok_ref, bexp_ref, short_ref, nact_ref, h_hbm, wg_ref, wu_ref, wd_ref, y_ref, xbuf, sem):
    i = pl.program_id(0)
    nact = nact_ref[0]
    rows = xbuf.shape[1]

    def start_gather(blk, slot):
        n_trips = jnp.where(short_ref[blk] == 1, rows // 2, rows) // GATHER_UNROLL

        def body(t, carry):
            for k in range(GATHER_UNROLL):
                r = t * GATHER_UNROLL + k
                tok = tok_ref[blk * rows + r]
                pltpu.make_async_copy(h_hbm.at[pl.ds(tok, 1)], xbuf.at[slot, pl.ds(r, 1)], sem.at[slot]).start()
            return carry
        lax.fori_loop(0, n_trips, body, 0)

    def run_block(slot, n):
        pltpu.make_async_copy(h_hbm.at[pl.ds(0, n)], xbuf.at[slot, pl.ds(0, n)], sem.at[slot]).wait()
        x_lo, x_hi = unpack_bf16_pairs(xbuf[slot, :n])
        half = x_lo.shape[1]
        g = _dot(x_lo, wg_ref[0, :half, :]) + _dot(x_hi, wg_ref[0, half:, :])
        u = _dot(x_lo, wu_ref[0, :half, :]) + _dot(x_hi, wu_ref[0, half:, :])
        act = (jax.nn.silu(g) * u).astype(BF16)
        y_ref[:n, :] = _dot(act, wd_ref[0])
        if n < rows:
            y_ref[n:, :] = jnp.zeros((rows - n, y_ref.shape[1]), F32)

    @pl.when(jnp.logical_and(i == 0, nact > 0))
    def _():
        start_gather(0, 0)

    @pl.when(i + 1 < nact)
    def _():
        start_gather(i + 1, (i + 1) % 2)

    active = i < nact
    short = short_ref[i] == 1

    @pl.when(jnp.logical_and(active, jnp.logical_not(short)))
    def _():
        run_block(i % 2, rows)

    @pl.when(jnp.logical_and(active, short))
    def _():
        run_block(i % 2, rows // 2)

    @pl.when(i >= nact)
    def _():
        y_ref[...] = jnp.zeros_like(y_ref)


def moe_experts(h2, tok_buf, block_expert, short_blocks, n_active, w_gate, w_up, w_down):
    t, dp = h2.shape
    d = 2 * dp
    n_rows = tok_buf.shape[0]
    rows = MOE_ROWS
    n_blocks = n_rows // rows
    de = w_gate.shape[-1]

    def wmap(i, tok, bexp, short, nact):
        return (bexp[jnp.minimum(i, jnp.maximum(nact[0] - 1, 0))], 0, 0)

    grid_spec = pltpu.PrefetchScalarGridSpec(
        num_scalar_prefetch=4,
        grid=(n_blocks,),
        in_specs=[pl.BlockSpec(memory_space=pl.ANY),
                  pl.BlockSpec((1, d, de), wmap),
                  pl.BlockSpec((1, d, de), wmap),
                  pl.BlockSpec((1, de, d), wmap)],
        out_specs=pl.BlockSpec((rows, d), lambda i, *_: (i, 0)),
        scratch_shapes=[pltpu.VMEM((2, rows, dp), jnp.uint32), pltpu.SemaphoreType.DMA((2,))],
    )
    return pl.pallas_call(
        _moe_kernel,
        grid_spec=grid_spec,
        out_shape=jax.ShapeDtypeStruct((n_rows, d), F32),
        compiler_params=_params(("arbitrary",), 56),
        name="moe_experts",
    )(tok_buf, block_expert, short_blocks, n_active, h2, w_gate, w_up, w_down)


def _combine_kernel(dest_ref, ys_hbm, rt_ref, x1_ref, gate_ref, lng_ref, lnb_ref, o_ref, ybuf, sem, *, alpha):
    i = pl.program_id(0)
    n = pl.num_programs(0)
    tm = x1_ref.shape[0]
    n_copies = TOP_K * tm

    def gather(blk, slot, start):
        if not start:
            pltpu.make_async_copy(ys_hbm.at[pl.ds(0, n_copies)], ybuf.at[slot], sem.at[slot]).wait()
            return

        def body(r, carry):
            row = dest_ref[blk * n_copies + r]
            pltpu.make_async_copy(ys_hbm.at[pl.ds(row, 1)], ybuf.at[slot, pl.ds(r, 1)], sem.at[slot]).start()
            return carry
        lax.fori_loop(0, n_copies, body, 0, unroll=GATHER_UNROLL)

    @pl.when(i == 0)
    def _():
        gather(0, 0, True)

    @pl.when(i + 1 < n)
    def _():
        gather(i + 1, (i + 1) % 2, True)

    slot = i % 2
    gather(i, slot, False)
    moe = rt_ref[:, ROUTE_W:ROUTE_W + 1] * ybuf[slot, :tm, :]
    for kk in range(1, TOP_K):
        moe = moe + rt_ref[:, ROUTE_W + kk:ROUTE_W + kk + 1] * ybuf[slot, kk * tm:(kk + 1) * tm, :]
    r = alpha * x1_ref[...] + (1.0 + gate_ref[0]) * moe
    mu = jnp.mean(r, axis=-1, keepdims=True)
    var = jnp.mean(jnp.square(r - mu), axis=-1, keepdims=True)
    o_ref[...] = (r - mu) * lax.rsqrt(var + LN_EPS) * lng_ref[...] + lnb_ref[...]


def moe_combine(ys, dest, table, x1, gate2, ln_g, ln_b, alpha, seq):
    t, d = x1.shape
    tm = 128
    tiles_per_seq = seq // tm
    grid_spec = pltpu.PrefetchScalarGridSpec(
        num_scalar_prefetch=1,
        grid=(t // tm,),
        in_specs=[pl.BlockSpec(memory_space=pl.ANY),
                  pl.BlockSpec((tm, LANES), lambda i, *_: (i, 0)),
                  pl.BlockSpec((tm, d), lambda i, *_: (i, 0)),
                  pl.BlockSpec((1, 1, d), lambda i, *_: (i // tiles_per_seq, 0, 0)),
                  pl.BlockSpec((1, d), lambda i, *_: (0, 0)),
                  pl.BlockSpec((1, d), lambda i, *_: (0, 0))],
        out_specs=pl.BlockSpec((tm, d), lambda i, *_: (i, 0)),
        scratch_shapes=[pltpu.VMEM((2, TOP_K * tm, d), F32), pltpu.SemaphoreType.DMA((2,))],
    )
    return pl.pallas_call(
        functools.partial(_combine_kernel, alpha=alpha),
        grid_spec=grid_spec,
        out_shape=jax.ShapeDtypeStruct((t, d), F32),
        compiler_params=_params(("arbitrary",), 32),
        name="moe_combine",
    )(dest, ys, table, x1, gate2, ln_g.reshape(1, d), ln_b.reshape(1, d))


def _route_kernel(lg_ref, b_ref, o_ref, cnt_ref, carry):
    i = pl.program_id(0)

    @pl.when(i == 0)
    def _():
        carry[...] = jnp.zeros_like(carry)

    tm = lg_ref.shape[0]
    x = lg_ref[...] + b_ref[...]
    lane = lax.broadcasted_iota(jnp.int32, (tm, LANES), 1)
    ninf = -jnp.inf

    def top(v):
        vmax = jnp.max(v, axis=1, keepdims=True)
        return vmax, jnp.min(jnp.where(v == vmax, lane, LANES), axis=1, keepdims=True)

    gmask = lane < N_EXPERT_GROUPS
    gmax, g_sel = top(jnp.where(gmask, x, ninf))
    p_group = 1.0 / jnp.sum(jnp.where(gmask, jnp.exp(x - gmax), 0.0), axis=1, keepdims=True)

    lo = N_EXPERT_GROUPS + g_sel * EXPERTS_PER_GROUP
    cur = jnp.where(jnp.logical_and(lane >= lo, lane < lo + EXPERTS_PER_GROUP), x, ninf)
    vals, idxs = [], []
    for _ in range(TOP_K):
        v, ix = top(cur)
        vals.append(v)
        idxs.append(ix)
        cur = jnp.where(lane == ix, ninf, cur)
    exps = [jnp.exp(v - vals[0]) for v in vals]
    den = functools.reduce(lambda a, c: a + c, exps)

    member = functools.reduce(jnp.logical_or, [lane == ix for ix in idxs])
    mf = jnp.where(member, 1.0, 0.0)
    r_i = lax.broadcasted_iota(jnp.int32, (tm, tm), 0)
    c_i = lax.broadcasted_iota(jnp.int32, (tm, tm), 1)
    before = _dot((c_i < r_i).astype(BF16), mf.astype(BF16)) + carry[...]
    carry[...] = carry[...] + jnp.sum(mf, axis=0, keepdims=True)
    cnt_ref[...] = carry[...]

    out = jnp.zeros((tm, LANES), F32)
    for k in range(TOP_K):
        rank = jnp.sum(jnp.where(lane == idxs[k], before, 0.0), axis=1, keepdims=True)
        out = jnp.where(lane == ROUTE_EID + k, (idxs[k] - N_EXPERT_GROUPS).astype(F32), out)
        out = jnp.where(lane == ROUTE_RANK + k, rank, out)
        out = jnp.where(lane == ROUTE_W + k, p_group * (exps[k] / den), out)
    o_ref[...] = out


def route(logits, b_rg, b_re, n_rows, tm_combine):
    n_tok = logits.shape[0]
    n_experts = N_EXPERT_GROUPS * EXPERTS_PER_GROUP
    bias = jnp.zeros((1, LANES), F32).at[0, :N_EXPERT_GROUPS].set(b_rg)
    bias = bias.at[0, N_EXPERT_GROUPS:N_EXPERT_GROUPS + n_experts].set(b_re)
    tm = 512
    table, cnt = pl.pallas_call(
        _route_kernel,
        grid=(n_tok // tm,),
        in_specs=[pl.BlockSpec((tm, LANES), lambda i: (i, 0)),
                  pl.BlockSpec((1, LANES), lambda i: (0, 0))],
        out_specs=[pl.BlockSpec((tm, LANES), lambda i: (i, 0)),
                   pl.BlockSpec((1, LANES), lambda i: (0, 0))],
        out_shape=[jax.ShapeDtypeStruct((n_tok, LANES), F32), jax.ShapeDtypeStruct((1, LANES), F32)],
        scratch_shapes=[pltpu.VMEM((1, LANES), F32)],
        compiler_params=_params(("arbitrary",), 16),
        name="route",
    )(logits, bias)

    eid = table[:, ROUTE_EID:ROUTE_EID + TOP_K].astype(jnp.int32)
    rank = table[:, ROUTE_RANK:ROUTE_RANK + TOP_K].astype(jnp.int32)
    counts = cnt[0, N_EXPERT_GROUPS:N_EXPERT_GROUPS + n_experts].astype(jnp.int32)
    padded = ((counts + MOE_ROWS - 1) // MOE_ROWS) * MOE_ROWS
    pends = jnp.cumsum(padded)
    pstarts = pends - padded
    onehot = eid[:, :, None] == jnp.arange(n_experts, dtype=jnp.int32)
    dest = jnp.sum(jnp.where(onehot, pstarts, 0), axis=-1) + rank
    tok = jnp.broadcast_to(jnp.arange(n_tok, dtype=jnp.int32)[:, None], (n_tok, TOP_K))
    tok_buf = jnp.zeros((n_rows,), jnp.int32).at[dest.reshape(-1)].set(tok.reshape(-1), unique_indices=True)
    n_blocks = n_rows // MOE_ROWS
    block_expert = jnp.clip(
        jnp.searchsorted(pends, jnp.arange(n_blocks, dtype=jnp.int32) * MOE_ROWS, side='right'),
        0, n_experts - 1).astype(jnp.int32)
    rows_left = (pstarts + counts)[block_expert] - jnp.arange(n_blocks, dtype=jnp.int32) * MOE_ROWS
    short_blocks = (rows_left <= MOE_ROWS // 2).astype(jnp.int32)
    n_active = (pends[-1] // MOE_ROWS).astype(jnp.int32).reshape(1)
    dest_tiles = dest.reshape(n_tok // tm_combine, tm_combine, TOP_K).transpose(0, 2, 1).reshape(-1)
    return table, tok_buf, block_expert, short_blocks, n_active, dest_tiles.astype(jnp.int32)


def _layer(x, c, w_ada, b_ada, w_in, b_forget, lam_re, lam_im, log_dt, b_re, b_im, c_re, c_im, d_skip,
           w_glu, b_glu, g_attn, g_ssm, w_out, ln1_g, ln1_b, w_rg, b_rg, w_re, b_re_r,
           w_gate, w_up, w_down, ln2_g, ln2_b, alpha):
    bsz, seq, d = x.shape
    n_tok = bsz * seq
    n_heads = b_forget.shape[0]
    d_att = n_heads * HEAD_DIM
    d_ssm = d_skip.shape[0] * d_skip.shape[1]

    mod = ada_mod(c, w_ada, b_ada).reshape(bsz, 1, -1)
    shift1, scale1, gate1, shift2, scale2, gate2 = jnp.split(mod, 6, axis=-1)

    n_qkv = 3 * d_att
    w_main = jnp.concatenate([w_in[:, :n_qkv], w_in[:, n_qkv + n_heads:]], axis=1).astype(BF16)
    w_f = jnp.zeros((d, LANES), BF16).at[:, :n_heads].set(w_in[:, n_qkv:n_qkv + n_heads].astype(BF16))
    qkv, u, f = in_proj(x, scale1, shift1, w_main, w_f, n_qkv)

    q_aug, k_aug = forget_cum(f, b_forget)
    attn, (w_glu_b, w_out_b, w_gate_b, w_up_b, w_down_b) = attention(
        qkv, q_aug, k_aug, n_heads, [w_glu, w_out, w_gate, w_up, w_down])

    w1, ft, al = ssm_prep(lam_re, lam_im, log_dt, b_re, b_im, c_re, c_im)
    y = ssm_scan(u, w1, ft, al, d_skip)
    ssm_n = glu_norm(y.reshape(n_tok, d_ssm), w_glu_b, b_glu, g_ssm).reshape(bsz, seq, d_ssm)

    n_experts = N_EXPERT_GROUPS * EXPERTS_PER_GROUP
    w_r = jnp.zeros((d, LANES), F32).at[:, :N_EXPERT_GROUPS].set(w_rg)
    w_r = w_r.at[:, N_EXPERT_GROUPS:N_EXPERT_GROUPS + n_experts].set(w_re)
    wr_hi = w_r.astype(BF16)
    wr_lo = (w_r - wr_hi.astype(F32)).astype(BF16)
    x1, h2, logits = out_proj(attn, g_attn, ssm_n, w_out_b, x, gate1, ln1_g, ln1_b,
                              scale2, shift2, wr_hi, wr_lo, alpha)

    n_assign = n_tok * TOP_K
    n_blocks = -(-(n_assign + n_experts * (MOE_ROWS - 1)) // MOE_ROWS)
    n_rows = n_blocks * MOE_ROWS
    table, tok_buf, block_expert, short_blocks, n_active, dest = route(
        logits.reshape(n_tok, LANES), b_rg, b_re_r, n_rows, 128)
    ys = moe_experts(h2.reshape(n_tok, d // 2), tok_buf, block_expert, short_blocks, n_active,
                     w_gate_b, w_up_b, w_down_b)
    out = moe_combine(ys, dest, table, x1.reshape(n_tok, d), gate2, ln2_g, ln2_b, alpha, seq)
    return out.reshape(bsz, seq, d)


def kernel(x, c, w_ada, b_ada, w_in, b_forget, ssm_lambda_re, ssm_lambda_im, ssm_log_dt, ssm_b_re, ssm_b_im,
           ssm_c_re, ssm_c_im, ssm_d, w_glu, b_glu, g_attn, g_ssm, w_out, ln1_g, ln1_b, w_router_group,
           b_router_group, w_router_expert, b_router_expert, w_gate, w_up, w_down, ln2_g, ln2_b):
    depth = w_ada.shape[0]
    alpha = (2.0 * depth) ** 0.25
    for l in range(depth):
        x = _layer(x, c, w_ada[l], b_ada[l], w_in[l], b_forget[l], ssm_lambda_re[l], ssm_lambda_im[l],
                   ssm_log_dt[l], ssm_b_re[l], ssm_b_im[l], ssm_c_re[l], ssm_c_im[l], ssm_d[l],
                   w_glu[l], b_glu[l], g_attn[l], g_ssm[l], w_out[l], ln1_g[l], ln1_b[l],
                   w_router_group[l], b_router_group[l], w_router_expert[l], b_router_expert[l],
                   w_gate[l], w_up[l], w_down[l], ln2_g[l], ln2_b[l], alpha)
    return x
```

```python
import functools
import math

import jax
import jax.numpy as jnp
from jax import lax
from jax.experimental import pallas as pl
from jax.experimental.pallas import tpu as pltpu

F32 = jnp.float32
BF16 = jnp.bfloat16

LANES = 128
HEAD_DIM = 128
SSM_GROUP = 16
SSM_STATE = 64
GROUPS_PER_SLAB = LANES // SSM_GROUP
SLAB_STATE = GROUPS_PER_SLAB * SSM_STATE
SSM_CHUNK = 16
SSM_COLS = 512
N_EXPERT_GROUPS = 8
EXPERTS_PER_GROUP = 8
TOP_K = 2
MOE_ROWS = 256
MOE_SUB = 64
GATHER_UNROLL = 8
ROW_CHUNK = 128
ROUTE_EID, ROUTE_RANK, ROUTE_W = 0, TOP_K, 2 * TOP_K
LN_EPS = 1e-5
RMS_EPS = 1e-6
NEG_BIG = -1e30
LOG2E = math.log2(math.e)
MIB = 1024 * 1024

_NT = (((1,), (1,)), ((), ()))


def _params(semantics, vmem_mib):
    return pltpu.CompilerParams(dimension_semantics=semantics, vmem_limit_bytes=vmem_mib * MIB)


def _dot(a, b):
    return jnp.dot(a, b, preferred_element_type=F32)


def _dot_nt(a, b):
    return lax.dot_general(a, b, _NT, preferred_element_type=F32)


def _split2(x):
    hi = x.astype(BF16)
    return hi, (x - hi.astype(F32)).astype(BF16)


def _ada_kernel(c_ref, w_ref, b_ref, o_ref):
    s = jax.nn.silu(c_ref[...]).astype(BF16)
    o_ref[...] = _dot(s, w_ref[...].astype(BF16)) + b_ref[...]


def ada_mod(c, w_ada, b_ada):
    bsz, d = c.shape
    n = w_ada.shape[1]
    rows = 8
    assert bsz <= rows
    cp = jnp.zeros((rows, d), F32).at[:bsz].set(c)
    tn = 512
    out = pl.pallas_call(
        _ada_kernel,
        grid=(n // tn,),
        in_specs=[pl.BlockSpec((rows, d), lambda j: (0, 0)),
                  pl.BlockSpec((d, tn), lambda j: (0, j)),
                  pl.BlockSpec((1, tn), lambda j: (0, j))],
        out_specs=pl.BlockSpec((rows, tn), lambda j: (0, j)),
        out_shape=jax.ShapeDtypeStruct((rows, n), F32),
        compiler_params=_params(("arbitrary",), 40),
        name="ada_mod",
    )(cp, w_ada, b_ada.reshape(1, n))
    return out[:bsz]


def _inproj_kernel(x_ref, sc_ref, sh_ref, w_ref, wf_ref, qkv_ref, u_ref, f_ref, h_scr, *,
                   n_q_tiles, n_qkv_tiles, q_scale):
    j = pl.program_id(2)

    @pl.when(j == 0)
    def _():
        hb = (x_ref[0] * (1.0 + sc_ref[0]) + sh_ref[0]).astype(BF16)
        h_scr[...] = hb
        f_ref[0] = _dot(hb, wf_ref[...])

    acc = _dot(h_scr[...], w_ref[...])

    @pl.when(j < n_q_tiles)
    def _():
        qkv_ref[0] = (acc * q_scale).astype(BF16)

    @pl.when(jnp.logical_and(j >= n_q_tiles, j < n_qkv_tiles))
    def _():
        qkv_ref[0] = acc.astype(BF16)

    @pl.when(j >= n_qkv_tiles)
    def _():
        u_ref[0] = acc


def in_proj(x, scale, shift, w_main, w_f, n_qkv):
    bsz, seq, d = x.shape
    n_all = w_main.shape[1]
    n_u = n_all - n_qkv
    tm, tn = 512, 1024
    nq = n_qkv // tn
    grid = (bsz, seq // tm, n_all // tn)
    return pl.pallas_call(
        functools.partial(_inproj_kernel, n_q_tiles=n_qkv // 3 // tn, n_qkv_tiles=nq,
                          q_scale=HEAD_DIM ** -0.5 * LOG2E),
        grid=grid,
        in_specs=[pl.BlockSpec((1, tm, d), lambda b, i, j: (b, i, 0)),
                  pl.BlockSpec((1, 1, d), lambda b, i, j: (b, 0, 0)),
                  pl.BlockSpec((1, 1, d), lambda b, i, j: (b, 0, 0)),
                  pl.BlockSpec((d, tn), lambda b, i, j: (0, j)),
                  pl.BlockSpec((d, LANES), lambda b, i, j: (0, 0))],
        out_specs=[pl.BlockSpec((1, tm, tn), lambda b, i, j: (b, i, jnp.minimum(j, nq - 1))),
                   pl.BlockSpec((1, tm, tn), lambda b, i, j: (b, i, jnp.maximum(j - nq, 0))),
                   pl.BlockSpec((1, tm, LANES), lambda b, i, j: (b, i, 0))],
        out_shape=[jax.ShapeDtypeStruct((bsz, seq, n_qkv), BF16),
                   jax.ShapeDtypeStruct((bsz, seq, n_u), F32),
                   jax.ShapeDtypeStruct((bsz, seq, LANES), F32)],
        scratch_shapes=[pltpu.VMEM((tm, d), BF16)],
        compiler_params=_params(("arbitrary", "arbitrary", "arbitrary"), 52),
        name="in_proj",
    )(x, scale, shift, w_main, w_f)


def _split3(x):
    p1 = x.astype(BF16)
    r1 = x - p1.astype(F32)
    p2 = r1.astype(BF16)
    p3 = (r1 - p2.astype(F32)).astype(BF16)
    return p1, p2, p3


N_PIECES = 3


def _cum_kernel(f_ref, b_ref, qa_ref, ka_ref, carry, *, n_heads):
    i = pl.program_id(1)

    @pl.when(i == 0)
    def _():
        carry[...] = jnp.zeros_like(carry)

    tc = f_ref.shape[1]
    lf = jax.nn.log_sigmoid(f_ref[0] + b_ref[...])
    row = lax.broadcasted_iota(jnp.int32, (tc, tc), 0)
    col = lax.broadcasted_iota(jnp.int32, (tc, tc), 1)
    tri = (col <= row).astype(BF16)
    p1, p2, p3 = _split3(lf)
    cs = _dot(tri, p1) + _dot(tri, p2) + _dot(tri, p3) + carry[...]
    carry[...] = cs[tc - 1:tc, :]

    pieces = jnp.concatenate(_split3(cs * LOG2E), axis=1)
    r = lax.broadcasted_iota(jnp.int32, (N_PIECES * LANES, LANES), 0)
    c = lax.broadcasted_iota(jnp.int32, (N_PIECES * LANES, LANES), 1)
    lane = lax.broadcasted_iota(jnp.int32, (tc, LANES), 1)
    ones_q = jnp.where(jnp.logical_and(lane >= N_PIECES, lane < 2 * N_PIECES), 1.0, 0.0)
    ones_k = jnp.where(lane < N_PIECES, 1.0, 0.0)
    for h in range(n_heads):
        sel_q = (r == c * LANES + h).astype(BF16)
        sel_k = (r == (c - N_PIECES) * LANES + h).astype(BF16)
        qa_ref[0, h] = (_dot(pieces, sel_q) + ones_q).astype(BF16)
        ka_ref[0, h] = (ones_k - _dot(pieces, sel_k)).astype(BF16)


def forget_cum(f, b_forget):
    bsz, seq, _ = f.shape
    n_heads = b_forget.shape[0]
    tc = 256
    bpad = jnp.zeros((1, LANES), F32).at[0, :n_heads].set(b_forget)
    out_spec = pl.BlockSpec((1, n_heads, tc, LANES), lambda b, i: (b, 0, i, 0))
    out_shape = jax.ShapeDtypeStruct((bsz, n_heads, seq, LANES), BF16)
    return pl.pallas_call(
        functools.partial(_cum_kernel, n_heads=n_heads),
        grid=(bsz, seq // tc),
        in_specs=[pl.BlockSpec((1, tc, LANES), lambda b, i: (b, i, 0)),
                  pl.BlockSpec((1, LANES), lambda b, i: (0, 0))],
        out_specs=[out_spec, out_spec],
        out_shape=[out_shape, out_shape],
        scratch_shapes=[pltpu.VMEM((1, LANES), F32)],
        compiler_params=_params(("arbitrary", "arbitrary"), 24),
        name="forget_cum",
    )(f, bpad)


ATT_QROWS = 1024
ATT_KEYS = 512


def _attn_kernel(*refs, n_cast):
    q_ref, qa_ref, k_ref, ka_ref, v_ref = refs[:5]
    src_refs = refs[5:5 + n_cast]
    o_ref = refs[5 + n_cast]
    dst_refs = refs[6 + n_cast:6 + 2 * n_cast]
    m_scr, acc_scr, s_scr = refs[6 + 2 * n_cast:]
    for src, dst in zip(src_refs, dst_refs):
        dst[...] = src[...].astype(BF16)

    tq = q_ref.shape[1]
    tk = ATT_KEYS
    n_groups = tq // tk
    qi = pl.program_id(2)
    q = jnp.concatenate([q_ref[0], qa_ref[0, 0]], axis=1)
    ones = jnp.ones((tk, HEAD_DIM), BF16)

    m_scr[...] = jnp.full_like(m_scr, NEG_BIG)
    acc_scr[...] = jnp.zeros_like(acc_scr)

    def scores(kb, slot, first_group=0):
        k0 = pl.multiple_of(kb * tk, tk)
        kt = jnp.concatenate([k_ref[0, pl.ds(k0, tk), :], ka_ref[0, 0, pl.ds(k0, tk), :]], axis=1)
        r0 = first_group * tk
        s_scr[slot, r0:, :] = lax.dot_general(q[r0:], kt, _NT, preferred_element_type=F32)

    def update(kb, slot, diag_group=None):
        k0 = pl.multiple_of(kb * tk, tk)
        vt = jnp.concatenate([v_ref[0, pl.ds(k0, tk), :], ones], axis=1)
        for g in range(n_groups):
            if diag_group is not None and g < diag_group:
                continue
            rs = slice(g * tk, (g + 1) * tk)
            s = s_scr[slot, rs, :]
            if g == diag_group:
                qpos = lax.broadcasted_iota(jnp.int32, (tk, tk), 0)
                kpos = lax.broadcasted_iota(jnp.int32, (tk, tk), 1)
                s = jnp.where(kpos <= qpos, s, NEG_BIG)
            m_prev = m_scr[rs]
            m_new = jnp.maximum(m_prev, jnp.max(s, axis=1, keepdims=True))
            alpha = jnp.exp2(m_prev - m_new)
            p = jnp.exp2(s - m_new)
            acc_scr[rs] = alpha * acc_scr[rs] + _dot(p.astype(BF16), vt)
            m_scr[rs] = m_new

    def body(i, carry):
        kb = n_groups * i
        for g in range(n_groups):
            scores(kb + g + 1, (g + 1) % 2)
            update(kb + g, g % 2)
        return carry

    assert n_groups % 2 == 0
    scores(0, 0)
    lax.fori_loop(0, qi, body, 0)
    kb = n_groups * qi
    for g in range(n_groups):
        if g + 1 < n_groups:
            scores(kb + g + 1, (g + 1) % 2, first_group=g + 1)
        update(kb + g, g % 2, diag_group=g)

    o_ref[0] = acc_scr[:, :HEAD_DIM] / acc_scr[:, HEAD_DIM:]


BF16_ROWS = 16


def _cast_chunks(w, n_steps):
    cols = w.shape[-1]
    total_rows = w.size // cols
    n_chunks = n_steps
    while total_rows % (n_chunks * BF16_ROWS):
        n_chunks //= 2
    return w.reshape(n_chunks, total_rows // n_chunks, cols)


def attention(qkv, q_aug, k_aug, n_heads, cast_weights):
    bsz, seq, _ = qkv.shape
    t = min(ATT_QROWS, seq)
    nq = seq // t
    n_steps = bsz * n_heads * nq
    srcs = [_cast_chunks(w, n_steps) for w in cast_weights]

    def chunk_spec(a):
        per = n_steps // a.shape[0]
        return pl.BlockSpec((1,) + a.shape[1:], lambda b, h, i: (((b * n_heads + h) * nq + i) // per, 0, 0))

    outs = pl.pallas_call(
        functools.partial(_attn_kernel, n_cast=len(srcs)),
        grid=(bsz, n_heads, nq),
        in_specs=[pl.BlockSpec((1, t, HEAD_DIM), lambda b, h, i: (b, i, h)),
                  pl.BlockSpec((1, 1, t, LANES), lambda b, h, i: (b, h, i, 0)),
                  pl.BlockSpec((1, seq, HEAD_DIM), lambda b, h, i: (b, 0, n_heads + h)),
                  pl.BlockSpec((1, 1, seq, LANES), lambda b, h, i: (b, h, 0, 0)),
                  pl.BlockSpec((1, seq, HEAD_DIM), lambda b, h, i: (b, 0, 2 * n_heads + h))]
                 + [chunk_spec(a) for a in srcs],
        out_specs=[pl.BlockSpec((1, t, HEAD_DIM), lambda b, h, i: (b, i, h))] + [chunk_spec(a) for a in srcs],
        out_shape=[jax.ShapeDtypeStruct((bsz, seq, n_heads * HEAD_DIM), F32)]
                  + [jax.ShapeDtypeStruct(a.shape, BF16) for a in srcs],
        scratch_shapes=[pltpu.VMEM((t, 1), F32), pltpu.VMEM((t, 2 * HEAD_DIM), F32),
                        pltpu.VMEM((2, t, ATT_KEYS), F32)],
        compiler_params=_params(("arbitrary", "arbitrary", "arbitrary"), 56),
        name="attention",
    )(qkv, q_aug, qkv, k_aug, qkv, *srcs)
    return outs[0], [o.reshape(w.shape) for o, w in zip(outs[1:], cast_weights)]


def _blockdiag(p):
    g, c, n = p.shape
    ns = g // GROUPS_PER_SLAB
    eye = jnp.eye(GROUPS_PER_SLAB, dtype=p.dtype)
    out = p.reshape(ns, GROUPS_PER_SLAB, c, 1, n) * eye[None, :, None, :, None]
    return out.reshape(ns, GROUPS_PER_SLAB * c, GROUPS_PER_SLAB * n)


def _ssm_prep_kernel(lr_ref, li_ref, ldt_ref, bre_ref, bim_ref, cre_ref, cim_ref, w1_ref, ft_ref, al_ref):
    L = SSM_CHUNK
    lr = lr_ref[0]
    li = li_ref[0]
    dt = jnp.exp(ldt_ref[0])
    mag = jnp.exp(lr * dt)
    a_re = mag * jnp.cos(li * dt)
    a_im = mag * jnp.sin(li * dt)
    den = lr * lr + li * li
    z_re = ((a_re - 1.0) * lr + a_im * li) / den
    z_im = (a_im * lr - (a_re - 1.0) * li) / den
    br = bre_ref[0]
    bi = bim_ref[0]
    bb_re = z_re * br - z_im * bi
    bb_im = z_re * bi + z_im * br
    cr = cre_ref[0]
    ci = cim_ref[0]
    ft0_hi, ft0_lo = _split2(jnp.concatenate([cr, -ci], axis=1))

    def power(d):
        m = jnp.exp(lr * dt * d)
        return m * jnp.cos(li * dt * d), m * jnp.sin(li * dt * d)

    w1_ref[0, :, :L * LANES] = jnp.zeros((L * LANES, L * LANES), BF16)
    for d in range(L):
        pr, pi = power(float(d))
        xe = jnp.concatenate([bb_re * pr - bb_im * pi, bb_re * pi + bb_im * pr], axis=1)
        j = L - 1 - d
        w1_ref[0, j * LANES:(j + 1) * LANES, L * LANES:] = xe.astype(BF16)
        xe_hi, xe_lo = _split2(xe)
        m_d = (_dot_nt(xe_hi, ft0_hi) + _dot_nt(xe_hi, ft0_lo) + _dot_nt(xe_lo, ft0_hi)).astype(BF16)
        for jj in range(L - d):
            w1_ref[0, jj * LANES:(jj + 1) * LANES, (jj + d) * LANES:(jj + d + 1) * LANES] = m_d
        pr1, pi1 = power(float(d + 1))
        ft_ref[0, d * LANES:(d + 1) * LANES, :] = jnp.concatenate(
            [cr * pr1 - ci * pi1, -(cr * pi1 + ci * pr1)], axis=1).astype(BF16)
    prl, pil = power(float(L))
    al_ref[0] = jnp.concatenate([prl, pil], axis=1)


def ssm_prep(lam_re, lam_im, log_dt, b_re, b_im, c_re, c_im):
    g, n = lam_re.shape
    ns = g // GROUPS_PER_SLAB
    L = SSM_CHUNK
    rowvec = lambda a: a.reshape(ns, 1, SLAB_STATE)
    args = (rowvec(lam_re), rowvec(lam_im), rowvec(jnp.repeat(log_dt, n)),
            _blockdiag(b_re.transpose(0, 2, 1)), _blockdiag(b_im.transpose(0, 2, 1)),
            _blockdiag(c_re), _blockdiag(c_im))
    vec_spec = pl.BlockSpec((1, 1, SLAB_STATE), lambda s: (s, 0, 0))
    mat_spec = pl.BlockSpec((1, LANES, SLAB_STATE), lambda s: (s, 0, 0))
    return pl.pallas_call(
        _ssm_prep_kernel,
        grid=(ns,),
        in_specs=[vec_spec] * 3 + [mat_spec] * 4,
        out_specs=[pl.BlockSpec((1, L * LANES, L * LANES + 2 * SLAB_STATE), lambda s: (s, 0, 0)),
                   pl.BlockSpec((1, L * LANES, 2 * SLAB_STATE), lambda s: (s, 0, 0)),
                   pl.BlockSpec((1, 1, 2 * SLAB_STATE), lambda s: (s, 0, 0))],
        out_shape=[jax.ShapeDtypeStruct((ns, L * LANES, L * LANES + 2 * SLAB_STATE), BF16),
                   jax.ShapeDtypeStruct((ns, L * LANES, 2 * SLAB_STATE), BF16),
                   jax.ShapeDtypeStruct((ns, 1, 2 * SLAB_STATE), F32)],
        compiler_params=_params(("arbitrary",), 48),
        name="ssm_prep",
    )(*args)


def _ssm_kernel(u_ref, w1_ref, ft_ref, al_ref, d_ref, y_ref, uf_scr, e_scr, y_scr):
    L = SSM_CHUNK
    nch = uf_scr.shape[0]
    lc = L * LANES
    for j in range(L):
        uf_scr[:, j * LANES:(j + 1) * LANES] = u_ref[0, pl.ds(j, nch, stride=L), :].astype(BF16)
    uf = uf_scr[...]
    e_scr[...] = _dot(uf, w1_ref[0, :, lc:])

    a_re = al_ref[0, :, :SLAB_STATE]
    a_im = al_ref[0, :, SLAB_STATE:]
    h_re = jnp.zeros((1, SLAB_STATE), F32)
    h_im = jnp.zeros((1, SLAB_STATE), F32)
    for k in range(nch):
        e = e_scr[k:k + 1, :]
        e_scr[k:k + 1, :] = jnp.concatenate([h_re, h_im], axis=1)
        h_re, h_im = (a_re * h_re - a_im * h_im + e[:, :SLAB_STATE],
                      a_re * h_im + a_im * h_re + e[:, SLAB_STATE:])

    for c in range(lc // SSM_COLS):
        k_hi = (c + 1) * SSM_COLS
        cols = slice(c * SSM_COLS, k_hi)
        y_scr[:, cols] = _dot(uf[:, :k_hi], w1_ref[0, :k_hi, cols])
    y = y_scr[...] + lax.dot_general(e_scr[...].astype(BF16), ft_ref[0], _NT, preferred_element_type=F32)
    for i in range(L):
        yi = y[:, i * LANES:(i + 1) * LANES] + d_ref[0] * u_ref[0, pl.ds(i, nch, stride=L), :]
        y_ref[0, pl.ds(i, nch, stride=L), :] = jax.nn.gelu(yi)


def ssm_scan(u, w1, ft, al, d_skip):
    bsz, seq, c = u.shape
    ns = c // LANES
    L = SSM_CHUNK
    nch = seq // L
    lc = L * LANES
    once = pl.Buffered(1)
    return pl.pallas_call(
        _ssm_kernel,
        grid=(ns, bsz),
        in_specs=[pl.BlockSpec((1, seq, LANES), lambda s, b: (b, 0, s)),
                  pl.BlockSpec((1, lc, lc + 2 * SLAB_STATE), lambda s, b: (s, 0, 0), pipeline_mode=once),
                  pl.BlockSpec((1, lc, 2 * SLAB_STATE), lambda s, b: (s, 0, 0), pipeline_mode=once),
                  pl.BlockSpec((1, 1, 2 * SLAB_STATE), lambda s, b: (s, 0, 0)),
                  pl.BlockSpec((1, 1, LANES), lambda s, b: (s, 0, 0))],
        out_specs=pl.BlockSpec((1, seq, LANES), lambda s, b: (b, 0, s)),
        out_shape=jax.ShapeDtypeStruct((bsz, seq, c), F32),
        scratch_shapes=[pltpu.VMEM((nch, lc), BF16), pltpu.VMEM((nch, 2 * SLAB_STATE), F32),
                        pltpu.VMEM((nch, lc), F32)],
        compiler_params=_params(("arbitrary", "arbitrary"), 56),
        name="ssm_scan",
    )(u, w1, ft, al, d_skip.reshape(ns, 1, LANES))


def _glu_kernel(y_ref, w_ref, b_ref, g_ref, o_ref):
    y = y_ref[...]
    o = y * jax.nn.sigmoid(_dot(y.astype(BF16), w_ref[...]) + b_ref[...])
    ms = jnp.mean(o * o, axis=-1, keepdims=True)
    o_ref[...] = (o * lax.rsqrt(ms + RMS_EPS) * g_ref[...]).astype(BF16)


def glu_norm(y, w_glu, b_glu, g):
    t, c = y.shape
    tm = 512
    return pl.pallas_call(
        _glu_kernel,
        grid=(t // tm,),
        in_specs=[pl.BlockSpec((tm, c), lambda i: (i, 0)),
                  pl.BlockSpec((c, c), lambda i: (0, 0)),
                  pl.BlockSpec((1, c), lambda i: (0, 0)),
                  pl.BlockSpec((1, c), lambda i: (0, 0))],
        out_specs=pl.BlockSpec((tm, c), lambda i: (i, 0)),
        out_shape=jax.ShapeDtypeStruct((t, c), BF16),
        compiler_params=_params(("arbitrary",), 48),
        name="glu_norm",
    )(y, w_glu, b_glu.reshape(1, c), g.reshape(1, c))


def _outproj_kernel(attn_ref, ga_ref, ssm_ref, w_ref, x_ref, gate_ref, lng_ref, lnb_ref, sc2_ref, sh2_ref,
                    wrh_ref, wrl_ref, x1_ref, h2_ref, lg_ref, a_scr, *, alpha, n_att):
    j = pl.program_id(2)
    nj = pl.num_programs(2)
    tn = w_ref.shape[1]

    tm = x1_ref.shape[1]
    n_chunks = tm // ROW_CHUNK

    @pl.when(j == 0)
    def _():
        def norm(c, carry):
            rs = pl.ds(pl.multiple_of(c * ROW_CHUNK, ROW_CHUNK), ROW_CHUNK)
            a = attn_ref[0, rs, :]
            ms = jnp.mean(a * a, axis=-1, keepdims=True)
            a_scr[rs, :n_att] = (a * lax.rsqrt(ms + RMS_EPS) * ga_ref[...]).astype(BF16)
            return carry
        lax.fori_loop(0, n_chunks, norm, 0)
        a_scr[:, n_att:] = ssm_ref[0]

    mixed = _dot(a_scr[...], w_ref[...])
    col = pl.multiple_of(j * tn, tn)
    x1_ref[0, :, pl.ds(col, tn)] = alpha * x_ref[0] + (1.0 + gate_ref[0]) * mixed

    @pl.when(j == nj - 1)
    def _():
        def finish(c, carry):
            rs = pl.ds(pl.multiple_of(c * ROW_CHUNK, ROW_CHUNK), ROW_CHUNK)
            r = x1_ref[0, rs, :]
            mu = jnp.mean(r, axis=-1, keepdims=True)
            var = jnp.mean(jnp.square(r - mu), axis=-1, keepdims=True)
            x1 = (r - mu) * lax.rsqrt(var + LN_EPS) * lng_ref[...] + lnb_ref[...]
            x1_ref[0, rs, :] = x1
            h2 = x1 * (1.0 + sc2_ref[0]) + sh2_ref[0]
            hi = h2.astype(BF16)
            hi_f = hi.astype(F32)
            lo = (h2 - hi_f).astype(BF16)
            lg_ref[0, rs, :] = _dot(hi, wrh_ref[...]) + _dot(hi, wrl_ref[...]) + _dot(lo, wrh_ref[...])
            h2_ref[0, rs, :] = pack_bf16_pairs(hi_f)
            return carry
        lax.fori_loop(0, n_chunks, finish, 0)


def pack_bf16_pairs(x):
    n = x.shape[-1] // 2
    bits = lax.bitcast_convert_type(x, jnp.uint32)
    return bits[:, n:] | (bits[:, :n] >> 16)


def unpack_bf16_pairs(p):
    lo = lax.bitcast_convert_type(p << 16, F32).astype(BF16)
    hi = lax.bitcast_convert_type(p & jnp.uint32(0xFFFF0000), F32).astype(BF16)
    return lo, hi


def out_proj(attn, g_attn, ssm_n, w_out, x, gate1, ln_g, ln_b, scale2, shift2, wr_hi, wr_lo, alpha):
    bsz, seq, d = x.shape
    n_att = attn.shape[-1]
    n_ssm = ssm_n.shape[-1]
    k = n_att + n_ssm
    tm, tn = 512, 256
    row = lambda a: a.reshape(1, -1)
    full = lambda n: pl.BlockSpec((1, n), lambda b, i, j: (0, 0))
    return pl.pallas_call(
        functools.partial(_outproj_kernel, alpha=alpha, n_att=n_att),
        grid=(bsz, seq // tm, d // tn),
        in_specs=[pl.BlockSpec((1, tm, n_att), lambda b, i, j: (b, i, 0)),
                  full(n_att),
                  pl.BlockSpec((1, tm, n_ssm), lambda b, i, j: (b, i, 0)),
                  pl.BlockSpec((k, tn), lambda b, i, j: (0, j)),
                  pl.BlockSpec((1, tm, tn), lambda b, i, j: (b, i, j)),
                  pl.BlockSpec((1, 1, tn), lambda b, i, j: (b, 0, j)),
                  full(d), full(d),
                  pl.BlockSpec((1, 1, d), lambda b, i, j: (b, 0, 0)),
                  pl.BlockSpec((1, 1, d), lambda b, i, j: (b, 0, 0)),
                  pl.BlockSpec((d, LANES), lambda b, i, j: (0, 0)),
                  pl.BlockSpec((d, LANES), lambda b, i, j: (0, 0))],
        out_specs=[pl.BlockSpec((1, tm, d), lambda b, i, j: (b, i, 0)),
                   pl.BlockSpec((1, tm, d // 2), lambda b, i, j: (b, i, 0)),
                   pl.BlockSpec((1, tm, LANES), lambda b, i, j: (b, i, 0))],
        out_shape=[jax.ShapeDtypeStruct((bsz, seq, d), F32),
                   jax.ShapeDtypeStruct((bsz, seq, d // 2), jnp.uint32),
                   jax.ShapeDtypeStruct((bsz, seq, LANES), F32)],
        scratch_shapes=[pltpu.VMEM((tm, k), BF16)],
        compiler_params=_params(("arbitrary", "arbitrary", "arbitrary"), 56),
        name="out_proj",
    )(attn, row(g_attn), ssm_n, w_out, x, gate1, row(ln_g), row(ln_b), scale2, shift2, wr_hi, wr_lo)


def _moe_kernel(tok_ref, bexp_ref, nsub_ref, nact_ref, h_hbm, wg_ref, wu_ref, wd_ref, y_ref, xbuf, sem):
    i = pl.program_id(0)
    nact = nact_ref[0]
    rows = xbuf.shape[1]

    def start_gather(blk, slot):
        n_trips = nsub_ref[blk] * (MOE_SUB // GATHER_UNROLL)

        def body(t, carry):
            for k in range(GATHER_UNROLL):
                r = t * GATHER_UNROLL + k
                tok = tok_ref[blk * rows + r]
                pltpu.make_async_copy(h_hbm.at[pl.ds(tok, 1)], xbuf.at[slot, pl.ds(r, 1)], sem.at[slot]).start()
            return carry
        lax.fori_loop(0, n_trips, body, 0)

    def run_block(slot, n):
        pltpu.make_async_copy(h_hbm.at[pl.ds(0, n)], xbuf.at[slot, pl.ds(0, n)], sem.at[slot]).wait()
        x_lo, x_hi = unpack_bf16_pairs(xbuf[slot, :n])
        half = x_lo.shape[1]
        g = _dot(x_lo, wg_ref[0, :half, :]) + _dot(x_hi, wg_ref[0, half:, :])
        u = _dot(x_lo, wu_ref[0, :half, :]) + _dot(x_hi, wu_ref[0, half:, :])
        act = (jax.nn.silu(g) * u).astype(BF16)
        y_ref[:n, :] = _dot(act, wd_ref[0])
        if n < rows:
            y_ref[n:, :] = jnp.zeros((rows - n, y_ref.shape[1]), F32)

    @pl.when(jnp.logical_and(i == 0, nact > 0))
    def _():
        start_gather(0, 0)

    @pl.when(i + 1 < nact)
    def _():
        start_gather(i + 1, (i + 1) % 2)

    for n_sub in range(1, rows // MOE_SUB + 1):
        @pl.when(jnp.logical_and(i < nact, nsub_ref[i] == n_sub))
        def _():
            run_block(i % 2, n_sub * MOE_SUB)

    @pl.when(i >= nact)
    def _():
        y_ref[...] = jnp.zeros_like(y_ref)


def moe_experts(h2, tok_buf, block_expert, short_blocks, n_active, w_gate, w_up, w_down):
    t, dp = h2.shape
    d = 2 * dp
    n_rows = tok_buf.shape[0]
    rows = MOE_ROWS
    n_blocks = n_rows // rows
    de = w_gate.shape[-1]

    def wmap(i, tok, bexp, short, nact):
        return (bexp[jnp.minimum(i, jnp.maximum(nact[0] - 1, 0))], 0, 0)

    grid_spec = pltpu.PrefetchScalarGridSpec(
        num_scalar_prefetch=4,
        grid=(n_blocks,),
        in_specs=[pl.BlockSpec(memory_space=pl.ANY),
                  pl.BlockSpec((1, d, de), wmap),
                  pl.BlockSpec((1, d, de), wmap),
                  pl.BlockSpec((1, de, d), wmap)],
        out_specs=pl.BlockSpec((rows, d), lambda i, *_: (i, 0)),
        scratch_shapes=[pltpu.VMEM((2, rows, dp), jnp.uint32), pltpu.SemaphoreType.DMA((2,))],
    )
    return pl.pallas_call(
        _moe_kernel,
        grid_spec=grid_spec,
        out_shape=jax.ShapeDtypeStruct((n_rows, d), F32),
        compiler_params=_params(("arbitrary",), 56),
        name="moe_experts",
    )(tok_buf, block_expert, short_blocks, n_active, h2, w_gate, w_up, w_down)


def _combine_kernel(dest_ref, ys_hbm, rt_ref, x1_ref, gate_ref, lng_ref, lnb_ref, o_ref, ybuf, sem, *, alpha):
    i = pl.program_id(0)
    n = pl.num_programs(0)
    tm = x1_ref.shape[0]
    n_copies = TOP_K * tm

    def gather(blk, slot, start):
        if not start:
            pltpu.make_async_copy(ys_hbm.at[pl.ds(0, n_copies)], ybuf.at[slot], sem.at[slot]).wait()
            return

        def body(r, carry):
            row = dest_ref[blk * n_copies + r]
            pltpu.make_async_copy(ys_hbm.at[pl.ds(row, 1)], ybuf.at[slot, pl.ds(r, 1)], sem.at[slot]).start()
            return carry
        lax.fori_loop(0, n_copies, body, 0, unroll=GATHER_UNROLL)

    @pl.when(i == 0)
    def _():
        gather(0, 0, True)

    @pl.when(i + 1 < n)
    def _():
        gather(i + 1, (i + 1) % 2, True)

    slot = i % 2
    gather(i, slot, False)
    moe = rt_ref[:, ROUTE_W:ROUTE_W + 1] * ybuf[slot, :tm, :]
    for kk in range(1, TOP_K):
        moe = moe + rt_ref[:, ROUTE_W + kk:ROUTE_W + kk + 1] * ybuf[slot, kk * tm:(kk + 1) * tm, :]
    r = alpha * x1_ref[...] + (1.0 + gate_ref[0]) * moe
    mu = jnp.mean(r, axis=-1, keepdims=True)
    var = jnp.mean(jnp.square(r - mu), axis=-1, keepdims=True)
    o_ref[...] = (r - mu) * lax.rsqrt(var + LN_EPS) * lng_ref[...] + lnb_ref[...]


def moe_combine(ys, dest, table, x1, gate2, ln_g, ln_b, alpha, seq):
    t, d = x1.shape
    tm = 128
    tiles_per_seq = seq // tm
    grid_spec = pltpu.PrefetchScalarGridSpec(
        num_scalar_prefetch=1,
        grid=(t // tm,),
        in_specs=[pl.BlockSpec(memory_space=pl.ANY),
                  pl.BlockSpec((tm, LANES), lambda i, *_: (i, 0)),
                  pl.BlockSpec((tm, d), lambda i, *_: (i, 0)),
                  pl.BlockSpec((1, 1, d), lambda i, *_: (i // tiles_per_seq, 0, 0)),
                  pl.BlockSpec((1, d), lambda i, *_: (0, 0)),
                  pl.BlockSpec((1, d), lambda i, *_: (0, 0))],
        out_specs=pl.BlockSpec((tm, d), lambda i, *_: (i, 0)),
        scratch_shapes=[pltpu.VMEM((2, TOP_K * tm, d), F32), pltpu.SemaphoreType.DMA((2,))],
    )
    return pl.pallas_call(
        functools.partial(_combine_kernel, alpha=alpha),
        grid_spec=grid_spec,
        out_shape=jax.ShapeDtypeStruct((t, d), F32),
        compiler_params=_params(("arbitrary",), 32),
        name="moe_combine",
    )(dest, ys, table, x1, gate2, ln_g.reshape(1, d), ln_b.reshape(1, d))


def _route_kernel(lg_ref, b_ref, o_ref, cnt_ref, carry):
    i = pl.program_id(0)

    @pl.when(i == 0)
    def _():
        carry[...] = jnp.zeros_like(carry)

    tm = lg_ref.shape[0]
    x = lg_ref[...] + b_ref[...]
    lane = lax.broadcasted_iota(jnp.int32, (tm, LANES), 1)
    ninf = -jnp.inf

    def top(v):
        vmax = jnp.max(v, axis=1, keepdims=True)
        return vmax, jnp.min(jnp.where(v == vmax, lane, LANES), axis=1, keepdims=True)

    gmask = lane < N_EXPERT_GROUPS
    gmax, g_sel = top(jnp.where(gmask, x, ninf))
    p_group = 1.0 / jnp.sum(jnp.where(gmask, jnp.exp(x - gmax), 0.0), axis=1, keepdims=True)

    lo = N_EXPERT_GROUPS + g_sel * EXPERTS_PER_GROUP
    cur = jnp.where(jnp.logical_and(lane >= lo, lane < lo + EXPERTS_PER_GROUP), x, ninf)
    vals, idxs = [], []
    for _ in range(TOP_K):
        v, ix = top(cur)
        vals.append(v)
        idxs.append(ix)
        cur = jnp.where(lane == ix, ninf, cur)
    exps = [jnp.exp(v - vals[0]) for v in vals]
    den = functools.reduce(lambda a, c: a + c, exps)

    member = functools.reduce(jnp.logical_or, [lane == ix for ix in idxs])
    mf = jnp.where(member, 1.0, 0.0)
    r_i = lax.broadcasted_iota(jnp.int32, (tm, tm), 0)
    c_i = lax.broadcasted_iota(jnp.int32, (tm, tm), 1)
    before = _dot((c_i < r_i).astype(BF16), mf.astype(BF16)) + carry[...]
    carry[...] = carry[...] + jnp.sum(mf, axis=0, keepdims=True)
    cnt_ref[...] = carry[...]

    out = jnp.zeros((tm, LANES), F32)
    for k in range(TOP_K):
        rank = jnp.sum(jnp.where(lane == idxs[k], before, 0.0), axis=1, keepdims=True)
        out = jnp.where(lane == ROUTE_EID + k, (idxs[k] - N_EXPERT_GROUPS).astype(F32), out)
        out = jnp.where(lane == ROUTE_RANK + k, rank, out)
        out = jnp.where(lane == ROUTE_W + k, p_group * (exps[k] / den), out)
    o_ref[...] = out


def route(logits, b_rg, b_re, n_rows, tm_combine):
    n_tok = logits.shape[0]
    n_experts = N_EXPERT_GROUPS * EXPERTS_PER_GROUP
    bias = jnp.zeros((1, LANES), F32).at[0, :N_EXPERT_GROUPS].set(b_rg)
    bias = bias.at[0, N_EXPERT_GROUPS:N_EXPERT_GROUPS + n_experts].set(b_re)
    tm = 512
    table, cnt = pl.pallas_call(
        _route_kernel,
        grid=(n_tok // tm,),
        in_specs=[pl.BlockSpec((tm, LANES), lambda i: (i, 0)),
                  pl.BlockSpec((1, LANES), lambda i: (0, 0))],
        out_specs=[pl.BlockSpec((tm, LANES), lambda i: (i, 0)),
                   pl.BlockSpec((1, LANES), lambda i: (0, 0))],
        out_shape=[jax.ShapeDtypeStruct((n_tok, LANES), F32), jax.ShapeDtypeStruct((1, LANES), F32)],
        scratch_shapes=[pltpu.VMEM((1, LANES), F32)],
        compiler_params=_params(("arbitrary",), 16),
        name="route",
    )(logits, bias)

    eid = table[:, ROUTE_EID:ROUTE_EID + TOP_K].astype(jnp.int32)
    rank = table[:, ROUTE_RANK:ROUTE_RANK + TOP_K].astype(jnp.int32)
    counts = cnt[0, N_EXPERT_GROUPS:N_EXPERT_GROUPS + n_experts].astype(jnp.int32)
    padded = ((counts + MOE_ROWS - 1) // MOE_ROWS) * MOE_ROWS
    pends = jnp.cumsum(padded)
    pstarts = pends - padded
    onehot = eid[:, :, None] == jnp.arange(n_experts, dtype=jnp.int32)
    dest = jnp.sum(jnp.where(onehot, pstarts, 0), axis=-1) + rank
    tok = jnp.broadcast_to(jnp.arange(n_tok, dtype=jnp.int32)[:, None], (n_tok, TOP_K))
    tok_buf = jnp.zeros((n_rows,), jnp.int32).at[dest.reshape(-1)].set(tok.reshape(-1), unique_indices=True)
    n_blocks = n_rows // MOE_ROWS
    block_expert = jnp.clip(
        jnp.searchsorted(pends, jnp.arange(n_blocks, dtype=jnp.int32) * MOE_ROWS, side='right'),
        0, n_experts - 1).astype(jnp.int32)
    rows_left = (pstarts + counts)[block_expert] - jnp.arange(n_blocks, dtype=jnp.int32) * MOE_ROWS
    short_blocks = jnp.clip(-(-rows_left // MOE_SUB), 1, MOE_ROWS // MOE_SUB).astype(jnp.int32)
    n_active = (pends[-1] // MOE_ROWS).astype(jnp.int32).reshape(1)
    dest_tiles = dest.reshape(n_tok // tm_combine, tm_combine, TOP_K).transpose(0, 2, 1).reshape(-1)
    return table, tok_buf, block_expert, short_blocks, n_active, dest_tiles.astype(jnp.int32)


def _layer(x, c, w_ada, b_ada, w_in, b_forget, lam_re, lam_im, log_dt, b_re, b_im, c_re, c_im, d_skip,
           w_glu, b_glu, g_attn, g_ssm, w_out, ln1_g, ln1_b, w_rg, b_rg, w_re, b_re_r,
           w_gate, w_up, w_down, ln2_g, ln2_b, alpha):
    bsz, seq, d = x.shape
    n_tok = bsz * seq
    n_heads = b_forget.shape[0]
    d_att = n_heads * HEAD_DIM
    d_ssm = d_skip.shape[0] * d_skip.shape[1]

    mod = ada_mod(c, w_ada, b_ada).reshape(bsz, 1, -1)
    shift1, scale1, gate1, shift2, scale2, gate2 = jnp.split(mod, 6, axis=-1)

    n_qkv = 3 * d_att
    w_main = jnp.concatenate([w_in[:, :n_qkv], w_in[:, n_qkv + n_heads:]], axis=1).astype(BF16)
    w_f = jnp.zeros((d, LANES), BF16).at[:, :n_heads].set(w_in[:, n_qkv:n_qkv + n_heads].astype(BF16))
    qkv, u, f = in_proj(x, scale1, shift1, w_main, w_f, n_qkv)

    q_aug, k_aug = forget_cum(f, b_forget)
    attn, (w_glu_b, w_out_b, w_gate_b, w_up_b, w_down_b) = attention(
        qkv, q_aug, k_aug, n_heads, [w_glu, w_out, w_gate, w_up, w_down])

    w1, ft, al = ssm_prep(lam_re, lam_im, log_dt, b_re, b_im, c_re, c_im)
    y = ssm_scan(u, w1, ft, al, d_skip)
    ssm_n = glu_norm(y.reshape(n_tok, d_ssm), w_glu_b, b_glu, g_ssm).reshape(bsz, seq, d_ssm)

    n_experts = N_EXPERT_GROUPS * EXPERTS_PER_GROUP
    w_r = jnp.zeros((d, LANES), F32).at[:, :N_EXPERT_GROUPS].set(w_rg)
    w_r = w_r.at[:, N_EXPERT_GROUPS:N_EXPERT_GROUPS + n_experts].set(w_re)
    wr_hi = w_r.astype(BF16)
    wr_lo = (w_r - wr_hi.astype(F32)).astype(BF16)
    x1, h2, logits = out_proj(attn, g_attn, ssm_n, w_out_b, x, gate1, ln1_g, ln1_b,
                              scale2, shift2, wr_hi, wr_lo, alpha)

    n_assign = n_tok * TOP_K
    n_blocks = -(-(n_assign + n_experts * (MOE_ROWS - 1)) // MOE_ROWS)
    n_rows = n_blocks * MOE_ROWS
    table, tok_buf, block_expert, short_blocks, n_active, dest = route(
        logits.reshape(n_tok, LANES), b_rg, b_re_r, n_rows, 128)
    ys = moe_experts(h2.reshape(n_tok, d // 2), tok_buf, block_expert, short_blocks, n_active,
                     w_gate_b, w_up_b, w_down_b)
    out = moe_combine(ys, dest, table, x1.reshape(n_tok, d), gate2, ln2_g, ln2_b, alpha, seq)
    return out.reshape(bsz, seq, d)


def kernel(x, c, w_ada, b_ada, w_in, b_forget, ssm_lambda_re, ssm_lambda_im, ssm_log_dt, ssm_b_re, ssm_b_im,
           ssm_c_re, ssm_c_im, ssm_d, w_glu, b_glu, g_attn, g_ssm, w_out, ln1_g, ln1_b, w_router_group,
           b_router_group, w_router_expert, b_router_expert, w_gate, w_up, w_down, ln2_g, ln2_b):
    depth = w_ada.shape[0]
    alpha = (2.0 * depth) ** 0.25
    for l in range(depth):
        x = _layer(x, c, w_ada[l], b_ada[l], w_in[l], b_forget[l], ssm_lambda_re[l], ssm_lambda_im[l],
                   ssm_log_dt[l], ssm_b_re[l], ssm_b_im[l], ssm_c_re[l], ssm_c_im[l], ssm_d[l],
                   w_glu[l], b_glu[l], g_attn[l], g_ssm[l], w_out[l], ln1_g[l], ln1_b[l],
                   w_router_group[l], b_router_group[l], w_router_expert[l], b_router_expert[l],
                   w_gate[l], w_up[l], w_down[l], ln2_g[l], ln2_b[l], alpha)
    return x
```

```python
import functools
import math

import jax
import jax.numpy as jnp
from jax import lax
from jax.experimental import pallas as pl
from jax.experimental.pallas import tpu as pltpu

F32 = jnp.float32
BF16 = jnp.bfloat16

LANES = 128
HEAD_DIM = 128
SSM_GROUP = 16
SSM_STATE = 64
GROUPS_PER_SLAB = LANES // SSM_GROUP
SLAB_STATE = GROUPS_PER_SLAB * SSM_STATE
SSM_CHUNK = 16
SSM_COLS = 512
N_EXPERT_GROUPS = 8
EXPERTS_PER_GROUP = 8
TOP_K = 2
MOE_ROWS = 256
MOE_SUB = 64
GATHER_UNROLL = 8
ROW_CHUNK = 128
ROUTE_EID, ROUTE_RANK, ROUTE_W = 0, TOP_K, 2 * TOP_K
LN_EPS = 1e-5
RMS_EPS = 1e-6
NEG_BIG = -1e30
LOG2E = math.log2(math.e)
MIB = 1024 * 1024

_NT = (((1,), (1,)), ((), ()))


def _params(semantics, vmem_mib):
    return pltpu.CompilerParams(dimension_semantics=semantics, vmem_limit_bytes=vmem_mib * MIB)


def _dot(a, b):
    return jnp.dot(a, b, preferred_element_type=F32)


def _dot_nt(a, b):
    return lax.dot_general(a, b, _NT, preferred_element_type=F32)


def _split2(x):
    hi = x.astype(BF16)
    return hi, (x - hi.astype(F32)).astype(BF16)


def _ada_kernel(c_ref, w_ref, b_ref, o_ref):
    s = jax.nn.silu(c_ref[...]).astype(BF16)
    o_ref[...] = _dot(s, w_ref[...].astype(BF16)) + b_ref[...]


def ada_mod(c, w_ada, b_ada):
    bsz, d = c.shape
    n = w_ada.shape[1]
    rows = 8
    assert bsz <= rows
    cp = jnp.zeros((rows, d), F32).at[:bsz].set(c)
    tn = 512
    out = pl.pallas_call(
        _ada_kernel,
        grid=(n // tn,),
        in_specs=[pl.BlockSpec((rows, d), lambda j: (0, 0)),
                  pl.BlockSpec((d, tn), lambda j: (0, j)),
                  pl.BlockSpec((1, tn), lambda j: (0, j))],
        out_specs=pl.BlockSpec((rows, tn), lambda j: (0, j)),
        out_shape=jax.ShapeDtypeStruct((rows, n), F32),
        compiler_params=_params(("arbitrary",), 40),
        name="ada_mod",
    )(cp, w_ada, b_ada.reshape(1, n))
    return out[:bsz]


def _inproj_kernel(x_ref, sc_ref, sh_ref, w_ref, wf_ref, qkv_ref, u_ref, f_ref, h_scr, *,
                   n_q_tiles, n_qkv_tiles, q_scale):
    j = pl.program_id(2)

    @pl.when(j == 0)
    def _():
        hb = (x_ref[0] * (1.0 + sc_ref[0]) + sh_ref[0]).astype(BF16)
        h_scr[...] = hb
        f_ref[0] = _dot(hb, wf_ref[...])

    acc = _dot(h_scr[...], w_ref[...])

    @pl.when(j < n_q_tiles)
    def _():
        qkv_ref[0] = (acc * q_scale).astype(BF16)

    @pl.when(jnp.logical_and(j >= n_q_tiles, j < n_qkv_tiles))
    def _():
        qkv_ref[0] = acc.astype(BF16)

    @pl.when(j >= n_qkv_tiles)
    def _():
        u_ref[0] = acc


def in_proj(x, scale, shift, w_main, w_f, n_qkv):
    bsz, seq, d = x.shape
    n_all = w_main.shape[1]
    n_u = n_all - n_qkv
    tm, tn = 512, 1024
    nq = n_qkv // tn
    grid = (bsz, seq // tm, n_all // tn)
    return pl.pallas_call(
        functools.partial(_inproj_kernel, n_q_tiles=n_qkv // 3 // tn, n_qkv_tiles=nq,
                          q_scale=HEAD_DIM ** -0.5 * LOG2E),
        grid=grid,
        in_specs=[pl.BlockSpec((1, tm, d), lambda b, i, j: (b, i, 0)),
                  pl.BlockSpec((1, 1, d), lambda b, i, j: (b, 0, 0)),
                  pl.BlockSpec((1, 1, d), lambda b, i, j: (b, 0, 0)),
                  pl.BlockSpec((d, tn), lambda b, i, j: (0, j)),
                  pl.BlockSpec((d, LANES), lambda b, i, j: (0, 0))],
        out_specs=[pl.BlockSpec((1, tm, tn), lambda b, i, j: (b, i, jnp.minimum(j, nq - 1))),
                   pl.BlockSpec((1, tm, tn), lambda b, i, j: (b, i, jnp.maximum(j - nq, 0))),
                   pl.BlockSpec((1, tm, LANES), lambda b, i, j: (b, i, 0))],
        out_shape=[jax.ShapeDtypeStruct((bsz, seq, n_qkv), BF16),
                   jax.ShapeDtypeStruct((bsz, seq, n_u), F32),
                   jax.ShapeDtypeStruct((bsz, seq, LANES), F32)],
        scratch_shapes=[pltpu.VMEM((tm, d), BF16)],
        compiler_params=_params(("arbitrary", "arbitrary", "arbitrary"), 52),
        name="in_proj",
    )(x, scale, shift, w_main, w_f)


def _split3(x):
    p1 = x.astype(BF16)
    r1 = x - p1.astype(F32)
    p2 = r1.astype(BF16)
    p3 = (r1 - p2.astype(F32)).astype(BF16)
    return p1, p2, p3


N_PIECES = 3


def _cum_kernel(f_ref, b_ref, qa_ref, ka_ref, carry, *, n_heads):
    i = pl.program_id(1)

    @pl.when(i == 0)
    def _():
        carry[...] = jnp.zeros_like(carry)

    tc = f_ref.shape[1]
    lf = jax.nn.log_sigmoid(f_ref[0] + b_ref[...])
    row = lax.broadcasted_iota(jnp.int32, (tc, tc), 0)
    col = lax.broadcasted_iota(jnp.int32, (tc, tc), 1)
    tri = (col <= row).astype(BF16)
    p1, p2, p3 = _split3(lf)
    cs = _dot(tri, p1) + _dot(tri, p2) + _dot(tri, p3) + carry[...]
    carry[...] = cs[tc - 1:tc, :]

    pieces = jnp.concatenate(_split3(cs * LOG2E), axis=1)
    r = lax.broadcasted_iota(jnp.int32, (N_PIECES * LANES, LANES), 0)
    c = lax.broadcasted_iota(jnp.int32, (N_PIECES * LANES, LANES), 1)
    lane = lax.broadcasted_iota(jnp.int32, (tc, LANES), 1)
    ones_q = jnp.where(jnp.logical_and(lane >= N_PIECES, lane < 2 * N_PIECES), 1.0, 0.0)
    ones_k = jnp.where(lane < N_PIECES, 1.0, 0.0)
    for h in range(n_heads):
        sel_q = (r == c * LANES + h).astype(BF16)
        sel_k = (r == (c - N_PIECES) * LANES + h).astype(BF16)
        qa_ref[0, h] = (_dot(pieces, sel_q) + ones_q).astype(BF16)
        ka_ref[0, h] = (ones_k - _dot(pieces, sel_k)).astype(BF16)


def forget_cum(f, b_forget):
    bsz, seq, _ = f.shape
    n_heads = b_forget.shape[0]
    tc = 256
    bpad = jnp.zeros((1, LANES), F32).at[0, :n_heads].set(b_forget)
    out_spec = pl.BlockSpec((1, n_heads, tc, LANES), lambda b, i: (b, 0, i, 0))
    out_shape = jax.ShapeDtypeStruct((bsz, n_heads, seq, LANES), BF16)
    return pl.pallas_call(
        functools.partial(_cum_kernel, n_heads=n_heads),
        grid=(bsz, seq // tc),
        in_specs=[pl.BlockSpec((1, tc, LANES), lambda b, i: (b, i, 0)),
                  pl.BlockSpec((1, LANES), lambda b, i: (0, 0))],
        out_specs=[out_spec, out_spec],
        out_shape=[out_shape, out_shape],
        scratch_shapes=[pltpu.VMEM((1, LANES), F32)],
        compiler_params=_params(("arbitrary", "arbitrary"), 24),
        name="forget_cum",
    )(f, bpad)


ATT_QROWS = 1024
ATT_KEYS = 512


def _attn_kernel(*refs, n_cast):
    q_ref, qa_ref, k_ref, ka_ref, v_ref = refs[:5]
    src_refs = refs[5:5 + n_cast]
    o_ref = refs[5 + n_cast]
    dst_refs = refs[6 + n_cast:6 + 2 * n_cast]
    m_scr, acc_scr, s_scr = refs[6 + 2 * n_cast:]
    for src, dst in zip(src_refs, dst_refs):
        dst[...] = src[...].astype(BF16)

    tq = q_ref.shape[1]
    tk = ATT_KEYS
    n_groups = tq // tk
    qi = pl.program_id(2)
    q = jnp.concatenate([q_ref[0], qa_ref[0, 0]], axis=1)
    ones = jnp.ones((tk, HEAD_DIM), BF16)

    m_scr[...] = jnp.full_like(m_scr, NEG_BIG)
    acc_scr[...] = jnp.zeros_like(acc_scr)

    def scores(kb, slot, first_group=0):
        k0 = pl.multiple_of(kb * tk, tk)
        kt = jnp.concatenate([k_ref[0, pl.ds(k0, tk), :], ka_ref[0, 0, pl.ds(k0, tk), :]], axis=1)
        r0 = first_group * tk
        s_scr[slot, r0:, :] = lax.dot_general(q[r0:], kt, _NT, preferred_element_type=F32)

    def update(kb, slot, diag_group=None):
        k0 = pl.multiple_of(kb * tk, tk)
        vt = jnp.concatenate([v_ref[0, pl.ds(k0, tk), :], ones], axis=1)
        for g in range(n_groups):
            if diag_group is not None and g < diag_group:
                continue
            rs = slice(g * tk, (g + 1) * tk)
            s = s_scr[slot, rs, :]
            if g == diag_group:
                qpos = lax.broadcasted_iota(jnp.int32, (tk, tk), 0)
                kpos = lax.broadcasted_iota(jnp.int32, (tk, tk), 1)
                s = jnp.where(kpos <= qpos, s, NEG_BIG)
            m_prev = m_scr[rs]
            m_new = jnp.maximum(m_prev, jnp.max(s, axis=1, keepdims=True))
            alpha = jnp.exp2(m_prev - m_new)
            p = jnp.exp2(s - m_new)
            acc_scr[rs] = alpha * acc_scr[rs] + _dot(p.astype(BF16), vt)
            m_scr[rs] = m_new

    def body(i, carry):
        kb = n_groups * i
        for g in range(n_groups):
            scores(kb + g + 1, (g + 1) % 2)
            update(kb + g, g % 2)
        return carry

    assert n_groups % 2 == 0
    scores(0, 0)
    lax.fori_loop(0, qi, body, 0)
    kb = n_groups * qi
    for g in range(n_groups):
        if g + 1 < n_groups:
            scores(kb + g + 1, (g + 1) % 2, first_group=g + 1)
        update(kb + g, g % 2, diag_group=g)

    o_ref[0] = acc_scr[:, :HEAD_DIM] / acc_scr[:, HEAD_DIM:]


BF16_ROWS = 16


def _cast_chunks(w, n_steps):
    cols = w.shape[-1]
    total_rows = w.size // cols
    n_chunks = n_steps
    while total_rows % (n_chunks * BF16_ROWS):
        n_chunks //= 2
    return w.reshape(n_chunks, total_rows // n_chunks, cols)


def attention(qkv, q_aug, k_aug, n_heads, cast_weights):
    bsz, seq, _ = qkv.shape
    t = min(ATT_QROWS, seq)
    nq = seq // t
    n_steps = bsz * n_heads * nq
    srcs = [_cast_chunks(w, n_steps) for w in cast_weights]

    def chunk_spec(a):
        per = n_steps // a.shape[0]
        return pl.BlockSpec((1,) + a.shape[1:], lambda b, h, i: (((b * n_heads + h) * nq + i) // per, 0, 0))

    outs = pl.pallas_call(
        functools.partial(_attn_kernel, n_cast=len(srcs)),
        grid=(bsz, n_heads, nq),
        in_specs=[pl.BlockSpec((1, t, HEAD_DIM), lambda b, h, i: (b, i, h)),
                  pl.BlockSpec((1, 1, t, LANES), lambda b, h, i: (b, h, i, 0)),
                  pl.BlockSpec((1, seq, HEAD_DIM), lambda b, h, i: (b, 0, n_heads + h)),
                  pl.BlockSpec((1, 1, seq, LANES), lambda b, h, i: (b, h, 0, 0)),
                  pl.BlockSpec((1, seq, HEAD_DIM), lambda b, h, i: (b, 0, 2 * n_heads + h))]
                 + [chunk_spec(a) for a in srcs],
        out_specs=[pl.BlockSpec((1, t, HEAD_DIM), lambda b, h, i: (b, i, h))] + [chunk_spec(a) for a in srcs],
        out_shape=[jax.ShapeDtypeStruct((bsz, seq, n_heads * HEAD_DIM), F32)]
                  + [jax.ShapeDtypeStruct(a.shape, BF16) for a in srcs],
        scratch_shapes=[pltpu.VMEM((t, 1), F32), pltpu.VMEM((t, 2 * HEAD_DIM), F32),
                        pltpu.VMEM((2, t, ATT_KEYS), F32)],
        compiler_params=_params(("arbitrary", "arbitrary", "arbitrary"), 56),
        name="attention",
    )(qkv, q_aug, qkv, k_aug, qkv, *srcs)
    return outs[0], [o.reshape(w.shape) for o, w in zip(outs[1:], cast_weights)]


def _blockdiag(p):
    g, c, n = p.shape
    ns = g // GROUPS_PER_SLAB
    eye = jnp.eye(GROUPS_PER_SLAB, dtype=p.dtype)
    out = p.reshape(ns, GROUPS_PER_SLAB, c, 1, n) * eye[None, :, None, :, None]
    return out.reshape(ns, GROUPS_PER_SLAB * c, GROUPS_PER_SLAB * n)


def _ssm_prep_kernel(lr_ref, li_ref, ldt_ref, bre_ref, bim_ref, cre_ref, cim_ref, w1_ref, ft_ref, al_ref):
    L = SSM_CHUNK
    lr = lr_ref[0]
    li = li_ref[0]
    dt = jnp.exp(ldt_ref[0])
    mag = jnp.exp(lr * dt)
    a_re = mag * jnp.cos(li * dt)
    a_im = mag * jnp.sin(li * dt)
    den = lr * lr + li * li
    z_re = ((a_re - 1.0) * lr + a_im * li) / den
    z_im = (a_im * lr - (a_re - 1.0) * li) / den
    br = bre_ref[0]
    bi = bim_ref[0]
    bb_re = z_re * br - z_im * bi
    bb_im = z_re * bi + z_im * br
    cr = cre_ref[0]
    ci = cim_ref[0]
    ft0_hi, ft0_lo = _split2(jnp.concatenate([cr, -ci], axis=1))

    def power(d):
        m = jnp.exp(lr * dt * d)
        return m * jnp.cos(li * dt * d), m * jnp.sin(li * dt * d)

    w1_ref[0, :, :L * LANES] = jnp.zeros((L * LANES, L * LANES), BF16)
    for d in range(L):
        pr, pi = power(float(d))
        xe = jnp.concatenate([bb_re * pr - bb_im * pi, bb_re * pi + bb_im * pr], axis=1)
        j = L - 1 - d
        w1_ref[0, j * LANES:(j + 1) * LANES, L * LANES:] = xe.astype(BF16)
        xe_hi, xe_lo = _split2(xe)
        m_d = (_dot_nt(xe_hi, ft0_hi) + _dot_nt(xe_hi, ft0_lo) + _dot_nt(xe_lo, ft0_hi)).astype(BF16)
        for jj in range(L - d):
            w1_ref[0, jj * LANES:(jj + 1) * LANES, (jj + d) * LANES:(jj + d + 1) * LANES] = m_d
        pr1, pi1 = power(float(d + 1))
        ft_ref[0, d * LANES:(d + 1) * LANES, :] = jnp.concatenate(
            [cr * pr1 - ci * pi1, -(cr * pi1 + ci * pr1)], axis=1).astype(BF16)
    prl, pil = power(float(L))
    al_ref[0] = jnp.concatenate([prl, pil], axis=1)


def ssm_prep(lam_re, lam_im, log_dt, b_re, b_im, c_re, c_im):
    g, n = lam_re.shape
    ns = g // GROUPS_PER_SLAB
    L = SSM_CHUNK
    rowvec = lambda a: a.reshape(ns, 1, SLAB_STATE)
    args = (rowvec(lam_re), rowvec(lam_im), rowvec(jnp.repeat(log_dt, n)),
            _blockdiag(b_re.transpose(0, 2, 1)), _blockdiag(b_im.transpose(0, 2, 1)),
            _blockdiag(c_re), _blockdiag(c_im))
    vec_spec = pl.BlockSpec((1, 1, SLAB_STATE), lambda s: (s, 0, 0))
    mat_spec = pl.BlockSpec((1, LANES, SLAB_STATE), lambda s: (s, 0, 0))
    return pl.pallas_call(
        _ssm_prep_kernel,
        grid=(ns,),
        in_specs=[vec_spec] * 3 + [mat_spec] * 4,
        out_specs=[pl.BlockSpec((1, L * LANES, L * LANES + 2 * SLAB_STATE), lambda s: (s, 0, 0)),
                   pl.BlockSpec((1, L * LANES, 2 * SLAB_STATE), lambda s: (s, 0, 0)),
                   pl.BlockSpec((1, 1, 2 * SLAB_STATE), lambda s: (s, 0, 0))],
        out_shape=[jax.ShapeDtypeStruct((ns, L * LANES, L * LANES + 2 * SLAB_STATE), BF16),
                   jax.ShapeDtypeStruct((ns, L * LANES, 2 * SLAB_STATE), BF16),
                   jax.ShapeDtypeStruct((ns, 1, 2 * SLAB_STATE), F32)],
        compiler_params=_params(("arbitrary",), 48),
        name="ssm_prep",
    )(*args)


def _ssm_kernel(u_ref, w1_ref, ft_ref, al_ref, d_ref, y_ref, uf_scr, e_scr, y_scr):
    L = SSM_CHUNK
    nch = uf_scr.shape[0]
    lc = L * LANES
    for j in range(L):
        uf_scr[:, j * LANES:(j + 1) * LANES] = u_ref[0, pl.ds(j, nch, stride=L), :].astype(BF16)
    uf = uf_scr[...]
    e_scr[...] = _dot(uf, w1_ref[0, :, lc:])

    a_re = al_ref[0, :, :SLAB_STATE]
    a_im = al_ref[0, :, SLAB_STATE:]
    h_re = jnp.zeros((1, SLAB_STATE), F32)
    h_im = jnp.zeros((1, SLAB_STATE), F32)
    for k in range(nch):
        e = e_scr[k:k + 1, :]
        e_scr[k:k + 1, :] = jnp.concatenate([h_re, h_im], axis=1)
        h_re, h_im = (a_re * h_re - a_im * h_im + e[:, :SLAB_STATE],
                      a_re * h_im + a_im * h_re + e[:, SLAB_STATE:])

    for c in range(lc // SSM_COLS):
        k_hi = (c + 1) * SSM_COLS
        cols = slice(c * SSM_COLS, k_hi)
        y_scr[:, cols] = _dot(uf[:, :k_hi], w1_ref[0, :k_hi, cols])
    y = y_scr[...] + lax.dot_general(e_scr[...].astype(BF16), ft_ref[0], _NT, preferred_element_type=F32)
    for i in range(L):
        yi = y[:, i * LANES:(i + 1) * LANES] + d_ref[0] * u_ref[0, pl.ds(i, nch, stride=L), :]
        y_ref[0, pl.ds(i, nch, stride=L), :] = jax.nn.gelu(yi)


def ssm_scan(u, w1, ft, al, d_skip):
    bsz, seq, c = u.shape
    ns = c // LANES
    L = SSM_CHUNK
    nch = seq // L
    lc = L * LANES
    once = pl.Buffered(1)
    return pl.pallas_call(
        _ssm_kernel,
        grid=(ns, bsz),
        in_specs=[pl.BlockSpec((1, seq, LANES), lambda s, b: (b, 0, s)),
                  pl.BlockSpec((1, lc, lc + 2 * SLAB_STATE), lambda s, b: (s, 0, 0), pipeline_mode=once),
                  pl.BlockSpec((1, lc, 2 * SLAB_STATE), lambda s, b: (s, 0, 0), pipeline_mode=once),
                  pl.BlockSpec((1, 1, 2 * SLAB_STATE), lambda s, b: (s, 0, 0)),
                  pl.BlockSpec((1, 1, LANES), lambda s, b: (s, 0, 0))],
        out_specs=pl.BlockSpec((1, seq, LANES), lambda s, b: (b, 0, s)),
        out_shape=jax.ShapeDtypeStruct((bsz, seq, c), F32),
        scratch_shapes=[pltpu.VMEM((nch, lc), BF16), pltpu.VMEM((nch, 2 * SLAB_STATE), F32),
                        pltpu.VMEM((nch, lc), F32)],
        compiler_params=_params(("arbitrary", "arbitrary"), 56),
        name="ssm_scan",
    )(u, w1, ft, al, d_skip.reshape(ns, 1, LANES))


def _glu_kernel(y_ref, w_ref, b_ref, g_ref, o_ref):
    y = y_ref[...]
    o = y * jax.nn.sigmoid(_dot(y.astype(BF16), w_ref[...]) + b_ref[...])
    ms = jnp.mean(o * o, axis=-1, keepdims=True)
    o_ref[...] = (o * lax.rsqrt(ms + RMS_EPS) * g_ref[...]).astype(BF16)


def glu_norm(y, w_glu, b_glu, g):
    t, c = y.shape
    tm = 512
    return pl.pallas_call(
        _glu_kernel,
        grid=(t // tm,),
        in_specs=[pl.BlockSpec((tm, c), lambda i: (i, 0)),
                  pl.BlockSpec((c, c), lambda i: (0, 0)),
                  pl.BlockSpec((1, c), lambda i: (0, 0)),
                  pl.BlockSpec((1, c), lambda i: (0, 0))],
        out_specs=pl.BlockSpec((tm, c), lambda i: (i, 0)),
        out_shape=jax.ShapeDtypeStruct((t, c), BF16),
        compiler_params=_params(("arbitrary",), 48),
        name="glu_norm",
    )(y, w_glu, b_glu.reshape(1, c), g.reshape(1, c))


def _outproj_kernel(attn_ref, ga_ref, ssm_ref, w_ref, x_ref, gate_ref, lng_ref, lnb_ref, sc2_ref, sh2_ref,
                    wrh_ref, wrl_ref, x1_ref, h2_ref, lg_ref, a_scr, *, alpha, n_att):
    j = pl.program_id(2)
    nj = pl.num_programs(2)
    tn = w_ref.shape[1]

    tm = x1_ref.shape[1]
    n_chunks = tm // ROW_CHUNK

    @pl.when(j == 0)
    def _():
        def norm(c, carry):
            rs = pl.ds(pl.multiple_of(c * ROW_CHUNK, ROW_CHUNK), ROW_CHUNK)
            a = attn_ref[0, rs, :]
            ms = jnp.mean(a * a, axis=-1, keepdims=True)
            a_scr[rs, :n_att] = (a * lax.rsqrt(ms + RMS_EPS) * ga_ref[...]).astype(BF16)
            return carry
        lax.fori_loop(0, n_chunks, norm, 0)
        a_scr[:, n_att:] = ssm_ref[0]

    mixed = _dot(a_scr[...], w_ref[...])
    col = pl.multiple_of(j * tn, tn)
    x1_ref[0, :, pl.ds(col, tn)] = alpha * x_ref[0] + (1.0 + gate_ref[0]) * mixed

    @pl.when(j == nj - 1)
    def _():
        def finish(c, carry):
            rs = pl.ds(pl.multiple_of(c * ROW_CHUNK, ROW_CHUNK), ROW_CHUNK)
            r = x1_ref[0, rs, :]
            mu = jnp.mean(r, axis=-1, keepdims=True)
            var = jnp.mean(jnp.square(r - mu), axis=-1, keepdims=True)
            x1 = (r - mu) * lax.rsqrt(var + LN_EPS) * lng_ref[...] + lnb_ref[...]
            x1_ref[0, rs, :] = x1
            h2 = x1 * (1.0 + sc2_ref[0]) + sh2_ref[0]
            hi = h2.astype(BF16)
            hi_f = hi.astype(F32)
            lo = (h2 - hi_f).astype(BF16)
            lg_ref[0, rs, :] = _dot(hi, wrh_ref[...]) + _dot(hi, wrl_ref[...]) + _dot(lo, wrh_ref[...])
            h2_ref[0, rs, :] = pack_bf16_pairs(hi_f)
            return carry
        lax.fori_loop(0, n_chunks, finish, 0)


def pack_bf16_pairs(x):
    n = x.shape[-1] // 2
    bits = lax.bitcast_convert_type(x, jnp.uint32)
    return bits[:, n:] | (bits[:, :n] >> 16)


def unpack_bf16_pairs(p):
    lo = lax.bitcast_convert_type(p << 16, F32).astype(BF16)
    hi = lax.bitcast_convert_type(p & jnp.uint32(0xFFFF0000), F32).astype(BF16)
    return lo, hi


def out_proj(attn, g_attn, ssm_n, w_out, x, gate1, ln_g, ln_b, scale2, shift2, wr_hi, wr_lo, alpha):
    bsz, seq, d = x.shape
    n_att = attn.shape[-1]
    n_ssm = ssm_n.shape[-1]
    k = n_att + n_ssm
    tm, tn = 512, 256
    row = lambda a: a.reshape(1, -1)
    full = lambda n: pl.BlockSpec((1, n), lambda b, i, j: (0, 0))
    return pl.pallas_call(
        functools.partial(_outproj_kernel, alpha=alpha, n_att=n_att),
        grid=(bsz, seq // tm, d // tn),
        in_specs=[pl.BlockSpec((1, tm, n_att), lambda b, i, j: (b, i, 0)),
                  full(n_att),
                  pl.BlockSpec((1, tm, n_ssm), lambda b, i, j: (b, i, 0)),
                  pl.BlockSpec((k, tn), lambda b, i, j: (0, j)),
                  pl.BlockSpec((1, tm, tn), lambda b, i, j: (b, i, j)),
                  pl.BlockSpec((1, 1, tn), lambda b, i, j: (b, 0, j)),
                  full(d), full(d),
                  pl.BlockSpec((1, 1, d), lambda b, i, j: (b, 0, 0)),
                  pl.BlockSpec((1, 1, d), lambda b, i, j: (b, 0, 0)),
                  pl.BlockSpec((d, LANES), lambda b, i, j: (0, 0)),
                  pl.BlockSpec((d, LANES), lambda b, i, j: (0, 0))],
        out_specs=[pl.BlockSpec((1, tm, d), lambda b, i, j: (b, i, 0)),
                   pl.BlockSpec((1, tm, d // 2), lambda b, i, j: (b, i, 0)),
                   pl.BlockSpec((1, tm, LANES), lambda b, i, j: (b, i, 0))],
        out_shape=[jax.ShapeDtypeStruct((bsz, seq, d), F32),
                   jax.ShapeDtypeStruct((bsz, seq, d // 2), jnp.uint32),
                   jax.ShapeDtypeStruct((bsz, seq, LANES), F32)],
        scratch_shapes=[pltpu.VMEM((tm, k), BF16)],
        compiler_params=_params(("arbitrary", "arbitrary", "arbitrary"), 56),
        name="out_proj",
    )(attn, row(g_attn), ssm_n, w_out, x, gate1, row(ln_g), row(ln_b), scale2, shift2, wr_hi, wr_lo)


def _moe_kernel(tok_ref, bexp_ref, nsub_ref, nact_ref, h_hbm, wg_ref, wu_ref, wd_ref, y_ref, xbuf, sem):
    i = pl.program_id(0)
    nact = nact_ref[0]
    rows = xbuf.shape[1]

    def start_gather(blk, slot):
        n_sub = nsub_ref[blk]
        row0 = rows - n_sub * MOE_SUB

        def body(t, carry):
            base = pl.multiple_of(row0 + t * GATHER_UNROLL, GATHER_UNROLL)
            first = pl.multiple_of(blk * rows, rows) + base
            for k in range(GATHER_UNROLL):
                tok = tok_ref[first + k]
                pltpu.make_async_copy(h_hbm.at[pl.ds(tok, 1)], xbuf.at[slot, pl.ds(base + k, 1)],
                                      sem.at[slot]).start()
            return carry
        lax.fori_loop(0, n_sub * (MOE_SUB // GATHER_UNROLL), body, 0)

    def run_block(slot, n):
        lo = rows - n
        pltpu.make_async_copy(h_hbm.at[pl.ds(0, n)], xbuf.at[slot, pl.ds(lo, n)], sem.at[slot]).wait()
        x_lo, x_hi = unpack_bf16_pairs(xbuf[slot, lo:])
        half = x_lo.shape[1]
        g = _dot(x_lo, wg_ref[0, :half, :]) + _dot(x_hi, wg_ref[0, half:, :])
        u = _dot(x_lo, wu_ref[0, :half, :]) + _dot(x_hi, wu_ref[0, half:, :])
        act = (jax.nn.silu(g) * u).astype(BF16)
        y_ref[lo:, :] = _dot(act, wd_ref[0])
        if lo > 0:
            y_ref[:lo, :] = jnp.zeros((lo, y_ref.shape[1]), F32)

    @pl.when(jnp.logical_and(i == 0, nact > 0))
    def _():
        start_gather(0, 0)

    @pl.when(i + 1 < nact)
    def _():
        start_gather(i + 1, (i + 1) % 2)

    for n_sub in range(1, rows // MOE_SUB + 1):
        @pl.when(jnp.logical_and(i < nact, nsub_ref[i] == n_sub))
        def _():
            run_block(i % 2, n_sub * MOE_SUB)

    @pl.when(i >= nact)
    def _():
        y_ref[...] = jnp.zeros_like(y_ref)


def moe_experts(h2, tok_buf, block_expert, short_blocks, n_active, w_gate, w_up, w_down):
    t, dp = h2.shape
    d = 2 * dp
    n_rows = tok_buf.shape[0]
    rows = MOE_ROWS
    n_blocks = n_rows // rows
    de = w_gate.shape[-1]

    def wmap(i, tok, bexp, short, nact):
        return (bexp[jnp.minimum(i, jnp.maximum(nact[0] - 1, 0))], 0, 0)

    grid_spec = pltpu.PrefetchScalarGridSpec(
        num_scalar_prefetch=4,
        grid=(n_blocks,),
        in_specs=[pl.BlockSpec(memory_space=pl.ANY),
                  pl.BlockSpec((1, d, de), wmap),
                  pl.BlockSpec((1, d, de), wmap),
                  pl.BlockSpec((1, de, d), wmap)],
        out_specs=pl.BlockSpec((rows, d), lambda i, *_: (i, 0)),
        scratch_shapes=[pltpu.VMEM((2, rows, dp), jnp.uint32), pltpu.SemaphoreType.DMA((2,))],
    )
    return pl.pallas_call(
        _moe_kernel,
        grid_spec=grid_spec,
        out_shape=jax.ShapeDtypeStruct((n_rows, d), F32),
        compiler_params=_params(("arbitrary",), 56),
        name="moe_experts",
    )(tok_buf, block_expert, short_blocks, n_active, h2, w_gate, w_up, w_down)


def _combine_kernel(dest_ref, ys_hbm, rt_ref, x1_ref, gate_ref, lng_ref, lnb_ref, o_ref, ybuf, sem, *, alpha):
    i = pl.program_id(0)
    n = pl.num_programs(0)
    tm = x1_ref.shape[0]
    n_copies = TOP_K * tm

    def gather(blk, slot, start):
        if not start:
            pltpu.make_async_copy(ys_hbm.at[pl.ds(0, n_copies)], ybuf.at[slot], sem.at[slot]).wait()
            return

        def body(t, carry):
            base = pl.multiple_of(t * GATHER_UNROLL, GATHER_UNROLL)
            first = pl.multiple_of(blk * n_copies, n_copies) + base
            for k in range(GATHER_UNROLL):
                row = dest_ref[first + k]
                pltpu.make_async_copy(ys_hbm.at[pl.ds(row, 1)], ybuf.at[slot, pl.ds(base + k, 1)],
                                      sem.at[slot]).start()
            return carry
        lax.fori_loop(0, n_copies // GATHER_UNROLL, body, 0)

    @pl.when(i == 0)
    def _():
        gather(0, 0, True)

    @pl.when(i + 1 < n)
    def _():
        gather(i + 1, (i + 1) % 2, True)

    slot = i % 2
    gather(i, slot, False)
    moe = rt_ref[:, ROUTE_W:ROUTE_W + 1] * ybuf[slot, :tm, :]
    for kk in range(1, TOP_K):
        moe = moe + rt_ref[:, ROUTE_W + kk:ROUTE_W + kk + 1] * ybuf[slot, kk * tm:(kk + 1) * tm, :]
    r = alpha * x1_ref[...] + (1.0 + gate_ref[0]) * moe
    mu = jnp.mean(r, axis=-1, keepdims=True)
    var = jnp.mean(jnp.square(r - mu), axis=-1, keepdims=True)
    o_ref[...] = (r - mu) * lax.rsqrt(var + LN_EPS) * lng_ref[...] + lnb_ref[...]


def moe_combine(ys, dest, table, x1, gate2, ln_g, ln_b, alpha, seq):
    t, d = x1.shape
    tm = 128
    tiles_per_seq = seq // tm
    grid_spec = pltpu.PrefetchScalarGridSpec(
        num_scalar_prefetch=1,
        grid=(t // tm,),
        in_specs=[pl.BlockSpec(memory_space=pl.ANY),
                  pl.BlockSpec((tm, LANES), lambda i, *_: (i, 0)),
                  pl.BlockSpec((tm, d), lambda i, *_: (i, 0)),
                  pl.BlockSpec((1, 1, d), lambda i, *_: (i // tiles_per_seq, 0, 0)),
                  pl.BlockSpec((1, d), lambda i, *_: (0, 0)),
                  pl.BlockSpec((1, d), lambda i, *_: (0, 0))],
        out_specs=pl.BlockSpec((tm, d), lambda i, *_: (i, 0)),
        scratch_shapes=[pltpu.VMEM((2, TOP_K * tm, d), F32), pltpu.SemaphoreType.DMA((2,))],
    )
    return pl.pallas_call(
        functools.partial(_combine_kernel, alpha=alpha),
        grid_spec=grid_spec,
        out_shape=jax.ShapeDtypeStruct((t, d), F32),
        compiler_params=_params(("arbitrary",), 32),
        name="moe_combine",
    )(dest, ys, table, x1, gate2, ln_g.reshape(1, d), ln_b.reshape(1, d))


def _route_kernel(lg_ref, b_ref, o_ref, cnt_ref, carry):
    i = pl.program_id(0)

    @pl.when(i == 0)
    def _():
        carry[...] = jnp.zeros_like(carry)

    tm = lg_ref.shape[0]
    x = lg_ref[...] + b_ref[...]
    lane = lax.broadcasted_iota(jnp.int32, (tm, LANES), 1)
    ninf = -jnp.inf

    def top(v):
        vmax = jnp.max(v, axis=1, keepdims=True)
        return vmax, jnp.min(jnp.where(v == vmax, lane, LANES), axis=1, keepdims=True)

    gmask = lane < N_EXPERT_GROUPS
    gmax, g_sel = top(jnp.where(gmask, x, ninf))
    p_group = 1.0 / jnp.sum(jnp.where(gmask, jnp.exp(x - gmax), 0.0), axis=1, keepdims=True)

    lo = N_EXPERT_GROUPS + g_sel * EXPERTS_PER_GROUP
    cur = jnp.where(jnp.logical_and(lane >= lo, lane < lo + EXPERTS_PER_GROUP), x, ninf)
    vals, idxs = [], []
    for _ in range(TOP_K):
        v, ix = top(cur)
        vals.append(v)
        idxs.append(ix)
        cur = jnp.where(lane == ix, ninf, cur)
    exps = [jnp.exp(v - vals[0]) for v in vals]
    den = functools.reduce(lambda a, c: a + c, exps)

    member = functools.reduce(jnp.logical_or, [lane == ix for ix in idxs])
    mf = jnp.where(member, 1.0, 0.0)
    r_i = lax.broadcasted_iota(jnp.int32, (tm, tm), 0)
    c_i = lax.broadcasted_iota(jnp.int32, (tm, tm), 1)
    before = _dot((c_i < r_i).astype(BF16), mf.astype(BF16)) + carry[...]
    carry[...] = carry[...] + jnp.sum(mf, axis=0, keepdims=True)
    cnt_ref[...] = carry[...]

    out = jnp.zeros((tm, LANES), F32)
    for k in range(TOP_K):
        rank = jnp.sum(jnp.where(lane == idxs[k], before, 0.0), axis=1, keepdims=True)
        out = jnp.where(lane == ROUTE_EID + k, (idxs[k] - N_EXPERT_GROUPS).astype(F32), out)
        out = jnp.where(lane == ROUTE_RANK + k, rank, out)
        out = jnp.where(lane == ROUTE_W + k, p_group * (exps[k] / den), out)
    o_ref[...] = out


def route(logits, b_rg, b_re, n_rows, tm_combine):
    n_tok = logits.shape[0]
    n_experts = N_EXPERT_GROUPS * EXPERTS_PER_GROUP
    bias = jnp.zeros((1, LANES), F32).at[0, :N_EXPERT_GROUPS].set(b_rg)
    bias = bias.at[0, N_EXPERT_GROUPS:N_EXPERT_GROUPS + n_experts].set(b_re)
    tm = 512
    table, cnt = pl.pallas_call(
        _route_kernel,
        grid=(n_tok // tm,),
        in_specs=[pl.BlockSpec((tm, LANES), lambda i: (i, 0)),
                  pl.BlockSpec((1, LANES), lambda i: (0, 0))],
        out_specs=[pl.BlockSpec((tm, LANES), lambda i: (i, 0)),
                   pl.BlockSpec((1, LANES), lambda i: (0, 0))],
        out_shape=[jax.ShapeDtypeStruct((n_tok, LANES), F32), jax.ShapeDtypeStruct((1, LANES), F32)],
        scratch_shapes=[pltpu.VMEM((1, LANES), F32)],
        compiler_params=_params(("arbitrary",), 16),
        name="route",
    )(logits, bias)

    eid = table[:, ROUTE_EID:ROUTE_EID + TOP_K].astype(jnp.int32)
    rank = table[:, ROUTE_RANK:ROUTE_RANK + TOP_K].astype(jnp.int32)
    counts = cnt[0, N_EXPERT_GROUPS:N_EXPERT_GROUPS + n_experts].astype(jnp.int32)
    padded = ((counts + MOE_ROWS - 1) // MOE_ROWS) * MOE_ROWS
    pends = jnp.cumsum(padded)
    pstarts = pends - padded
    onehot = eid[:, :, None] == jnp.arange(n_experts, dtype=jnp.int32)
    first_real = pstarts + (padded - counts)
    dest = jnp.sum(jnp.where(onehot, first_real, 0), axis=-1) + rank
    tok = jnp.broadcast_to(jnp.arange(n_tok, dtype=jnp.int32)[:, None], (n_tok, TOP_K))
    tok_buf = jnp.zeros((n_rows,), jnp.int32).at[dest.reshape(-1)].set(tok.reshape(-1), unique_indices=True)
    n_blocks = n_rows // MOE_ROWS
    block_expert = jnp.clip(
        jnp.searchsorted(pends, jnp.arange(n_blocks, dtype=jnp.int32) * MOE_ROWS, side='right'),
        0, n_experts - 1).astype(jnp.int32)
    real_rows = (jnp.arange(n_blocks, dtype=jnp.int32) + 1) * MOE_ROWS - first_real[block_expert]
    short_blocks = jnp.clip(-(-real_rows // MOE_SUB), 1, MOE_ROWS // MOE_SUB).astype(jnp.int32)
    n_active = (pends[-1] // MOE_ROWS).astype(jnp.int32).reshape(1)
    dest_tiles = dest.reshape(n_tok // tm_combine, tm_combine, TOP_K).transpose(0, 2, 1).reshape(-1)
    return table, tok_buf, block_expert, short_blocks, n_active, dest_tiles.astype(jnp.int32)


def _layer(x, c, w_ada, b_ada, w_in, b_forget, lam_re, lam_im, log_dt, b_re, b_im, c_re, c_im, d_skip,
           w_glu, b_glu, g_attn, g_ssm, w_out, ln1_g, ln1_b, w_rg, b_rg, w_re, b_re_r,
           w_gate, w_up, w_down, ln2_g, ln2_b, alpha):
    bsz, seq, d = x.shape
    n_tok = bsz * seq
    n_heads = b_forget.shape[0]
    d_att = n_heads * HEAD_DIM
    d_ssm = d_skip.shape[0] * d_skip.shape[1]

    mod = ada_mod(c, w_ada, b_ada).reshape(bsz, 1, -1)
    shift1, scale1, gate1, shift2, scale2, gate2 = jnp.split(mod, 6, axis=-1)

    n_qkv = 3 * d_att
    w_main = jnp.concatenate([w_in[:, :n_qkv], w_in[:, n_qkv + n_heads:]], axis=1).astype(BF16)
    w_f = jnp.zeros((d, LANES), BF16).at[:, :n_heads].set(w_in[:, n_qkv:n_qkv + n_heads].astype(BF16))
    qkv, u, f = in_proj(x, scale1, shift1, w_main, w_f, n_qkv)

    q_aug, k_aug = forget_cum(f, b_forget)
    attn, (w_glu_b, w_out_b, w_gate_b, w_up_b, w_down_b) = attention(
        qkv, q_aug, k_aug, n_heads, [w_glu, w_out, w_gate, w_up, w_down])

    w1, ft, al = ssm_prep(lam_re, lam_im, log_dt, b_re, b_im, c_re, c_im)
    y = ssm_scan(u, w1, ft, al, d_skip)
    ssm_n = glu_norm(y.reshape(n_tok, d_ssm), w_glu_b, b_glu, g_ssm).reshape(bsz, seq, d_ssm)

    n_experts = N_EXPERT_GROUPS * EXPERTS_PER_GROUP
    w_r = jnp.zeros((d, LANES), F32).at[:, :N_EXPERT_GROUPS].set(w_rg)
    w_r = w_r.at[:, N_EXPERT_GROUPS:N_EXPERT_GROUPS + n_experts].set(w_re)
    wr_hi = w_r.astype(BF16)
    wr_lo = (w_r - wr_hi.astype(F32)).astype(BF16)
    x1, h2, logits = out_proj(attn, g_attn, ssm_n, w_out_b, x, gate1, ln1_g, ln1_b,
                              scale2, shift2, wr_hi, wr_lo, alpha)

    n_assign = n_tok * TOP_K
    n_blocks = -(-(n_assign + n_experts * (MOE_ROWS - 1)) // MOE_ROWS)
    n_rows = n_blocks * MOE_ROWS
    table, tok_buf, block_expert, short_blocks, n_active, dest = route(
        logits.reshape(n_tok, LANES), b_rg, b_re_r, n_rows, 128)
    ys = moe_experts(h2.reshape(n_tok, d // 2), tok_buf, block_expert, short_blocks, n_active,
                     w_gate_b, w_up_b, w_down_b)
    out = moe_combine(ys, dest, table, x1.reshape(n_tok, d), gate2, ln2_g, ln2_b, alpha, seq)
    return out.reshape(bsz, seq, d)


def kernel(x, c, w_ada, b_ada, w_in, b_forget, ssm_lambda_re, ssm_lambda_im, ssm_log_dt, ssm_b_re, ssm_b_im,
           ssm_c_re, ssm_c_im, ssm_d, w_glu, b_glu, g_attn, g_ssm, w_out, ln1_g, ln1_b, w_router_group,
           b_router_group, w_router_expert, b_router_expert, w_gate, w_up, w_down, ln2_g, ln2_b):
    depth = w_ada.shape[0]
    alpha = (2.0 * depth) ** 0.25
    for l in range(depth):
        x = _layer(x, c, w_ada[l], b_ada[l], w_in[l], b_forget[l], ssm_lambda_re[l], ssm_lambda_im[l],
                   ssm_log_dt[l], ssm_b_re[l], ssm_b_im[l], ssm_c_re[l], ssm_c_im[l], ssm_d[l],
                   w_glu[l], b_glu[l], g_attn[l], g_ssm[l], w_out[l], ln1_g[l], ln1_b[l],
                   w_router_group[l], b_router_group[l], w_router_expert[l], b_router_expert[l],
                   w_gate[l], w_up[l], w_down[l], ln2_g[l], ln2_b[l], alpha)
    return x
```

```python
import functools
import math

import jax
import jax.numpy as jnp
from jax import lax
from jax.experimental import pallas as pl
from jax.experimental.pallas import tpu as pltpu

F32 = jnp.float32
BF16 = jnp.bfloat16

LANES = 128
HEAD_DIM = 128
SSM_GROUP = 16
SSM_STATE = 64
GROUPS_PER_SLAB = LANES // SSM_GROUP
SLAB_STATE = GROUPS_PER_SLAB * SSM_STATE
SSM_CHUNK = 16
SSM_COLS = 512
N_EXPERT_GROUPS = 8
EXPERTS_PER_GROUP = 8
TOP_K = 2
MOE_ROWS = 512
MOE_SUB = 64
GATHER_UNROLL = 8
ROW_CHUNK = 128
ROUTE_EID, ROUTE_RANK, ROUTE_W = 0, TOP_K, 2 * TOP_K
LN_EPS = 1e-5
RMS_EPS = 1e-6
NEG_BIG = -1e30
LOG2E = math.log2(math.e)
MIB = 1024 * 1024

_NT = (((1,), (1,)), ((), ()))


def _params(semantics, vmem_mib):
    return pltpu.CompilerParams(dimension_semantics=semantics, vmem_limit_bytes=vmem_mib * MIB)


def _dot(a, b):
    return jnp.dot(a, b, preferred_element_type=F32)


def _dot_nt(a, b):
    return lax.dot_general(a, b, _NT, preferred_element_type=F32)


def _split2(x):
    hi = x.astype(BF16)
    return hi, (x - hi.astype(F32)).astype(BF16)


def _ada_kernel(c_ref, w_ref, b_ref, o_ref):
    s = jax.nn.silu(c_ref[...]).astype(BF16)
    o_ref[...] = _dot(s, w_ref[...].astype(BF16)) + b_ref[...]


def ada_mod(c, w_ada, b_ada):
    bsz, d = c.shape
    n = w_ada.shape[1]
    rows = 8
    assert bsz <= rows
    cp = jnp.zeros((rows, d), F32).at[:bsz].set(c)
    tn = 512
    out = pl.pallas_call(
        _ada_kernel,
        grid=(n // tn,),
        in_specs=[pl.BlockSpec((rows, d), lambda j: (0, 0)),
                  pl.BlockSpec((d, tn), lambda j: (0, j)),
                  pl.BlockSpec((1, tn), lambda j: (0, j))],
        out_specs=pl.BlockSpec((rows, tn), lambda j: (0, j)),
        out_shape=jax.ShapeDtypeStruct((rows, n), F32),
        compiler_params=_params(("arbitrary",), 40),
        name="ada_mod",
    )(cp, w_ada, b_ada.reshape(1, n))
    return out[:bsz]


def _inproj_kernel(x_ref, sc_ref, sh_ref, w_ref, wf_ref, qkv_ref, u_ref, f_ref, h_scr, *,
                   n_q_tiles, n_qkv_tiles, q_scale):
    j = pl.program_id(2)

    @pl.when(j == 0)
    def _():
        hb = (x_ref[0] * (1.0 + sc_ref[0]) + sh_ref[0]).astype(BF16)
        h_scr[...] = hb
        f_ref[0] = _dot(hb, wf_ref[...])

    acc = _dot(h_scr[...], w_ref[...])

    @pl.when(j < n_q_tiles)
    def _():
        qkv_ref[0] = (acc * q_scale).astype(BF16)

    @pl.when(jnp.logical_and(j >= n_q_tiles, j < n_qkv_tiles))
    def _():
        qkv_ref[0] = acc.astype(BF16)

    @pl.when(j >= n_qkv_tiles)
    def _():
        u_ref[0] = acc


def in_proj(x, scale, shift, w_main, w_f, n_qkv):
    bsz, seq, d = x.shape
    n_all = w_main.shape[1]
    n_u = n_all - n_qkv
    tm, tn = 512, 1024
    nq = n_qkv // tn
    grid = (bsz, seq // tm, n_all // tn)
    return pl.pallas_call(
        functools.partial(_inproj_kernel, n_q_tiles=n_qkv // 3 // tn, n_qkv_tiles=nq,
                          q_scale=HEAD_DIM ** -0.5 * LOG2E),
        grid=grid,
        in_specs=[pl.BlockSpec((1, tm, d), lambda b, i, j: (b, i, 0)),
                  pl.BlockSpec((1, 1, d), lambda b, i, j: (b, 0, 0)),
                  pl.BlockSpec((1, 1, d), lambda b, i, j: (b, 0, 0)),
                  pl.BlockSpec((d, tn), lambda b, i, j: (0, j)),
                  pl.BlockSpec((d, LANES), lambda b, i, j: (0, 0))],
        out_specs=[pl.BlockSpec((1, tm, tn), lambda b, i, j: (b, i, jnp.minimum(j, nq - 1))),
                   pl.BlockSpec((1, tm, tn), lambda b, i, j: (b, i, jnp.maximum(j - nq, 0))),
                   pl.BlockSpec((1, tm, LANES), lambda b, i, j: (b, i, 0))],
        out_shape=[jax.ShapeDtypeStruct((bsz, seq, n_qkv), BF16),
                   jax.ShapeDtypeStruct((bsz, seq, n_u), F32),
                   jax.ShapeDtypeStruct((bsz, seq, LANES), F32)],
        scratch_shapes=[pltpu.VMEM((tm, d), BF16)],
        compiler_params=_params(("arbitrary", "arbitrary", "arbitrary"), 52),
        name="in_proj",
    )(x, scale, shift, w_main, w_f)


def _split3(x):
    p1 = x.astype(BF16)
    r1 = x - p1.astype(F32)
    p2 = r1.astype(BF16)
    p3 = (r1 - p2.astype(F32)).astype(BF16)
    return p1, p2, p3


N_PIECES = 3


def _cum_kernel(f_ref, b_ref, qa_ref, ka_ref, carry, *, n_heads):
    i = pl.program_id(1)

    @pl.when(i == 0)
    def _():
        carry[...] = jnp.zeros_like(carry)

    tc = f_ref.shape[1]
    lf = jax.nn.log_sigmoid(f_ref[0] + b_ref[...])
    row = lax.broadcasted_iota(jnp.int32, (tc, tc), 0)
    col = lax.broadcasted_iota(jnp.int32, (tc, tc), 1)
    tri = (col <= row).astype(BF16)
    p1, p2, p3 = _split3(lf)
    cs = _dot(tri, p1) + _dot(tri, p2) + _dot(tri, p3) + carry[...]
    carry[...] = cs[tc - 1:tc, :]

    pieces = jnp.concatenate(_split3(cs * LOG2E), axis=1)
    r = lax.broadcasted_iota(jnp.int32, (N_PIECES * LANES, LANES), 0)
    c = lax.broadcasted_iota(jnp.int32, (N_PIECES * LANES, LANES), 1)
    lane = lax.broadcasted_iota(jnp.int32, (tc, LANES), 1)
    ones_q = jnp.where(jnp.logical_and(lane >= N_PIECES, lane < 2 * N_PIECES), 1.0, 0.0)
    ones_k = jnp.where(lane < N_PIECES, 1.0, 0.0)
    for h in range(n_heads):
        sel_q = (r == c * LANES + h).astype(BF16)
        sel_k = (r == (c - N_PIECES) * LANES + h).astype(BF16)
        qa_ref[0, h] = (_dot(pieces, sel_q) + ones_q).astype(BF16)
        ka_ref[0, h] = (ones_k - _dot(pieces, sel_k)).astype(BF16)


def forget_cum(f, b_forget):
    bsz, seq, _ = f.shape
    n_heads = b_forget.shape[0]
    tc = 256
    bpad = jnp.zeros((1, LANES), F32).at[0, :n_heads].set(b_forget)
    out_spec = pl.BlockSpec((1, n_heads, tc, LANES), lambda b, i: (b, 0, i, 0))
    out_shape = jax.ShapeDtypeStruct((bsz, n_heads, seq, LANES), BF16)
    return pl.pallas_call(
        functools.partial(_cum_kernel, n_heads=n_heads),
        grid=(bsz, seq // tc),
        in_specs=[pl.BlockSpec((1, tc, LANES), lambda b, i: (b, i, 0)),
                  pl.BlockSpec((1, LANES), lambda b, i: (0, 0))],
        out_specs=[out_spec, out_spec],
        out_shape=[out_shape, out_shape],
        scratch_shapes=[pltpu.VMEM((1, LANES), F32)],
        compiler_params=_params(("arbitrary", "arbitrary"), 24),
        name="forget_cum",
    )(f, bpad)


ATT_QROWS = 1024
ATT_KEYS = 512


def _attn_kernel(*refs, n_cast):
    q_ref, qa_ref, k_ref, ka_ref, v_ref = refs[:5]
    src_refs = refs[5:5 + n_cast]
    o_ref = refs[5 + n_cast]
    dst_refs = refs[6 + n_cast:6 + 2 * n_cast]
    m_scr, acc_scr, s_scr = refs[6 + 2 * n_cast:]
    for src, dst in zip(src_refs, dst_refs):
        dst[...] = src[...].astype(BF16)

    tq = q_ref.shape[1]
    tk = ATT_KEYS
    n_groups = tq // tk
    qi = pl.program_id(2)
    q = jnp.concatenate([q_ref[0], qa_ref[0, 0]], axis=1)
    ones = jnp.ones((tk, HEAD_DIM), BF16)

    m_scr[...] = jnp.full_like(m_scr, NEG_BIG)
    acc_scr[...] = jnp.zeros_like(acc_scr)

    def scores(kb, slot, first_group=0):
        k0 = pl.multiple_of(kb * tk, tk)
        kt = jnp.concatenate([k_ref[0, pl.ds(k0, tk), :], ka_ref[0, 0, pl.ds(k0, tk), :]], axis=1)
        r0 = first_group * tk
        s_scr[slot, r0:, :] = lax.dot_general(q[r0:], kt, _NT, preferred_element_type=F32)

    def update(kb, slot, diag_group=None):
        k0 = pl.multiple_of(kb * tk, tk)
        vt = jnp.concatenate([v_ref[0, pl.ds(k0, tk), :], ones], axis=1)
        for g in range(n_groups):
            if diag_group is not None and g < diag_group:
                continue
            rs = slice(g * tk, (g + 1) * tk)
            s = s_scr[slot, rs, :]
            if g == diag_group:
                qpos = lax.broadcasted_iota(jnp.int32, (tk, tk), 0)
                kpos = lax.broadcasted_iota(jnp.int32, (tk, tk), 1)
                s = jnp.where(kpos <= qpos, s, NEG_BIG)
            m_prev = m_scr[rs]
            m_new = jnp.maximum(m_prev, jnp.max(s, axis=1, keepdims=True))
            alpha = jnp.exp2(m_prev - m_new)
            p = jnp.exp2(s - m_new)
            acc_scr[rs] = alpha * acc_scr[rs] + _dot(p.astype(BF16), vt)
            m_scr[rs] = m_new

    def body(i, carry):
        kb = n_groups * i
        for g in range(n_groups):
            scores(kb + g + 1, (g + 1) % 2)
            update(kb + g, g % 2)
        return carry

    assert n_groups % 2 == 0
    scores(0, 0)
    lax.fori_loop(0, qi, body, 0)
    kb = n_groups * qi
    for g in range(n_groups):
        if g + 1 < n_groups:
            scores(kb + g + 1, (g + 1) % 2, first_group=g + 1)
        update(kb + g, g % 2, diag_group=g)

    o_ref[0] = acc_scr[:, :HEAD_DIM] / acc_scr[:, HEAD_DIM:]


BF16_ROWS = 16


def _cast_chunks(w, n_steps):
    cols = w.shape[-1]
    total_rows = w.size // cols
    n_chunks = n_steps
    while total_rows % (n_chunks * BF16_ROWS):
        n_chunks //= 2
    return w.reshape(n_chunks, total_rows // n_chunks, cols)


def attention(qkv, q_aug, k_aug, n_heads, cast_weights):
    bsz, seq, _ = qkv.shape
    t = min(ATT_QROWS, seq)
    nq = seq // t
    n_steps = bsz * n_heads * nq
    srcs = [_cast_chunks(w, n_steps) for w in cast_weights]

    def chunk_spec(a):
        per = n_steps // a.shape[0]
        return pl.BlockSpec((1,) + a.shape[1:], lambda b, h, i: (((b * n_heads + h) * nq + i) // per, 0, 0))

    outs = pl.pallas_call(
        functools.partial(_attn_kernel, n_cast=len(srcs)),
        grid=(bsz, n_heads, nq),
        in_specs=[pl.BlockSpec((1, t, HEAD_DIM), lambda b, h, i: (b, i, h)),
                  pl.BlockSpec((1, 1, t, LANES), lambda b, h, i: (b, h, i, 0)),
                  pl.BlockSpec((1, seq, HEAD_DIM), lambda b, h, i: (b, 0, n_heads + h)),
                  pl.BlockSpec((1, 1, seq, LANES), lambda b, h, i: (b, h, 0, 0)),
                  pl.BlockSpec((1, seq, HEAD_DIM), lambda b, h, i: (b, 0, 2 * n_heads + h))]
                 + [chunk_spec(a) for a in srcs],
        out_specs=[pl.BlockSpec((1, t, HEAD_DIM), lambda b, h, i: (b, i, h))] + [chunk_spec(a) for a in srcs],
        out_shape=[jax.ShapeDtypeStruct((bsz, seq, n_heads * HEAD_DIM), F32)]
                  + [jax.ShapeDtypeStruct(a.shape, BF16) for a in srcs],
        scratch_shapes=[pltpu.VMEM((t, 1), F32), pltpu.VMEM((t, 2 * HEAD_DIM), F32),
                        pltpu.VMEM((2, t, ATT_KEYS), F32)],
        compiler_params=_params(("arbitrary", "arbitrary", "arbitrary"), 56),
        name="attention",
    )(qkv, q_aug, qkv, k_aug, qkv, *srcs)
    return outs[0], [o.reshape(w.shape) for o, w in zip(outs[1:], cast_weights)]


def _blockdiag(p):
    g, c, n = p.shape
    ns = g // GROUPS_PER_SLAB
    eye = jnp.eye(GROUPS_PER_SLAB, dtype=p.dtype)
    out = p.reshape(ns, GROUPS_PER_SLAB, c, 1, n) * eye[None, :, None, :, None]
    return out.reshape(ns, GROUPS_PER_SLAB * c, GROUPS_PER_SLAB * n)


def _ssm_prep_kernel(lr_ref, li_ref, ldt_ref, bre_ref, bim_ref, cre_ref, cim_ref, w1_ref, ft_ref, al_ref):
    L = SSM_CHUNK
    lr = lr_ref[0]
    li = li_ref[0]
    dt = jnp.exp(ldt_ref[0])
    mag = jnp.exp(lr * dt)
    a_re = mag * jnp.cos(li * dt)
    a_im = mag * jnp.sin(li * dt)
    den = lr * lr + li * li
    z_re = ((a_re - 1.0) * lr + a_im * li) / den
    z_im = (a_im * lr - (a_re - 1.0) * li) / den
    br = bre_ref[0]
    bi = bim_ref[0]
    bb_re = z_re * br - z_im * bi
    bb_im = z_re * bi + z_im * br
    cr = cre_ref[0]
    ci = cim_ref[0]
    ft0_hi, ft0_lo = _split2(jnp.concatenate([cr, -ci], axis=1))

    def power(d):
        m = jnp.exp(lr * dt * d)
        return m * jnp.cos(li * dt * d), m * jnp.sin(li * dt * d)

    w1_ref[0, :, :L * LANES] = jnp.zeros((L * LANES, L * LANES), BF16)
    for d in range(L):
        pr, pi = power(float(d))
        xe = jnp.concatenate([bb_re * pr - bb_im * pi, bb_re * pi + bb_im * pr], axis=1)
        j = L - 1 - d
        w1_ref[0, j * LANES:(j + 1) * LANES, L * LANES:] = xe.astype(BF16)
        xe_hi, xe_lo = _split2(xe)
        m_d = (_dot_nt(xe_hi, ft0_hi) + _dot_nt(xe_hi, ft0_lo) + _dot_nt(xe_lo, ft0_hi)).astype(BF16)
        for jj in range(L - d):
            w1_ref[0, jj * LANES:(jj + 1) * LANES, (jj + d) * LANES:(jj + d + 1) * LANES] = m_d
        pr1, pi1 = power(float(d + 1))
        ft_ref[0, d * LANES:(d + 1) * LANES, :] = jnp.concatenate(
            [cr * pr1 - ci * pi1, -(cr * pi1 + ci * pr1)], axis=1).astype(BF16)
    prl, pil = power(float(L))
    al_ref[0] = jnp.concatenate([prl, pil], axis=1)


def ssm_prep(lam_re, lam_im, log_dt, b_re, b_im, c_re, c_im):
    g, n = lam_re.shape
    ns = g // GROUPS_PER_SLAB
    L = SSM_CHUNK
    rowvec = lambda a: a.reshape(ns, 1, SLAB_STATE)
    args = (rowvec(lam_re), rowvec(lam_im), rowvec(jnp.repeat(log_dt, n)),
            _blockdiag(b_re.transpose(0, 2, 1)), _blockdiag(b_im.transpose(0, 2, 1)),
            _blockdiag(c_re), _blockdiag(c_im))
    vec_spec = pl.BlockSpec((1, 1, SLAB_STATE), lambda s: (s, 0, 0))
    mat_spec = pl.BlockSpec((1, LANES, SLAB_STATE), lambda s: (s, 0, 0))
    return pl.pallas_call(
        _ssm_prep_kernel,
        grid=(ns,),
        in_specs=[vec_spec] * 3 + [mat_spec] * 4,
        out_specs=[pl.BlockSpec((1, L * LANES, L * LANES + 2 * SLAB_STATE), lambda s: (s, 0, 0)),
                   pl.BlockSpec((1, L * LANES, 2 * SLAB_STATE), lambda s: (s, 0, 0)),
                   pl.BlockSpec((1, 1, 2 * SLAB_STATE), lambda s: (s, 0, 0))],
        out_shape=[jax.ShapeDtypeStruct((ns, L * LANES, L * LANES + 2 * SLAB_STATE), BF16),
                   jax.ShapeDtypeStruct((ns, L * LANES, 2 * SLAB_STATE), BF16),
                   jax.ShapeDtypeStruct((ns, 1, 2 * SLAB_STATE), F32)],
        compiler_params=_params(("arbitrary",), 48),
        name="ssm_prep",
    )(*args)


def _ssm_kernel(u_ref, w1_ref, ft_ref, al_ref, d_ref, y_ref, uf_scr, e_scr, y_scr):
    L = SSM_CHUNK
    nch = uf_scr.shape[0]
    lc = L * LANES
    for j in range(L):
        uf_scr[:, j * LANES:(j + 1) * LANES] = u_ref[0, pl.ds(j, nch, stride=L), :].astype(BF16)
    uf = uf_scr[...]
    e_scr[...] = _dot(uf, w1_ref[0, :, lc:])

    a_re = al_ref[0, :, :SLAB_STATE]
    a_im = al_ref[0, :, SLAB_STATE:]
    h_re = jnp.zeros((1, SLAB_STATE), F32)
    h_im = jnp.zeros((1, SLAB_STATE), F32)
    for k in range(nch):
        e = e_scr[k:k + 1, :]
        e_scr[k:k + 1, :] = jnp.concatenate([h_re, h_im], axis=1)
        h_re, h_im = (a_re * h_re - a_im * h_im + e[:, :SLAB_STATE],
                      a_re * h_im + a_im * h_re + e[:, SLAB_STATE:])

    for c in range(lc // SSM_COLS):
        k_hi = (c + 1) * SSM_COLS
        cols = slice(c * SSM_COLS, k_hi)
        y_scr[:, cols] = _dot(uf[:, :k_hi], w1_ref[0, :k_hi, cols])
    y = y_scr[...] + lax.dot_general(e_scr[...].astype(BF16), ft_ref[0], _NT, preferred_element_type=F32)
    for i in range(L):
        yi = y[:, i * LANES:(i + 1) * LANES] + d_ref[0] * u_ref[0, pl.ds(i, nch, stride=L), :]
        y_ref[0, pl.ds(i, nch, stride=L), :] = jax.nn.gelu(yi)


def ssm_scan(u, w1, ft, al, d_skip):
    bsz, seq, c = u.shape
    ns = c // LANES
    L = SSM_CHUNK
    nch = seq // L
    lc = L * LANES
    once = pl.Buffered(1)
    return pl.pallas_call(
        _ssm_kernel,
        grid=(ns, bsz),
        in_specs=[pl.BlockSpec((1, seq, LANES), lambda s, b: (b, 0, s)),
                  pl.BlockSpec((1, lc, lc + 2 * SLAB_STATE), lambda s, b: (s, 0, 0), pipeline_mode=once),
                  pl.BlockSpec((1, lc, 2 * SLAB_STATE), lambda s, b: (s, 0, 0), pipeline_mode=once),
                  pl.BlockSpec((1, 1, 2 * SLAB_STATE), lambda s, b: (s, 0, 0)),
                  pl.BlockSpec((1, 1, LANES), lambda s, b: (s, 0, 0))],
        out_specs=pl.BlockSpec((1, seq, LANES), lambda s, b: (b, 0, s)),
        out_shape=jax.ShapeDtypeStruct((bsz, seq, c), F32),
        scratch_shapes=[pltpu.VMEM((nch, lc), BF16), pltpu.VMEM((nch, 2 * SLAB_STATE), F32),
                        pltpu.VMEM((nch, lc), F32)],
        compiler_params=_params(("arbitrary", "arbitrary"), 56),
        name="ssm_scan",
    )(u, w1, ft, al, d_skip.reshape(ns, 1, LANES))


def _glu_kernel(y_ref, w_ref, b_ref, g_ref, o_ref):
    y = y_ref[...]
    o = y * jax.nn.sigmoid(_dot(y.astype(BF16), w_ref[...]) + b_ref[...])
    ms = jnp.mean(o * o, axis=-1, keepdims=True)
    o_ref[...] = (o * lax.rsqrt(ms + RMS_EPS) * g_ref[...]).astype(BF16)


def glu_norm(y, w_glu, b_glu, g):
    t, c = y.shape
    tm = 512
    return pl.pallas_call(
        _glu_kernel,
        grid=(t // tm,),
        in_specs=[pl.BlockSpec((tm, c), lambda i: (i, 0)),
                  pl.BlockSpec((c, c), lambda i: (0, 0)),
                  pl.BlockSpec((1, c), lambda i: (0, 0)),
                  pl.BlockSpec((1, c), lambda i: (0, 0))],
        out_specs=pl.BlockSpec((tm, c), lambda i: (i, 0)),
        out_shape=jax.ShapeDtypeStruct((t, c), BF16),
        compiler_params=_params(("arbitrary",), 48),
        name="glu_norm",
    )(y, w_glu, b_glu.reshape(1, c), g.reshape(1, c))


def _outproj_kernel(attn_ref, ga_ref, ssm_ref, w_ref, x_ref, gate_ref, lng_ref, lnb_ref, sc2_ref, sh2_ref,
                    wrh_ref, wrl_ref, x1_ref, h2_ref, lg_ref, a_scr, *, alpha, n_att):
    j = pl.program_id(2)
    nj = pl.num_programs(2)
    tn = w_ref.shape[1]

    tm = x1_ref.shape[1]
    n_chunks = tm // ROW_CHUNK

    @pl.when(j == 0)
    def _():
        def norm(c, carry):
            rs = pl.ds(pl.multiple_of(c * ROW_CHUNK, ROW_CHUNK), ROW_CHUNK)
            a = attn_ref[0, rs, :]
            ms = jnp.mean(a * a, axis=-1, keepdims=True)
            a_scr[rs, :n_att] = (a * lax.rsqrt(ms + RMS_EPS) * ga_ref[...]).astype(BF16)
            return carry
        lax.fori_loop(0, n_chunks, norm, 0)
        a_scr[:, n_att:] = ssm_ref[0]

    mixed = _dot(a_scr[...], w_ref[...])
    col = pl.multiple_of(j * tn, tn)
    x1_ref[0, :, pl.ds(col, tn)] = alpha * x_ref[0] + (1.0 + gate_ref[0]) * mixed

    @pl.when(j == nj - 1)
    def _():
        def finish(c, carry):
            rs = pl.ds(pl.multiple_of(c * ROW_CHUNK, ROW_CHUNK), ROW_CHUNK)
            r = x1_ref[0, rs, :]
            mu = jnp.mean(r, axis=-1, keepdims=True)
            var = jnp.mean(jnp.square(r - mu), axis=-1, keepdims=True)
            x1 = (r - mu) * lax.rsqrt(var + LN_EPS) * lng_ref[...] + lnb_ref[...]
            x1_ref[0, rs, :] = x1
            h2 = x1 * (1.0 + sc2_ref[0]) + sh2_ref[0]
            hi = h2.astype(BF16)
            hi_f = hi.astype(F32)
            lo = (h2 - hi_f).astype(BF16)
            lg_ref[0, rs, :] = _dot(hi, wrh_ref[...]) + _dot(hi, wrl_ref[...]) + _dot(lo, wrh_ref[...])
            h2_ref[0, rs, :] = pack_bf16_pairs(hi_f)
            return carry
        lax.fori_loop(0, n_chunks, finish, 0)


def pack_bf16_pairs(x):
    n = x.shape[-1] // 2
    bits = lax.bitcast_convert_type(x, jnp.uint32)
    return bits[:, n:] | (bits[:, :n] >> 16)


def unpack_bf16_pairs(p):
    lo = lax.bitcast_convert_type(p << 16, F32).astype(BF16)
    hi = lax.bitcast_convert_type(p & jnp.uint32(0xFFFF0000), F32).astype(BF16)
    return lo, hi


def out_proj(attn, g_attn, ssm_n, w_out, x, gate1, ln_g, ln_b, scale2, shift2, wr_hi, wr_lo, alpha):
    bsz, seq, d = x.shape
    n_att = attn.shape[-1]
    n_ssm = ssm_n.shape[-1]
    k = n_att + n_ssm
    tm, tn = 512, 256
    row = lambda a: a.reshape(1, -1)
    full = lambda n: pl.BlockSpec((1, n), lambda b, i, j: (0, 0))
    return pl.pallas_call(
        functools.partial(_outproj_kernel, alpha=alpha, n_att=n_att),
        grid=(bsz, seq // tm, d // tn),
        in_specs=[pl.BlockSpec((1, tm, n_att), lambda b, i, j: (b, i, 0)),
                  full(n_att),
                  pl.BlockSpec((1, tm, n_ssm), lambda b, i, j: (b, i, 0)),
                  pl.BlockSpec((k, tn), lambda b, i, j: (0, j)),
                  pl.BlockSpec((1, tm, tn), lambda b, i, j: (b, i, j)),
                  pl.BlockSpec((1, 1, tn), lambda b, i, j: (b, 0, j)),
                  full(d), full(d),
                  pl.BlockSpec((1, 1, d), lambda b, i, j: (b, 0, 0)),
                  pl.BlockSpec((1, 1, d), lambda b, i, j: (b, 0, 0)),
                  pl.BlockSpec((d, LANES), lambda b, i, j: (0, 0)),
                  pl.BlockSpec((d, LANES), lambda b, i, j: (0, 0))],
        out_specs=[pl.BlockSpec((1, tm, d), lambda b, i, j: (b, i, 0)),
                   pl.BlockSpec((1, tm, d // 2), lambda b, i, j: (b, i, 0)),
                   pl.BlockSpec((1, tm, LANES), lambda b, i, j: (b, i, 0))],
        out_shape=[jax.ShapeDtypeStruct((bsz, seq, d), F32),
                   jax.ShapeDtypeStruct((bsz, seq, d // 2), jnp.uint32),
                   jax.ShapeDtypeStruct((bsz, seq, LANES), F32)],
        scratch_shapes=[pltpu.VMEM((tm, k), BF16)],
        compiler_params=_params(("arbitrary", "arbitrary", "arbitrary"), 56),
        name="out_proj",
    )(attn, row(g_attn), ssm_n, w_out, x, gate1, row(ln_g), row(ln_b), scale2, shift2, wr_hi, wr_lo)


def _moe_kernel(tok_ref, bexp_ref, nsub_ref, nact_ref, h_hbm, wg_ref, wu_ref, wd_ref, y_ref, xbuf, sem):
    i = pl.program_id(0)
    nact = nact_ref[0]
    rows = xbuf.shape[1]

    def start_gather(blk, slot):
        n_trips = nsub_ref[blk] * (MOE_SUB // GATHER_UNROLL)

        def body(t, carry):
            for k in range(GATHER_UNROLL):
                r = t * GATHER_UNROLL + k
                tok = tok_ref[blk * rows + r]
                pltpu.make_async_copy(h_hbm.at[pl.ds(tok, 1)], xbuf.at[slot, pl.ds(r, 1)], sem.at[slot]).start()
            return carry
        lax.fori_loop(0, n_trips, body, 0)

    def run_block(slot, n):
        pltpu.make_async_copy(h_hbm.at[pl.ds(0, n)], xbuf.at[slot, pl.ds(0, n)], sem.at[slot]).wait()
        x_lo, x_hi = unpack_bf16_pairs(xbuf[slot, :n])
        half = x_lo.shape[1]
        g = _dot(x_lo, wg_ref[0, :half, :]) + _dot(x_hi, wg_ref[0, half:, :])
        u = _dot(x_lo, wu_ref[0, :half, :]) + _dot(x_hi, wu_ref[0, half:, :])
        act = (jax.nn.silu(g) * u).astype(BF16)
        y_ref[:n, :] = _dot(act, wd_ref[0])
        if n < rows:
            y_ref[n:, :] = jnp.zeros((rows - n, y_ref.shape[1]), F32)

    @pl.when(jnp.logical_and(i == 0, nact > 0))
    def _():
        start_gather(0, 0)

    @pl.when(i + 1 < nact)
    def _():
        start_gather(i + 1, (i + 1) % 2)

    for n_sub in range(1, rows // MOE_SUB + 1):
        @pl.when(jnp.logical_and(i < nact, nsub_ref[i] == n_sub))
        def _():
            run_block(i % 2, n_sub * MOE_SUB)

    @pl.when(i >= nact)
    def _():
        y_ref[...] = jnp.zeros_like(y_ref)


def moe_experts(h2, tok_buf, block_expert, short_blocks, n_active, w_gate, w_up, w_down):
    t, dp = h2.shape
    d = 2 * dp
    n_rows = tok_buf.shape[0]
    rows = MOE_ROWS
    n_blocks = n_rows // rows
    de = w_gate.shape[-1]

    def wmap(i, tok, bexp, short, nact):
        return (bexp[jnp.minimum(i, jnp.maximum(nact[0] - 1, 0))], 0, 0)

    grid_spec = pltpu.PrefetchScalarGridSpec(
        num_scalar_prefetch=4,
        grid=(n_blocks,),
        in_specs=[pl.BlockSpec(memory_space=pl.ANY),
                  pl.BlockSpec((1, d, de), wmap),
                  pl.BlockSpec((1, d, de), wmap),
                  pl.BlockSpec((1, de, d), wmap)],
        out_specs=pl.BlockSpec((rows, d), lambda i, *_: (i, 0)),
        scratch_shapes=[pltpu.VMEM((2, rows, dp), jnp.uint32), pltpu.SemaphoreType.DMA((2,))],
    )
    return pl.pallas_call(
        _moe_kernel,
        grid_spec=grid_spec,
        out_shape=jax.ShapeDtypeStruct((n_rows, d), F32),
        compiler_params=_params(("arbitrary",), 56),
        name="moe_experts",
    )(tok_buf, block_expert, short_blocks, n_active, h2, w_gate, w_up, w_down)


def _combine_kernel(dest_ref, ys_hbm, rt_ref, x1_ref, gate_ref, lng_ref, lnb_ref, o_ref, ybuf, sem, *, alpha):
    i = pl.program_id(0)
    n = pl.num_programs(0)
    tm = x1_ref.shape[0]
    n_copies = TOP_K * tm

    def gather(blk, slot, start):
        if not start:
            pltpu.make_async_copy(ys_hbm.at[pl.ds(0, n_copies)], ybuf.at[slot], sem.at[slot]).wait()
            return

        def body(r, carry):
            row = dest_ref[blk * n_copies + r]
            pltpu.make_async_copy(ys_hbm.at[pl.ds(row, 1)], ybuf.at[slot, pl.ds(r, 1)], sem.at[slot]).start()
            return carry
        lax.fori_loop(0, n_copies, body, 0, unroll=GATHER_UNROLL)

    @pl.when(i == 0)
    def _():
        gather(0, 0, True)

    @pl.when(i + 1 < n)
    def _():
        gather(i + 1, (i + 1) % 2, True)

    slot = i % 2
    gather(i, slot, False)
    moe = rt_ref[:, ROUTE_W:ROUTE_W + 1] * ybuf[slot, :tm, :]
    for kk in range(1, TOP_K):
        moe = moe + rt_ref[:, ROUTE_W + kk:ROUTE_W + kk + 1] * ybuf[slot, kk * tm:(kk + 1) * tm, :]
    r = alpha * x1_ref[...] + (1.0 + gate_ref[0]) * moe
    mu = jnp.mean(r, axis=-1, keepdims=True)
    var = jnp.mean(jnp.square(r - mu), axis=-1, keepdims=True)
    o_ref[...] = (r - mu) * lax.rsqrt(var + LN_EPS) * lng_ref[...] + lnb_ref[...]


def moe_combine(ys, dest, table, x1, gate2, ln_g, ln_b, alpha, seq):
    t, d = x1.shape
    tm = 128
    tiles_per_seq = seq // tm
    grid_spec = pltpu.PrefetchScalarGridSpec(
        num_scalar_prefetch=1,
        grid=(t // tm,),
        in_specs=[pl.BlockSpec(memory_space=pl.ANY),
                  pl.BlockSpec((tm, LANES), lambda i, *_: (i, 0)),
                  pl.BlockSpec((tm, d), lambda i, *_: (i, 0)),
                  pl.BlockSpec((1, 1, d), lambda i, *_: (i // tiles_per_seq, 0, 0)),
                  pl.BlockSpec((1, d), lambda i, *_: (0, 0)),
                  pl.BlockSpec((1, d), lambda i, *_: (0, 0))],
        out_specs=pl.BlockSpec((tm, d), lambda i, *_: (i, 0)),
        scratch_shapes=[pltpu.VMEM((2, TOP_K * tm, d), F32), pltpu.SemaphoreType.DMA((2,))],
    )
    return pl.pallas_call(
        functools.partial(_combine_kernel, alpha=alpha),
        grid_spec=grid_spec,
        out_shape=jax.ShapeDtypeStruct((t, d), F32),
        compiler_params=_params(("arbitrary",), 32),
        name="moe_combine",
    )(dest, ys, table, x1, gate2, ln_g.reshape(1, d), ln_b.reshape(1, d))


def _route_kernel(lg_ref, b_ref, o_ref, cnt_ref, carry):
    i = pl.program_id(0)

    @pl.when(i == 0)
    def _():
        carry[...] = jnp.zeros_like(carry)

    tm = lg_ref.shape[0]
    x = lg_ref[...] + b_ref[...]
    lane = lax.broadcasted_iota(jnp.int32, (tm, LANES), 1)
    ninf = -jnp.inf

    def top(v):
        vmax = jnp.max(v, axis=1, keepdims=True)
        return vmax, jnp.min(jnp.where(v == vmax, lane, LANES), axis=1, keepdims=True)

    gmask = lane < N_EXPERT_GROUPS
    gmax, g_sel = top(jnp.where(gmask, x, ninf))
    p_group = 1.0 / jnp.sum(jnp.where(gmask, jnp.exp(x - gmax), 0.0), axis=1, keepdims=True)

    lo = N_EXPERT_GROUPS + g_sel * EXPERTS_PER_GROUP
    cur = jnp.where(jnp.logical_and(lane >= lo, lane < lo + EXPERTS_PER_GROUP), x, ninf)
    vals, idxs = [], []
    for _ in range(TOP_K):
        v, ix = top(cur)
        vals.append(v)
        idxs.append(ix)
        cur = jnp.where(lane == ix, ninf, cur)
    exps = [jnp.exp(v - vals[0]) for v in vals]
    den = functools.reduce(lambda a, c: a + c, exps)

    member = functools.reduce(jnp.logical_or, [lane == ix for ix in idxs])
    mf = jnp.where(member, 1.0, 0.0)
    r_i = lax.broadcasted_iota(jnp.int32, (tm, tm), 0)
    c_i = lax.broadcasted_iota(jnp.int32, (tm, tm), 1)
    before = _dot((c_i < r_i).astype(BF16), mf.astype(BF16)) + carry[...]
    carry[...] = carry[...] + jnp.sum(mf, axis=0, keepdims=True)
    cnt_ref[...] = carry[...]

    out = jnp.zeros((tm, LANES), F32)
    for k in range(TOP_K):
        rank = jnp.sum(jnp.where(lane == idxs[k], before, 0.0), axis=1, keepdims=True)
        out = jnp.where(lane == ROUTE_EID + k, (idxs[k] - N_EXPERT_GROUPS).astype(F32), out)
        out = jnp.where(lane == ROUTE_RANK + k, rank, out)
        out = jnp.where(lane == ROUTE_W + k, p_group * (exps[k] / den), out)
    o_ref[...] = out


def route(logits, b_rg, b_re, n_rows, tm_combine):
    n_tok = logits.shape[0]
    n_experts = N_EXPERT_GROUPS * EXPERTS_PER_GROUP
    bias = jnp.zeros((1, LANES), F32).at[0, :N_EXPERT_GROUPS].set(b_rg)
    bias = bias.at[0, N_EXPERT_GROUPS:N_EXPERT_GROUPS + n_experts].set(b_re)
    tm = 512
    table, cnt = pl.pallas_call(
        _route_kernel,
        grid=(n_tok // tm,),
        in_specs=[pl.BlockSpec((tm, LANES), lambda i: (i, 0)),
                  pl.BlockSpec((1, LANES), lambda i: (0, 0))],
        out_specs=[pl.BlockSpec((tm, LANES), lambda i: (i, 0)),
                   pl.BlockSpec((1, LANES), lambda i: (0, 0))],
        out_shape=[jax.ShapeDtypeStruct((n_tok, LANES), F32), jax.ShapeDtypeStruct((1, LANES), F32)],
        scratch_shapes=[pltpu.VMEM((1, LANES), F32)],
        compiler_params=_params(("arbitrary",), 16),
        name="route",
    )(logits, bias)

    eid = table[:, ROUTE_EID:ROUTE_EID + TOP_K].astype(jnp.int32)
    rank = table[:, ROUTE_RANK:ROUTE_RANK + TOP_K].astype(jnp.int32)
    counts = cnt[0, N_EXPERT_GROUPS:N_EXPERT_GROUPS + n_experts].astype(jnp.int32)
    padded = ((counts + MOE_ROWS - 1) // MOE_ROWS) * MOE_ROWS
    pends = jnp.cumsum(padded)
    pstarts = pends - padded
    onehot = eid[:, :, None] == jnp.arange(n_experts, dtype=jnp.int32)
    dest = jnp.sum(jnp.where(onehot, pstarts, 0), axis=-1) + rank
    tok = jnp.broadcast_to(jnp.arange(n_tok, dtype=jnp.int32)[:, None], (n_tok, TOP_K))
    tok_buf = jnp.zeros((n_rows,), jnp.int32).at[dest.reshape(-1)].set(tok.reshape(-1), unique_indices=True)
    n_blocks = n_rows // MOE_ROWS
    block_expert = jnp.clip(
        jnp.searchsorted(pends, jnp.arange(n_blocks, dtype=jnp.int32) * MOE_ROWS, side='right'),
        0, n_experts - 1).astype(jnp.int32)
    rows_left = (pstarts + counts)[block_expert] - jnp.arange(n_blocks, dtype=jnp.int32) * MOE_ROWS
    short_blocks = jnp.clip(-(-rows_left // MOE_SUB), 1, MOE_ROWS // MOE_SUB).astype(jnp.int32)
    n_active = (pends[-1] // MOE_ROWS).astype(jnp.int32).reshape(1)
    dest_tiles = dest.reshape(n_tok // tm_combine, tm_combine, TOP_K).transpose(0, 2, 1).reshape(-1)
    return table, tok_buf, block_expert, short_blocks, n_active, dest_tiles.astype(jnp.int32)


def _layer(x, c, w_ada, b_ada, w_in, b_forget, lam_re, lam_im, log_dt, b_re, b_im, c_re, c_im, d_skip,
           w_glu, b_glu, g_attn, g_ssm, w_out, ln1_g, ln1_b, w_rg, b_rg, w_re, b_re_r,
           w_gate, w_up, w_down, ln2_g, ln2_b, alpha):
    bsz, seq, d = x.shape
    n_tok = bsz * seq
    n_heads = b_forget.shape[0]
    d_att = n_heads * HEAD_DIM
    d_ssm = d_skip.shape[0] * d_skip.shape[1]

    mod = ada_mod(c, w_ada, b_ada).reshape(bsz, 1, -1)
    shift1, scale1, gate1, shift2, scale2, gate2 = jnp.split(mod, 6, axis=-1)

    n_qkv = 3 * d_att
    w_main = jnp.concatenate([w_in[:, :n_qkv], w_in[:, n_qkv + n_heads:]], axis=1).astype(BF16)
    w_f = jnp.zeros((d, LANES), BF16).at[:, :n_heads].set(w_in[:, n_qkv:n_qkv + n_heads].astype(BF16))
    qkv, u, f = in_proj(x, scale1, shift1, w_main, w_f, n_qkv)

    q_aug, k_aug = forget_cum(f, b_forget)
    attn, (w_glu_b, w_out_b, w_gate_b, w_up_b, w_down_b) = attention(
        qkv, q_aug, k_aug, n_heads, [w_glu, w_out, w_gate, w_up, w_down])

    w1, ft, al = ssm_prep(lam_re, lam_im, log_dt, b_re, b_im, c_re, c_im)
    y = ssm_scan(u, w1, ft, al, d_skip)
    ssm_n = glu_norm(y.reshape(n_tok, d_ssm), w_glu_b, b_glu, g_ssm).reshape(bsz, seq, d_ssm)

    n_experts = N_EXPERT_GROUPS * EXPERTS_PER_GROUP
    w_r = jnp.zeros((d, LANES), F32).at[:, :N_EXPERT_GROUPS].set(w_rg)
    w_r = w_r.at[:, N_EXPERT_GROUPS:N_EXPERT_GROUPS + n_experts].set(w_re)
    wr_hi = w_r.astype(BF16)
    wr_lo = (w_r - wr_hi.astype(F32)).astype(BF16)
    x1, h2, logits = out_proj(attn, g_attn, ssm_n, w_out_b, x, gate1, ln1_g, ln1_b,
                              scale2, shift2, wr_hi, wr_lo, alpha)

    n_assign = n_tok * TOP_K
    n_blocks = -(-(n_assign + n_experts * (MOE_ROWS - 1)) // MOE_ROWS)
    n_rows = n_blocks * MOE_ROWS
    table, tok_buf, block_expert, short_blocks, n_active, dest = route(
        logits.reshape(n_tok, LANES), b_rg, b_re_r, n_rows, 128)
    ys = moe_experts(h2.reshape(n_tok, d // 2), tok_buf, block_expert, short_blocks, n_active,
                     w_gate_b, w_up_b, w_down_b)
    out = moe_combine(ys, dest, table, x1.reshape(n_tok, d), gate2, ln2_g, ln2_b, alpha, seq)
    return out.reshape(bsz, seq, d)


def kernel(x, c, w_ada, b_ada, w_in, b_forget, ssm_lambda_re, ssm_lambda_im, ssm_log_dt, ssm_b_re, ssm_b_im,
           ssm_c_re, ssm_c_im, ssm_d, w_glu, b_glu, g_attn, g_ssm, w_out, ln1_g, ln1_b, w_router_group,
           b_router_group, w_router_expert, b_router_expert, w_gate, w_up, w_down, ln2_g, ln2_b):
    depth = w_ada.shape[0]
    alpha = (2.0 * depth) ** 0.25
    for l in range(depth):
        x = _layer(x, c, w_ada[l], b_ada[l], w_in[l], b_forget[l], ssm_lambda_re[l], ssm_lambda_im[l],
                   ssm_log_dt[l], ssm_b_re[l], ssm_b_im[l], ssm_c_re[l], ssm_c_im[l], ssm_d[l],
                   w_glu[l], b_glu[l], g_attn[l], g_ssm[l], w_out[l], ln1_g[l], ln1_b[l],
                   w_router_group[l], b_router_group[l], w_router_expert[l], b_router_expert[l],
                   w_gate[l], w_up[l], w_down[l], ln2_g[l], ln2_b[l], alpha)
    return x
```

```python
import functools
import math

import jax
import jax.numpy as jnp
from jax import lax
from jax.experimental import pallas as pl
from jax.experimental.pallas import tpu as pltpu

F32 = jnp.float32
BF16 = jnp.bfloat16

LANES = 128
HEAD_DIM = 128
SSM_GROUP = 16
SSM_STATE = 64
GROUPS_PER_SLAB = LANES // SSM_GROUP
SLAB_STATE = GROUPS_PER_SLAB * SSM_STATE
SSM_CHUNK = 16
SSM_COLS = 512
N_EXPERT_GROUPS = 8
EXPERTS_PER_GROUP = 8
TOP_K = 2
MOE_ROWS = 256
MOE_SUB = 64
GATHER_UNROLL = 8
ROW_CHUNK = 128
ROUTE_EID, ROUTE_RANK, ROUTE_W = 0, TOP_K, 2 * TOP_K
LN_EPS = 1e-5
RMS_EPS = 1e-6
NEG_BIG = -1e30
LOG2E = math.log2(math.e)
MIB = 1024 * 1024

_NT = (((1,), (1,)), ((), ()))


def _params(semantics, vmem_mib):
    return pltpu.CompilerParams(dimension_semantics=semantics, vmem_limit_bytes=vmem_mib * MIB)


def _dot(a, b):
    return jnp.dot(a, b, preferred_element_type=F32)


def _dot_nt(a, b):
    return lax.dot_general(a, b, _NT, preferred_element_type=F32)


def _split2(x):
    hi = x.astype(BF16)
    return hi, (x - hi.astype(F32)).astype(BF16)


def _ada_kernel(c_ref, w_ref, b_ref, o_ref):
    s = jax.nn.silu(c_ref[...]).astype(BF16)
    o_ref[...] = _dot(s, w_ref[...].astype(BF16)) + b_ref[...]


def ada_mod(c, w_ada, b_ada):
    bsz, d = c.shape
    n = w_ada.shape[1]
    rows = 8
    assert bsz <= rows
    cp = jnp.zeros((rows, d), F32).at[:bsz].set(c)
    tn = 512
    out = pl.pallas_call(
        _ada_kernel,
        grid=(n // tn,),
        in_specs=[pl.BlockSpec((rows, d), lambda j: (0, 0)),
                  pl.BlockSpec((d, tn), lambda j: (0, j)),
                  pl.BlockSpec((1, tn), lambda j: (0, j))],
        out_specs=pl.BlockSpec((rows, tn), lambda j: (0, j)),
        out_shape=jax.ShapeDtypeStruct((rows, n), F32),
        compiler_params=_params(("arbitrary",), 40),
        name="ada_mod",
    )(cp, w_ada, b_ada.reshape(1, n))
    return out[:bsz]


def _inproj_kernel(x_ref, sc_ref, sh_ref, w_ref, wf_ref, qkv_ref, u_ref, f_ref, h_scr, *,
                   n_q_tiles, n_qkv_tiles, q_scale):
    j = pl.program_id(2)

    @pl.when(j == 0)
    def _():
        hb = (x_ref[0] * (1.0 + sc_ref[0]) + sh_ref[0]).astype(BF16)
        h_scr[...] = hb
        f_ref[0] = _dot(hb, wf_ref[...])

    acc = _dot(h_scr[...], w_ref[...])

    @pl.when(j < n_q_tiles)
    def _():
        qkv_ref[0] = (acc * q_scale).astype(BF16)

    @pl.when(jnp.logical_and(j >= n_q_tiles, j < n_qkv_tiles))
    def _():
        qkv_ref[0] = acc.astype(BF16)

    @pl.when(j >= n_qkv_tiles)
    def _():
        u_ref[0] = acc


def in_proj(x, scale, shift, w_main, w_f, n_qkv):
    bsz, seq, d = x.shape
    n_all = w_main.shape[1]
    n_u = n_all - n_qkv
    tm, tn = 512, 1024
    nq = n_qkv // tn
    grid = (bsz, seq // tm, n_all // tn)
    return pl.pallas_call(
        functools.partial(_inproj_kernel, n_q_tiles=n_qkv // 3 // tn, n_qkv_tiles=nq,
                          q_scale=HEAD_DIM ** -0.5 * LOG2E),
        grid=grid,
        in_specs=[pl.BlockSpec((1, tm, d), lambda b, i, j: (b, i, 0)),
                  pl.BlockSpec((1, 1, d), lambda b, i, j: (b, 0, 0)),
                  pl.BlockSpec((1, 1, d), lambda b, i, j: (b, 0, 0)),
                  pl.BlockSpec((d, tn), lambda b, i, j: (0, j)),
                  pl.BlockSpec((d, LANES), lambda b, i, j: (0, 0))],
        out_specs=[pl.BlockSpec((1, tm, tn), lambda b, i, j: (b, i, jnp.minimum(j, nq - 1))),
                   pl.BlockSpec((1, tm, tn), lambda b, i, j: (b, i, jnp.maximum(j - nq, 0))),
                   pl.BlockSpec((1, tm, LANES), lambda b, i, j: (b, i, 0))],
        out_shape=[jax.ShapeDtypeStruct((bsz, seq, n_qkv), BF16),
                   jax.ShapeDtypeStruct((bsz, seq, n_u), F32),
                   jax.ShapeDtypeStruct((bsz, seq, LANES), F32)],
        scratch_shapes=[pltpu.VMEM((tm, d), BF16)],
        compiler_params=_params(("arbitrary", "arbitrary", "arbitrary"), 52),
        name="in_proj",
    )(x, scale, shift, w_main, w_f)


def _split3(x):
    p1 = x.astype(BF16)
    r1 = x - p1.astype(F32)
    p2 = r1.astype(BF16)
    p3 = (r1 - p2.astype(F32)).astype(BF16)
    return p1, p2, p3


N_PIECES = 3


def _cum_kernel(f_ref, b_ref, qa_ref, ka_ref, carry, *, n_heads):
    i = pl.program_id(1)

    @pl.when(i == 0)
    def _():
        carry[...] = jnp.zeros_like(carry)

    tc = f_ref.shape[1]
    lf = jax.nn.log_sigmoid(f_ref[0] + b_ref[...])
    row = lax.broadcasted_iota(jnp.int32, (tc, tc), 0)
    col = lax.broadcasted_iota(jnp.int32, (tc, tc), 1)
    tri = (col <= row).astype(BF16)
    p1, p2, p3 = _split3(lf)
    cs = _dot(tri, p1) + _dot(tri, p2) + _dot(tri, p3) + carry[...]
    carry[...] = cs[tc - 1:tc, :]

    pieces = jnp.concatenate(_split3(cs * LOG2E), axis=1)
    r = lax.broadcasted_iota(jnp.int32, (N_PIECES * LANES, LANES), 0)
    c = lax.broadcasted_iota(jnp.int32, (N_PIECES * LANES, LANES), 1)
    lane = lax.broadcasted_iota(jnp.int32, (tc, LANES), 1)
    ones_q = jnp.where(jnp.logical_and(lane >= N_PIECES, lane < 2 * N_PIECES), 1.0, 0.0)
    ones_k = jnp.where(lane < N_PIECES, 1.0, 0.0)
    for h in range(n_heads):
        sel_q = (r == c * LANES + h).astype(BF16)
        sel_k = (r == (c - N_PIECES) * LANES + h).astype(BF16)
        qa_ref[0, h] = (_dot(pieces, sel_q) + ones_q).astype(BF16)
        ka_ref[0, h] = (ones_k - _dot(pieces, sel_k)).astype(BF16)


def forget_cum(f, b_forget):
    bsz, seq, _ = f.shape
    n_heads = b_forget.shape[0]
    tc = 512
    bpad = jnp.zeros((1, LANES), F32).at[0, :n_heads].set(b_forget)
    out_spec = pl.BlockSpec((1, n_heads, tc, LANES), lambda b, i: (b, 0, i, 0))
    out_shape = jax.ShapeDtypeStruct((bsz, n_heads, seq, LANES), BF16)
    return pl.pallas_call(
        functools.partial(_cum_kernel, n_heads=n_heads),
        grid=(bsz, seq // tc),
        in_specs=[pl.BlockSpec((1, tc, LANES), lambda b, i: (b, i, 0)),
                  pl.BlockSpec((1, LANES), lambda b, i: (0, 0))],
        out_specs=[out_spec, out_spec],
        out_shape=[out_shape, out_shape],
        scratch_shapes=[pltpu.VMEM((1, LANES), F32)],
        compiler_params=_params(("arbitrary", "arbitrary"), 24),
        name="forget_cum",
    )(f, bpad)


ATT_QROWS = 1024
ATT_KEYS = 512


def _attn_kernel(*refs, n_cast):
    q_ref, qa_ref, k_ref, ka_ref, v_ref = refs[:5]
    src_refs = refs[5:5 + n_cast]
    o_ref = refs[5 + n_cast]
    dst_refs = refs[6 + n_cast:6 + 2 * n_cast]
    m_scr, acc_scr, s_scr = refs[6 + 2 * n_cast:]
    for src, dst in zip(src_refs, dst_refs):
        dst[...] = src[...].astype(BF16)

    tq = q_ref.shape[1]
    tk = ATT_KEYS
    n_groups = tq // tk
    qi = pl.program_id(2)
    q = jnp.concatenate([q_ref[0], qa_ref[0, 0]], axis=1)
    ones = jnp.ones((tk, HEAD_DIM), BF16)

    m_scr[...] = jnp.full_like(m_scr, NEG_BIG)
    acc_scr[...] = jnp.zeros_like(acc_scr)

    def scores(kb, slot, first_group=0):
        k0 = pl.multiple_of(kb * tk, tk)
        kt = jnp.concatenate([k_ref[0, pl.ds(k0, tk), :], ka_ref[0, 0, pl.ds(k0, tk), :]], axis=1)
        r0 = first_group * tk
        s_scr[slot, r0:, :] = lax.dot_general(q[r0:], kt, _NT, preferred_element_type=F32)

    def update(kb, slot, diag_group=None):
        k0 = pl.multiple_of(kb * tk, tk)
        vt = jnp.concatenate([v_ref[0, pl.ds(k0, tk), :], ones], axis=1)
        for g in range(n_groups):
            if diag_group is not None and g < diag_group:
                continue
            rs = slice(g * tk, (g + 1) * tk)
            s = s_scr[slot, rs, :]
            if g == diag_group:
                qpos = lax.broadcasted_iota(jnp.int32, (tk, tk), 0)
                kpos = lax.broadcasted_iota(jnp.int32, (tk, tk), 1)
                s = jnp.where(kpos <= qpos, s, NEG_BIG)
            m_prev = m_scr[rs]
            m_new = jnp.maximum(m_prev, jnp.max(s, axis=1, keepdims=True))
            alpha = jnp.exp2(m_prev - m_new)
            p = jnp.exp2(s - m_new)
            acc_scr[rs] = alpha * acc_scr[rs] + _dot(p.astype(BF16), vt)
            m_scr[rs] = m_new

    def body(i, carry):
        kb = n_groups * i
        for g in range(n_groups):
            scores(kb + g + 1, (g + 1) % 2)
            update(kb + g, g % 2)
        return carry

    assert n_groups % 2 == 0
    scores(0, 0)
    lax.fori_loop(0, qi, body, 0)
    kb = n_groups * qi
    for g in range(n_groups):
        if g + 1 < n_groups:
            scores(kb + g + 1, (g + 1) % 2, first_group=g + 1)
        update(kb + g, g % 2, diag_group=g)

    o_ref[0] = acc_scr[:, :HEAD_DIM] / acc_scr[:, HEAD_DIM:]


BF16_ROWS = 16


def _cast_chunks(w, n_steps):
    cols = w.shape[-1]
    total_rows = w.size // cols
    n_chunks = n_steps
    while total_rows % (n_chunks * BF16_ROWS):
        n_chunks //= 2
    return w.reshape(n_chunks, total_rows // n_chunks, cols)


def attention(qkv, q_aug, k_aug, n_heads, cast_weights):
    bsz, seq, _ = qkv.shape
    t = min(ATT_QROWS, seq)
    nq = seq // t
    n_steps = bsz * n_heads * nq
    srcs = [_cast_chunks(w, n_steps) for w in cast_weights]

    def chunk_spec(a):
        per = n_steps // a.shape[0]
        return pl.BlockSpec((1,) + a.shape[1:], lambda b, h, i: (((b * n_heads + h) * nq + i) // per, 0, 0))

    outs = pl.pallas_call(
        functools.partial(_attn_kernel, n_cast=len(srcs)),
        grid=(bsz, n_heads, nq),
        in_specs=[pl.BlockSpec((1, t, HEAD_DIM), lambda b, h, i: (b, i, h)),
                  pl.BlockSpec((1, 1, t, LANES), lambda b, h, i: (b, h, i, 0)),
                  pl.BlockSpec((1, seq, HEAD_DIM), lambda b, h, i: (b, 0, n_heads + h)),
                  pl.BlockSpec((1, 1, seq, LANES), lambda b, h, i: (b, h, 0, 0)),
                  pl.BlockSpec((1, seq, HEAD_DIM), lambda b, h, i: (b, 0, 2 * n_heads + h))]
                 + [chunk_spec(a) for a in srcs],
        out_specs=[pl.BlockSpec((1, t, HEAD_DIM), lambda b, h, i: (b, i, h))] + [chunk_spec(a) for a in srcs],
        out_shape=[jax.ShapeDtypeStruct((bsz, seq, n_heads * HEAD_DIM), F32)]
                  + [jax.ShapeDtypeStruct(a.shape, BF16) for a in srcs],
        scratch_shapes=[pltpu.VMEM((t, 1), F32), pltpu.VMEM((t, 2 * HEAD_DIM), F32),
                        pltpu.VMEM((2, t, ATT_KEYS), F32)],
        compiler_params=_params(("arbitrary", "arbitrary", "arbitrary"), 56),
        name="attention",
    )(qkv, q_aug, qkv, k_aug, qkv, *srcs)
    return outs[0], [o.reshape(w.shape) for o, w in zip(outs[1:], cast_weights)]


def _blockdiag(p):
    g, c, n = p.shape
    ns = g // GROUPS_PER_SLAB
    eye = jnp.eye(GROUPS_PER_SLAB, dtype=p.dtype)
    out = p.reshape(ns, GROUPS_PER_SLAB, c, 1, n) * eye[None, :, None, :, None]
    return out.reshape(ns, GROUPS_PER_SLAB * c, GROUPS_PER_SLAB * n)


def _ssm_prep_kernel(lr_ref, li_ref, ldt_ref, bre_ref, bim_ref, cre_ref, cim_ref, w1_ref, ft_ref, al_ref):
    L = SSM_CHUNK
    lr = lr_ref[0]
    li = li_ref[0]
    dt = jnp.exp(ldt_ref[0])
    mag = jnp.exp(lr * dt)
    a_re = mag * jnp.cos(li * dt)
    a_im = mag * jnp.sin(li * dt)
    den = lr * lr + li * li
    z_re = ((a_re - 1.0) * lr + a_im * li) / den
    z_im = (a_im * lr - (a_re - 1.0) * li) / den
    br = bre_ref[0]
    bi = bim_ref[0]
    bb_re = z_re * br - z_im * bi
    bb_im = z_re * bi + z_im * br
    cr = cre_ref[0]
    ci = cim_ref[0]
    ft0_hi, ft0_lo = _split2(jnp.concatenate([cr, -ci], axis=1))

    def power(d):
        m = jnp.exp(lr * dt * d)
        return m * jnp.cos(li * dt * d), m * jnp.sin(li * dt * d)

    w1_ref[0, :, :L * LANES] = jnp.zeros((L * LANES, L * LANES), BF16)
    for d in range(L):
        pr, pi = power(float(d))
        xe = jnp.concatenate([bb_re * pr - bb_im * pi, bb_re * pi + bb_im * pr], axis=1)
        j = L - 1 - d
        w1_ref[0, j * LANES:(j + 1) * LANES, L * LANES:] = xe.astype(BF16)
        xe_hi, xe_lo = _split2(xe)
        m_d = (_dot_nt(xe_hi, ft0_hi) + _dot_nt(xe_hi, ft0_lo) + _dot_nt(xe_lo, ft0_hi)).astype(BF16)
        for jj in range(L - d):
            w1_ref[0, jj * LANES:(jj + 1) * LANES, (jj + d) * LANES:(jj + d + 1) * LANES] = m_d
        pr1, pi1 = power(float(d + 1))
        ft_ref[0, d * LANES:(d + 1) * LANES, :] = jnp.concatenate(
            [cr * pr1 - ci * pi1, -(cr * pi1 + ci * pr1)], axis=1).astype(BF16)
    prl, pil = power(float(L))
    al_ref[0] = jnp.concatenate([prl, pil], axis=1)


def ssm_prep(lam_re, lam_im, log_dt, b_re, b_im, c_re, c_im):
    g, n = lam_re.shape
    ns = g // GROUPS_PER_SLAB
    L = SSM_CHUNK
    rowvec = lambda a: a.reshape(ns, 1, SLAB_STATE)
    args = (rowvec(lam_re), rowvec(lam_im), rowvec(jnp.repeat(log_dt, n)),
            _blockdiag(b_re.transpose(0, 2, 1)), _blockdiag(b_im.transpose(0, 2, 1)),
            _blockdiag(c_re), _blockdiag(c_im))
    vec_spec = pl.BlockSpec((1, 1, SLAB_STATE), lambda s: (s, 0, 0))
    mat_spec = pl.BlockSpec((1, LANES, SLAB_STATE), lambda s: (s, 0, 0))
    return pl.pallas_call(
        _ssm_prep_kernel,
        grid=(ns,),
        in_specs=[vec_spec] * 3 + [mat_spec] * 4,
        out_specs=[pl.BlockSpec((1, L * LANES, L * LANES + 2 * SLAB_STATE), lambda s: (s, 0, 0)),
                   pl.BlockSpec((1, L * LANES, 2 * SLAB_STATE), lambda s: (s, 0, 0)),
                   pl.BlockSpec((1, 1, 2 * SLAB_STATE), lambda s: (s, 0, 0))],
        out_shape=[jax.ShapeDtypeStruct((ns, L * LANES, L * LANES + 2 * SLAB_STATE), BF16),
                   jax.ShapeDtypeStruct((ns, L * LANES, 2 * SLAB_STATE), BF16),
                   jax.ShapeDtypeStruct((ns, 1, 2 * SLAB_STATE), F32)],
        compiler_params=_params(("arbitrary",), 48),
        name="ssm_prep",
    )(*args)


def _ssm_kernel(u_ref, w1_ref, ft_ref, al_ref, d_ref, y_ref, uf_scr, e_scr, y_scr):
    L = SSM_CHUNK
    nch = uf_scr.shape[0]
    lc = L * LANES
    for j in range(L):
        uf_scr[:, j * LANES:(j + 1) * LANES] = u_ref[0, pl.ds(j, nch, stride=L), :].astype(BF16)
    uf = uf_scr[...]
    e_scr[...] = _dot(uf, w1_ref[0, :, lc:])

    a_re = al_ref[0, :, :SLAB_STATE]
    a_im = al_ref[0, :, SLAB_STATE:]
    h_re = jnp.zeros((1, SLAB_STATE), F32)
    h_im = jnp.zeros((1, SLAB_STATE), F32)
    for k in range(nch):
        e = e_scr[k:k + 1, :]
        e_scr[k:k + 1, :] = jnp.concatenate([h_re, h_im], axis=1)
        h_re, h_im = (a_re * h_re - a_im * h_im + e[:, :SLAB_STATE],
                      a_re * h_im + a_im * h_re + e[:, SLAB_STATE:])

    for c in range(lc // SSM_COLS):
        k_hi = (c + 1) * SSM_COLS
        cols = slice(c * SSM_COLS, k_hi)
        y_scr[:, cols] = _dot(uf[:, :k_hi], w1_ref[0, :k_hi, cols])
    y = y_scr[...] + lax.dot_general(e_scr[...].astype(BF16), ft_ref[0], _NT, preferred_element_type=F32)
    for i in range(L):
        yi = y[:, i * LANES:(i + 1) * LANES] + d_ref[0] * u_ref[0, pl.ds(i, nch, stride=L), :]
        y_ref[0, pl.ds(i, nch, stride=L), :] = jax.nn.gelu(yi)


def ssm_scan(u, w1, ft, al, d_skip):
    bsz, seq, c = u.shape
    ns = c // LANES
    L = SSM_CHUNK
    nch = seq // L
    lc = L * LANES
    return pl.pallas_call(
        _ssm_kernel,
        grid=(ns, bsz),
        in_specs=[pl.BlockSpec((1, seq, LANES), lambda s, b: (b, 0, s)),
                  pl.BlockSpec((1, lc, lc + 2 * SLAB_STATE), lambda s, b: (s, 0, 0)),
                  pl.BlockSpec((1, lc, 2 * SLAB_STATE), lambda s, b: (s, 0, 0)),
                  pl.BlockSpec((1, 1, 2 * SLAB_STATE), lambda s, b: (s, 0, 0)),
                  pl.BlockSpec((1, 1, LANES), lambda s, b: (s, 0, 0))],
        out_specs=pl.BlockSpec((1, seq, LANES), lambda s, b: (b, 0, s)),
        out_shape=jax.ShapeDtypeStruct((bsz, seq, c), F32),
        scratch_shapes=[pltpu.VMEM((nch, lc), BF16), pltpu.VMEM((nch, 2 * SLAB_STATE), F32),
                        pltpu.VMEM((nch, lc), F32)],
        compiler_params=_params(("arbitrary", "arbitrary"), 60),
        name="ssm_scan",
    )(u, w1, ft, al, d_skip.reshape(ns, 1, LANES))


def _glu_kernel(y_ref, w_ref, b_ref, g_ref, o_ref):
    y = y_ref[...]
    o = y * jax.nn.sigmoid(_dot(y.astype(BF16), w_ref[...]) + b_ref[...])
    ms = jnp.mean(o * o, axis=-1, keepdims=True)
    o_ref[...] = (o * lax.rsqrt(ms + RMS_EPS) * g_ref[...]).astype(BF16)


def glu_norm(y, w_glu, b_glu, g):
    t, c = y.shape
    tm = 512
    return pl.pallas_call(
        _glu_kernel,
        grid=(t // tm,),
        in_specs=[pl.BlockSpec((tm, c), lambda i: (i, 0)),
                  pl.BlockSpec((c, c), lambda i: (0, 0)),
                  pl.BlockSpec((1, c), lambda i: (0, 0)),
                  pl.BlockSpec((1, c), lambda i: (0, 0))],
        out_specs=pl.BlockSpec((tm, c), lambda i: (i, 0)),
        out_shape=jax.ShapeDtypeStruct((t, c), BF16),
        compiler_params=_params(("arbitrary",), 48),
        name="glu_norm",
    )(y, w_glu, b_glu.reshape(1, c), g.reshape(1, c))


def _outproj_kernel(attn_ref, ga_ref, ssm_ref, w_ref, x_ref, gate_ref, lng_ref, lnb_ref, sc2_ref, sh2_ref,
                    wrh_ref, wrl_ref, x1_ref, h2_ref, lg_ref, a_scr, *, alpha, n_att):
    j = pl.program_id(2)
    nj = pl.num_programs(2)
    tn = w_ref.shape[1]

    tm = x1_ref.shape[1]
    n_chunks = tm // ROW_CHUNK

    @pl.when(j == 0)
    def _():
        def norm(c, carry):
            rs = pl.ds(pl.multiple_of(c * ROW_CHUNK, ROW_CHUNK), ROW_CHUNK)
            a = attn_ref[0, rs, :]
            ms = jnp.mean(a * a, axis=-1, keepdims=True)
            a_scr[rs, :n_att] = (a * lax.rsqrt(ms + RMS_EPS) * ga_ref[...]).astype(BF16)
            return carry
        lax.fori_loop(0, n_chunks, norm, 0)
        a_scr[:, n_att:] = ssm_ref[0]

    mixed = _dot(a_scr[...], w_ref[...])
    col = pl.multiple_of(j * tn, tn)
    x1_ref[0, :, pl.ds(col, tn)] = alpha * x_ref[0] + (1.0 + gate_ref[0]) * mixed

    @pl.when(j == nj - 1)
    def _():
        def finish(c, carry):
            rs = pl.ds(pl.multiple_of(c * ROW_CHUNK, ROW_CHUNK), ROW_CHUNK)
            r = x1_ref[0, rs, :]
            mu = jnp.mean(r, axis=-1, keepdims=True)
            var = jnp.mean(jnp.square(r - mu), axis=-1, keepdims=True)
            x1 = (r - mu) * lax.rsqrt(var + LN_EPS) * lng_ref[...] + lnb_ref[...]
            x1_ref[0, rs, :] = x1
            h2 = x1 * (1.0 + sc2_ref[0]) + sh2_ref[0]
            hi = h2.astype(BF16)
            hi_f = hi.astype(F32)
            lo = (h2 - hi_f).astype(BF16)
            lg_ref[0, rs, :] = _dot(hi, wrh_ref[...]) + _dot(hi, wrl_ref[...]) + _dot(lo, wrh_ref[...])
            h2_ref[0, rs, :] = pack_bf16_pairs(hi_f)
            return carry
        lax.fori_loop(0, n_chunks, finish, 0)


def pack_bf16_pairs(x):
    n = x.shape[-1] // 2
    bits = lax.bitcast_convert_type(x, jnp.uint32)
    return bits[:, n:] | (bits[:, :n] >> 16)


def unpack_bf16_pairs(p):
    lo = lax.bitcast_convert_type(p << 16, F32).astype(BF16)
    hi = lax.bitcast_convert_type(p & jnp.uint32(0xFFFF0000), F32).astype(BF16)
    return lo, hi


def out_proj(attn, g_attn, ssm_n, w_out, x, gate1, ln_g, ln_b, scale2, shift2, wr_hi, wr_lo, alpha):
    bsz, seq, d = x.shape
    n_att = attn.shape[-1]
    n_ssm = ssm_n.shape[-1]
    k = n_att + n_ssm
    tm, tn = 512, 256
    row = lambda a: a.reshape(1, -1)
    full = lambda n: pl.BlockSpec((1, n), lambda b, i, j: (0, 0))
    return pl.pallas_call(
        functools.partial(_outproj_kernel, alpha=alpha, n_att=n_att),
        grid=(bsz, seq // tm, d // tn),
        in_specs=[pl.BlockSpec((1, tm, n_att), lambda b, i, j: (b, i, 0)),
                  full(n_att),
                  pl.BlockSpec((1, tm, n_ssm), lambda b, i, j: (b, i, 0)),
                  pl.BlockSpec((k, tn), lambda b, i, j: (0, j)),
                  pl.BlockSpec((1, tm, tn), lambda b, i, j: (b, i, j)),
                  pl.BlockSpec((1, 1, tn), lambda b, i, j: (b, 0, j)),
                  full(d), full(d),
                  pl.BlockSpec((1, 1, d), lambda b, i, j: (b, 0, 0)),
                  pl.BlockSpec((1, 1, d), lambda b, i, j: (b, 0, 0)),
                  pl.BlockSpec((d, LANES), lambda b, i, j: (0, 0)),
                  pl.BlockSpec((d, LANES), lambda b, i, j: (0, 0))],
        out_specs=[pl.BlockSpec((1, tm, d), lambda b, i, j: (b, i, 0)),
                   pl.BlockSpec((1, tm, d // 2), lambda b, i, j: (b, i, 0)),
                   pl.BlockSpec((1, tm, LANES), lambda b, i, j: (b, i, 0))],
        out_shape=[jax.ShapeDtypeStruct((bsz, seq, d), F32),
                   jax.ShapeDtypeStruct((bsz, seq, d // 2), jnp.uint32),
                   jax.ShapeDtypeStruct((bsz, seq, LANES), F32)],
        scratch_shapes=[pltpu.VMEM((tm, k), BF16)],
        compiler_params=_params(("arbitrary", "arbitrary", "arbitrary"), 56),
        name="out_proj",
    )(attn, row(g_attn), ssm_n, w_out, x, gate1, row(ln_g), row(ln_b), scale2, shift2, wr_hi, wr_lo)


def _moe_kernel(tok_ref, bexp_ref, nsub_ref, nact_ref, h_hbm, wg_ref, wu_ref, wd_ref, y_ref, xbuf, sem):
    i = pl.program_id(0)
    nact = nact_ref[0]
    rows = xbuf.shape[1]

    def start_gather(blk, slot):
        n_trips = nsub_ref[blk] * (MOE_SUB // GATHER_UNROLL)

        def body(t, carry):
            for k in range(GATHER_UNROLL):
                r = t * GATHER_UNROLL + k
                tok = tok_ref[blk * rows + r]
                pltpu.make_async_copy(h_hbm.at[pl.ds(tok, 1)], xbuf.at[slot, pl.ds(r, 1)], sem.at[slot]).start()
            return carry
        lax.fori_loop(0, n_trips, body, 0)

    def run_block(slot, n):
        pltpu.make_async_copy(h_hbm.at[pl.ds(0, n)], xbuf.at[slot, pl.ds(0, n)], sem.at[slot]).wait()
        x_lo, x_hi = unpack_bf16_pairs(xbuf[slot, :n])
        half = x_lo.shape[1]
        g = _dot(x_lo, wg_ref[0, :half, :]) + _dot(x_hi, wg_ref[0, half:, :])
        u = _dot(x_lo, wu_ref[0, :half, :]) + _dot(x_hi, wu_ref[0, half:, :])
        act = (jax.nn.silu(g) * u).astype(BF16)
        y_ref[:n, :] = _dot(act, wd_ref[0])
        if n < rows:
            y_ref[n:, :] = jnp.zeros((rows - n, y_ref.shape[1]), F32)

    @pl.when(jnp.logical_and(i == 0, nact > 0))
    def _():
        start_gather(0, 0)

    @pl.when(i + 1 < nact)
    def _():
        start_gather(i + 1, (i + 1) % 2)

    for n_sub in range(1, rows // MOE_SUB + 1):
        @pl.when(jnp.logical_and(i < nact, nsub_ref[i] == n_sub))
        def _():
            run_block(i % 2, n_sub * MOE_SUB)

    @pl.when(i >= nact)
    def _():
        y_ref[...] = jnp.zeros_like(y_ref)


def moe_experts(h2, tok_buf, block_expert, short_blocks, n_active, w_gate, w_up, w_down):
    t, dp = h2.shape
    d = 2 * dp
    n_rows = tok_buf.shape[0]
    rows = MOE_ROWS
    n_blocks = n_rows // rows
    de = w_gate.shape[-1]

    def wmap(i, tok, bexp, short, nact):
        return (bexp[jnp.minimum(i, jnp.maximum(nact[0] - 1, 0))], 0, 0)

    grid_spec = pltpu.PrefetchScalarGridSpec(
        num_scalar_prefetch=4,
        grid=(n_blocks,),
        in_specs=[pl.BlockSpec(memory_space=pl.ANY),
                  pl.BlockSpec((1, d, de), wmap),
                  pl.BlockSpec((1, d, de), wmap),
                  pl.BlockSpec((1, de, d), wmap)],
        out_specs=pl.BlockSpec((rows, d), lambda i, *_: (i, 0)),
        scratch_shapes=[pltpu.VMEM((2, rows, dp), jnp.uint32), pltpu.SemaphoreType.DMA((2,))],
    )
    return pl.pallas_call(
        _moe_kernel,
        grid_spec=grid_spec,
        out_shape=jax.ShapeDtypeStruct((n_rows, d), F32),
        compiler_params=_params(("arbitrary",), 56),
        name="moe_experts",
    )(tok_buf, block_expert, short_blocks, n_active, h2, w_gate, w_up, w_down)


def _combine_kernel(dest_ref, ys_hbm, rt_ref, x1_ref, gate_ref, lng_ref, lnb_ref, o_ref, ybuf, sem, *, alpha):
    i = pl.program_id(0)
    n = pl.num_programs(0)
    tm = x1_ref.shape[0]
    n_copies = TOP_K * tm

    def gather(blk, slot, start):
        if not start:
            pltpu.make_async_copy(ys_hbm.at[pl.ds(0, n_copies)], ybuf.at[slot], sem.at[slot]).wait()
            return

        def body(r, carry):
            row = dest_ref[blk * n_copies + r]
            pltpu.make_async_copy(ys_hbm.at[pl.ds(row, 1)], ybuf.at[slot, pl.ds(r, 1)], sem.at[slot]).start()
            return carry
        lax.fori_loop(0, n_copies, body, 0, unroll=GATHER_UNROLL)

    @pl.when(i == 0)
    def _():
        gather(0, 0, True)

    @pl.when(i + 1 < n)
    def _():
        gather(i + 1, (i + 1) % 2, True)

    slot = i % 2
    gather(i, slot, False)
    moe = rt_ref[:, ROUTE_W:ROUTE_W + 1] * ybuf[slot, :tm, :]
    for kk in range(1, TOP_K):
        moe = moe + rt_ref[:, ROUTE_W + kk:ROUTE_W + kk + 1] * ybuf[slot, kk * tm:(kk + 1) * tm, :]
    r = alpha * x1_ref[...] + (1.0 + gate_ref[0]) * moe
    mu = jnp.mean(r, axis=-1, keepdims=True)
    var = jnp.mean(jnp.square(r - mu), axis=-1, keepdims=True)
    o_ref[...] = (r - mu) * lax.rsqrt(var + LN_EPS) * lng_ref[...] + lnb_ref[...]


def moe_combine(ys, dest, table, x1, gate2, ln_g, ln_b, alpha, seq):
    t, d = x1.shape
    tm = 128
    tiles_per_seq = seq // tm
    grid_spec = pltpu.PrefetchScalarGridSpec(
        num_scalar_prefetch=1,
        grid=(t // tm,),
        in_specs=[pl.BlockSpec(memory_space=pl.ANY),
                  pl.BlockSpec((tm, LANES), lambda i, *_: (i, 0)),
                  pl.BlockSpec((tm, d), lambda i, *_: (i, 0)),
                  pl.BlockSpec((1, 1, d), lambda i, *_: (i // tiles_per_seq, 0, 0)),
                  pl.BlockSpec((1, d), lambda i, *_: (0, 0)),
                  pl.BlockSpec((1, d), lambda i, *_: (0, 0))],
        out_specs=pl.BlockSpec((tm, d), lambda i, *_: (i, 0)),
        scratch_shapes=[pltpu.VMEM((2, TOP_K * tm, d), F32), pltpu.SemaphoreType.DMA((2,))],
    )
    return pl.pallas_call(
        functools.partial(_combine_kernel, alpha=alpha),
        grid_spec=grid_spec,
        out_shape=jax.ShapeDtypeStruct((t, d), F32),
        compiler_params=_params(("arbitrary",), 32),
        name="moe_combine",
    )(dest, ys, table, x1, gate2, ln_g.reshape(1, d), ln_b.reshape(1, d))


def _route_kernel(lg_ref, b_ref, o_ref, cnt_ref, carry):
    i = pl.program_id(0)

    @pl.when(i == 0)
    def _():
        carry[...] = jnp.zeros_like(carry)

    tm = lg_ref.shape[0]
    x = lg_ref[...] + b_ref[...]
    lane = lax.broadcasted_iota(jnp.int32, (tm, LANES), 1)
    ninf = -jnp.inf

    def top(v):
        vmax = jnp.max(v, axis=1, keepdims=True)
        return vmax, jnp.min(jnp.where(v == vmax, lane, LANES), axis=1, keepdims=True)

    gmask = lane < N_EXPERT_GROUPS
    gmax, g_sel = top(jnp.where(gmask, x, ninf))
    p_group = 1.0 / jnp.sum(jnp.where(gmask, jnp.exp(x - gmax), 0.0), axis=1, keepdims=True)

    lo = N_EXPERT_GROUPS + g_sel * EXPERTS_PER_GROUP
    cur = jnp.where(jnp.logical_and(lane >= lo, lane < lo + EXPERTS_PER_GROUP), x, ninf)
    vals, idxs = [], []
    for _ in range(TOP_K):
        v, ix = top(cur)
        vals.append(v)
        idxs.append(ix)
        cur = jnp.where(lane == ix, ninf, cur)
    exps = [jnp.exp(v - vals[0]) for v in vals]
    den = functools.reduce(lambda a, c: a + c, exps)

    member = functools.reduce(jnp.logical_or, [lane == ix for ix in idxs])
    mf = jnp.where(member, 1.0, 0.0)
    r_i = lax.broadcasted_iota(jnp.int32, (tm, tm), 0)
    c_i = lax.broadcasted_iota(jnp.int32, (tm, tm), 1)
    before = _dot((c_i < r_i).astype(BF16), mf.astype(BF16)) + carry[...]
    carry[...] = carry[...] + jnp.sum(mf, axis=0, keepdims=True)
    cnt_ref[...] = carry[...]

    out = jnp.zeros((tm, LANES), F32)
    for k in range(TOP_K):
        rank = jnp.sum(jnp.where(lane == idxs[k], before, 0.0), axis=1, keepdims=True)
        out = jnp.where(lane == ROUTE_EID + k, (idxs[k] - N_EXPERT_GROUPS).astype(F32), out)
        out = jnp.where(lane == ROUTE_RANK + k, rank, out)
        out = jnp.where(lane == ROUTE_W + k, p_group * (exps[k] / den), out)
    o_ref[...] = out


def route(logits, b_rg, b_re, n_rows, tm_combine):
    n_tok = logits.shape[0]
    n_experts = N_EXPERT_GROUPS * EXPERTS_PER_GROUP
    bias = jnp.zeros((1, LANES), F32).at[0, :N_EXPERT_GROUPS].set(b_rg)
    bias = bias.at[0, N_EXPERT_GROUPS:N_EXPERT_GROUPS + n_experts].set(b_re)
    tm = 512
    table, cnt = pl.pallas_call(
        _route_kernel,
        grid=(n_tok // tm,),
        in_specs=[pl.BlockSpec((tm, LANES), lambda i: (i, 0)),
                  pl.BlockSpec((1, LANES), lambda i: (0, 0))],
        out_specs=[pl.BlockSpec((tm, LANES), lambda i: (i, 0)),
                   pl.BlockSpec((1, LANES), lambda i: (0, 0))],
        out_shape=[jax.ShapeDtypeStruct((n_tok, LANES), F32), jax.ShapeDtypeStruct((1, LANES), F32)],
        scratch_shapes=[pltpu.VMEM((1, LANES), F32)],
        compiler_params=_params(("arbitrary",), 16),
        name="route",
    )(logits, bias)

    eid = table[:, ROUTE_EID:ROUTE_EID + TOP_K].astype(jnp.int32)
    rank = table[:, ROUTE_RANK:ROUTE_RANK + TOP_K].astype(jnp.int32)
    counts = cnt[0, N_EXPERT_GROUPS:N_EXPERT_GROUPS + n_experts].astype(jnp.int32)
    padded = ((counts + MOE_ROWS - 1) // MOE_ROWS) * MOE_ROWS
    pends = jnp.cumsum(padded)
    pstarts = pends - padded
    onehot = eid[:, :, None] == jnp.arange(n_experts, dtype=jnp.int32)
    dest = jnp.sum(jnp.where(onehot, pstarts, 0), axis=-1) + rank
    tok = jnp.broadcast_to(jnp.arange(n_tok, dtype=jnp.int32)[:, None], (n_tok, TOP_K))
    tok_buf = jnp.zeros((n_rows,), jnp.int32).at[dest.reshape(-1)].set(tok.reshape(-1), unique_indices=True)
    n_blocks = n_rows // MOE_ROWS
    block_expert = jnp.clip(
        jnp.searchsorted(pends, jnp.arange(n_blocks, dtype=jnp.int32) * MOE_ROWS, side='right'),
        0, n_experts - 1).astype(jnp.int32)
    rows_left = (pstarts + counts)[block_expert] - jnp.arange(n_blocks, dtype=jnp.int32) * MOE_ROWS
    short_blocks = jnp.clip(-(-rows_left // MOE_SUB), 1, MOE_ROWS // MOE_SUB).astype(jnp.int32)
    n_active = (pends[-1] // MOE_ROWS).astype(jnp.int32).reshape(1)
    dest_tiles = dest.reshape(n_tok // tm_combine, tm_combine, TOP_K).transpose(0, 2, 1).reshape(-1)
    return table, tok_buf, block_expert, short_blocks, n_active, dest_tiles.astype(jnp.int32)


def _layer(x, c, w_ada, b_ada, w_in, b_forget, lam_re, lam_im, log_dt, b_re, b_im, c_re, c_im, d_skip,
           w_glu, b_glu, g_attn, g_ssm, w_out, ln1_g, ln1_b, w_rg, b_rg, w_re, b_re_r,
           w_gate, w_up, w_down, ln2_g, ln2_b, alpha):
    bsz, seq, d = x.shape
    n_tok = bsz * seq
    n_heads = b_forget.shape[0]
    d_att = n_heads * HEAD_DIM
    d_ssm = d_skip.shape[0] * d_skip.shape[1]

    mod = ada_mod(c, w_ada, b_ada).reshape(bsz, 1, -1)
    shift1, scale1, gate1, shift2, scale2, gate2 = jnp.split(mod, 6, axis=-1)

    n_qkv = 3 * d_att
    w_main = jnp.concatenate([w_in[:, :n_qkv], w_in[:, n_qkv + n_heads:]], axis=1).astype(BF16)
    w_f = jnp.zeros((d, LANES), BF16).at[:, :n_heads].set(w_in[:, n_qkv:n_qkv + n_heads].astype(BF16))
    qkv, u, f = in_proj(x, scale1, shift1, w_main, w_f, n_qkv)

    q_aug, k_aug = forget_cum(f, b_forget)
    attn, (w_glu_b, w_out_b, w_gate_b, w_up_b, w_down_b) = attention(
        qkv, q_aug, k_aug, n_heads, [w_glu, w_out, w_gate, w_up, w_down])

    w1, ft, al = ssm_prep(lam_re, lam_im, log_dt, b_re, b_im, c_re, c_im)
    y = ssm_scan(u, w1, ft, al, d_skip)
    ssm_n = glu_norm(y.reshape(n_tok, d_ssm), w_glu_b, b_glu, g_ssm).reshape(bsz, seq, d_ssm)

    n_experts = N_EXPERT_GROUPS * EXPERTS_PER_GROUP
    w_r = jnp.zeros((d, LANES), F32).at[:, :N_EXPERT_GROUPS].set(w_rg)
    w_r = w_r.at[:, N_EXPERT_GROUPS:N_EXPERT_GROUPS + n_experts].set(w_re)
    wr_hi = w_r.astype(BF16)
    wr_lo = (w_r - wr_hi.astype(F32)).astype(BF16)
    x1, h2, logits = out_proj(attn, g_attn, ssm_n, w_out_b, x, gate1, ln1_g, ln1_b,
                              scale2, shift2, wr_hi, wr_lo, alpha)

    n_assign = n_tok * TOP_K
    n_blocks = -(-(n_assign + n_experts * (MOE_ROWS - 1)) // MOE_ROWS)
    n_rows = n_blocks * MOE_ROWS
    table, tok_buf, block_expert, short_blocks, n_active, dest = route(
        logits.reshape(n_tok, LANES), b_rg, b_re_r, n_rows, 128)
    ys = moe_experts(h2.reshape(n_tok, d // 2), tok_buf, block_expert, short_blocks, n_active,
                     w_gate_b, w_up_b, w_down_b)
    out = moe_combine(ys, dest, table, x1.reshape(n_tok, d), gate2, ln2_g, ln2_b, alpha, seq)
    return out.reshape(bsz, seq, d)


def kernel(x, c, w_ada, b_ada, w_in, b_forget, ssm_lambda_re, ssm_lambda_im, ssm_log_dt, ssm_b_re, ssm_b_im,
           ssm_c_re, ssm_c_im, ssm_d, w_glu, b_glu, g_attn, g_ssm, w_out, ln1_g, ln1_b, w_router_group,
           b_router_group, w_router_expert, b_router_expert, w_gate, w_up, w_down, ln2_g, ln2_b):
    depth = w_ada.shape[0]
    alpha = (2.0 * depth) ** 0.25
    for l in range(depth):
        x = _layer(x, c, w_ada[l], b_ada[l], w_in[l], b_forget[l], ssm_lambda_re[l], ssm_lambda_im[l],
                   ssm_log_dt[l], ssm_b_re[l], ssm_b_im[l], ssm_c_re[l], ssm_c_im[l], ssm_d[l],
                   w_glu[l], b_glu[l], g_attn[l], g_ssm[l], w_out[l], ln1_g[l], ln1_b[l],
                   w_router_group[l], b_router_group[l], w_router_expert[l], b_router_expert[l],
                   w_gate[l], w_up[l], w_down[l], ln2_g[l], ln2_b[l], alpha)
    return x
```

```python
import functools
import math

import jax
import jax.numpy as jnp
from jax import lax
from jax.experimental import pallas as pl
from jax.experimental.pallas import tpu as pltpu

F32 = jnp.float32
BF16 = jnp.bfloat16

LANES = 128
HEAD_DIM = 128
SSM_GROUP = 16
SSM_STATE = 64
GROUPS_PER_SLAB = LANES // SSM_GROUP
SLAB_STATE = GROUPS_PER_SLAB * SSM_STATE
SSM_CHUNK = 16
SSM_COLS = 512
N_EXPERT_GROUPS = 8
EXPERTS_PER_GROUP = 8
TOP_K = 2
MOE_ROWS = 256
MOE_SUB = 64
GATHER_UNROLL = 8
ROW_CHUNK = 128
ROUTE_EID, ROUTE_RANK, ROUTE_W = 0, TOP_K, 2 * TOP_K
LN_EPS = 1e-5
RMS_EPS = 1e-6
NEG_BIG = -1e30
LOG2E = math.log2(math.e)
MIB = 1024 * 1024

_NT = (((1,), (1,)), ((), ()))


def _params(semantics, vmem_mib):
    return pltpu.CompilerParams(dimension_semantics=semantics, vmem_limit_bytes=vmem_mib * MIB)


def _dot(a, b):
    return jnp.dot(a, b, preferred_element_type=F32)


def _dot_nt(a, b):
    return lax.dot_general(a, b, _NT, preferred_element_type=F32)


def _split2(x):
    hi = x.astype(BF16)
    return hi, (x - hi.astype(F32)).astype(BF16)


def _ada_kernel(c_ref, w_ref, b_ref, o_ref):
    s = jax.nn.silu(c_ref[...]).astype(BF16)
    o_ref[...] = _dot(s, w_ref[...].astype(BF16)) + b_ref[...]


def ada_mod(c, w_ada, b_ada):
    bsz, d = c.shape
    n = w_ada.shape[1]
    rows = 8
    assert bsz <= rows
    cp = jnp.zeros((rows, d), F32).at[:bsz].set(c)
    tn = 512
    out = pl.pallas_call(
        _ada_kernel,
        grid=(n // tn,),
        in_specs=[pl.BlockSpec((rows, d), lambda j: (0, 0)),
                  pl.BlockSpec((d, tn), lambda j: (0, j)),
                  pl.BlockSpec((1, tn), lambda j: (0, j))],
        out_specs=pl.BlockSpec((rows, tn), lambda j: (0, j)),
        out_shape=jax.ShapeDtypeStruct((rows, n), F32),
        compiler_params=_params(("arbitrary",), 40),
        name="ada_mod",
    )(cp, w_ada, b_ada.reshape(1, n))
    return out[:bsz]


def _inproj_kernel(x_ref, sc_ref, sh_ref, w_ref, wf_ref, qkv_ref, u_ref, f_ref, h_scr, *,
                   n_q_tiles, n_qkv_tiles, q_scale):
    j = pl.program_id(2)

    @pl.when(j == 0)
    def _():
        hb = (x_ref[0] * (1.0 + sc_ref[0]) + sh_ref[0]).astype(BF16)
        h_scr[...] = hb
        f_ref[0] = _dot(hb, wf_ref[...])

    acc = _dot(h_scr[...], w_ref[...])

    @pl.when(j < n_q_tiles)
    def _():
        qkv_ref[0] = (acc * q_scale).astype(BF16)

    @pl.when(jnp.logical_and(j >= n_q_tiles, j < n_qkv_tiles))
    def _():
        qkv_ref[0] = acc.astype(BF16)

    @pl.when(j >= n_qkv_tiles)
    def _():
        u_ref[0] = acc


def in_proj(x, scale, shift, w_main, w_f, n_qkv):
    bsz, seq, d = x.shape
    n_all = w_main.shape[1]
    n_u = n_all - n_qkv
    tm, tn = 512, 1024
    nq = n_qkv // tn
    grid = (bsz, seq // tm, n_all // tn)
    return pl.pallas_call(
        functools.partial(_inproj_kernel, n_q_tiles=n_qkv // 3 // tn, n_qkv_tiles=nq,
                          q_scale=HEAD_DIM ** -0.5 * LOG2E),
        grid=grid,
        in_specs=[pl.BlockSpec((1, tm, d), lambda b, i, j: (b, i, 0)),
                  pl.BlockSpec((1, 1, d), lambda b, i, j: (b, 0, 0)),
                  pl.BlockSpec((1, 1, d), lambda b, i, j: (b, 0, 0)),
                  pl.BlockSpec((d, tn), lambda b, i, j: (0, j)),
                  pl.BlockSpec((d, LANES), lambda b, i, j: (0, 0))],
        out_specs=[pl.BlockSpec((1, tm, tn), lambda b, i, j: (b, i, jnp.minimum(j, nq - 1))),
                   pl.BlockSpec((1, tm, tn), lambda b, i, j: (b, i, jnp.maximum(j - nq, 0))),
                   pl.BlockSpec((1, tm, LANES), lambda b, i, j: (b, i, 0))],
        out_shape=[jax.ShapeDtypeStruct((bsz, seq, n_qkv), BF16),
                   jax.ShapeDtypeStruct((bsz, seq, n_u), F32),
                   jax.ShapeDtypeStruct((bsz, seq, LANES), F32)],
        scratch_shapes=[pltpu.VMEM((tm, d), BF16)],
        compiler_params=_params(("arbitrary", "arbitrary", "arbitrary"), 52),
        name="in_proj",
    )(x, scale, shift, w_main, w_f)


def _split3(x):
    p1 = x.astype(BF16)
    r1 = x - p1.astype(F32)
    p2 = r1.astype(BF16)
    p3 = (r1 - p2.astype(F32)).astype(BF16)
    return p1, p2, p3


N_PIECES = 3


def _cum_kernel(f_ref, b_ref, qa_ref, ka_ref, carry, *, n_heads):
    i = pl.program_id(1)

    @pl.when(i == 0)
    def _():
        carry[...] = jnp.zeros_like(carry)

    tc = f_ref.shape[1]
    lf = jax.nn.log_sigmoid(f_ref[0] + b_ref[...])
    row = lax.broadcasted_iota(jnp.int32, (tc, tc), 0)
    col = lax.broadcasted_iota(jnp.int32, (tc, tc), 1)
    tri = (col <= row).astype(BF16)
    p1, p2, p3 = _split3(lf)
    cs = _dot(tri, p1) + _dot(tri, p2) + _dot(tri, p3) + carry[...]
    carry[...] = cs[tc - 1:tc, :]

    pieces = jnp.concatenate(_split3(cs * LOG2E), axis=1)
    r = lax.broadcasted_iota(jnp.int32, (N_PIECES * LANES, LANES), 0)
    c = lax.broadcasted_iota(jnp.int32, (N_PIECES * LANES, LANES), 1)
    lane = lax.broadcasted_iota(jnp.int32, (tc, LANES), 1)
    ones_q = jnp.where(jnp.logical_and(lane >= N_PIECES, lane < 2 * N_PIECES), 1.0, 0.0)
    ones_k = jnp.where(lane < N_PIECES, 1.0, 0.0)
    for h in range(n_heads):
        sel_q = (r == c * LANES + h).astype(BF16)
        sel_k = (r == (c - N_PIECES) * LANES + h).astype(BF16)
        picked = _dot(pieces, jnp.concatenate([sel_q, sel_k], axis=1))
        qa_ref[0, h] = (picked[:, :LANES] + ones_q).astype(BF16)
        ka_ref[0, h] = (ones_k - picked[:, LANES:]).astype(BF16)


def forget_cum(f, b_forget):
    bsz, seq, _ = f.shape
    n_heads = b_forget.shape[0]
    tc = 512
    bpad = jnp.zeros((1, LANES), F32).at[0, :n_heads].set(b_forget)
    out_spec = pl.BlockSpec((1, n_heads, tc, LANES), lambda b, i: (b, 0, i, 0))
    out_shape = jax.ShapeDtypeStruct((bsz, n_heads, seq, LANES), BF16)
    return pl.pallas_call(
        functools.partial(_cum_kernel, n_heads=n_heads),
        grid=(bsz, seq // tc),
        in_specs=[pl.BlockSpec((1, tc, LANES), lambda b, i: (b, i, 0)),
                  pl.BlockSpec((1, LANES), lambda b, i: (0, 0))],
        out_specs=[out_spec, out_spec],
        out_shape=[out_shape, out_shape],
        scratch_shapes=[pltpu.VMEM((1, LANES), F32)],
        compiler_params=_params(("arbitrary", "arbitrary"), 24),
        name="forget_cum",
    )(f, bpad)


ATT_QROWS = 1024
ATT_KEYS = 512


def _attn_kernel(*refs, n_cast):
    q_ref, qa_ref, k_ref, ka_ref, v_ref = refs[:5]
    src_refs = refs[5:5 + n_cast]
    o_ref = refs[5 + n_cast]
    dst_refs = refs[6 + n_cast:6 + 2 * n_cast]
    m_scr, acc_scr, s_scr = refs[6 + 2 * n_cast:]
    for src, dst in zip(src_refs, dst_refs):
        dst[...] = src[...].astype(BF16)

    tq = q_ref.shape[1]
    tk = ATT_KEYS
    n_groups = tq // tk
    qi = pl.program_id(2)
    q = jnp.concatenate([q_ref[0], qa_ref[0, 0]], axis=1)
    ones = jnp.ones((tk, HEAD_DIM), BF16)

    m_scr[...] = jnp.full_like(m_scr, NEG_BIG)
    acc_scr[...] = jnp.zeros_like(acc_scr)

    def scores(kb, slot, first_group=0):
        k0 = pl.multiple_of(kb * tk, tk)
        kt = jnp.concatenate([k_ref[0, pl.ds(k0, tk), :], ka_ref[0, 0, pl.ds(k0, tk), :]], axis=1)
        r0 = first_group * tk
        s_scr[slot, r0:, :] = lax.dot_general(q[r0:], kt, _NT, preferred_element_type=F32)

    def update(kb, slot, diag_group=None):
        k0 = pl.multiple_of(kb * tk, tk)
        vt = jnp.concatenate([v_ref[0, pl.ds(k0, tk), :], ones], axis=1)
        for g in range(n_groups):
            if diag_group is not None and g < diag_group:
                continue
            rs = slice(g * tk, (g + 1) * tk)
            s = s_scr[slot, rs, :]
            if g == diag_group:
                qpos = lax.broadcasted_iota(jnp.int32, (tk, tk), 0)
                kpos = lax.broadcasted_iota(jnp.int32, (tk, tk), 1)
                s = jnp.where(kpos <= qpos, s, NEG_BIG)
            m_prev = m_scr[rs]
            m_new = jnp.maximum(m_prev, jnp.max(s, axis=1, keepdims=True))
            alpha = jnp.exp2(m_prev - m_new)
            p = jnp.exp2(s - m_new)
            acc_scr[rs] = alpha * acc_scr[rs] + _dot(p.astype(BF16), vt)
            m_scr[rs] = m_new

    def body(i, carry):
        kb = n_groups * i
        for g in range(n_groups):
            scores(kb + g + 1, (g + 1) % 2)
            update(kb + g, g % 2)
        return carry

    assert n_groups % 2 == 0
    scores(0, 0)
    lax.fori_loop(0, qi, body, 0)
    kb = n_groups * qi
    for g in range(n_groups):
        if g + 1 < n_groups:
            scores(kb + g + 1, (g + 1) % 2, first_group=g + 1)
        update(kb + g, g % 2, diag_group=g)

    o_ref[0] = acc_scr[:, :HEAD_DIM] / acc_scr[:, HEAD_DIM:]


BF16_ROWS = 16


def _cast_chunks(w, n_steps):
    cols = w.shape[-1]
    total_rows = w.size // cols
    n_chunks = n_steps
    while total_rows % (n_chunks * BF16_ROWS):
        n_chunks //= 2
    return w.reshape(n_chunks, total_rows // n_chunks, cols)


def attention(qkv, q_aug, k_aug, n_heads, cast_weights):
    bsz, seq, _ = qkv.shape
    t = min(ATT_QROWS, seq)
    nq = seq // t
    n_steps = bsz * n_heads * nq
    srcs = [_cast_chunks(w, n_steps) for w in cast_weights]

    def chunk_spec(a):
        per = n_steps // a.shape[0]
        return pl.BlockSpec((1,) + a.shape[1:], lambda b, h, i: (((b * n_heads + h) * nq + i) // per, 0, 0))

    outs = pl.pallas_call(
        functools.partial(_attn_kernel, n_cast=len(srcs)),
        grid=(bsz, n_heads, nq),
        in_specs=[pl.BlockSpec((1, t, HEAD_DIM), lambda b, h, i: (b, i, h)),
                  pl.BlockSpec((1, 1, t, LANES), lambda b, h, i: (b, h, i, 0)),
                  pl.BlockSpec((1, seq, HEAD_DIM), lambda b, h, i: (b, 0, n_heads + h)),
                  pl.BlockSpec((1, 1, seq, LANES), lambda b, h, i: (b, h, 0, 0)),
                  pl.BlockSpec((1, seq, HEAD_DIM), lambda b, h, i: (b, 0, 2 * n_heads + h))]
                 + [chunk_spec(a) for a in srcs],
        out_specs=[pl.BlockSpec((1, t, HEAD_DIM), lambda b, h, i: (b, i, h))] + [chunk_spec(a) for a in srcs],
        out_shape=[jax.ShapeDtypeStruct((bsz, seq, n_heads * HEAD_DIM), F32)]
                  + [jax.ShapeDtypeStruct(a.shape, BF16) for a in srcs],
        scratch_shapes=[pltpu.VMEM((t, 1), F32), pltpu.VMEM((t, 2 * HEAD_DIM), F32),
                        pltpu.VMEM((2, t, ATT_KEYS), F32)],
        compiler_params=_params(("arbitrary", "arbitrary", "arbitrary"), 56),
        name="attention",
    )(qkv, q_aug, qkv, k_aug, qkv, *srcs)
    return outs[0], [o.reshape(w.shape) for o, w in zip(outs[1:], cast_weights)]


def _blockdiag(p):
    g, c, n = p.shape
    ns = g // GROUPS_PER_SLAB
    eye = jnp.eye(GROUPS_PER_SLAB, dtype=p.dtype)
    out = p.reshape(ns, GROUPS_PER_SLAB, c, 1, n) * eye[None, :, None, :, None]
    return out.reshape(ns, GROUPS_PER_SLAB * c, GROUPS_PER_SLAB * n)


def _ssm_prep_kernel(lr_ref, li_ref, ldt_ref, bre_ref, bim_ref, cre_ref, cim_ref, w1_ref, ft_ref, al_ref):
    L = SSM_CHUNK
    lr = lr_ref[0]
    li = li_ref[0]
    dt = jnp.exp(ldt_ref[0])
    mag = jnp.exp(lr * dt)
    a_re = mag * jnp.cos(li * dt)
    a_im = mag * jnp.sin(li * dt)
    den = lr * lr + li * li
    z_re = ((a_re - 1.0) * lr + a_im * li) / den
    z_im = (a_im * lr - (a_re - 1.0) * li) / den
    br = bre_ref[0]
    bi = bim_ref[0]
    bb_re = z_re * br - z_im * bi
    bb_im = z_re * bi + z_im * br
    cr = cre_ref[0]
    ci = cim_ref[0]
    ft0_hi, ft0_lo = _split2(jnp.concatenate([cr, -ci], axis=1))

    def power(d):
        m = jnp.exp(lr * dt * d)
        return m * jnp.cos(li * dt * d), m * jnp.sin(li * dt * d)

    w1_ref[0, :, :L * LANES] = jnp.zeros((L * LANES, L * LANES), BF16)
    for d in range(L):
        pr, pi = power(float(d))
        xe = jnp.concatenate([bb_re * pr - bb_im * pi, bb_re * pi + bb_im * pr], axis=1)
        j = L - 1 - d
        w1_ref[0, j * LANES:(j + 1) * LANES, L * LANES:] = xe.astype(BF16)
        xe_hi, xe_lo = _split2(xe)
        m_d = (_dot_nt(xe_hi, ft0_hi) + _dot_nt(xe_hi, ft0_lo) + _dot_nt(xe_lo, ft0_hi)).astype(BF16)
        for jj in range(L - d):
            w1_ref[0, jj * LANES:(jj + 1) * LANES, (jj + d) * LANES:(jj + d + 1) * LANES] = m_d
        pr1, pi1 = power(float(d + 1))
        ft_ref[0, d * LANES:(d + 1) * LANES, :] = jnp.concatenate(
            [cr * pr1 - ci * pi1, -(cr * pi1 + ci * pr1)], axis=1).astype(BF16)
    prl, pil = power(float(L))
    al_ref[0] = jnp.concatenate([prl, pil], axis=1)


def ssm_prep(lam_re, lam_im, log_dt, b_re, b_im, c_re, c_im):
    g, n = lam_re.shape
    ns = g // GROUPS_PER_SLAB
    L = SSM_CHUNK
    rowvec = lambda a: a.reshape(ns, 1, SLAB_STATE)
    args = (rowvec(lam_re), rowvec(lam_im), rowvec(jnp.repeat(log_dt, n)),
            _blockdiag(b_re.transpose(0, 2, 1)), _blockdiag(b_im.transpose(0, 2, 1)),
            _blockdiag(c_re), _blockdiag(c_im))
    vec_spec = pl.BlockSpec((1, 1, SLAB_STATE), lambda s: (s, 0, 0))
    mat_spec = pl.BlockSpec((1, LANES, SLAB_STATE), lambda s: (s, 0, 0))
    return pl.pallas_call(
        _ssm_prep_kernel,
        grid=(ns,),
        in_specs=[vec_spec] * 3 + [mat_spec] * 4,
        out_specs=[pl.BlockSpec((1, L * LANES, L * LANES + 2 * SLAB_STATE), lambda s: (s, 0, 0)),
                   pl.BlockSpec((1, L * LANES, 2 * SLAB_STATE), lambda s: (s, 0, 0)),
                   pl.BlockSpec((1, 1, 2 * SLAB_STATE), lambda s: (s, 0, 0))],
        out_shape=[jax.ShapeDtypeStruct((ns, L * LANES, L * LANES + 2 * SLAB_STATE), BF16),
                   jax.ShapeDtypeStruct((ns, L * LANES, 2 * SLAB_STATE), BF16),
                   jax.ShapeDtypeStruct((ns, 1, 2 * SLAB_STATE), F32)],
        compiler_params=_params(("arbitrary",), 48),
        name="ssm_prep",
    )(*args)


def _ssm_kernel(u_ref, w1_ref, ft_ref, al_ref, d_ref, y_ref, uf_scr, e_scr, y_scr):
    L = SSM_CHUNK
    nch = uf_scr.shape[0]
    lc = L * LANES
    for j in range(L):
        uf_scr[:, j * LANES:(j + 1) * LANES] = u_ref[0, pl.ds(j, nch, stride=L), :].astype(BF16)
    uf = uf_scr[...]
    e_scr[...] = _dot(uf, w1_ref[0, :, lc:])

    a_re = al_ref[0, :, :SLAB_STATE]
    a_im = al_ref[0, :, SLAB_STATE:]
    h_re = jnp.zeros((1, SLAB_STATE), F32)
    h_im = jnp.zeros((1, SLAB_STATE), F32)
    for k in range(nch):
        e = e_scr[k:k + 1, :]
        e_scr[k:k + 1, :] = jnp.concatenate([h_re, h_im], axis=1)
        h_re, h_im = (a_re * h_re - a_im * h_im + e[:, :SLAB_STATE],
                      a_re * h_im + a_im * h_re + e[:, SLAB_STATE:])

    for c in range(lc // SSM_COLS):
        k_hi = (c + 1) * SSM_COLS
        cols = slice(c * SSM_COLS, k_hi)
        y_scr[:, cols] = _dot(uf[:, :k_hi], w1_ref[0, :k_hi, cols])
    y = y_scr[...] + lax.dot_general(e_scr[...].astype(BF16), ft_ref[0], _NT, preferred_element_type=F32)
    for i in range(L):
        yi = y[:, i * LANES:(i + 1) * LANES] + d_ref[0] * u_ref[0, pl.ds(i, nch, stride=L), :]
        y_ref[0, pl.ds(i, nch, stride=L), :] = jax.nn.gelu(yi)


def ssm_scan(u, w1, ft, al, d_skip):
    bsz, seq, c = u.shape
    ns = c // LANES
    L = SSM_CHUNK
    nch = seq // L
    lc = L * LANES
    return pl.pallas_call(
        _ssm_kernel,
        grid=(ns, bsz),
        in_specs=[pl.BlockSpec((1, seq, LANES), lambda s, b: (b, 0, s)),
                  pl.BlockSpec((1, lc, lc + 2 * SLAB_STATE), lambda s, b: (s, 0, 0)),
                  pl.BlockSpec((1, lc, 2 * SLAB_STATE), lambda s, b: (s, 0, 0)),
                  pl.BlockSpec((1, 1, 2 * SLAB_STATE), lambda s, b: (s, 0, 0)),
                  pl.BlockSpec((1, 1, LANES), lambda s, b: (s, 0, 0))],
        out_specs=pl.BlockSpec((1, seq, LANES), lambda s, b: (b, 0, s)),
        out_shape=jax.ShapeDtypeStruct((bsz, seq, c), F32),
        scratch_shapes=[pltpu.VMEM((nch, lc), BF16), pltpu.VMEM((nch, 2 * SLAB_STATE), F32),
                        pltpu.VMEM((nch, lc), F32)],
        compiler_params=_params(("arbitrary", "arbitrary"), 60),
        name="ssm_scan",
    )(u, w1, ft, al, d_skip.reshape(ns, 1, LANES))


def _glu_kernel(y_ref, w_ref, b_ref, g_ref, o_ref):
    y = y_ref[...]
    o = y * jax.nn.sigmoid(_dot(y.astype(BF16), w_ref[...]) + b_ref[...])
    ms = jnp.mean(o * o, axis=-1, keepdims=True)
    o_ref[...] = (o * lax.rsqrt(ms + RMS_EPS) * g_ref[...]).astype(BF16)


def glu_norm(y, w_glu, b_glu, g):
    t, c = y.shape
    tm = 512
    return pl.pallas_call(
        _glu_kernel,
        grid=(t // tm,),
        in_specs=[pl.BlockSpec((tm, c), lambda i: (i, 0)),
                  pl.BlockSpec((c, c), lambda i: (0, 0)),
                  pl.BlockSpec((1, c), lambda i: (0, 0)),
                  pl.BlockSpec((1, c), lambda i: (0, 0))],
        out_specs=pl.BlockSpec((tm, c), lambda i: (i, 0)),
        out_shape=jax.ShapeDtypeStruct((t, c), BF16),
        compiler_params=_params(("arbitrary",), 48),
        name="glu_norm",
    )(y, w_glu, b_glu.reshape(1, c), g.reshape(1, c))


def _outproj_kernel(attn_ref, ga_ref, ssm_ref, w_ref, x_ref, gate_ref, lng_ref, lnb_ref, sc2_ref, sh2_ref,
                    wrh_ref, wrl_ref, x1_ref, h2_ref, lg_ref, a_scr, *, alpha, n_att):
    j = pl.program_id(2)
    nj = pl.num_programs(2)
    tn = w_ref.shape[1]

    tm = x1_ref.shape[1]
    n_chunks = tm // ROW_CHUNK

    @pl.when(j == 0)
    def _():
        def norm(c, carry):
            rs = pl.ds(pl.multiple_of(c * ROW_CHUNK, ROW_CHUNK), ROW_CHUNK)
            a = attn_ref[0, rs, :]
            ms = jnp.mean(a * a, axis=-1, keepdims=True)
            a_scr[rs, :n_att] = (a * lax.rsqrt(ms + RMS_EPS) * ga_ref[...]).astype(BF16)
            return carry
        lax.fori_loop(0, n_chunks, norm, 0)
        a_scr[:, n_att:] = ssm_ref[0]

    mixed = _dot(a_scr[...], w_ref[...])
    col = pl.multiple_of(j * tn, tn)
    x1_ref[0, :, pl.ds(col, tn)] = alpha * x_ref[0] + (1.0 + gate_ref[0]) * mixed

    @pl.when(j == nj - 1)
    def _():
        def finish(c, carry):
            rs = pl.ds(pl.multiple_of(c * ROW_CHUNK, ROW_CHUNK), ROW_CHUNK)
            r = x1_ref[0, rs, :]
            mu = jnp.mean(r, axis=-1, keepdims=True)
            var = jnp.mean(jnp.square(r - mu), axis=-1, keepdims=True)
            x1 = (r - mu) * lax.rsqrt(var + LN_EPS) * lng_ref[...] + lnb_ref[...]
            x1_ref[0, rs, :] = x1
            h2 = x1 * (1.0 + sc2_ref[0]) + sh2_ref[0]
            hi = h2.astype(BF16)
            hi_f = hi.astype(F32)
            lo = (h2 - hi_f).astype(BF16)
            lg_ref[0, rs, :] = _dot(hi, wrh_ref[...]) + _dot(hi, wrl_ref[...]) + _dot(lo, wrh_ref[...])
            h2_ref[0, rs, :] = pack_bf16_pairs(hi_f)
            return carry
        lax.fori_loop(0, n_chunks, finish, 0)


def pack_bf16_pairs(x):
    n = x.shape[-1] // 2
    bits = lax.bitcast_convert_type(x, jnp.uint32)
    return bits[:, n:] | (bits[:, :n] >> 16)


def unpack_bf16_pairs(p):
    lo = lax.bitcast_convert_type(p << 16, F32).astype(BF16)
    hi = lax.bitcast_convert_type(p & jnp.uint32(0xFFFF0000), F32).astype(BF16)
    return lo, hi


def out_proj(attn, g_attn, ssm_n, w_out, x, gate1, ln_g, ln_b, scale2, shift2, wr_hi, wr_lo, alpha):
    bsz, seq, d = x.shape
    n_att = attn.shape[-1]
    n_ssm = ssm_n.shape[-1]
    k = n_att + n_ssm
    tm, tn = 512, 256
    row = lambda a: a.reshape(1, -1)
    full = lambda n: pl.BlockSpec((1, n), lambda b, i, j: (0, 0))
    return pl.pallas_call(
        functools.partial(_outproj_kernel, alpha=alpha, n_att=n_att),
        grid=(bsz, seq // tm, d // tn),
        in_specs=[pl.BlockSpec((1, tm, n_att), lambda b, i, j: (b, i, 0)),
                  full(n_att),
                  pl.BlockSpec((1, tm, n_ssm), lambda b, i, j: (b, i, 0)),
                  pl.BlockSpec((k, tn), lambda b, i, j: (0, j)),
                  pl.BlockSpec((1, tm, tn), lambda b, i, j: (b, i, j)),
                  pl.BlockSpec((1, 1, tn), lambda b, i, j: (b, 0, j)),
                  full(d), full(d),
                  pl.BlockSpec((1, 1, d), lambda b, i, j: (b, 0, 0)),
                  pl.BlockSpec((1, 1, d), lambda b, i, j: (b, 0, 0)),
                  pl.BlockSpec((d, LANES), lambda b, i, j: (0, 0)),
                  pl.BlockSpec((d, LANES), lambda b, i, j: (0, 0))],
        out_specs=[pl.BlockSpec((1, tm, d), lambda b, i, j: (b, i, 0)),
                   pl.BlockSpec((1, tm, d // 2), lambda b, i, j: (b, i, 0)),
                   pl.BlockSpec((1, tm, LANES), lambda b, i, j: (b, i, 0))],
        out_shape=[jax.ShapeDtypeStruct((bsz, seq, d), F32),
                   jax.ShapeDtypeStruct((bsz, seq, d // 2), jnp.uint32),
                   jax.ShapeDtypeStruct((bsz, seq, LANES), F32)],
        scratch_shapes=[pltpu.VMEM((tm, k), BF16)],
        compiler_params=_params(("arbitrary", "arbitrary", "arbitrary"), 56),
        name="out_proj",
    )(attn, row(g_attn), ssm_n, w_out, x, gate1, row(ln_g), row(ln_b), scale2, shift2, wr_hi, wr_lo)


def _moe_kernel(tok_ref, bexp_ref, nsub_ref, nact_ref, h_hbm, wg_ref, wu_ref, wd_ref, y_ref, xbuf, sem):
    i = pl.program_id(0)
    nact = nact_ref[0]
    rows = xbuf.shape[1]

    def start_gather(blk, slot):
        n_trips = nsub_ref[blk] * (MOE_SUB // GATHER_UNROLL)

        def body(t, carry):
            for k in range(GATHER_UNROLL):
                r = t * GATHER_UNROLL + k
                tok = tok_ref[blk * rows + r]
                pltpu.make_async_copy(h_hbm.at[pl.ds(tok, 1)], xbuf.at[slot, pl.ds(r, 1)], sem.at[slot]).start()
            return carry
        lax.fori_loop(0, n_trips, body, 0)

    def run_block(slot, n):
        pltpu.make_async_copy(h_hbm.at[pl.ds(0, n)], xbuf.at[slot, pl.ds(0, n)], sem.at[slot]).wait()
        x_lo, x_hi = unpack_bf16_pairs(xbuf[slot, :n])
        half = x_lo.shape[1]
        g = _dot(x_lo, wg_ref[0, :half, :]) + _dot(x_hi, wg_ref[0, half:, :])
        u = _dot(x_lo, wu_ref[0, :half, :]) + _dot(x_hi, wu_ref[0, half:, :])
        act = (jax.nn.silu(g) * u).astype(BF16)
        y_ref[:n, :] = _dot(act, wd_ref[0])
        if n < rows:
            y_ref[n:, :] = jnp.zeros((rows - n, y_ref.shape[1]), F32)

    @pl.when(jnp.logical_and(i == 0, nact > 0))
    def _():
        start_gather(0, 0)

    @pl.when(i + 1 < nact)
    def _():
        start_gather(i + 1, (i + 1) % 2)

    for n_sub in range(1, rows // MOE_SUB + 1):
        @pl.when(jnp.logical_and(i < nact, nsub_ref[i] == n_sub))
        def _():
            run_block(i % 2, n_sub * MOE_SUB)

    @pl.when(i >= nact)
    def _():
        y_ref[...] = jnp.zeros_like(y_ref)


def moe_experts(h2, tok_buf, block_expert, short_blocks, n_active, w_gate, w_up, w_down):
    t, dp = h2.shape
    d = 2 * dp
    n_rows = tok_buf.shape[0]
    rows = MOE_ROWS
    n_blocks = n_rows // rows
    de = w_gate.shape[-1]

    def wmap(i, tok, bexp, short, nact):
        return (bexp[jnp.minimum(i, jnp.maximum(nact[0] - 1, 0))], 0, 0)

    grid_spec = pltpu.PrefetchScalarGridSpec(
        num_scalar_prefetch=4,
        grid=(n_blocks,),
        in_specs=[pl.BlockSpec(memory_space=pl.ANY),
                  pl.BlockSpec((1, d, de), wmap),
                  pl.BlockSpec((1, d, de), wmap),
                  pl.BlockSpec((1, de, d), wmap)],
        out_specs=pl.BlockSpec((rows, d), lambda i, *_: (i, 0)),
        scratch_shapes=[pltpu.VMEM((2, rows, dp), jnp.uint32), pltpu.SemaphoreType.DMA((2,))],
    )
    return pl.pallas_call(
        _moe_kernel,
        grid_spec=grid_spec,
        out_shape=jax.ShapeDtypeStruct((n_rows, d), F32),
        compiler_params=_params(("arbitrary",), 56),
        name="moe_experts",
    )(tok_buf, block_expert, short_blocks, n_active, h2, w_gate, w_up, w_down)


def _combine_kernel(dest_ref, ys_hbm, rt_ref, x1_ref, gate_ref, lng_ref, lnb_ref, o_ref, ybuf, sem, *, alpha):
    i = pl.program_id(0)
    n = pl.num_programs(0)
    tm = x1_ref.shape[0]
    n_copies = TOP_K * tm

    def gather(blk, slot, start):
        if not start:
            pltpu.make_async_copy(ys_hbm.at[pl.ds(0, n_copies)], ybuf.at[slot], sem.at[slot]).wait()
            return

        def body(r, carry):
            row = dest_ref[blk * n_copies + r]
            pltpu.make_async_copy(ys_hbm.at[pl.ds(row, 1)], ybuf.at[slot, pl.ds(r, 1)], sem.at[slot]).start()
            return carry
        lax.fori_loop(0, n_copies, body, 0, unroll=GATHER_UNROLL)

    @pl.when(i == 0)
    def _():
        gather(0, 0, True)

    @pl.when(i + 1 < n)
    def _():
        gather(i + 1, (i + 1) % 2, True)

    slot = i % 2
    gather(i, slot, False)
    moe = rt_ref[:, ROUTE_W:ROUTE_W + 1] * ybuf[slot, :tm, :]
    for kk in range(1, TOP_K):
        moe = moe + rt_ref[:, ROUTE_W + kk:ROUTE_W + kk + 1] * ybuf[slot, kk * tm:(kk + 1) * tm, :]
    r = alpha * x1_ref[...] + (1.0 + gate_ref[0]) * moe
    mu = jnp.mean(r, axis=-1, keepdims=True)
    var = jnp.mean(jnp.square(r - mu), axis=-1, keepdims=True)
    o_ref[...] = (r - mu) * lax.rsqrt(var + LN_EPS) * lng_ref[...] + lnb_ref[...]


def moe_combine(ys, dest, table, x1, gate2, ln_g, ln_b, alpha, seq):
    t, d = x1.shape
    tm = 128
    tiles_per_seq = seq // tm
    grid_spec = pltpu.PrefetchScalarGridSpec(
        num_scalar_prefetch=1,
        grid=(t // tm,),
        in_specs=[pl.BlockSpec(memory_space=pl.ANY),
                  pl.BlockSpec((tm, LANES), lambda i, *_: (i, 0)),
                  pl.BlockSpec((tm, d), lambda i, *_: (i, 0)),
                  pl.BlockSpec((1, 1, d), lambda i, *_: (i // tiles_per_seq, 0, 0)),
                  pl.BlockSpec((1, d), lambda i, *_: (0, 0)),
                  pl.BlockSpec((1, d), lambda i, *_: (0, 0))],
        out_specs=pl.BlockSpec((tm, d), lambda i, *_: (i, 0)),
        scratch_shapes=[pltpu.VMEM((2, TOP_K * tm, d), F32), pltpu.SemaphoreType.DMA((2,))],
    )
    return pl.pallas_call(
        functools.partial(_combine_kernel, alpha=alpha),
        grid_spec=grid_spec,
        out_shape=jax.ShapeDtypeStruct((t, d), F32),
        compiler_params=_params(("arbitrary",), 32),
        name="moe_combine",
    )(dest, ys, table, x1, gate2, ln_g.reshape(1, d), ln_b.reshape(1, d))


def _route_kernel(lg_ref, b_ref, o_ref, cnt_ref, carry):
    i = pl.program_id(0)

    @pl.when(i == 0)
    def _():
        carry[...] = jnp.zeros_like(carry)

    tm = lg_ref.shape[0]
    x = lg_ref[...] + b_ref[...]
    lane = lax.broadcasted_iota(jnp.int32, (tm, LANES), 1)
    ninf = -jnp.inf

    def top(v):
        vmax = jnp.max(v, axis=1, keepdims=True)
        return vmax, jnp.min(jnp.where(v == vmax, lane, LANES), axis=1, keepdims=True)

    gmask = lane < N_EXPERT_GROUPS
    gmax, g_sel = top(jnp.where(gmask, x, ninf))
    p_group = 1.0 / jnp.sum(jnp.where(gmask, jnp.exp(x - gmax), 0.0), axis=1, keepdims=True)

    lo = N_EXPERT_GROUPS + g_sel * EXPERTS_PER_GROUP
    cur = jnp.where(jnp.logical_and(lane >= lo, lane < lo + EXPERTS_PER_GROUP), x, ninf)
    vals, idxs = [], []
    for _ in range(TOP_K):
        v, ix = top(cur)
        vals.append(v)
        idxs.append(ix)
        cur = jnp.where(lane == ix, ninf, cur)
    exps = [jnp.exp(v - vals[0]) for v in vals]
    den = functools.reduce(lambda a, c: a + c, exps)

    member = functools.reduce(jnp.logical_or, [lane == ix for ix in idxs])
    mf = jnp.where(member, 1.0, 0.0)
    r_i = lax.broadcasted_iota(jnp.int32, (tm, tm), 0)
    c_i = lax.broadcasted_iota(jnp.int32, (tm, tm), 1)
    before = _dot((c_i < r_i).astype(BF16), mf.astype(BF16)) + carry[...]
    carry[...] = carry[...] + jnp.sum(mf, axis=0, keepdims=True)
    cnt_ref[...] = carry[...]

    out = jnp.zeros((tm, LANES), F32)
    for k in range(TOP_K):
        rank = jnp.sum(jnp.where(lane == idxs[k], before, 0.0), axis=1, keepdims=True)
        out = jnp.where(lane == ROUTE_EID + k, (idxs[k] - N_EXPERT_GROUPS).astype(F32), out)
        out = jnp.where(lane == ROUTE_RANK + k, rank, out)
        out = jnp.where(lane == ROUTE_W + k, p_group * (exps[k] / den), out)
    o_ref[...] = out


def route(logits, b_rg, b_re, n_rows, tm_combine):
    n_tok = logits.shape[0]
    n_experts = N_EXPERT_GROUPS * EXPERTS_PER_GROUP
    bias = jnp.zeros((1, LANES), F32).at[0, :N_EXPERT_GROUPS].set(b_rg)
    bias = bias.at[0, N_EXPERT_GROUPS:N_EXPERT_GROUPS + n_experts].set(b_re)
    tm = 512
    table, cnt = pl.pallas_call(
        _route_kernel,
        grid=(n_tok // tm,),
        in_specs=[pl.BlockSpec((tm, LANES), lambda i: (i, 0)),
                  pl.BlockSpec((1, LANES), lambda i: (0, 0))],
        out_specs=[pl.BlockSpec((tm, LANES), lambda i: (i, 0)),
                   pl.BlockSpec((1, LANES), lambda i: (0, 0))],
        out_shape=[jax.ShapeDtypeStruct((n_tok, LANES), F32), jax.ShapeDtypeStruct((1, LANES), F32)],
        scratch_shapes=[pltpu.VMEM((1, LANES), F32)],
        compiler_params=_params(("arbitrary",), 16),
        name="route",
    )(logits, bias)

    eid = table[:, ROUTE_EID:ROUTE_EID + TOP_K].astype(jnp.int32)
    rank = table[:, ROUTE_RANK:ROUTE_RANK + TOP_K].astype(jnp.int32)
    counts = cnt[0, N_EXPERT_GROUPS:N_EXPERT_GROUPS + n_experts].astype(jnp.int32)
    padded = ((counts + MOE_ROWS - 1) // MOE_ROWS) * MOE_ROWS
    pends = jnp.cumsum(padded)
    pstarts = pends - padded
    onehot = eid[:, :, None] == jnp.arange(n_experts, dtype=jnp.int32)
    dest = jnp.sum(jnp.where(onehot, pstarts, 0), axis=-1) + rank
    tok = jnp.broadcast_to(jnp.arange(n_tok, dtype=jnp.int32)[:, None], (n_tok, TOP_K))
    tok_buf = jnp.zeros((n_rows,), jnp.int32).at[dest.reshape(-1)].set(tok.reshape(-1), unique_indices=True)
    n_blocks = n_rows // MOE_ROWS
    block_expert = jnp.clip(
        jnp.searchsorted(pends, jnp.arange(n_blocks, dtype=jnp.int32) * MOE_ROWS, side='right'),
        0, n_experts - 1).astype(jnp.int32)
    rows_left = (pstarts + counts)[block_expert] - jnp.arange(n_blocks, dtype=jnp.int32) * MOE_ROWS
    short_blocks = jnp.clip(-(-rows_left // MOE_SUB), 1, MOE_ROWS // MOE_SUB).astype(jnp.int32)
    n_active = (pends[-1] // MOE_ROWS).astype(jnp.int32).reshape(1)
    dest_tiles = dest.reshape(n_tok // tm_combine, tm_combine, TOP_K).transpose(0, 2, 1).reshape(-1)
    return table, tok_buf, block_expert, short_blocks, n_active, dest_tiles.astype(jnp.int32)


def _layer(x, c, w_ada, b_ada, w_in, b_forget, lam_re, lam_im, log_dt, b_re, b_im, c_re, c_im, d_skip,
           w_glu, b_glu, g_attn, g_ssm, w_out, ln1_g, ln1_b, w_rg, b_rg, w_re, b_re_r,
           w_gate, w_up, w_down, ln2_g, ln2_b, alpha):
    bsz, seq, d = x.shape
    n_tok = bsz * seq
    n_heads = b_forget.shape[0]
    d_att = n_heads * HEAD_DIM
    d_ssm = d_skip.shape[0] * d_skip.shape[1]

    mod = ada_mod(c, w_ada, b_ada).reshape(bsz, 1, -1)
    shift1, scale1, gate1, shift2, scale2, gate2 = jnp.split(mod, 6, axis=-1)

    n_qkv = 3 * d_att
    w_main = jnp.concatenate([w_in[:, :n_qkv], w_in[:, n_qkv + n_heads:]], axis=1).astype(BF16)
    w_f = jnp.zeros((d, LANES), BF16).at[:, :n_heads].set(w_in[:, n_qkv:n_qkv + n_heads].astype(BF16))
    qkv, u, f = in_proj(x, scale1, shift1, w_main, w_f, n_qkv)

    q_aug, k_aug = forget_cum(f, b_forget)
    attn, (w_glu_b, w_out_b, w_gate_b, w_up_b, w_down_b) = attention(
        qkv, q_aug, k_aug, n_heads, [w_glu, w_out, w_gate, w_up, w_down])

    w1, ft, al = ssm_prep(lam_re, lam_im, log_dt, b_re, b_im, c_re, c_im)
    y = ssm_scan(u, w1, ft, al, d_skip)
    ssm_n = glu_norm(y.reshape(n_tok, d_ssm), w_glu_b, b_glu, g_ssm).reshape(bsz, seq, d_ssm)

    n_experts = N_EXPERT_GROUPS * EXPERTS_PER_GROUP
    w_r = jnp.zeros((d, LANES), F32).at[:, :N_EXPERT_GROUPS].set(w_rg)
    w_r = w_r.at[:, N_EXPERT_GROUPS:N_EXPERT_GROUPS + n_experts].set(w_re)
    wr_hi = w_r.astype(BF16)
    wr_lo = (w_r - wr_hi.astype(F32)).astype(BF16)
    x1, h2, logits = out_proj(attn, g_attn, ssm_n, w_out_b, x, gate1, ln1_g, ln1_b,
                              scale2, shift2, wr_hi, wr_lo, alpha)

    n_assign = n_tok * TOP_K
    n_blocks = -(-(n_assign + n_experts * (MOE_ROWS - 1)) // MOE_ROWS)
    n_rows = n_blocks * MOE_ROWS
    table, tok_buf, block_expert, short_blocks, n_active, dest = route(
        logits.reshape(n_tok, LANES), b_rg, b_re_r, n_rows, 128)
    ys = moe_experts(h2.reshape(n_tok, d // 2), tok_buf, block_expert, short_blocks, n_active,
                     w_gate_b, w_up_b, w_down_b)
    out = moe_combine(ys, dest, table, x1.reshape(n_tok, d), gate2, ln2_g, ln2_b, alpha, seq)
    return out.reshape(bsz, seq, d)


def kernel(x, c, w_ada, b_ada, w_in, b_forget, ssm_lambda_re, ssm_lambda_im, ssm_log_dt, ssm_b_re, ssm_b_im,
           ssm_c_re, ssm_c_im, ssm_d, w_glu, b_glu, g_attn, g_ssm, w_out, ln1_g, ln1_b, w_router_group,
           b_router_group, w_router_expert, b_router_expert, w_gate, w_up, w_down, ln2_g, ln2_b):
    depth = w_ada.shape[0]
    alpha = (2.0 * depth) ** 0.25
    for l in range(depth):
        x = _layer(x, c, w_ada[l], b_ada[l], w_in[l], b_forget[l], ssm_lambda_re[l], ssm_lambda_im[l],
                   ssm_log_dt[l], ssm_b_re[l], ssm_b_im[l], ssm_c_re[l], ssm_c_im[l], ssm_d[l],
                   w_glu[l], b_glu[l], g_attn[l], g_ssm[l], w_out[l], ln1_g[l], ln1_b[l],
                   w_router_group[l], b_router_group[l], w_router_expert[l], b_router_expert[l],
                   w_gate[l], w_up[l], w_down[l], ln2_g[l], ln2_b[l], alpha)
    return x
```

```python
import functools
import math

import jax
import jax.numpy as jnp
from jax import lax
from jax.experimental import pallas as pl
from jax.experimental.pallas import tpu as pltpu

F32 = jnp.float32
BF16 = jnp.bfloat16

LANES = 128
HEAD_DIM = 128
SSM_GROUP = 16
SSM_STATE = 64
GROUPS_PER_SLAB = LANES // SSM_GROUP
SLAB_STATE = GROUPS_PER_SLAB * SSM_STATE
SSM_CHUNK = 16
SSM_COLS = 512
N_EXPERT_GROUPS = 8
EXPERTS_PER_GROUP = 8
TOP_K = 2
MOE_ROWS = 256
MOE_SUB = 64
GATHER_UNROLL = 8
ROW_CHUNK = 128
ROUTE_EID, ROUTE_RANK, ROUTE_W = 0, TOP_K, 2 * TOP_K
LN_EPS = 1e-5
RMS_EPS = 1e-6
NEG_BIG = -1e30
LOG2E = math.log2(math.e)
MIB = 1024 * 1024

_NT = (((1,), (1,)), ((), ()))


def _params(semantics, vmem_mib):
    return pltpu.CompilerParams(dimension_semantics=semantics, vmem_limit_bytes=vmem_mib * MIB)


def _dot(a, b):
    return jnp.dot(a, b, preferred_element_type=F32)


def _dot_nt(a, b):
    return lax.dot_general(a, b, _NT, preferred_element_type=F32)


def _split2(x):
    hi = x.astype(BF16)
    return hi, (x - hi.astype(F32)).astype(BF16)


def _ada_kernel(c_ref, w_ref, b_ref, o_ref):
    s = jax.nn.silu(c_ref[...]).astype(BF16)
    o_ref[...] = _dot(s, w_ref[...].astype(BF16)) + b_ref[...]


def ada_mod(c, w_ada, b_ada):
    bsz, d = c.shape
    n = w_ada.shape[1]
    rows = 8
    assert bsz <= rows
    cp = jnp.zeros((rows, d), F32).at[:bsz].set(c)
    tn = 512
    out = pl.pallas_call(
        _ada_kernel,
        grid=(n // tn,),
        in_specs=[pl.BlockSpec((rows, d), lambda j: (0, 0)),
                  pl.BlockSpec((d, tn), lambda j: (0, j)),
                  pl.BlockSpec((1, tn), lambda j: (0, j))],
        out_specs=pl.BlockSpec((rows, tn), lambda j: (0, j)),
        out_shape=jax.ShapeDtypeStruct((rows, n), F32),
        compiler_params=_params(("arbitrary",), 40),
        name="ada_mod",
    )(cp, w_ada, b_ada.reshape(1, n))
    return out[:bsz]


def _inproj_kernel(x_ref, sc_ref, sh_ref, w_ref, wf_ref, qkv_ref, u_ref, f_ref, h_scr, *,
                   n_q_tiles, n_qkv_tiles, q_scale):
    j = pl.program_id(2)

    @pl.when(j == 0)
    def _():
        hb = (x_ref[0] * (1.0 + sc_ref[0]) + sh_ref[0]).astype(BF16)
        h_scr[...] = hb
        f_ref[0] = _dot(hb, wf_ref[...])

    acc = _dot(h_scr[...], w_ref[...])

    @pl.when(j < n_q_tiles)
    def _():
        qkv_ref[0] = (acc * q_scale).astype(BF16)

    @pl.when(jnp.logical_and(j >= n_q_tiles, j < n_qkv_tiles))
    def _():
        qkv_ref[0] = acc.astype(BF16)

    @pl.when(j >= n_qkv_tiles)
    def _():
        u_ref[0] = acc


def in_proj(x, scale, shift, w_main, w_f, n_qkv):
    bsz, seq, d = x.shape
    n_all = w_main.shape[1]
    n_u = n_all - n_qkv
    tm, tn = 512, 1024
    nq = n_qkv // tn
    grid = (bsz, seq // tm, n_all // tn)
    return pl.pallas_call(
        functools.partial(_inproj_kernel, n_q_tiles=n_qkv // 3 // tn, n_qkv_tiles=nq,
                          q_scale=HEAD_DIM ** -0.5 * LOG2E),
        grid=grid,
        in_specs=[pl.BlockSpec((1, tm, d), lambda b, i, j: (b, i, 0)),
                  pl.BlockSpec((1, 1, d), lambda b, i, j: (b, 0, 0)),
                  pl.BlockSpec((1, 1, d), lambda b, i, j: (b, 0, 0)),
                  pl.BlockSpec((d, tn), lambda b, i, j: (0, j)),
                  pl.BlockSpec((d, LANES), lambda b, i, j: (0, 0))],
        out_specs=[pl.BlockSpec((1, tm, tn), lambda b, i, j: (b, i, jnp.minimum(j, nq - 1))),
                   pl.BlockSpec((1, tm, tn), lambda b, i, j: (b, i, jnp.maximum(j - nq, 0))),
                   pl.BlockSpec((1, tm, LANES), lambda b, i, j: (b, i, 0))],
        out_shape=[jax.ShapeDtypeStruct((bsz, seq, n_qkv), BF16),
                   jax.ShapeDtypeStruct((bsz, seq, n_u), F32),
                   jax.ShapeDtypeStruct((bsz, seq, LANES), F32)],
        scratch_shapes=[pltpu.VMEM((tm, d), BF16)],
        compiler_params=_params(("arbitrary", "arbitrary", "arbitrary"), 52),
        name="in_proj",
    )(x, scale, shift, w_main, w_f)


def _split3(x):
    p1 = x.astype(BF16)
    r1 = x - p1.astype(F32)
    p2 = r1.astype(BF16)
    p3 = (r1 - p2.astype(F32)).astype(BF16)
    return p1, p2, p3


N_PIECES = 3


def _cum_kernel(f_ref, b_ref, qa_ref, ka_ref, carry, *, n_heads):
    i = pl.program_id(1)

    @pl.when(i == 0)
    def _():
        carry[...] = jnp.zeros_like(carry)

    tc = f_ref.shape[1]
    lf = jax.nn.log_sigmoid(f_ref[0] + b_ref[...])
    row = lax.broadcasted_iota(jnp.int32, (tc, tc), 0)
    col = lax.broadcasted_iota(jnp.int32, (tc, tc), 1)
    tri = (col <= row).astype(BF16)
    p1, p2, p3 = _split3(lf)
    cs = _dot(tri, p1) + _dot(tri, p2) + _dot(tri, p3) + carry[...]
    carry[...] = cs[tc - 1:tc, :]

    pieces = jnp.concatenate(_split3(cs * LOG2E), axis=1)
    r = lax.broadcasted_iota(jnp.int32, (N_PIECES * LANES, LANES), 0)
    c = lax.broadcasted_iota(jnp.int32, (N_PIECES * LANES, LANES), 1)
    lane = lax.broadcasted_iota(jnp.int32, (tc, LANES), 1)
    ones_q = jnp.where(jnp.logical_and(lane >= N_PIECES, lane < 2 * N_PIECES), 1.0, 0.0)
    ones_k = jnp.where(lane < N_PIECES, 1.0, 0.0)
    for h in range(n_heads):
        sel_q = (r == c * LANES + h).astype(BF16)
        sel_k = (r == (c - N_PIECES) * LANES + h).astype(BF16)
        picked = _dot(pieces, jnp.concatenate([sel_q, sel_k], axis=1))
        qa_ref[0, h] = (picked[:, :LANES] + ones_q).astype(BF16)
        ka_ref[0, h] = (ones_k - picked[:, LANES:]).astype(BF16)


def forget_cum(f, b_forget):
    bsz, seq, _ = f.shape
    n_heads = b_forget.shape[0]
    tc = 512
    bpad = jnp.zeros((1, LANES), F32).at[0, :n_heads].set(b_forget)
    out_spec = pl.BlockSpec((1, n_heads, tc, LANES), lambda b, i: (b, 0, i, 0))
    out_shape = jax.ShapeDtypeStruct((bsz, n_heads, seq, LANES), BF16)
    return pl.pallas_call(
        functools.partial(_cum_kernel, n_heads=n_heads),
        grid=(bsz, seq // tc),
        in_specs=[pl.BlockSpec((1, tc, LANES), lambda b, i: (b, i, 0)),
                  pl.BlockSpec((1, LANES), lambda b, i: (0, 0))],
        out_specs=[out_spec, out_spec],
        out_shape=[out_shape, out_shape],
        scratch_shapes=[pltpu.VMEM((1, LANES), F32)],
        compiler_params=_params(("arbitrary", "arbitrary"), 24),
        name="forget_cum",
    )(f, bpad)


ATT_QROWS = 1024
ATT_KEYS = 512


def _attn_kernel(*refs, n_cast):
    q_ref, qa_ref, k_ref, ka_ref, v_ref = refs[:5]
    src_refs = refs[5:5 + n_cast]
    o_ref = refs[5 + n_cast]
    dst_refs = refs[6 + n_cast:6 + 2 * n_cast]
    m_scr, acc_scr, s_scr = refs[6 + 2 * n_cast:]
    for src, dst in zip(src_refs, dst_refs):
        dst[...] = src[...].astype(BF16)

    tq = q_ref.shape[1]
    tk = ATT_KEYS
    n_groups = tq // tk
    qi = pl.program_id(2)
    q = jnp.concatenate([q_ref[0], qa_ref[0, 0]], axis=1)
    ones = jnp.ones((tk, HEAD_DIM), BF16)

    m_scr[...] = jnp.full_like(m_scr, NEG_BIG)
    acc_scr[...] = jnp.zeros_like(acc_scr)

    def scores(kb, slot, first_group=0):
        k0 = pl.multiple_of(kb * tk, tk)
        kt = jnp.concatenate([k_ref[0, pl.ds(k0, tk), :], ka_ref[0, 0, pl.ds(k0, tk), :]], axis=1)
        r0 = first_group * tk
        s_scr[slot, r0:, :] = lax.dot_general(q[r0:], kt, _NT, preferred_element_type=F32)

    def update(kb, slot, diag_group=None):
        k0 = pl.multiple_of(kb * tk, tk)
        vt = jnp.concatenate([v_ref[0, pl.ds(k0, tk), :], ones], axis=1)
        for g in range(n_groups):
            if diag_group is not None and g < diag_group:
                continue
            rs = slice(g * tk, (g + 1) * tk)
            s = s_scr[slot, rs, :]
            if g == diag_group:
                qpos = lax.broadcasted_iota(jnp.int32, (tk, tk), 0)
                kpos = lax.broadcasted_iota(jnp.int32, (tk, tk), 1)
                s = jnp.where(kpos <= qpos, s, NEG_BIG)
            m_prev = m_scr[rs]
            m_new = jnp.maximum(m_prev, jnp.max(s, axis=1, keepdims=True))
            alpha = jnp.exp2(m_prev - m_new)
            p = jnp.exp2(s - m_new)
            acc_scr[rs] = alpha * acc_scr[rs] + _dot(p.astype(BF16), vt)
            m_scr[rs] = m_new

    def body(i, carry):
        kb = n_groups * i
        for g in range(n_groups):
            scores(kb + g + 1, (g + 1) % 2)
            update(kb + g, g % 2)
        return carry

    assert n_groups % 2 == 0
    scores(0, 0)
    lax.fori_loop(0, qi, body, 0)
    kb = n_groups * qi
    for g in range(n_groups):
        if g + 1 < n_groups:
            scores(kb + g + 1, (g + 1) % 2, first_group=g + 1)
        update(kb + g, g % 2, diag_group=g)

    o_ref[0] = acc_scr[:, :HEAD_DIM] / acc_scr[:, HEAD_DIM:]


BF16_ROWS = 16


def _cast_chunks(w, n_steps):
    cols = w.shape[-1]
    total_rows = w.size // cols
    n_chunks = n_steps
    while total_rows % (n_chunks * BF16_ROWS):
        n_chunks //= 2
    return w.reshape(n_chunks, total_rows // n_chunks, cols)


def attention(qkv, q_aug, k_aug, n_heads, cast_weights):
    bsz, seq, _ = qkv.shape
    t = min(ATT_QROWS, seq)
    nq = seq // t
    n_steps = bsz * n_heads * nq
    srcs = [_cast_chunks(w, n_steps) for w in cast_weights]

    def chunk_spec(a):
        per = n_steps // a.shape[0]
        return pl.BlockSpec((1,) + a.shape[1:], lambda b, h, i: (((b * n_heads + h) * nq + i) // per, 0, 0))

    outs = pl.pallas_call(
        functools.partial(_attn_kernel, n_cast=len(srcs)),
        grid=(bsz, n_heads, nq),
        in_specs=[pl.BlockSpec((1, t, HEAD_DIM), lambda b, h, i: (b, i, h)),
                  pl.BlockSpec((1, 1, t, LANES), lambda b, h, i: (b, h, i, 0)),
                  pl.BlockSpec((1, seq, HEAD_DIM), lambda b, h, i: (b, 0, n_heads + h)),
                  pl.BlockSpec((1, 1, seq, LANES), lambda b, h, i: (b, h, 0, 0)),
                  pl.BlockSpec((1, seq, HEAD_DIM), lambda b, h, i: (b, 0, 2 * n_heads + h))]
                 + [chunk_spec(a) for a in srcs],
        out_specs=[pl.BlockSpec((1, t, HEAD_DIM), lambda b, h, i: (b, i, h))] + [chunk_spec(a) for a in srcs],
        out_shape=[jax.ShapeDtypeStruct((bsz, seq, n_heads * HEAD_DIM), F32)]
                  + [jax.ShapeDtypeStruct(a.shape, BF16) for a in srcs],
        scratch_shapes=[pltpu.VMEM((t, 1), F32), pltpu.VMEM((t, 2 * HEAD_DIM), F32),
                        pltpu.VMEM((2, t, ATT_KEYS), F32)],
        compiler_params=_params(("arbitrary", "arbitrary", "arbitrary"), 56),
        name="attention",
    )(qkv, q_aug, qkv, k_aug, qkv, *srcs)
    return outs[0], [o.reshape(w.shape) for o, w in zip(outs[1:], cast_weights)]


def _blockdiag(p):
    g, c, n = p.shape
    ns = g // GROUPS_PER_SLAB
    eye = jnp.eye(GROUPS_PER_SLAB, dtype=p.dtype)
    out = p.reshape(ns, GROUPS_PER_SLAB, c, 1, n) * eye[None, :, None, :, None]
    return out.reshape(ns, GROUPS_PER_SLAB * c, GROUPS_PER_SLAB * n)


def _ssm_prep_kernel(lr_ref, li_ref, ldt_ref, bre_ref, bim_ref, cre_ref, cim_ref, w1_ref, ft_ref, al_ref):
    L = SSM_CHUNK
    lr = lr_ref[0]
    li = li_ref[0]
    dt = jnp.exp(ldt_ref[0])
    mag = jnp.exp(lr * dt)
    a_re = mag * jnp.cos(li * dt)
    a_im = mag * jnp.sin(li * dt)
    den = lr * lr + li * li
    z_re = ((a_re - 1.0) * lr + a_im * li) / den
    z_im = (a_im * lr - (a_re - 1.0) * li) / den
    br = bre_ref[0]
    bi = bim_ref[0]
    bb_re = z_re * br - z_im * bi
    bb_im = z_re * bi + z_im * br
    cr = cre_ref[0]
    ci = cim_ref[0]
    ft0_hi, ft0_lo = _split2(jnp.concatenate([cr, -ci], axis=1))

    def power(d):
        m = jnp.exp(lr * dt * d)
        return m * jnp.cos(li * dt * d), m * jnp.sin(li * dt * d)

    w1_ref[0, :, :L * LANES] = jnp.zeros((L * LANES, L * LANES), BF16)
    for d in range(L):
        pr, pi = power(float(d))
        xe = jnp.concatenate([bb_re * pr - bb_im * pi, bb_re * pi + bb_im * pr], axis=1)
        j = L - 1 - d
        w1_ref[0, j * LANES:(j + 1) * LANES, L * LANES:] = xe.astype(BF16)
        xe_hi, xe_lo = _split2(xe)
        m_d = (_dot_nt(xe_hi, ft0_hi) + _dot_nt(xe_hi, ft0_lo) + _dot_nt(xe_lo, ft0_hi)).astype(BF16)
        for jj in range(L - d):
            w1_ref[0, jj * LANES:(jj + 1) * LANES, (jj + d) * LANES:(jj + d + 1) * LANES] = m_d
        pr1, pi1 = power(float(d + 1))
        ft_ref[0, d * LANES:(d + 1) * LANES, :] = jnp.concatenate(
            [cr * pr1 - ci * pi1, -(cr * pi1 + ci * pr1)], axis=1).astype(BF16)
    prl, pil = power(float(L))
    al_ref[0] = jnp.concatenate([prl, pil], axis=1)


def ssm_prep(lam_re, lam_im, log_dt, b_re, b_im, c_re, c_im):
    g, n = lam_re.shape
    ns = g // GROUPS_PER_SLAB
    L = SSM_CHUNK
    rowvec = lambda a: a.reshape(ns, 1, SLAB_STATE)
    args = (rowvec(lam_re), rowvec(lam_im), rowvec(jnp.repeat(log_dt, n)),
            _blockdiag(b_re.transpose(0, 2, 1)), _blockdiag(b_im.transpose(0, 2, 1)),
            _blockdiag(c_re), _blockdiag(c_im))
    vec_spec = pl.BlockSpec((1, 1, SLAB_STATE), lambda s: (s, 0, 0))
    mat_spec = pl.BlockSpec((1, LANES, SLAB_STATE), lambda s: (s, 0, 0))
    return pl.pallas_call(
        _ssm_prep_kernel,
        grid=(ns,),
        in_specs=[vec_spec] * 3 + [mat_spec] * 4,
        out_specs=[pl.BlockSpec((1, L * LANES, L * LANES + 2 * SLAB_STATE), lambda s: (s, 0, 0)),
                   pl.BlockSpec((1, L * LANES, 2 * SLAB_STATE), lambda s: (s, 0, 0)),
                   pl.BlockSpec((1, 1, 2 * SLAB_STATE), lambda s: (s, 0, 0))],
        out_shape=[jax.ShapeDtypeStruct((ns, L * LANES, L * LANES + 2 * SLAB_STATE), BF16),
                   jax.ShapeDtypeStruct((ns, L * LANES, 2 * SLAB_STATE), BF16),
                   jax.ShapeDtypeStruct((ns, 1, 2 * SLAB_STATE), F32)],
        compiler_params=_params(("arbitrary",), 48),
        name="ssm_prep",
    )(*args)


def _ssm_kernel(u_ref, w1_ref, ft_ref, al_ref, d_ref, y_ref, uf_scr, e_scr, y_scr):
    L = SSM_CHUNK
    nch = uf_scr.shape[0]
    lc = L * LANES
    for j in range(L):
        uf_scr[:, j * LANES:(j + 1) * LANES] = u_ref[0, pl.ds(j, nch, stride=L), :].astype(BF16)
    uf = uf_scr[...]
    e_scr[...] = _dot(uf, w1_ref[0, :, lc:])

    a_re = al_ref[0, :, :SLAB_STATE]
    a_im = al_ref[0, :, SLAB_STATE:]
    h_re = jnp.zeros((1, SLAB_STATE), F32)
    h_im = jnp.zeros((1, SLAB_STATE), F32)
    for k in range(nch):
        e = e_scr[k:k + 1, :]
        e_scr[k:k + 1, :] = jnp.concatenate([h_re, h_im], axis=1)
        h_re, h_im = (a_re * h_re - a_im * h_im + e[:, :SLAB_STATE],
                      a_re * h_im + a_im * h_re + e[:, SLAB_STATE:])

    for c in range(lc // SSM_COLS):
        k_hi = (c + 1) * SSM_COLS
        cols = slice(c * SSM_COLS, k_hi)
        y_scr[:, cols] = _dot(uf[:, :k_hi], w1_ref[0, :k_hi, cols])
    y = y_scr[...] + lax.dot_general(e_scr[...].astype(BF16), ft_ref[0], _NT, preferred_element_type=F32)
    for i in range(L):
        yi = y[:, i * LANES:(i + 1) * LANES] + d_ref[0] * u_ref[0, pl.ds(i, nch, stride=L), :]
        y_ref[0, pl.ds(i, nch, stride=L), :] = jax.nn.gelu(yi)


def ssm_scan(u, w1, ft, al, d_skip):
    bsz, seq, c = u.shape
    ns = c // LANES
    L = SSM_CHUNK
    nch = seq // L
    lc = L * LANES
    return pl.pallas_call(
        _ssm_kernel,
        grid=(ns, bsz),
        in_specs=[pl.BlockSpec((1, seq, LANES), lambda s, b: (b, 0, s)),
                  pl.BlockSpec((1, lc, lc + 2 * SLAB_STATE), lambda s, b: (s, 0, 0)),
                  pl.BlockSpec((1, lc, 2 * SLAB_STATE), lambda s, b: (s, 0, 0)),
                  pl.BlockSpec((1, 1, 2 * SLAB_STATE), lambda s, b: (s, 0, 0)),
                  pl.BlockSpec((1, 1, LANES), lambda s, b: (s, 0, 0))],
        out_specs=pl.BlockSpec((1, seq, LANES), lambda s, b: (b, 0, s)),
        out_shape=jax.ShapeDtypeStruct((bsz, seq, c), F32),
        scratch_shapes=[pltpu.VMEM((nch, lc), BF16), pltpu.VMEM((nch, 2 * SLAB_STATE), F32),
                        pltpu.VMEM((nch, lc), F32)],
        compiler_params=_params(("arbitrary", "arbitrary"), 60),
        name="ssm_scan",
    )(u, w1, ft, al, d_skip.reshape(ns, 1, LANES))


def _glu_kernel(y_ref, w_ref, b_ref, g_ref, o_ref):
    y = y_ref[...]
    o = y * jax.nn.sigmoid(_dot(y.astype(BF16), w_ref[...]) + b_ref[...])
    ms = jnp.mean(o * o, axis=-1, keepdims=True)
    o_ref[...] = (o * lax.rsqrt(ms + RMS_EPS) * g_ref[...]).astype(BF16)


def glu_norm(y, w_glu, b_glu, g):
    t, c = y.shape
    tm = 512
    return pl.pallas_call(
        _glu_kernel,
        grid=(t // tm,),
        in_specs=[pl.BlockSpec((tm, c), lambda i: (i, 0)),
                  pl.BlockSpec((c, c), lambda i: (0, 0)),
                  pl.BlockSpec((1, c), lambda i: (0, 0)),
                  pl.BlockSpec((1, c), lambda i: (0, 0))],
        out_specs=pl.BlockSpec((tm, c), lambda i: (i, 0)),
        out_shape=jax.ShapeDtypeStruct((t, c), BF16),
        compiler_params=_params(("arbitrary",), 48),
        name="glu_norm",
    )(y, w_glu, b_glu.reshape(1, c), g.reshape(1, c))


def _outproj_kernel(attn_ref, ga_ref, ssm_ref, w_ref, x_ref, gate_ref, lng_ref, lnb_ref, sc2_ref, sh2_ref,
                    wrh_ref, wrl_ref, x1_ref, h2_ref, lg_ref, a_scr, *, alpha, n_att):
    j = pl.program_id(2)
    nj = pl.num_programs(2)
    tn = w_ref.shape[1]

    tm = x1_ref.shape[1]
    n_chunks = tm // ROW_CHUNK

    @pl.when(j == 0)
    def _():
        def norm(c, carry):
            rs = pl.ds(pl.multiple_of(c * ROW_CHUNK, ROW_CHUNK), ROW_CHUNK)
            a = attn_ref[0, rs, :]
            ms = jnp.mean(a * a, axis=-1, keepdims=True)
            a_scr[rs, :n_att] = (a * lax.rsqrt(ms + RMS_EPS) * ga_ref[...]).astype(BF16)
            return carry
        lax.fori_loop(0, n_chunks, norm, 0)
        a_scr[:, n_att:] = ssm_ref[0]

    mixed = _dot(a_scr[...], w_ref[...])
    col = pl.multiple_of(j * tn, tn)
    x1_ref[0, :, pl.ds(col, tn)] = alpha * x_ref[0] + (1.0 + gate_ref[0]) * mixed

    @pl.when(j == nj - 1)
    def _():
        def finish(c, carry):
            rs = pl.ds(pl.multiple_of(c * ROW_CHUNK, ROW_CHUNK), ROW_CHUNK)
            r = x1_ref[0, rs, :]
            mu = jnp.mean(r, axis=-1, keepdims=True)
            var = jnp.mean(jnp.square(r - mu), axis=-1, keepdims=True)
            x1 = (r - mu) * lax.rsqrt(var + LN_EPS) * lng_ref[...] + lnb_ref[...]
            x1_ref[0, rs, :] = x1
            h2 = x1 * (1.0 + sc2_ref[0]) + sh2_ref[0]
            hi = h2.astype(BF16)
            hi_f = hi.astype(F32)
            lo = (h2 - hi_f).astype(BF16)
            lg_ref[0, rs, :] = _dot(hi, wrh_ref[...]) + _dot(hi, wrl_ref[...]) + _dot(lo, wrh_ref[...])
            h2_ref[0, rs, :] = pack_bf16_pairs(hi_f)
            return carry
        lax.fori_loop(0, n_chunks, finish, 0)


def pack_bf16_pairs(x):
    n = x.shape[-1] // 2
    bits = lax.bitcast_convert_type(x, jnp.uint32)
    return bits[:, n:] | (bits[:, :n] >> 16)


def unpack_bf16_pairs(p):
    lo = lax.bitcast_convert_type(p << 16, F32).astype(BF16)
    hi = lax.bitcast_convert_type(p & jnp.uint32(0xFFFF0000), F32).astype(BF16)
    return lo, hi


def out_proj(attn, g_attn, ssm_n, w_out, x, gate1, ln_g, ln_b, scale2, shift2, wr_hi, wr_lo, alpha):
    bsz, seq, d = x.shape
    n_att = attn.shape[-1]
    n_ssm = ssm_n.shape[-1]
    k = n_att + n_ssm
    tm, tn = 512, 256
    row = lambda a: a.reshape(1, -1)
    full = lambda n: pl.BlockSpec((1, n), lambda b, i, j: (0, 0))
    return pl.pallas_call(
        functools.partial(_outproj_kernel, alpha=alpha, n_att=n_att),
        grid=(bsz, seq // tm, d // tn),
        in_specs=[pl.BlockSpec((1, tm, n_att), lambda b, i, j: (b, i, 0)),
                  full(n_att),
                  pl.BlockSpec((1, tm, n_ssm), lambda b, i, j: (b, i, 0)),
                  pl.BlockSpec((k, tn), lambda b, i, j: (0, j)),
                  pl.BlockSpec((1, tm, tn), lambda b, i, j: (b, i, j)),
                  pl.BlockSpec((1, 1, tn), lambda b, i, j: (b, 0, j)),
                  full(d), full(d),
                  pl.BlockSpec((1, 1, d), lambda b, i, j: (b, 0, 0)),
                  pl.BlockSpec((1, 1, d), lambda b, i, j: (b, 0, 0)),
                  pl.BlockSpec((d, LANES), lambda b, i, j: (0, 0)),
                  pl.BlockSpec((d, LANES), lambda b, i, j: (0, 0))],
        out_specs=[pl.BlockSpec((1, tm, d), lambda b, i, j: (b, i, 0)),
                   pl.BlockSpec((1, tm, d // 2), lambda b, i, j: (b, i, 0)),
                   pl.BlockSpec((1, tm, LANES), lambda b, i, j: (b, i, 0))],
        out_shape=[jax.ShapeDtypeStruct((bsz, seq, d), F32),
                   jax.ShapeDtypeStruct((bsz, seq, d // 2), jnp.uint32),
                   jax.ShapeDtypeStruct((bsz, seq, LANES), F32)],
        scratch_shapes=[pltpu.VMEM((tm, k), BF16)],
        compiler_params=_params(("arbitrary", "arbitrary", "arbitrary"), 56),
        name="out_proj",
    )(attn, row(g_attn), ssm_n, w_out, x, gate1, row(ln_g), row(ln_b), scale2, shift2, wr_hi, wr_lo)


def _moe_kernel(tok_ref, bexp_ref, nsub_ref, nact_ref, h_hbm, wg_ref, wu_ref, wd_ref, y_ref, xbuf, sem):
    i = pl.program_id(0)
    nact = nact_ref[0]
    rows = xbuf.shape[1]

    def start_gather(blk, slot):
        n_trips = nsub_ref[blk] * (MOE_SUB // GATHER_UNROLL)

        def body(t, carry):
            for k in range(GATHER_UNROLL):
                r = t * GATHER_UNROLL + k
                tok = tok_ref[blk * rows + r]
                pltpu.make_async_copy(h_hbm.at[pl.ds(tok, 1)], xbuf.at[slot, pl.ds(r, 1)], sem.at[slot]).start()
            return carry
        lax.fori_loop(0, n_trips, body, 0)

    def run_block(slot, n):
        pltpu.make_async_copy(h_hbm.at[pl.ds(0, n)], xbuf.at[slot, pl.ds(0, n)], sem.at[slot]).wait()
        x_lo, x_hi = unpack_bf16_pairs(xbuf[slot, :n])
        half = x_lo.shape[1]
        g = _dot(x_lo, wg_ref[0, :half, :]) + _dot(x_hi, wg_ref[0, half:, :])
        u = _dot(x_lo, wu_ref[0, :half, :]) + _dot(x_hi, wu_ref[0, half:, :])
        act = (jax.nn.silu(g) * u).astype(BF16)
        y_ref[:n, :] = _dot(act, wd_ref[0])
        if n < rows:
            y_ref[n:, :] = jnp.zeros((rows - n, y_ref.shape[1]), F32)

    @pl.when(jnp.logical_and(i == 0, nact > 0))
    def _():
        start_gather(0, 0)

    nxt = jnp.maximum(jnp.minimum(i + 1, nact - 1), 0)
    for nxt_slot in range(2):
        for piece in range(rows // MOE_SUB):
            @pl.when(jnp.logical_and(jnp.logical_and(i + 1 < nact, (i + 1) % 2 == nxt_slot),
                                     piece < nsub_ref[nxt]))
            def _():
                first = (i + 1) * rows
                for r in range(piece * MOE_SUB, (piece + 1) * MOE_SUB):
                    pltpu.make_async_copy(h_hbm.at[pl.ds(tok_ref[first + r], 1)], xbuf.at[nxt_slot, pl.ds(r, 1)],
                                          sem.at[nxt_slot]).start()

    for n_sub in range(1, rows // MOE_SUB + 1):
        @pl.when(jnp.logical_and(i < nact, nsub_ref[i] == n_sub))
        def _():
            run_block(i % 2, n_sub * MOE_SUB)

    @pl.when(i >= nact)
    def _():
        y_ref[...] = jnp.zeros_like(y_ref)


def moe_experts(h2, tok_buf, block_expert, short_blocks, n_active, w_gate, w_up, w_down):
    t, dp = h2.shape
    d = 2 * dp
    n_rows = tok_buf.shape[0]
    rows = MOE_ROWS
    n_blocks = n_rows // rows
    de = w_gate.shape[-1]

    def wmap(i, tok, bexp, short, nact):
        return (bexp[jnp.minimum(i, jnp.maximum(nact[0] - 1, 0))], 0, 0)

    grid_spec = pltpu.PrefetchScalarGridSpec(
        num_scalar_prefetch=4,
        grid=(n_blocks,),
        in_specs=[pl.BlockSpec(memory_space=pl.ANY),
                  pl.BlockSpec((1, d, de), wmap),
                  pl.BlockSpec((1, d, de), wmap),
                  pl.BlockSpec((1, de, d), wmap)],
        out_specs=pl.BlockSpec((rows, d), lambda i, *_: (i, 0)),
        scratch_shapes=[pltpu.VMEM((2, rows, dp), jnp.uint32), pltpu.SemaphoreType.DMA((2,))],
    )
    return pl.pallas_call(
        _moe_kernel,
        grid_spec=grid_spec,
        out_shape=jax.ShapeDtypeStruct((n_rows, d), F32),
        compiler_params=_params(("arbitrary",), 56),
        name="moe_experts",
    )(tok_buf, block_expert, short_blocks, n_active, h2, w_gate, w_up, w_down)


def _combine_kernel(dest_ref, ys_hbm, rt_ref, x1_ref, gate_ref, lng_ref, lnb_ref, o_ref, ybuf, sem, *, alpha):
    i = pl.program_id(0)
    n = pl.num_programs(0)
    tm = x1_ref.shape[0]
    n_copies = TOP_K * tm

    def gather(blk, slot, start):
        if not start:
            pltpu.make_async_copy(ys_hbm.at[pl.ds(0, n_copies)], ybuf.at[slot], sem.at[slot]).wait()
            return

        def body(r, carry):
            row = dest_ref[blk * n_copies + r]
            pltpu.make_async_copy(ys_hbm.at[pl.ds(row, 1)], ybuf.at[slot, pl.ds(r, 1)], sem.at[slot]).start()
            return carry
        lax.fori_loop(0, n_copies, body, 0, unroll=GATHER_UNROLL)

    @pl.when(i == 0)
    def _():
        gather(0, 0, True)

    for nxt_slot in range(2):
        @pl.when(jnp.logical_and(i + 1 < n, (i + 1) % 2 == nxt_slot))
        def _():
            first = (i + 1) * n_copies
            for r in range(n_copies):
                pltpu.make_async_copy(ys_hbm.at[pl.ds(dest_ref[first + r], 1)], ybuf.at[nxt_slot, pl.ds(r, 1)],
                                      sem.at[nxt_slot]).start()

    slot = i % 2
    gather(i, slot, False)
    moe = rt_ref[:, ROUTE_W:ROUTE_W + 1] * ybuf[slot, :tm, :]
    for kk in range(1, TOP_K):
        moe = moe + rt_ref[:, ROUTE_W + kk:ROUTE_W + kk + 1] * ybuf[slot, kk * tm:(kk + 1) * tm, :]
    r = alpha * x1_ref[...] + (1.0 + gate_ref[0]) * moe
    mu = jnp.mean(r, axis=-1, keepdims=True)
    var = jnp.mean(jnp.square(r - mu), axis=-1, keepdims=True)
    o_ref[...] = (r - mu) * lax.rsqrt(var + LN_EPS) * lng_ref[...] + lnb_ref[...]


def moe_combine(ys, dest, table, x1, gate2, ln_g, ln_b, alpha, seq):
    t, d = x1.shape
    tm = 128
    tiles_per_seq = seq // tm
    grid_spec = pltpu.PrefetchScalarGridSpec(
        num_scalar_prefetch=1,
        grid=(t // tm,),
        in_specs=[pl.BlockSpec(memory_space=pl.ANY),
                  pl.BlockSpec((tm, LANES), lambda i, *_: (i, 0)),
                  pl.BlockSpec((tm, d), lambda i, *_: (i, 0)),
                  pl.BlockSpec((1, 1, d), lambda i, *_: (i // tiles_per_seq, 0, 0)),
                  pl.BlockSpec((1, d), lambda i, *_: (0, 0)),
                  pl.BlockSpec((1, d), lambda i, *_: (0, 0))],
        out_specs=pl.BlockSpec((tm, d), lambda i, *_: (i, 0)),
        scratch_shapes=[pltpu.VMEM((2, TOP_K * tm, d), F32), pltpu.SemaphoreType.DMA((2,))],
    )
    return pl.pallas_call(
        functools.partial(_combine_kernel, alpha=alpha),
        grid_spec=grid_spec,
        out_shape=jax.ShapeDtypeStruct((t, d), F32),
        compiler_params=_params(("arbitrary",), 32),
        name="moe_combine",
    )(dest, ys, table, x1, gate2, ln_g.reshape(1, d), ln_b.reshape(1, d))


def _route_kernel(lg_ref, b_ref, o_ref, cnt_ref, carry):
    i = pl.program_id(0)

    @pl.when(i == 0)
    def _():
        carry[...] = jnp.zeros_like(carry)

    tm = lg_ref.shape[0]
    x = lg_ref[...] + b_ref[...]
    lane = lax.broadcasted_iota(jnp.int32, (tm, LANES), 1)
    ninf = -jnp.inf

    def top(v):
        vmax = jnp.max(v, axis=1, keepdims=True)
        return vmax, jnp.min(jnp.where(v == vmax, lane, LANES), axis=1, keepdims=True)

    gmask = lane < N_EXPERT_GROUPS
    gmax, g_sel = top(jnp.where(gmask, x, ninf))
    p_group = 1.0 / jnp.sum(jnp.where(gmask, jnp.exp(x - gmax), 0.0), axis=1, keepdims=True)

    lo = N_EXPERT_GROUPS + g_sel * EXPERTS_PER_GROUP
    cur = jnp.where(jnp.logical_and(lane >= lo, lane < lo + EXPERTS_PER_GROUP), x, ninf)
    vals, idxs = [], []
    for _ in range(TOP_K):
        v, ix = top(cur)
        vals.append(v)
        idxs.append(ix)
        cur = jnp.where(lane == ix, ninf, cur)
    exps = [jnp.exp(v - vals[0]) for v in vals]
    den = functools.reduce(lambda a, c: a + c, exps)

    member = functools.reduce(jnp.logical_or, [lane == ix for ix in idxs])
    mf = jnp.where(member, 1.0, 0.0)
    r_i = lax.broadcasted_iota(jnp.int32, (tm, tm), 0)
    c_i = lax.broadcasted_iota(jnp.int32, (tm, tm), 1)
    before = _dot((c_i < r_i).astype(BF16), mf.astype(BF16)) + carry[...]
    carry[...] = carry[...] + jnp.sum(mf, axis=0, keepdims=True)
    cnt_ref[...] = carry[...]

    out = jnp.zeros((tm, LANES), F32)
    for k in range(TOP_K):
        rank = jnp.sum(jnp.where(lane == idxs[k], before, 0.0), axis=1, keepdims=True)
        out = jnp.where(lane == ROUTE_EID + k, (idxs[k] - N_EXPERT_GROUPS).astype(F32), out)
        out = jnp.where(lane == ROUTE_RANK + k, rank, out)
        out = jnp.where(lane == ROUTE_W + k, p_group * (exps[k] / den), out)
    o_ref[...] = out


def route(logits, b_rg, b_re, n_rows, tm_combine):
    n_tok = logits.shape[0]
    n_experts = N_EXPERT_GROUPS * EXPERTS_PER_GROUP
    bias = jnp.zeros((1, LANES), F32).at[0, :N_EXPERT_GROUPS].set(b_rg)
    bias = bias.at[0, N_EXPERT_GROUPS:N_EXPERT_GROUPS + n_experts].set(b_re)
    tm = 512
    table, cnt = pl.pallas_call(
        _route_kernel,
        grid=(n_tok // tm,),
        in_specs=[pl.BlockSpec((tm, LANES), lambda i: (i, 0)),
                  pl.BlockSpec((1, LANES), lambda i: (0, 0))],
        out_specs=[pl.BlockSpec((tm, LANES), lambda i: (i, 0)),
                   pl.BlockSpec((1, LANES), lambda i: (0, 0))],
        out_shape=[jax.ShapeDtypeStruct((n_tok, LANES), F32), jax.ShapeDtypeStruct((1, LANES), F32)],
        scratch_shapes=[pltpu.VMEM((1, LANES), F32)],
        compiler_params=_params(("arbitrary",), 16),
        name="route",
    )(logits, bias)

    eid = table[:, ROUTE_EID:ROUTE_EID + TOP_K].astype(jnp.int32)
    rank = table[:, ROUTE_RANK:ROUTE_RANK + TOP_K].astype(jnp.int32)
    counts = cnt[0, N_EXPERT_GROUPS:N_EXPERT_GROUPS + n_experts].astype(jnp.int32)
    padded = ((counts + MOE_ROWS - 1) // MOE_ROWS) * MOE_ROWS
    pends = jnp.cumsum(padded)
    pstarts = pends - padded
    onehot = eid[:, :, None] == jnp.arange(n_experts, dtype=jnp.int32)
    dest = jnp.sum(jnp.where(onehot, pstarts, 0), axis=-1) + rank
    tok = jnp.broadcast_to(jnp.arange(n_tok, dtype=jnp.int32)[:, None], (n_tok, TOP_K))
    tok_buf = jnp.zeros((n_rows,), jnp.int32).at[dest.reshape(-1)].set(tok.reshape(-1), unique_indices=True)
    n_blocks = n_rows // MOE_ROWS
    block_expert = jnp.clip(
        jnp.searchsorted(pends, jnp.arange(n_blocks, dtype=jnp.int32) * MOE_ROWS, side='right'),
        0, n_experts - 1).astype(jnp.int32)
    rows_left = (pstarts + counts)[block_expert] - jnp.arange(n_blocks, dtype=jnp.int32) * MOE_ROWS
    short_blocks = jnp.clip(-(-rows_left // MOE_SUB), 1, MOE_ROWS // MOE_SUB).astype(jnp.int32)
    n_active = (pends[-1] // MOE_ROWS).astype(jnp.int32).reshape(1)
    dest_tiles = dest.reshape(n_tok // tm_combine, tm_combine, TOP_K).transpose(0, 2, 1).reshape(-1)
    return table, tok_buf, block_expert, short_blocks, n_active, dest_tiles.astype(jnp.int32)


def _layer(x, c, w_ada, b_ada, w_in, b_forget, lam_re, lam_im, log_dt, b_re, b_im, c_re, c_im, d_skip,
           w_glu, b_glu, g_attn, g_ssm, w_out, ln1_g, ln1_b, w_rg, b_rg, w_re, b_re_r,
           w_gate, w_up, w_down, ln2_g, ln2_b, alpha):
    bsz, seq, d = x.shape
    n_tok = bsz * seq
    n_heads = b_forget.shape[0]
    d_att = n_heads * HEAD_DIM
    d_ssm = d_skip.shape[0] * d_skip.shape[1]

    mod = ada_mod(c, w_ada, b_ada).reshape(bsz, 1, -1)
    shift1, scale1, gate1, shift2, scale2, gate2 = jnp.split(mod, 6, axis=-1)

    n_qkv = 3 * d_att
    w_main = jnp.concatenate([w_in[:, :n_qkv], w_in[:, n_qkv + n_heads:]], axis=1).astype(BF16)
    w_f = jnp.zeros((d, LANES), BF16).at[:, :n_heads].set(w_in[:, n_qkv:n_qkv + n_heads].astype(BF16))
    qkv, u, f = in_proj(x, scale1, shift1, w_main, w_f, n_qkv)

    q_aug, k_aug = forget_cum(f, b_forget)
    attn, (w_glu_b, w_out_b, w_gate_b, w_up_b, w_down_b) = attention(
        qkv, q_aug, k_aug, n_heads, [w_glu, w_out, w_gate, w_up, w_down])

    w1, ft, al = ssm_prep(lam_re, lam_im, log_dt, b_re, b_im, c_re, c_im)
    y = ssm_scan(u, w1, ft, al, d_skip)
    ssm_n = glu_norm(y.reshape(n_tok, d_ssm), w_glu_b, b_glu, g_ssm).reshape(bsz, seq, d_ssm)

    n_experts = N_EXPERT_GROUPS * EXPERTS_PER_GROUP
    w_r = jnp.zeros((d, LANES), F32).at[:, :N_EXPERT_GROUPS].set(w_rg)
    w_r = w_r.at[:, N_EXPERT_GROUPS:N_EXPERT_GROUPS + n_experts].set(w_re)
    wr_hi = w_r.astype(BF16)
    wr_lo = (w_r - wr_hi.astype(F32)).astype(BF16)
    x1, h2, logits = out_proj(attn, g_attn, ssm_n, w_out_b, x, gate1, ln1_g, ln1_b,
                              scale2, shift2, wr_hi, wr_lo, alpha)

    n_assign = n_tok * TOP_K
    n_blocks = -(-(n_assign + n_experts * (MOE_ROWS - 1)) // MOE_ROWS)
    n_rows = n_blocks * MOE_ROWS
    table, tok_buf, block_expert, short_blocks, n_active, dest = route(
        logits.reshape(n_tok, LANES), b_rg, b_re_r, n_rows, 128)
    ys = moe_experts(h2.reshape(n_tok, d // 2), tok_buf, block_expert, short_blocks, n_active,
                     w_gate_b, w_up_b, w_down_b)
    out = moe_combine(ys, dest, table, x1.reshape(n_tok, d), gate2, ln2_g, ln2_b, alpha, seq)
    return out.reshape(bsz, seq, d)


def kernel(x, c, w_ada, b_ada, w_in, b_forget, ssm_lambda_re, ssm_lambda_im, ssm_log_dt, ssm_b_re, ssm_b_im,
           ssm_c_re, ssm_c_im, ssm_d, w_glu, b_glu, g_attn, g_ssm, w_out, ln1_g, ln1_b, w_router_group,
           b_router_group, w_router_expert, b_router_expert, w_gate, w_up, w_down, ln2_g, ln2_b):
    depth = w_ada.shape[0]
    alpha = (2.0 * depth) ** 0.25
    for l in range(depth):
        x = _layer(x, c, w_ada[l], b_ada[l], w_in[l], b_forget[l], ssm_lambda_re[l], ssm_lambda_im[l],
                   ssm_log_dt[l], ssm_b_re[l], ssm_b_im[l], ssm_c_re[l], ssm_c_im[l], ssm_d[l],
                   w_glu[l], b_glu[l], g_attn[l], g_ssm[l], w_out[l], ln1_g[l], ln1_b[l],
                   w_router_group[l], b_router_group[l], w_router_expert[l], b_router_expert[l],
                   w_gate[l], w_up[l], w_down[l], ln2_g[l], ln2_b[l], alpha)
    return x
```

```python
import functools
import math

import jax
import jax.numpy as jnp
from jax import lax
from jax.experimental import pallas as pl
from jax.experimental.pallas import tpu as pltpu

F32 = jnp.float32
BF16 = jnp.bfloat16

LANES = 128
HEAD_DIM = 128
SSM_GROUP = 16
SSM_STATE = 64
GROUPS_PER_SLAB = LANES // SSM_GROUP
SLAB_STATE = GROUPS_PER_SLAB * SSM_STATE
SSM_CHUNK = 16
SSM_COLS = 512
N_EXPERT_GROUPS = 8
EXPERTS_PER_GROUP = 8
TOP_K = 2
MOE_ROWS = 256
MOE_SUB = 32
GATHER_UNROLL = 8
ROW_CHUNK = 128
ROUTE_EID, ROUTE_RANK, ROUTE_W = 0, TOP_K, 2 * TOP_K
LN_EPS = 1e-5
RMS_EPS = 1e-6
NEG_BIG = -1e30
LOG2E = math.log2(math.e)
MIB = 1024 * 1024

_NT = (((1,), (1,)), ((), ()))


def _params(semantics, vmem_mib):
    return pltpu.CompilerParams(dimension_semantics=semantics, vmem_limit_bytes=vmem_mib * MIB)


def _dot(a, b):
    return jnp.dot(a, b, preferred_element_type=F32)


def _dot_nt(a, b):
    return lax.dot_general(a, b, _NT, preferred_element_type=F32)


def _split2(x):
    hi = x.astype(BF16)
    return hi, (x - hi.astype(F32)).astype(BF16)


def _ada_kernel(c_ref, w_ref, b_ref, o_ref):
    s = jax.nn.silu(c_ref[...]).astype(BF16)
    o_ref[...] = _dot(s, w_ref[...].astype(BF16)) + b_ref[...]


def ada_mod(c, w_ada, b_ada):
    bsz, d = c.shape
    n = w_ada.shape[1]
    rows = 8
    assert bsz <= rows
    cp = jnp.zeros((rows, d), F32).at[:bsz].set(c)
    tn = 512
    out = pl.pallas_call(
        _ada_kernel,
        grid=(n // tn,),
        in_specs=[pl.BlockSpec((rows, d), lambda j: (0, 0)),
                  pl.BlockSpec((d, tn), lambda j: (0, j)),
                  pl.BlockSpec((1, tn), lambda j: (0, j))],
        out_specs=pl.BlockSpec((rows, tn), lambda j: (0, j)),
        out_shape=jax.ShapeDtypeStruct((rows, n), F32),
        compiler_params=_params(("arbitrary",), 40),
        name="ada_mod",
    )(cp, w_ada, b_ada.reshape(1, n))
    return out[:bsz]


def _inproj_kernel(x_ref, sc_ref, sh_ref, w_ref, wf_ref, qkv_ref, u_ref, f_ref, h_scr, *,
                   n_q_tiles, n_qkv_tiles, q_scale):
    j = pl.program_id(2)

    @pl.when(j == 0)
    def _():
        hb = (x_ref[0] * (1.0 + sc_ref[0]) + sh_ref[0]).astype(BF16)
        h_scr[...] = hb
        f_ref[0] = _dot(hb, wf_ref[...])

    acc = _dot(h_scr[...], w_ref[...])

    @pl.when(j < n_q_tiles)
    def _():
        qkv_ref[0] = (acc * q_scale).astype(BF16)

    @pl.when(jnp.logical_and(j >= n_q_tiles, j < n_qkv_tiles))
    def _():
        qkv_ref[0] = acc.astype(BF16)

    @pl.when(j >= n_qkv_tiles)
    def _():
        u_ref[0] = acc


def in_proj(x, scale, shift, w_main, w_f, n_qkv):
    bsz, seq, d = x.shape
    n_all = w_main.shape[1]
    n_u = n_all - n_qkv
    tm, tn = 512, 1024
    nq = n_qkv // tn
    grid = (bsz, seq // tm, n_all // tn)
    return pl.pallas_call(
        functools.partial(_inproj_kernel, n_q_tiles=n_qkv // 3 // tn, n_qkv_tiles=nq,
                          q_scale=HEAD_DIM ** -0.5 * LOG2E),
        grid=grid,
        in_specs=[pl.BlockSpec((1, tm, d), lambda b, i, j: (b, i, 0)),
                  pl.BlockSpec((1, 1, d), lambda b, i, j: (b, 0, 0)),
                  pl.BlockSpec((1, 1, d), lambda b, i, j: (b, 0, 0)),
                  pl.BlockSpec((d, tn), lambda b, i, j: (0, j)),
                  pl.BlockSpec((d, LANES), lambda b, i, j: (0, 0))],
        out_specs=[pl.BlockSpec((1, tm, tn), lambda b, i, j: (b, i, jnp.minimum(j, nq - 1))),
                   pl.BlockSpec((1, tm, tn), lambda b, i, j: (b, i, jnp.maximum(j - nq, 0))),
                   pl.BlockSpec((1, tm, LANES), lambda b, i, j: (b, i, 0))],
        out_shape=[jax.ShapeDtypeStruct((bsz, seq, n_qkv), BF16),
                   jax.ShapeDtypeStruct((bsz, seq, n_u), F32),
                   jax.ShapeDtypeStruct((bsz, seq, LANES), F32)],
        scratch_shapes=[pltpu.VMEM((tm, d), BF16)],
        compiler_params=_params(("arbitrary", "arbitrary", "arbitrary"), 52),
        name="in_proj",
    )(x, scale, shift, w_main, w_f)


def _split3(x):
    p1 = x.astype(BF16)
    r1 = x - p1.astype(F32)
    p2 = r1.astype(BF16)
    p3 = (r1 - p2.astype(F32)).astype(BF16)
    return p1, p2, p3


N_PIECES = 3


def _cum_kernel(f_ref, b_ref, qa_ref, ka_ref, carry, *, n_heads):
    i = pl.program_id(1)

    @pl.when(i == 0)
    def _():
        carry[...] = jnp.zeros_like(carry)

    tc = f_ref.shape[1]
    lf = jax.nn.log_sigmoid(f_ref[0] + b_ref[...])
    row = lax.broadcasted_iota(jnp.int32, (tc, tc), 0)
    col = lax.broadcasted_iota(jnp.int32, (tc, tc), 1)
    tri = (col <= row).astype(BF16)
    p1, p2, p3 = _split3(lf)
    cs = _dot(tri, p1) + _dot(tri, p2) + _dot(tri, p3) + carry[...]
    carry[...] = cs[tc - 1:tc, :]

    pieces = jnp.concatenate(_split3(cs * LOG2E), axis=1)
    r = lax.broadcasted_iota(jnp.int32, (N_PIECES * LANES, LANES), 0)
    c = lax.broadcasted_iota(jnp.int32, (N_PIECES * LANES, LANES), 1)
    lane = lax.broadcasted_iota(jnp.int32, (tc, LANES), 1)
    ones_q = jnp.where(jnp.logical_and(lane >= N_PIECES, lane < 2 * N_PIECES), 1.0, 0.0)
    ones_k = jnp.where(lane < N_PIECES, 1.0, 0.0)
    for h in range(n_heads):
        sel_q = (r == c * LANES + h).astype(BF16)
        sel_k = (r == (c - N_PIECES) * LANES + h).astype(BF16)
        picked = _dot(pieces, jnp.concatenate([sel_q, sel_k], axis=1))
        qa_ref[0, h] = (picked[:, :LANES] + ones_q).astype(BF16)
        ka_ref[0, h] = (ones_k - picked[:, LANES:]).astype(BF16)


def forget_cum(f, b_forget):
    bsz, seq, _ = f.shape
    n_heads = b_forget.shape[0]
    tc = 512
    bpad = jnp.zeros((1, LANES), F32).at[0, :n_heads].set(b_forget)
    out_spec = pl.BlockSpec((1, n_heads, tc, LANES), lambda b, i: (b, 0, i, 0))
    out_shape = jax.ShapeDtypeStruct((bsz, n_heads, seq, LANES), BF16)
    return pl.pallas_call(
        functools.partial(_cum_kernel, n_heads=n_heads),
        grid=(bsz, seq // tc),
        in_specs=[pl.BlockSpec((1, tc, LANES), lambda b, i: (b, i, 0)),
                  pl.BlockSpec((1, LANES), lambda b, i: (0, 0))],
        out_specs=[out_spec, out_spec],
        out_shape=[out_shape, out_shape],
        scratch_shapes=[pltpu.VMEM((1, LANES), F32)],
        compiler_params=_params(("arbitrary", "arbitrary"), 24),
        name="forget_cum",
    )(f, bpad)


ATT_QROWS = 1024
ATT_KEYS = 512


def _attn_kernel(*refs, n_cast):
    q_ref, qa_ref, k_ref, ka_ref, v_ref = refs[:5]
    src_refs = refs[5:5 + n_cast]
    o_ref = refs[5 + n_cast]
    dst_refs = refs[6 + n_cast:6 + 2 * n_cast]
    m_scr, acc_scr, s_scr = refs[6 + 2 * n_cast:]
    for src, dst in zip(src_refs, dst_refs):
        dst[...] = src[...].astype(BF16)

    tq = q_ref.shape[1]
    tk = ATT_KEYS
    n_groups = tq // tk
    qi = pl.program_id(2)
    q = jnp.concatenate([q_ref[0], qa_ref[0, 0]], axis=1)
    ones = jnp.ones((tk, HEAD_DIM), BF16)

    m_scr[...] = jnp.full_like(m_scr, NEG_BIG)
    acc_scr[...] = jnp.zeros_like(acc_scr)

    def scores(kb, slot, first_group=0):
        k0 = pl.multiple_of(kb * tk, tk)
        kt = jnp.concatenate([k_ref[0, pl.ds(k0, tk), :], ka_ref[0, 0, pl.ds(k0, tk), :]], axis=1)
        r0 = first_group * tk
        s_scr[slot, r0:, :] = lax.dot_general(q[r0:], kt, _NT, preferred_element_type=F32)

    def update(kb, slot, diag_group=None):
        k0 = pl.multiple_of(kb * tk, tk)
        vt = jnp.concatenate([v_ref[0, pl.ds(k0, tk), :], ones], axis=1)
        for g in range(n_groups):
            if diag_group is not None and g < diag_group:
                continue
            rs = slice(g * tk, (g + 1) * tk)
            s = s_scr[slot, rs, :]
            if g == diag_group:
                qpos = lax.broadcasted_iota(jnp.int32, (tk, tk), 0)
                kpos = lax.broadcasted_iota(jnp.int32, (tk, tk), 1)
                s = jnp.where(kpos <= qpos, s, NEG_BIG)
            m_prev = m_scr[rs]
            m_new = jnp.maximum(m_prev, jnp.max(s, axis=1, keepdims=True))
            alpha = jnp.exp2(m_prev - m_new)
            p = jnp.exp2(s - m_new)
            acc_scr[rs] = alpha * acc_scr[rs] + _dot(p.astype(BF16), vt)
            m_scr[rs] = m_new

    def body(i, carry):
        kb = n_groups * i
        for g in range(n_groups):
            scores(kb + g + 1, (g + 1) % 2)
            update(kb + g, g % 2)
        return carry

    assert n_groups % 2 == 0
    scores(0, 0)
    lax.fori_loop(0, qi, body, 0)
    kb = n_groups * qi
    for g in range(n_groups):
        if g + 1 < n_groups:
            scores(kb + g + 1, (g + 1) % 2, first_group=g + 1)
        update(kb + g, g % 2, diag_group=g)

    o_ref[0] = acc_scr[:, :HEAD_DIM] / acc_scr[:, HEAD_DIM:]


BF16_ROWS = 16


def _cast_chunks(w, n_steps):
    cols = w.shape[-1]
    total_rows = w.size // cols
    n_chunks = n_steps
    while total_rows % (n_chunks * BF16_ROWS):
        n_chunks //= 2
    return w.reshape(n_chunks, total_rows // n_chunks, cols)


def attention(qkv, q_aug, k_aug, n_heads, cast_weights):
    bsz, seq, _ = qkv.shape
    t = min(ATT_QROWS, seq)
    nq = seq // t
    n_steps = bsz * n_heads * nq
    srcs = [_cast_chunks(w, n_steps) for w in cast_weights]

    def chunk_spec(a):
        per = n_steps // a.shape[0]
        return pl.BlockSpec((1,) + a.shape[1:], lambda b, h, i: (((b * n_heads + h) * nq + i) // per, 0, 0))

    outs = pl.pallas_call(
        functools.partial(_attn_kernel, n_cast=len(srcs)),
        grid=(bsz, n_heads, nq),
        in_specs=[pl.BlockSpec((1, t, HEAD_DIM), lambda b, h, i: (b, i, h)),
                  pl.BlockSpec((1, 1, t, LANES), lambda b, h, i: (b, h, i, 0)),
                  pl.BlockSpec((1, seq, HEAD_DIM), lambda b, h, i: (b, 0, n_heads + h)),
                  pl.BlockSpec((1, 1, seq, LANES), lambda b, h, i: (b, h, 0, 0)),
                  pl.BlockSpec((1, seq, HEAD_DIM), lambda b, h, i: (b, 0, 2 * n_heads + h))]
                 + [chunk_spec(a) for a in srcs],
        out_specs=[pl.BlockSpec((1, t, HEAD_DIM), lambda b, h, i: (b, i, h))] + [chunk_spec(a) for a in srcs],
        out_shape=[jax.ShapeDtypeStruct((bsz, seq, n_heads * HEAD_DIM), F32)]
                  + [jax.ShapeDtypeStruct(a.shape, BF16) for a in srcs],
        scratch_shapes=[pltpu.VMEM((t, 1), F32), pltpu.VMEM((t, 2 * HEAD_DIM), F32),
                        pltpu.VMEM((2, t, ATT_KEYS), F32)],
        compiler_params=_params(("arbitrary", "arbitrary", "arbitrary"), 56),
        name="attention",
    )(qkv, q_aug, qkv, k_aug, qkv, *srcs)
    return outs[0], [o.reshape(w.shape) for o, w in zip(outs[1:], cast_weights)]


def _blockdiag(p):
    g, c, n = p.shape
    ns = g // GROUPS_PER_SLAB
    eye = jnp.eye(GROUPS_PER_SLAB, dtype=p.dtype)
    out = p.reshape(ns, GROUPS_PER_SLAB, c, 1, n) * eye[None, :, None, :, None]
    return out.reshape(ns, GROUPS_PER_SLAB * c, GROUPS_PER_SLAB * n)


def _ssm_prep_kernel(lr_ref, li_ref, ldt_ref, bre_ref, bim_ref, cre_ref, cim_ref, w1_ref, ft_ref, al_ref):
    L = SSM_CHUNK
    lr = lr_ref[0]
    li = li_ref[0]
    dt = jnp.exp(ldt_ref[0])
    mag = jnp.exp(lr * dt)
    a_re = mag * jnp.cos(li * dt)
    a_im = mag * jnp.sin(li * dt)
    den = lr * lr + li * li
    z_re = ((a_re - 1.0) * lr + a_im * li) / den
    z_im = (a_im * lr - (a_re - 1.0) * li) / den
    br = bre_ref[0]
    bi = bim_ref[0]
    bb_re = z_re * br - z_im * bi
    bb_im = z_re * bi + z_im * br
    cr = cre_ref[0]
    ci = cim_ref[0]
    ft0_hi, ft0_lo = _split2(jnp.concatenate([cr, -ci], axis=1))

    def power(d):
        m = jnp.exp(lr * dt * d)
        return m * jnp.cos(li * dt * d), m * jnp.sin(li * dt * d)

    w1_ref[0, :, :L * LANES] = jnp.zeros((L * LANES, L * LANES), BF16)
    for d in range(L):
        pr, pi = power(float(d))
        xe = jnp.concatenate([bb_re * pr - bb_im * pi, bb_re * pi + bb_im * pr], axis=1)
        j = L - 1 - d
        w1_ref[0, j * LANES:(j + 1) * LANES, L * LANES:] = xe.astype(BF16)
        xe_hi, xe_lo = _split2(xe)
        m_d = (_dot_nt(xe_hi, ft0_hi) + _dot_nt(xe_hi, ft0_lo) + _dot_nt(xe_lo, ft0_hi)).astype(BF16)
        for jj in range(L - d):
            w1_ref[0, jj * LANES:(jj + 1) * LANES, (jj + d) * LANES:(jj + d + 1) * LANES] = m_d
        pr1, pi1 = power(float(d + 1))
        ft_ref[0, d * LANES:(d + 1) * LANES, :] = jnp.concatenate(
            [cr * pr1 - ci * pi1, -(cr * pi1 + ci * pr1)], axis=1).astype(BF16)
    prl, pil = power(float(L))
    al_ref[0] = jnp.concatenate([prl, pil], axis=1)


def ssm_prep(lam_re, lam_im, log_dt, b_re, b_im, c_re, c_im):
    g, n = lam_re.shape
    ns = g // GROUPS_PER_SLAB
    L = SSM_CHUNK
    rowvec = lambda a: a.reshape(ns, 1, SLAB_STATE)
    args = (rowvec(lam_re), rowvec(lam_im), rowvec(jnp.repeat(log_dt, n)),
            _blockdiag(b_re.transpose(0, 2, 1)), _blockdiag(b_im.transpose(0, 2, 1)),
            _blockdiag(c_re), _blockdiag(c_im))
    vec_spec = pl.BlockSpec((1, 1, SLAB_STATE), lambda s: (s, 0, 0))
    mat_spec = pl.BlockSpec((1, LANES, SLAB_STATE), lambda s: (s, 0, 0))
    return pl.pallas_call(
        _ssm_prep_kernel,
        grid=(ns,),
        in_specs=[vec_spec] * 3 + [mat_spec] * 4,
        out_specs=[pl.BlockSpec((1, L * LANES, L * LANES + 2 * SLAB_STATE), lambda s: (s, 0, 0)),
                   pl.BlockSpec((1, L * LANES, 2 * SLAB_STATE), lambda s: (s, 0, 0)),
                   pl.BlockSpec((1, 1, 2 * SLAB_STATE), lambda s: (s, 0, 0))],
        out_shape=[jax.ShapeDtypeStruct((ns, L * LANES, L * LANES + 2 * SLAB_STATE), BF16),
                   jax.ShapeDtypeStruct((ns, L * LANES, 2 * SLAB_STATE), BF16),
                   jax.ShapeDtypeStruct((ns, 1, 2 * SLAB_STATE), F32)],
        compiler_params=_params(("arbitrary",), 48),
        name="ssm_prep",
    )(*args)


def _ssm_kernel(u_ref, w1_ref, ft_ref, al_ref, d_ref, y_ref, uf_scr, e_scr, y_scr):
    L = SSM_CHUNK
    nch = uf_scr.shape[0]
    lc = L * LANES
    for j in range(L):
        uf_scr[:, j * LANES:(j + 1) * LANES] = u_ref[0, pl.ds(j, nch, stride=L), :].astype(BF16)
    uf = uf_scr[...]
    e_scr[...] = _dot(uf, w1_ref[0, :, lc:])

    a_re = al_ref[0, :, :SLAB_STATE]
    a_im = al_ref[0, :, SLAB_STATE:]
    h_re = jnp.zeros((1, SLAB_STATE), F32)
    h_im = jnp.zeros((1, SLAB_STATE), F32)
    for k in range(nch):
        e = e_scr[k:k + 1, :]
        e_scr[k:k + 1, :] = jnp.concatenate([h_re, h_im], axis=1)
        h_re, h_im = (a_re * h_re - a_im * h_im + e[:, :SLAB_STATE],
                      a_re * h_im + a_im * h_re + e[:, SLAB_STATE:])

    for c in range(lc // SSM_COLS):
        k_hi = (c + 1) * SSM_COLS
        cols = slice(c * SSM_COLS, k_hi)
        y_scr[:, cols] = _dot(uf[:, :k_hi], w1_ref[0, :k_hi, cols])
    y = y_scr[...] + lax.dot_general(e_scr[...].astype(BF16), ft_ref[0], _NT, preferred_element_type=F32)
    for i in range(L):
        yi = y[:, i * LANES:(i + 1) * LANES] + d_ref[0] * u_ref[0, pl.ds(i, nch, stride=L), :]
        y_ref[0, pl.ds(i, nch, stride=L), :] = jax.nn.gelu(yi)


def ssm_scan(u, w1, ft, al, d_skip):
    bsz, seq, c = u.shape
    ns = c // LANES
    L = SSM_CHUNK
    nch = seq // L
    lc = L * LANES
    return pl.pallas_call(
        _ssm_kernel,
        grid=(ns, bsz),
        in_specs=[pl.BlockSpec((1, seq, LANES), lambda s, b: (b, 0, s)),
                  pl.BlockSpec((1, lc, lc + 2 * SLAB_STATE), lambda s, b: (s, 0, 0)),
                  pl.BlockSpec((1, lc, 2 * SLAB_STATE), lambda s, b: (s, 0, 0)),
                  pl.BlockSpec((1, 1, 2 * SLAB_STATE), lambda s, b: (s, 0, 0)),
                  pl.BlockSpec((1, 1, LANES), lambda s, b: (s, 0, 0))],
        out_specs=pl.BlockSpec((1, seq, LANES), lambda s, b: (b, 0, s)),
        out_shape=jax.ShapeDtypeStruct((bsz, seq, c), F32),
        scratch_shapes=[pltpu.VMEM((nch, lc), BF16), pltpu.VMEM((nch, 2 * SLAB_STATE), F32),
                        pltpu.VMEM((nch, lc), F32)],
        compiler_params=_params(("arbitrary", "arbitrary"), 60),
        name="ssm_scan",
    )(u, w1, ft, al, d_skip.reshape(ns, 1, LANES))


def _glu_kernel(y_ref, w_ref, b_ref, g_ref, o_ref):
    y = y_ref[...]
    o = y * jax.nn.sigmoid(_dot(y.astype(BF16), w_ref[...]) + b_ref[...])
    ms = jnp.mean(o * o, axis=-1, keepdims=True)
    o_ref[...] = (o * lax.rsqrt(ms + RMS_EPS) * g_ref[...]).astype(BF16)


def glu_norm(y, w_glu, b_glu, g):
    t, c = y.shape
    tm = 512
    return pl.pallas_call(
        _glu_kernel,
        grid=(t // tm,),
        in_specs=[pl.BlockSpec((tm, c), lambda i: (i, 0)),
                  pl.BlockSpec((c, c), lambda i: (0, 0)),
                  pl.BlockSpec((1, c), lambda i: (0, 0)),
                  pl.BlockSpec((1, c), lambda i: (0, 0))],
        out_specs=pl.BlockSpec((tm, c), lambda i: (i, 0)),
        out_shape=jax.ShapeDtypeStruct((t, c), BF16),
        compiler_params=_params(("arbitrary",), 48),
        name="glu_norm",
    )(y, w_glu, b_glu.reshape(1, c), g.reshape(1, c))


def _outproj_kernel(attn_ref, ga_ref, ssm_ref, w_ref, x_ref, gate_ref, lng_ref, lnb_ref, sc2_ref, sh2_ref,
                    wrh_ref, wrl_ref, x1_ref, h2_ref, lg_ref, a_scr, *, alpha, n_att):
    j = pl.program_id(2)
    nj = pl.num_programs(2)
    tn = w_ref.shape[1]

    tm = x1_ref.shape[1]
    n_chunks = tm // ROW_CHUNK

    @pl.when(j == 0)
    def _():
        def norm(c, carry):
            rs = pl.ds(pl.multiple_of(c * ROW_CHUNK, ROW_CHUNK), ROW_CHUNK)
            a = attn_ref[0, rs, :]
            ms = jnp.mean(a * a, axis=-1, keepdims=True)
            a_scr[rs, :n_att] = (a * lax.rsqrt(ms + RMS_EPS) * ga_ref[...]).astype(BF16)
            return carry
        lax.fori_loop(0, n_chunks, norm, 0)
        a_scr[:, n_att:] = ssm_ref[0]

    mixed = _dot(a_scr[...], w_ref[...])
    col = pl.multiple_of(j * tn, tn)
    x1_ref[0, :, pl.ds(col, tn)] = alpha * x_ref[0] + (1.0 + gate_ref[0]) * mixed

    @pl.when(j == nj - 1)
    def _():
        def finish(c, carry):
            rs = pl.ds(pl.multiple_of(c * ROW_CHUNK, ROW_CHUNK), ROW_CHUNK)
            r = x1_ref[0, rs, :]
            mu = jnp.mean(r, axis=-1, keepdims=True)
            var = jnp.mean(jnp.square(r - mu), axis=-1, keepdims=True)
            x1 = (r - mu) * lax.rsqrt(var + LN_EPS) * lng_ref[...] + lnb_ref[...]
            x1_ref[0, rs, :] = x1
            h2 = x1 * (1.0 + sc2_ref[0]) + sh2_ref[0]
            hi = h2.astype(BF16)
            hi_f = hi.astype(F32)
            lo = (h2 - hi_f).astype(BF16)
            lg_ref[0, rs, :] = _dot(hi, wrh_ref[...]) + _dot(hi, wrl_ref[...]) + _dot(lo, wrh_ref[...])
            h2_ref[0, rs, :] = pack_bf16_pairs(hi_f)
            return carry
        lax.fori_loop(0, n_chunks, finish, 0)


def pack_bf16_pairs(x):
    n = x.shape[-1] // 2
    bits = lax.bitcast_convert_type(x, jnp.uint32)
    return bits[:, n:] | (bits[:, :n] >> 16)


def unpack_bf16_pairs(p):
    lo = lax.bitcast_convert_type(p << 16, F32).astype(BF16)
    hi = lax.bitcast_convert_type(p & jnp.uint32(0xFFFF0000), F32).astype(BF16)
    return lo, hi


def out_proj(attn, g_attn, ssm_n, w_out, x, gate1, ln_g, ln_b, scale2, shift2, wr_hi, wr_lo, alpha):
    bsz, seq, d = x.shape
    n_att = attn.shape[-1]
    n_ssm = ssm_n.shape[-1]
    k = n_att + n_ssm
    tm, tn = 512, 256
    row = lambda a: a.reshape(1, -1)
    full = lambda n: pl.BlockSpec((1, n), lambda b, i, j: (0, 0))
    return pl.pallas_call(
        functools.partial(_outproj_kernel, alpha=alpha, n_att=n_att),
        grid=(bsz, seq // tm, d // tn),
        in_specs=[pl.BlockSpec((1, tm, n_att), lambda b, i, j: (b, i, 0)),
                  full(n_att),
                  pl.BlockSpec((1, tm, n_ssm), lambda b, i, j: (b, i, 0)),
                  pl.BlockSpec((k, tn), lambda b, i, j: (0, j)),
                  pl.BlockSpec((1, tm, tn), lambda b, i, j: (b, i, j)),
                  pl.BlockSpec((1, 1, tn), lambda b, i, j: (b, 0, j)),
                  full(d), full(d),
                  pl.BlockSpec((1, 1, d), lambda b, i, j: (b, 0, 0)),
                  pl.BlockSpec((1, 1, d), lambda b, i, j: (b, 0, 0)),
                  pl.BlockSpec((d, LANES), lambda b, i, j: (0, 0)),
                  pl.BlockSpec((d, LANES), lambda b, i, j: (0, 0))],
        out_specs=[pl.BlockSpec((1, tm, d), lambda b, i, j: (b, i, 0)),
                   pl.BlockSpec((1, tm, d // 2), lambda b, i, j: (b, i, 0)),
                   pl.BlockSpec((1, tm, LANES), lambda b, i, j: (b, i, 0))],
        out_shape=[jax.ShapeDtypeStruct((bsz, seq, d), F32),
                   jax.ShapeDtypeStruct((bsz, seq, d // 2), jnp.uint32),
                   jax.ShapeDtypeStruct((bsz, seq, LANES), F32)],
        scratch_shapes=[pltpu.VMEM((tm, k), BF16)],
        compiler_params=_params(("arbitrary", "arbitrary", "arbitrary"), 56),
        name="out_proj",
    )(attn, row(g_attn), ssm_n, w_out, x, gate1, row(ln_g), row(ln_b), scale2, shift2, wr_hi, wr_lo)


def _moe_kernel(tok_ref, bexp_ref, nsub_ref, nact_ref, h_hbm, wg_ref, wu_ref, wd_ref, y_ref, xbuf, sem):
    i = pl.program_id(0)
    nact = nact_ref[0]
    rows = xbuf.shape[1]

    def start_gather(blk, slot):
        n_trips = nsub_ref[blk] * (MOE_SUB // GATHER_UNROLL)

        def body(t, carry):
            for k in range(GATHER_UNROLL):
                r = t * GATHER_UNROLL + k
                tok = tok_ref[blk * rows + r]
                pltpu.make_async_copy(h_hbm.at[pl.ds(tok, 1)], xbuf.at[slot, pl.ds(r, 1)], sem.at[slot]).start()
            return carry
        lax.fori_loop(0, n_trips, body, 0)

    def run_block(slot, n):
        pltpu.make_async_copy(h_hbm.at[pl.ds(0, n)], xbuf.at[slot, pl.ds(0, n)], sem.at[slot]).wait()
        x_lo, x_hi = unpack_bf16_pairs(xbuf[slot, :n])
        half = x_lo.shape[1]
        g = _dot(x_lo, wg_ref[0, :half, :]) + _dot(x_hi, wg_ref[0, half:, :])
        u = _dot(x_lo, wu_ref[0, :half, :]) + _dot(x_hi, wu_ref[0, half:, :])
        act = (jax.nn.silu(g) * u).astype(BF16)
        y_ref[:n, :] = _dot(act, wd_ref[0])
        if n < rows:
            y_ref[n:, :] = jnp.zeros((rows - n, y_ref.shape[1]), F32)

    @pl.when(jnp.logical_and(i == 0, nact > 0))
    def _():
        start_gather(0, 0)

    nxt = jnp.maximum(jnp.minimum(i + 1, nact - 1), 0)
    for nxt_slot in range(2):
        for piece in range(rows // MOE_SUB):
            @pl.when(jnp.logical_and(jnp.logical_and(i + 1 < nact, (i + 1) % 2 == nxt_slot),
                                     piece < nsub_ref[nxt]))
            def _():
                first = (i + 1) * rows
                for r in range(piece * MOE_SUB, (piece + 1) * MOE_SUB):
                    pltpu.make_async_copy(h_hbm.at[pl.ds(tok_ref[first + r], 1)], xbuf.at[nxt_slot, pl.ds(r, 1)],
                                          sem.at[nxt_slot]).start()

    for n_sub in range(1, rows // MOE_SUB + 1):
        @pl.when(jnp.logical_and(i < nact, nsub_ref[i] == n_sub))
        def _():
            run_block(i % 2, n_sub * MOE_SUB)

    @pl.when(i >= nact)
    def _():
        y_ref[...] = jnp.zeros_like(y_ref)


def moe_experts(h2, tok_buf, block_expert, short_blocks, n_active, w_gate, w_up, w_down):
    t, dp = h2.shape
    d = 2 * dp
    n_rows = tok_buf.shape[0]
    rows = MOE_ROWS
    n_blocks = n_rows // rows
    de = w_gate.shape[-1]

    def wmap(i, tok, bexp, short, nact):
        return (bexp[jnp.minimum(i, jnp.maximum(nact[0] - 1, 0))], 0, 0)

    grid_spec = pltpu.PrefetchScalarGridSpec(
        num_scalar_prefetch=4,
        grid=(n_blocks,),
        in_specs=[pl.BlockSpec(memory_space=pl.ANY),
                  pl.BlockSpec((1, d, de), wmap),
                  pl.BlockSpec((1, d, de), wmap),
                  pl.BlockSpec((1, de, d), wmap)],
        out_specs=pl.BlockSpec((rows, d), lambda i, *_: (i, 0)),
        scratch_shapes=[pltpu.VMEM((2, rows, dp), jnp.uint32), pltpu.SemaphoreType.DMA((2,))],
    )
    return pl.pallas_call(
        _moe_kernel,
        grid_spec=grid_spec,
        out_shape=jax.ShapeDtypeStruct((n_rows, d), F32),
        compiler_params=_params(("arbitrary",), 56),
        name="moe_experts",
    )(tok_buf, block_expert, short_blocks, n_active, h2, w_gate, w_up, w_down)


def _combine_kernel(dest_ref, ys_hbm, rt_ref, x1_ref, gate_ref, lng_ref, lnb_ref, o_ref, ybuf, sem, *, alpha):
    i = pl.program_id(0)
    n = pl.num_programs(0)
    tm = x1_ref.shape[0]
    n_copies = TOP_K * tm

    def gather(blk, slot, start):
        if not start:
            pltpu.make_async_copy(ys_hbm.at[pl.ds(0, n_copies)], ybuf.at[slot], sem.at[slot]).wait()
            return

        def body(r, carry):
            row = dest_ref[blk * n_copies + r]
            pltpu.make_async_copy(ys_hbm.at[pl.ds(row, 1)], ybuf.at[slot, pl.ds(r, 1)], sem.at[slot]).start()
            return carry
        lax.fori_loop(0, n_copies, body, 0, unroll=GATHER_UNROLL)

    @pl.when(i == 0)
    def _():
        gather(0, 0, True)

    for nxt_slot in range(2):
        @pl.when(jnp.logical_and(i + 1 < n, (i + 1) % 2 == nxt_slot))
        def _():
            first = (i + 1) * n_copies
            for r in range(n_copies):
                pltpu.make_async_copy(ys_hbm.at[pl.ds(dest_ref[first + r], 1)], ybuf.at[nxt_slot, pl.ds(r, 1)],
                                      sem.at[nxt_slot]).start()

    slot = i % 2
    gather(i, slot, False)
    moe = rt_ref[:, ROUTE_W:ROUTE_W + 1] * ybuf[slot, :tm, :]
    for kk in range(1, TOP_K):
        moe = moe + rt_ref[:, ROUTE_W + kk:ROUTE_W + kk + 1] * ybuf[slot, kk * tm:(kk + 1) * tm, :]
    r = alpha * x1_ref[...] + (1.0 + gate_ref[0]) * moe
    mu = jnp.mean(r, axis=-1, keepdims=True)
    var = jnp.mean(jnp.square(r - mu), axis=-1, keepdims=True)
    o_ref[...] = (r - mu) * lax.rsqrt(var + LN_EPS) * lng_ref[...] + lnb_ref[...]


def moe_combine(ys, dest, table, x1, gate2, ln_g, ln_b, alpha, seq):
    t, d = x1.shape
    tm = 128
    tiles_per_seq = seq // tm
    grid_spec = pltpu.PrefetchScalarGridSpec(
        num_scalar_prefetch=1,
        grid=(t // tm,),
        in_specs=[pl.BlockSpec(memory_space=pl.ANY),
                  pl.BlockSpec((tm, LANES), lambda i, *_: (i, 0)),
                  pl.BlockSpec((tm, d), lambda i, *_: (i, 0)),
                  pl.BlockSpec((1, 1, d), lambda i, *_: (i // tiles_per_seq, 0, 0)),
                  pl.BlockSpec((1, d), lambda i, *_: (0, 0)),
                  pl.BlockSpec((1, d), lambda i, *_: (0, 0))],
        out_specs=pl.BlockSpec((tm, d), lambda i, *_: (i, 0)),
        scratch_shapes=[pltpu.VMEM((2, TOP_K * tm, d), F32), pltpu.SemaphoreType.DMA((2,))],
    )
    return pl.pallas_call(
        functools.partial(_combine_kernel, alpha=alpha),
        grid_spec=grid_spec,
        out_shape=jax.ShapeDtypeStruct((t, d), F32),
        compiler_params=_params(("arbitrary",), 32),
        name="moe_combine",
    )(dest, ys, table, x1, gate2, ln_g.reshape(1, d), ln_b.reshape(1, d))


def _route_kernel(lg_ref, b_ref, o_ref, cnt_ref, carry):
    i = pl.program_id(0)

    @pl.when(i == 0)
    def _():
        carry[...] = jnp.zeros_like(carry)

    tm = lg_ref.shape[0]
    x = lg_ref[...] + b_ref[...]
    lane = lax.broadcasted_iota(jnp.int32, (tm, LANES), 1)
    ninf = -jnp.inf

    def top(v):
        vmax = jnp.max(v, axis=1, keepdims=True)
        return vmax, jnp.min(jnp.where(v == vmax, lane, LANES), axis=1, keepdims=True)

    gmask = lane < N_EXPERT_GROUPS
    gmax, g_sel = top(jnp.where(gmask, x, ninf))
    p_group = 1.0 / jnp.sum(jnp.where(gmask, jnp.exp(x - gmax), 0.0), axis=1, keepdims=True)

    lo = N_EXPERT_GROUPS + g_sel * EXPERTS_PER_GROUP
    cur = jnp.where(jnp.logical_and(lane >= lo, lane < lo + EXPERTS_PER_GROUP), x, ninf)
    vals, idxs = [], []
    for _ in range(TOP_K):
        v, ix = top(cur)
        vals.append(v)
        idxs.append(ix)
        cur = jnp.where(lane == ix, ninf, cur)
    exps = [jnp.exp(v - vals[0]) for v in vals]
    den = functools.reduce(lambda a, c: a + c, exps)

    member = functools.reduce(jnp.logical_or, [lane == ix for ix in idxs])
    mf = jnp.where(member, 1.0, 0.0)
    r_i = lax.broadcasted_iota(jnp.int32, (tm, tm), 0)
    c_i = lax.broadcasted_iota(jnp.int32, (tm, tm), 1)
    before = _dot((c_i < r_i).astype(BF16), mf.astype(BF16)) + carry[...]
    carry[...] = carry[...] + jnp.sum(mf, axis=0, keepdims=True)
    cnt_ref[...] = carry[...]

    out = jnp.zeros((tm, LANES), F32)
    for k in range(TOP_K):
        rank = jnp.sum(jnp.where(lane == idxs[k], before, 0.0), axis=1, keepdims=True)
        out = jnp.where(lane == ROUTE_EID + k, (idxs[k] - N_EXPERT_GROUPS).astype(F32), out)
        out = jnp.where(lane == ROUTE_RANK + k, rank, out)
        out = jnp.where(lane == ROUTE_W + k, p_group * (exps[k] / den), out)
    o_ref[...] = out


def route(logits, b_rg, b_re, n_rows, tm_combine):
    n_tok = logits.shape[0]
    n_experts = N_EXPERT_GROUPS * EXPERTS_PER_GROUP
    bias = jnp.zeros((1, LANES), F32).at[0, :N_EXPERT_GROUPS].set(b_rg)
    bias = bias.at[0, N_EXPERT_GROUPS:N_EXPERT_GROUPS + n_experts].set(b_re)
    tm = 512
    table, cnt = pl.pallas_call(
        _route_kernel,
        grid=(n_tok // tm,),
        in_specs=[pl.BlockSpec((tm, LANES), lambda i: (i, 0)),
                  pl.BlockSpec((1, LANES), lambda i: (0, 0))],
        out_specs=[pl.BlockSpec((tm, LANES), lambda i: (i, 0)),
                   pl.BlockSpec((1, LANES), lambda i: (0, 0))],
        out_shape=[jax.ShapeDtypeStruct((n_tok, LANES), F32), jax.ShapeDtypeStruct((1, LANES), F32)],
        scratch_shapes=[pltpu.VMEM((1, LANES), F32)],
        compiler_params=_params(("arbitrary",), 16),
        name="route",
    )(logits, bias)

    eid = table[:, ROUTE_EID:ROUTE_EID + TOP_K].astype(jnp.int32)
    rank = table[:, ROUTE_RANK:ROUTE_RANK + TOP_K].astype(jnp.int32)
    counts = cnt[0, N_EXPERT_GROUPS:N_EXPERT_GROUPS + n_experts].astype(jnp.int32)
    padded = ((counts + MOE_ROWS - 1) // MOE_ROWS) * MOE_ROWS
    pends = jnp.cumsum(padded)
    pstarts = pends - padded
    onehot = eid[:, :, None] == jnp.arange(n_experts, dtype=jnp.int32)
    dest = jnp.sum(jnp.where(onehot, pstarts, 0), axis=-1) + rank
    tok = jnp.broadcast_to(jnp.arange(n_tok, dtype=jnp.int32)[:, None], (n_tok, TOP_K))
    tok_buf = jnp.zeros((n_rows,), jnp.int32).at[dest.reshape(-1)].set(tok.reshape(-1), unique_indices=True)
    n_blocks = n_rows // MOE_ROWS
    block_expert = jnp.clip(
        jnp.searchsorted(pends, jnp.arange(n_blocks, dtype=jnp.int32) * MOE_ROWS, side='right'),
        0, n_experts - 1).astype(jnp.int32)
    rows_left = (pstarts + counts)[block_expert] - jnp.arange(n_blocks, dtype=jnp.int32) * MOE_ROWS
    short_blocks = jnp.clip(-(-rows_left // MOE_SUB), 1, MOE_ROWS // MOE_SUB).astype(jnp.int32)
    n_active = (pends[-1] // MOE_ROWS).astype(jnp.int32).reshape(1)
    dest_tiles = dest.reshape(n_tok // tm_combine, tm_combine, TOP_K).transpose(0, 2, 1).reshape(-1)
    return table, tok_buf, block_expert, short_blocks, n_active, dest_tiles.astype(jnp.int32)


def _layer(x, c, w_ada, b_ada, w_in, b_forget, lam_re, lam_im, log_dt, b_re, b_im, c_re, c_im, d_skip,
           w_glu, b_glu, g_attn, g_ssm, w_out, ln1_g, ln1_b, w_rg, b_rg, w_re, b_re_r,
           w_gate, w_up, w_down, ln2_g, ln2_b, alpha):
    bsz, seq, d = x.shape
    n_tok = bsz * seq
    n_heads = b_forget.shape[0]
    d_att = n_heads * HEAD_DIM
    d_ssm = d_skip.shape[0] * d_skip.shape[1]

    mod = ada_mod(c, w_ada, b_ada).reshape(bsz, 1, -1)
    shift1, scale1, gate1, shift2, scale2, gate2 = jnp.split(mod, 6, axis=-1)

    n_qkv = 3 * d_att
    w_main = jnp.concatenate([w_in[:, :n_qkv], w_in[:, n_qkv + n_heads:]], axis=1).astype(BF16)
    w_f = jnp.zeros((d, LANES), BF16).at[:, :n_heads].set(w_in[:, n_qkv:n_qkv + n_heads].astype(BF16))
    qkv, u, f = in_proj(x, scale1, shift1, w_main, w_f, n_qkv)

    q_aug, k_aug = forget_cum(f, b_forget)
    attn, (w_glu_b, w_out_b, w_gate_b, w_up_b, w_down_b) = attention(
        qkv, q_aug, k_aug, n_heads, [w_glu, w_out, w_gate, w_up, w_down])

    w1, ft, al = ssm_prep(lam_re, lam_im, log_dt, b_re, b_im, c_re, c_im)
    y = ssm_scan(u, w1, ft, al, d_skip)
    ssm_n = glu_norm(y.reshape(n_tok, d_ssm), w_glu_b, b_glu, g_ssm).reshape(bsz, seq, d_ssm)

    n_experts = N_EXPERT_GROUPS * EXPERTS_PER_GROUP
    w_r = jnp.zeros((d, LANES), F32).at[:, :N_EXPERT_GROUPS].set(w_rg)
    w_r = w_r.at[:, N_EXPERT_GROUPS:N_EXPERT_GROUPS + n_experts].set(w_re)
    wr_hi = w_r.astype(BF16)
    wr_lo = (w_r - wr_hi.astype(F32)).astype(BF16)
    x1, h2, logits = out_proj(attn, g_attn, ssm_n, w_out_b, x, gate1, ln1_g, ln1_b,
                              scale2, shift2, wr_hi, wr_lo, alpha)

    n_assign = n_tok * TOP_K
    n_blocks = -(-(n_assign + n_experts * (MOE_ROWS - 1)) // MOE_ROWS)
    n_rows = n_blocks * MOE_ROWS
    table, tok_buf, block_expert, short_blocks, n_active, dest = route(
        logits.reshape(n_tok, LANES), b_rg, b_re_r, n_rows, 128)
    ys = moe_experts(h2.reshape(n_tok, d // 2), tok_buf, block_expert, short_blocks, n_active,
                     w_gate_b, w_up_b, w_down_b)
    out = moe_combine(ys, dest, table, x1.reshape(n_tok, d), gate2, ln2_g, ln2_b, alpha, seq)
    return out.reshape(bsz, seq, d)


def kernel(x, c, w_ada, b_ada, w_in, b_forget, ssm_lambda_re, ssm_lambda_im, ssm_log_dt, ssm_b_re, ssm_b_im,
           ssm_c_re, ssm_c_im, ssm_d, w_glu, b_glu, g_attn, g_ssm, w_out, ln1_g, ln1_b, w_router_group,
           b_router_group, w_router_expert, b_router_expert, w_gate, w_up, w_down, ln2_g, ln2_b):
    depth = w_ada.shape[0]
    alpha = (2.0 * depth) ** 0.25
    for l in range(depth):
        x = _layer(x, c, w_ada[l], b_ada[l], w_in[l], b_forget[l], ssm_lambda_re[l], ssm_lambda_im[l],
                   ssm_log_dt[l], ssm_b_re[l], ssm_b_im[l], ssm_c_re[l], ssm_c_im[l], ssm_d[l],
                   w_glu[l], b_glu[l], g_attn[l], g_ssm[l], w_out[l], ln1_g[l], ln1_b[l],
                   w_router_group[l], b_router_group[l], w_router_expert[l], b_router_expert[l],
                   w_gate[l], w_up[l], w_down[l], ln2_g[l], ln2_b[l], alpha)
    return x
```

```python
import functools
import math

import jax
import jax.numpy as jnp
from jax import lax
from jax.experimental import pallas as pl
from jax.experimental.pallas import tpu as pltpu

F32 = jnp.float32
BF16 = jnp.bfloat16

LANES = 128
HEAD_DIM = 128
SSM_GROUP = 16
SSM_STATE = 64
GROUPS_PER_SLAB = LANES // SSM_GROUP
SLAB_STATE = GROUPS_PER_SLAB * SSM_STATE
SSM_CHUNK = 16
SSM_COLS = 512
N_EXPERT_GROUPS = 8
EXPERTS_PER_GROUP = 8
TOP_K = 2
MOE_ROWS = 256
MOE_SUB = 64
COMBINE_ROWS = 256
GATHER_UNROLL = 8
ROW_CHUNK = 128
ROUTE_EID, ROUTE_RANK, ROUTE_W = 0, TOP_K, 2 * TOP_K
LN_EPS = 1e-5
RMS_EPS = 1e-6
NEG_BIG = -1e30
LOG2E = math.log2(math.e)
MIB = 1024 * 1024

_NT = (((1,), (1,)), ((), ()))


def _params(semantics, vmem_mib):
    return pltpu.CompilerParams(dimension_semantics=semantics, vmem_limit_bytes=vmem_mib * MIB)


def _dot(a, b):
    return jnp.dot(a, b, preferred_element_type=F32)


def _dot_nt(a, b):
    return lax.dot_general(a, b, _NT, preferred_element_type=F32)


def _split2(x):
    hi = x.astype(BF16)
    return hi, (x - hi.astype(F32)).astype(BF16)


def _ada_kernel(c_ref, w_ref, b_ref, o_ref):
    s = jax.nn.silu(c_ref[...]).astype(BF16)
    o_ref[...] = _dot(s, w_ref[...].astype(BF16)) + b_ref[...]


def ada_mod(c, w_ada, b_ada):
    bsz, d = c.shape
    n = w_ada.shape[1]
    rows = 8
    assert bsz <= rows
    cp = jnp.zeros((rows, d), F32).at[:bsz].set(c)
    tn = 512
    out = pl.pallas_call(
        _ada_kernel,
        grid=(n // tn,),
        in_specs=[pl.BlockSpec((rows, d), lambda j: (0, 0)),
                  pl.BlockSpec((d, tn), lambda j: (0, j)),
                  pl.BlockSpec((1, tn), lambda j: (0, j))],
        out_specs=pl.BlockSpec((rows, tn), lambda j: (0, j)),
        out_shape=jax.ShapeDtypeStruct((rows, n), F32),
        compiler_params=_params(("arbitrary",), 40),
        name="ada_mod",
    )(cp, w_ada, b_ada.reshape(1, n))
    return out[:bsz]


def _inproj_kernel(x_ref, sc_ref, sh_ref, w_ref, wf_ref, qkv_ref, u_ref, f_ref, h_scr, *,
                   n_q_tiles, n_qkv_tiles, q_scale):
    j = pl.program_id(2)

    @pl.when(j == 0)
    def _():
        hb = (x_ref[0] * (1.0 + sc_ref[0]) + sh_ref[0]).astype(BF16)
        h_scr[...] = hb
        f_ref[0] = _dot(hb, wf_ref[...])

    acc = _dot(h_scr[...], w_ref[...])

    @pl.when(j < n_q_tiles)
    def _():
        qkv_ref[0] = (acc * q_scale).astype(BF16)

    @pl.when(jnp.logical_and(j >= n_q_tiles, j < n_qkv_tiles))
    def _():
        qkv_ref[0] = acc.astype(BF16)

    @pl.when(j >= n_qkv_tiles)
    def _():
        u_ref[0] = acc


def in_proj(x, scale, shift, w_main, w_f, n_qkv):
    bsz, seq, d = x.shape
    n_all = w_main.shape[1]
    n_u = n_all - n_qkv
    tm, tn = 512, 1024
    nq = n_qkv // tn
    grid = (bsz, seq // tm, n_all // tn)
    return pl.pallas_call(
        functools.partial(_inproj_kernel, n_q_tiles=n_qkv // 3 // tn, n_qkv_tiles=nq,
                          q_scale=HEAD_DIM ** -0.5 * LOG2E),
        grid=grid,
        in_specs=[pl.BlockSpec((1, tm, d), lambda b, i, j: (b, i, 0)),
                  pl.BlockSpec((1, 1, d), lambda b, i, j: (b, 0, 0)),
                  pl.BlockSpec((1, 1, d), lambda b, i, j: (b, 0, 0)),
                  pl.BlockSpec((d, tn), lambda b, i, j: (0, j)),
                  pl.BlockSpec((d, LANES), lambda b, i, j: (0, 0))],
        out_specs=[pl.BlockSpec((1, tm, tn), lambda b, i, j: (b, i, jnp.minimum(j, nq - 1))),
                   pl.BlockSpec((1, tm, tn), lambda b, i, j: (b, i, jnp.maximum(j - nq, 0))),
                   pl.BlockSpec((1, tm, LANES), lambda b, i, j: (b, i, 0))],
        out_shape=[jax.ShapeDtypeStruct((bsz, seq, n_qkv), BF16),
                   jax.ShapeDtypeStruct((bsz, seq, n_u), F32),
                   jax.ShapeDtypeStruct((bsz, seq, LANES), F32)],
        scratch_shapes=[pltpu.VMEM((tm, d), BF16)],
        compiler_params=_params(("arbitrary", "arbitrary", "arbitrary"), 52),
        name="in_proj",
    )(x, scale, shift, w_main, w_f)


def _split3(x):
    p1 = x.astype(BF16)
    r1 = x - p1.astype(F32)
    p2 = r1.astype(BF16)
    p3 = (r1 - p2.astype(F32)).astype(BF16)
    return p1, p2, p3


N_PIECES = 3


def _cum_kernel(f_ref, b_ref, qa_ref, ka_ref, carry, *, n_heads):
    i = pl.program_id(1)

    @pl.when(i == 0)
    def _():
        carry[...] = jnp.zeros_like(carry)

    tc = f_ref.shape[1]
    lf = jax.nn.log_sigmoid(f_ref[0] + b_ref[...])
    row = lax.broadcasted_iota(jnp.int32, (tc, tc), 0)
    col = lax.broadcasted_iota(jnp.int32, (tc, tc), 1)
    tri = (col <= row).astype(BF16)
    p1, p2, p3 = _split3(lf)
    cs = _dot(tri, p1) + _dot(tri, p2) + _dot(tri, p3) + carry[...]
    carry[...] = cs[tc - 1:tc, :]

    pieces = jnp.concatenate(_split3(cs * LOG2E), axis=1)
    r = lax.broadcasted_iota(jnp.int32, (N_PIECES * LANES, LANES), 0)
    c = lax.broadcasted_iota(jnp.int32, (N_PIECES * LANES, LANES), 1)
    lane = lax.broadcasted_iota(jnp.int32, (tc, LANES), 1)
    ones_q = jnp.where(jnp.logical_and(lane >= N_PIECES, lane < 2 * N_PIECES), 1.0, 0.0)
    ones_k = jnp.where(lane < N_PIECES, 1.0, 0.0)
    for h in range(n_heads):
        sel_q = (r == c * LANES + h).astype(BF16)
        sel_k = (r == (c - N_PIECES) * LANES + h).astype(BF16)
        picked = _dot(pieces, jnp.concatenate([sel_q, sel_k], axis=1))
        qa_ref[0, h] = (picked[:, :LANES] + ones_q).astype(BF16)
        ka_ref[0, h] = (ones_k - picked[:, LANES:]).astype(BF16)


def forget_cum(f, b_forget):
    bsz, seq, _ = f.shape
    n_heads = b_forget.shape[0]
    tc = 512
    bpad = jnp.zeros((1, LANES), F32).at[0, :n_heads].set(b_forget)
    out_spec = pl.BlockSpec((1, n_heads, tc, LANES), lambda b, i: (b, 0, i, 0))
    out_shape = jax.ShapeDtypeStruct((bsz, n_heads, seq, LANES), BF16)
    return pl.pallas_call(
        functools.partial(_cum_kernel, n_heads=n_heads),
        grid=(bsz, seq // tc),
        in_specs=[pl.BlockSpec((1, tc, LANES), lambda b, i: (b, i, 0)),
                  pl.BlockSpec((1, LANES), lambda b, i: (0, 0))],
        out_specs=[out_spec, out_spec],
        out_shape=[out_shape, out_shape],
        scratch_shapes=[pltpu.VMEM((1, LANES), F32)],
        compiler_params=_params(("arbitrary", "arbitrary"), 24),
        name="forget_cum",
    )(f, bpad)


ATT_QROWS = 1024
ATT_KEYS = 512


def _attn_kernel(*refs, n_cast):
    q_ref, qa_ref, k_ref, ka_ref, v_ref = refs[:5]
    src_refs = refs[5:5 + n_cast]
    o_ref = refs[5 + n_cast]
    dst_refs = refs[6 + n_cast:6 + 2 * n_cast]
    m_scr, acc_scr, s_scr = refs[6 + 2 * n_cast:]
    for src, dst in zip(src_refs, dst_refs):
        dst[...] = src[...].astype(BF16)

    tq = q_ref.shape[1]
    tk = ATT_KEYS
    n_groups = tq // tk
    qi = pl.program_id(2)
    q = jnp.concatenate([q_ref[0], qa_ref[0, 0]], axis=1)
    ones = jnp.ones((tk, HEAD_DIM), BF16)

    m_scr[...] = jnp.full_like(m_scr, NEG_BIG)
    acc_scr[...] = jnp.zeros_like(acc_scr)

    def scores(kb, slot, first_group=0):
        k0 = pl.multiple_of(kb * tk, tk)
        kt = jnp.concatenate([k_ref[0, pl.ds(k0, tk), :], ka_ref[0, 0, pl.ds(k0, tk), :]], axis=1)
        r0 = first_group * tk
        s_scr[slot, r0:, :] = lax.dot_general(q[r0:], kt, _NT, preferred_element_type=F32)

    def update(kb, slot, diag_group=None):
        k0 = pl.multiple_of(kb * tk, tk)
        vt = jnp.concatenate([v_ref[0, pl.ds(k0, tk), :], ones], axis=1)
        for g in range(n_groups):
            if diag_group is not None and g < diag_group:
                continue
            rs = slice(g * tk, (g + 1) * tk)
            s = s_scr[slot, rs, :]
            if g == diag_group:
                qpos = lax.broadcasted_iota(jnp.int32, (tk, tk), 0)
                kpos = lax.broadcasted_iota(jnp.int32, (tk, tk), 1)
                s = jnp.where(kpos <= qpos, s, NEG_BIG)
            m_prev = m_scr[rs]
            m_new = jnp.maximum(m_prev, jnp.max(s, axis=1, keepdims=True))
            alpha = jnp.exp2(m_prev - m_new)
            p = jnp.exp2(s - m_new)
            acc_scr[rs] = alpha * acc_scr[rs] + _dot(p.astype(BF16), vt)
            m_scr[rs] = m_new

    def body(i, carry):
        kb = n_groups * i
        for g in range(n_groups):
            scores(kb + g + 1, (g + 1) % 2)
            update(kb + g, g % 2)
        return carry

    assert n_groups % 2 == 0
    scores(0, 0)
    lax.fori_loop(0, qi, body, 0)
    kb = n_groups * qi
    for g in range(n_groups):
        if g + 1 < n_groups:
            scores(kb + g + 1, (g + 1) % 2, first_group=g + 1)
        update(kb + g, g % 2, diag_group=g)

    o_ref[0] = acc_scr[:, :HEAD_DIM] / acc_scr[:, HEAD_DIM:]


BF16_ROWS = 16


def _cast_chunks(w, n_steps):
    cols = w.shape[-1]
    total_rows = w.size // cols
    n_chunks = n_steps
    while total_rows % (n_chunks * BF16_ROWS):
        n_chunks //= 2
    return w.reshape(n_chunks, total_rows // n_chunks, cols)


def attention(qkv, q_aug, k_aug, n_heads, cast_weights):
    bsz, seq, _ = qkv.shape
    t = min(ATT_QROWS, seq)
    nq = seq // t
    n_steps = bsz * n_heads * nq
    srcs = [_cast_chunks(w, n_steps) for w in cast_weights]

    def chunk_spec(a):
        per = n_steps // a.shape[0]
        return pl.BlockSpec((1,) + a.shape[1:], lambda b, h, i: (((b * n_heads + h) * nq + i) // per, 0, 0))

    outs = pl.pallas_call(
        functools.partial(_attn_kernel, n_cast=len(srcs)),
        grid=(bsz, n_heads, nq),
        in_specs=[pl.BlockSpec((1, t, HEAD_DIM), lambda b, h, i: (b, i, h)),
                  pl.BlockSpec((1, 1, t, LANES), lambda b, h, i: (b, h, i, 0)),
                  pl.BlockSpec((1, seq, HEAD_DIM), lambda b, h, i: (b, 0, n_heads + h)),
                  pl.BlockSpec((1, 1, seq, LANES), lambda b, h, i: (b, h, 0, 0)),
                  pl.BlockSpec((1, seq, HEAD_DIM), lambda b, h, i: (b, 0, 2 * n_heads + h))]
                 + [chunk_spec(a) for a in srcs],
        out_specs=[pl.BlockSpec((1, t, HEAD_DIM), lambda b, h, i: (b, i, h))] + [chunk_spec(a) for a in srcs],
        out_shape=[jax.ShapeDtypeStruct((bsz, seq, n_heads * HEAD_DIM), F32)]
                  + [jax.ShapeDtypeStruct(a.shape, BF16) for a in srcs],
        scratch_shapes=[pltpu.VMEM((t, 1), F32), pltpu.VMEM((t, 2 * HEAD_DIM), F32),
                        pltpu.VMEM((2, t, ATT_KEYS), F32)],
        compiler_params=_params(("arbitrary", "arbitrary", "arbitrary"), 56),
        name="attention",
    )(qkv, q_aug, qkv, k_aug, qkv, *srcs)
    return outs[0], [o.reshape(w.shape) for o, w in zip(outs[1:], cast_weights)]


def _blockdiag(p):
    g, c, n = p.shape
    ns = g // GROUPS_PER_SLAB
    eye = jnp.eye(GROUPS_PER_SLAB, dtype=p.dtype)
    out = p.reshape(ns, GROUPS_PER_SLAB, c, 1, n) * eye[None, :, None, :, None]
    return out.reshape(ns, GROUPS_PER_SLAB * c, GROUPS_PER_SLAB * n)


def _ssm_prep_kernel(lr_ref, li_ref, ldt_ref, bre_ref, bim_ref, cre_ref, cim_ref, w1_ref, ft_ref, al_ref):
    L = SSM_CHUNK
    lr = lr_ref[0]
    li = li_ref[0]
    dt = jnp.exp(ldt_ref[0])
    mag = jnp.exp(lr * dt)
    a_re = mag * jnp.cos(li * dt)
    a_im = mag * jnp.sin(li * dt)
    den = lr * lr + li * li
    z_re = ((a_re - 1.0) * lr + a_im * li) / den
    z_im = (a_im * lr - (a_re - 1.0) * li) / den
    br = bre_ref[0]
    bi = bim_ref[0]
    bb_re = z_re * br - z_im * bi
    bb_im = z_re * bi + z_im * br
    cr = cre_ref[0]
    ci = cim_ref[0]
    ft0_hi, ft0_lo = _split2(jnp.concatenate([cr, -ci], axis=1))

    def power(d):
        m = jnp.exp(lr * dt * d)
        return m * jnp.cos(li * dt * d), m * jnp.sin(li * dt * d)

    w1_ref[0, :, :L * LANES] = jnp.zeros((L * LANES, L * LANES), BF16)
    for d in range(L):
        pr, pi = power(float(d))
        xe = jnp.concatenate([bb_re * pr - bb_im * pi, bb_re * pi + bb_im * pr], axis=1)
        j = L - 1 - d
        w1_ref[0, j * LANES:(j + 1) * LANES, L * LANES:] = xe.astype(BF16)
        xe_hi, xe_lo = _split2(xe)
        m_d = (_dot_nt(xe_hi, ft0_hi) + _dot_nt(xe_hi, ft0_lo) + _dot_nt(xe_lo, ft0_hi)).astype(BF16)
        for jj in range(L - d):
            w1_ref[0, jj * LANES:(jj + 1) * LANES, (jj + d) * LANES:(jj + d + 1) * LANES] = m_d
        pr1, pi1 = power(float(d + 1))
        ft_ref[0, d * LANES:(d + 1) * LANES, :] = jnp.concatenate(
            [cr * pr1 - ci * pi1, -(cr * pi1 + ci * pr1)], axis=1).astype(BF16)
    prl, pil = power(float(L))
    al_ref[0] = jnp.concatenate([prl, pil], axis=1)


def ssm_prep(lam_re, lam_im, log_dt, b_re, b_im, c_re, c_im):
    g, n = lam_re.shape
    ns = g // GROUPS_PER_SLAB
    L = SSM_CHUNK
    rowvec = lambda a: a.reshape(ns, 1, SLAB_STATE)
    args = (rowvec(lam_re), rowvec(lam_im), rowvec(jnp.repeat(log_dt, n)),
            _blockdiag(b_re.transpose(0, 2, 1)), _blockdiag(b_im.transpose(0, 2, 1)),
            _blockdiag(c_re), _blockdiag(c_im))
    vec_spec = pl.BlockSpec((1, 1, SLAB_STATE), lambda s: (s, 0, 0))
    mat_spec = pl.BlockSpec((1, LANES, SLAB_STATE), lambda s: (s, 0, 0))
    return pl.pallas_call(
        _ssm_prep_kernel,
        grid=(ns,),
        in_specs=[vec_spec] * 3 + [mat_spec] * 4,
        out_specs=[pl.BlockSpec((1, L * LANES, L * LANES + 2 * SLAB_STATE), lambda s: (s, 0, 0)),
                   pl.BlockSpec((1, L * LANES, 2 * SLAB_STATE), lambda s: (s, 0, 0)),
                   pl.BlockSpec((1, 1, 2 * SLAB_STATE), lambda s: (s, 0, 0))],
        out_shape=[jax.ShapeDtypeStruct((ns, L * LANES, L * LANES + 2 * SLAB_STATE), BF16),
                   jax.ShapeDtypeStruct((ns, L * LANES, 2 * SLAB_STATE), BF16),
                   jax.ShapeDtypeStruct((ns, 1, 2 * SLAB_STATE), F32)],
        compiler_params=_params(("arbitrary",), 48),
        name="ssm_prep",
    )(*args)


def _ssm_kernel(u_ref, w1_ref, ft_ref, al_ref, d_ref, y_ref, uf_scr, e_scr, y_scr):
    L = SSM_CHUNK
    nch = uf_scr.shape[0]
    lc = L * LANES
    for j in range(L):
        uf_scr[:, j * LANES:(j + 1) * LANES] = u_ref[0, pl.ds(j, nch, stride=L), :].astype(BF16)
    uf = uf_scr[...]
    e_scr[...] = _dot(uf, w1_ref[0, :, lc:])

    a_re = al_ref[0, :, :SLAB_STATE]
    a_im = al_ref[0, :, SLAB_STATE:]
    h_re = jnp.zeros((1, SLAB_STATE), F32)
    h_im = jnp.zeros((1, SLAB_STATE), F32)
    for k in range(nch):
        e = e_scr[k:k + 1, :]
        e_scr[k:k + 1, :] = jnp.concatenate([h_re, h_im], axis=1)
        h_re, h_im = (a_re * h_re - a_im * h_im + e[:, :SLAB_STATE],
                      a_re * h_im + a_im * h_re + e[:, SLAB_STATE:])

    for c in range(lc // SSM_COLS):
        k_hi = (c + 1) * SSM_COLS
        cols = slice(c * SSM_COLS, k_hi)
        y_scr[:, cols] = _dot(uf[:, :k_hi], w1_ref[0, :k_hi, cols])
    y = y_scr[...] + lax.dot_general(e_scr[...].astype(BF16), ft_ref[0], _NT, preferred_element_type=F32)
    for i in range(L):
        yi = y[:, i * LANES:(i + 1) * LANES] + d_ref[0] * u_ref[0, pl.ds(i, nch, stride=L), :]
        y_ref[0, pl.ds(i, nch, stride=L), :] = jax.nn.gelu(yi)


def ssm_scan(u, w1, ft, al, d_skip):
    bsz, seq, c = u.shape
    ns = c // LANES
    L = SSM_CHUNK
    nch = seq // L
    lc = L * LANES
    return pl.pallas_call(
        _ssm_kernel,
        grid=(ns, bsz),
        in_specs=[pl.BlockSpec((1, seq, LANES), lambda s, b: (b, 0, s)),
                  pl.BlockSpec((1, lc, lc + 2 * SLAB_STATE), lambda s, b: (s, 0, 0)),
                  pl.BlockSpec((1, lc, 2 * SLAB_STATE), lambda s, b: (s, 0, 0)),
                  pl.BlockSpec((1, 1, 2 * SLAB_STATE), lambda s, b: (s, 0, 0)),
                  pl.BlockSpec((1, 1, LANES), lambda s, b: (s, 0, 0))],
        out_specs=pl.BlockSpec((1, seq, LANES), lambda s, b: (b, 0, s)),
        out_shape=jax.ShapeDtypeStruct((bsz, seq, c), F32),
        scratch_shapes=[pltpu.VMEM((nch, lc), BF16), pltpu.VMEM((nch, 2 * SLAB_STATE), F32),
                        pltpu.VMEM((nch, lc), F32)],
        compiler_params=_params(("arbitrary", "arbitrary"), 60),
        name="ssm_scan",
    )(u, w1, ft, al, d_skip.reshape(ns, 1, LANES))


def _glu_kernel(y_ref, w_ref, b_ref, g_ref, o_ref):
    y = y_ref[...]
    o = y * jax.nn.sigmoid(_dot(y.astype(BF16), w_ref[...]) + b_ref[...])
    ms = jnp.mean(o * o, axis=-1, keepdims=True)
    o_ref[...] = (o * lax.rsqrt(ms + RMS_EPS) * g_ref[...]).astype(BF16)


def glu_norm(y, w_glu, b_glu, g):
    t, c = y.shape
    tm = 512
    return pl.pallas_call(
        _glu_kernel,
        grid=(t // tm,),
        in_specs=[pl.BlockSpec((tm, c), lambda i: (i, 0)),
                  pl.BlockSpec((c, c), lambda i: (0, 0)),
                  pl.BlockSpec((1, c), lambda i: (0, 0)),
                  pl.BlockSpec((1, c), lambda i: (0, 0))],
        out_specs=pl.BlockSpec((tm, c), lambda i: (i, 0)),
        out_shape=jax.ShapeDtypeStruct((t, c), BF16),
        compiler_params=_params(("arbitrary",), 48),
        name="glu_norm",
    )(y, w_glu, b_glu.reshape(1, c), g.reshape(1, c))


def _outproj_kernel(attn_ref, ga_ref, ssm_ref, w_ref, x_ref, gate_ref, lng_ref, lnb_ref, sc2_ref, sh2_ref,
                    wrh_ref, wrl_ref, x1_ref, h2_ref, lg_ref, a_scr, *, alpha, n_att):
    j = pl.program_id(2)
    nj = pl.num_programs(2)
    tn = w_ref.shape[1]

    tm = x1_ref.shape[1]
    n_chunks = tm // ROW_CHUNK

    @pl.when(j == 0)
    def _():
        def norm(c, carry):
            rs = pl.ds(pl.multiple_of(c * ROW_CHUNK, ROW_CHUNK), ROW_CHUNK)
            a = attn_ref[0, rs, :]
            ms = jnp.mean(a * a, axis=-1, keepdims=True)
            a_scr[rs, :n_att] = (a * lax.rsqrt(ms + RMS_EPS) * ga_ref[...]).astype(BF16)
            return carry
        lax.fori_loop(0, n_chunks, norm, 0)
        a_scr[:, n_att:] = ssm_ref[0]

    mixed = _dot(a_scr[...], w_ref[...])
    col = pl.multiple_of(j * tn, tn)
    x1_ref[0, :, pl.ds(col, tn)] = alpha * x_ref[0] + (1.0 + gate_ref[0]) * mixed

    @pl.when(j == nj - 1)
    def _():
        def finish(c, carry):
            rs = pl.ds(pl.multiple_of(c * ROW_CHUNK, ROW_CHUNK), ROW_CHUNK)
            r = x1_ref[0, rs, :]
            mu = jnp.mean(r, axis=-1, keepdims=True)
            var = jnp.mean(jnp.square(r - mu), axis=-1, keepdims=True)
            x1 = (r - mu) * lax.rsqrt(var + LN_EPS) * lng_ref[...] + lnb_ref[...]
            x1_ref[0, rs, :] = x1
            h2 = x1 * (1.0 + sc2_ref[0]) + sh2_ref[0]
            hi = h2.astype(BF16)
            hi_f = hi.astype(F32)
            lo = (h2 - hi_f).astype(BF16)
            lg_ref[0, rs, :] = _dot(hi, wrh_ref[...]) + _dot(hi, wrl_ref[...]) + _dot(lo, wrh_ref[...])
            h2_ref[0, rs, :] = pack_bf16_pairs(hi_f)
            return carry
        lax.fori_loop(0, n_chunks, finish, 0)


def pack_bf16_pairs(x):
    n = x.shape[-1] // 2
    bits = lax.bitcast_convert_type(x, jnp.uint32)
    return bits[:, n:] | (bits[:, :n] >> 16)


def unpack_bf16_pairs(p):
    lo = lax.bitcast_convert_type(p << 16, F32).astype(BF16)
    hi = lax.bitcast_convert_type(p & jnp.uint32(0xFFFF0000), F32).astype(BF16)
    return lo, hi


def out_proj(attn, g_attn, ssm_n, w_out, x, gate1, ln_g, ln_b, scale2, shift2, wr_hi, wr_lo, alpha):
    bsz, seq, d = x.shape
    n_att = attn.shape[-1]
    n_ssm = ssm_n.shape[-1]
    k = n_att + n_ssm
    tm, tn = 512, 256
    row = lambda a: a.reshape(1, -1)
    full = lambda n: pl.BlockSpec((1, n), lambda b, i, j: (0, 0))
    return pl.pallas_call(
        functools.partial(_outproj_kernel, alpha=alpha, n_att=n_att),
        grid=(bsz, seq // tm, d // tn),
        in_specs=[pl.BlockSpec((1, tm, n_att), lambda b, i, j: (b, i, 0)),
                  full(n_att),
                  pl.BlockSpec((1, tm, n_ssm), lambda b, i, j: (b, i, 0)),
                  pl.BlockSpec((k, tn), lambda b, i, j: (0, j)),
                  pl.BlockSpec((1, tm, tn), lambda b, i, j: (b, i, j)),
                  pl.BlockSpec((1, 1, tn), lambda b, i, j: (b, 0, j)),
                  full(d), full(d),
                  pl.BlockSpec((1, 1, d), lambda b, i, j: (b, 0, 0)),
                  pl.BlockSpec((1, 1, d), lambda b, i, j: (b, 0, 0)),
                  pl.BlockSpec((d, LANES), lambda b, i, j: (0, 0)),
                  pl.BlockSpec((d, LANES), lambda b, i, j: (0, 0))],
        out_specs=[pl.BlockSpec((1, tm, d), lambda b, i, j: (b, i, 0)),
                   pl.BlockSpec((1, tm, d // 2), lambda b, i, j: (b, i, 0)),
                   pl.BlockSpec((1, tm, LANES), lambda b, i, j: (b, i, 0))],
        out_shape=[jax.ShapeDtypeStruct((bsz, seq, d), F32),
                   jax.ShapeDtypeStruct((bsz, seq, d // 2), jnp.uint32),
                   jax.ShapeDtypeStruct((bsz, seq, LANES), F32)],
        scratch_shapes=[pltpu.VMEM((tm, k), BF16)],
        compiler_params=_params(("arbitrary", "arbitrary", "arbitrary"), 56),
        name="out_proj",
    )(attn, row(g_attn), ssm_n, w_out, x, gate1, row(ln_g), row(ln_b), scale2, shift2, wr_hi, wr_lo)


def _moe_kernel(tok_ref, bexp_ref, nsub_ref, nact_ref, h_hbm, wg_ref, wu_ref, wd_ref, y_ref, xbuf, sem):
    i = pl.program_id(0)
    nact = nact_ref[0]
    rows = xbuf.shape[1]

    def start_gather(blk, slot):
        n_trips = nsub_ref[blk] * (MOE_SUB // GATHER_UNROLL)

        def body(t, carry):
            for k in range(GATHER_UNROLL):
                r = t * GATHER_UNROLL + k
                tok = tok_ref[blk * rows + r]
                pltpu.make_async_copy(h_hbm.at[pl.ds(tok, 1)], xbuf.at[slot, pl.ds(r, 1)], sem.at[slot]).start()
            return carry
        lax.fori_loop(0, n_trips, body, 0)

    def run_block(slot, n):
        pltpu.make_async_copy(h_hbm.at[pl.ds(0, n)], xbuf.at[slot, pl.ds(0, n)], sem.at[slot]).wait()
        x_lo, x_hi = unpack_bf16_pairs(xbuf[slot, :n])
        half = x_lo.shape[1]
        g = _dot(x_lo, wg_ref[0, :half, :]) + _dot(x_hi, wg_ref[0, half:, :])
        u = _dot(x_lo, wu_ref[0, :half, :]) + _dot(x_hi, wu_ref[0, half:, :])
        act = (jax.nn.silu(g) * u).astype(BF16)
        y_ref[:n, :] = _dot(act, wd_ref[0])
        if n < rows:
            y_ref[n:, :] = jnp.zeros((rows - n, y_ref.shape[1]), F32)

    @pl.when(jnp.logical_and(i == 0, nact > 0))
    def _():
        start_gather(0, 0)

    nxt = jnp.maximum(jnp.minimum(i + 1, nact - 1), 0)
    for nxt_slot in range(2):
        for piece in range(rows // MOE_SUB):
            @pl.when(jnp.logical_and(jnp.logical_and(i + 1 < nact, (i + 1) % 2 == nxt_slot),
                                     piece < nsub_ref[nxt]))
            def _():
                first = (i + 1) * rows
                for r in range(piece * MOE_SUB, (piece + 1) * MOE_SUB):
                    pltpu.make_async_copy(h_hbm.at[pl.ds(tok_ref[first + r], 1)], xbuf.at[nxt_slot, pl.ds(r, 1)],
                                          sem.at[nxt_slot]).start()

    for n_sub in range(1, rows // MOE_SUB + 1):
        @pl.when(jnp.logical_and(i < nact, nsub_ref[i] == n_sub))
        def _():
            run_block(i % 2, n_sub * MOE_SUB)

    @pl.when(i >= nact)
    def _():
        y_ref[...] = jnp.zeros_like(y_ref)


def moe_experts(h2, tok_buf, block_expert, short_blocks, n_active, w_gate, w_up, w_down):
    t, dp = h2.shape
    d = 2 * dp
    n_rows = tok_buf.shape[0]
    rows = MOE_ROWS
    n_blocks = n_rows // rows
    de = w_gate.shape[-1]

    def wmap(i, tok, bexp, short, nact):
        return (bexp[jnp.minimum(i, jnp.maximum(nact[0] - 1, 0))], 0, 0)

    grid_spec = pltpu.PrefetchScalarGridSpec(
        num_scalar_prefetch=4,
        grid=(n_blocks,),
        in_specs=[pl.BlockSpec(memory_space=pl.ANY),
                  pl.BlockSpec((1, d, de), wmap),
                  pl.BlockSpec((1, d, de), wmap),
                  pl.BlockSpec((1, de, d), wmap)],
        out_specs=pl.BlockSpec((rows, d), lambda i, *_: (i, 0)),
        scratch_shapes=[pltpu.VMEM((2, rows, dp), jnp.uint32), pltpu.SemaphoreType.DMA((2,))],
    )
    return pl.pallas_call(
        _moe_kernel,
        grid_spec=grid_spec,
        out_shape=jax.ShapeDtypeStruct((n_rows, d), F32),
        compiler_params=_params(("arbitrary",), 56),
        name="moe_experts",
    )(tok_buf, block_expert, short_blocks, n_active, h2, w_gate, w_up, w_down)


def _combine_kernel(dest_ref, ys_hbm, rt_ref, x1_ref, gate_ref, lng_ref, lnb_ref, o_ref, ybuf, sem, *, alpha):
    i = pl.program_id(0)
    n = pl.num_programs(0)
    tm = x1_ref.shape[0]
    n_copies = TOP_K * tm

    def gather(blk, slot, start):
        if not start:
            pltpu.make_async_copy(ys_hbm.at[pl.ds(0, n_copies)], ybuf.at[slot], sem.at[slot]).wait()
            return

        def body(r, carry):
            row = dest_ref[blk * n_copies + r]
            pltpu.make_async_copy(ys_hbm.at[pl.ds(row, 1)], ybuf.at[slot, pl.ds(r, 1)], sem.at[slot]).start()
            return carry
        lax.fori_loop(0, n_copies, body, 0, unroll=GATHER_UNROLL)

    @pl.when(i == 0)
    def _():
        gather(0, 0, True)

    for nxt_slot in range(2):
        @pl.when(jnp.logical_and(i + 1 < n, (i + 1) % 2 == nxt_slot))
        def _():
            first = (i + 1) * n_copies
            for r in range(n_copies):
                pltpu.make_async_copy(ys_hbm.at[pl.ds(dest_ref[first + r], 1)], ybuf.at[nxt_slot, pl.ds(r, 1)],
                                      sem.at[nxt_slot]).start()

    slot = i % 2
    gather(i, slot, False)
    moe = rt_ref[:, ROUTE_W:ROUTE_W + 1] * ybuf[slot, :tm, :]
    for kk in range(1, TOP_K):
        moe = moe + rt_ref[:, ROUTE_W + kk:ROUTE_W + kk + 1] * ybuf[slot, kk * tm:(kk + 1) * tm, :]
    r = alpha * x1_ref[...] + (1.0 + gate_ref[0]) * moe
    mu = jnp.mean(r, axis=-1, keepdims=True)
    var = jnp.mean(jnp.square(r - mu), axis=-1, keepdims=True)
    o_ref[...] = (r - mu) * lax.rsqrt(var + LN_EPS) * lng_ref[...] + lnb_ref[...]


def moe_combine(ys, dest, table, x1, gate2, ln_g, ln_b, alpha, seq):
    t, d = x1.shape
    tm = COMBINE_ROWS
    tiles_per_seq = seq // tm
    grid_spec = pltpu.PrefetchScalarGridSpec(
        num_scalar_prefetch=1,
        grid=(t // tm,),
        in_specs=[pl.BlockSpec(memory_space=pl.ANY),
                  pl.BlockSpec((tm, LANES), lambda i, *_: (i, 0)),
                  pl.BlockSpec((tm, d), lambda i, *_: (i, 0)),
                  pl.BlockSpec((1, 1, d), lambda i, *_: (i // tiles_per_seq, 0, 0)),
                  pl.BlockSpec((1, d), lambda i, *_: (0, 0)),
                  pl.BlockSpec((1, d), lambda i, *_: (0, 0))],
        out_specs=pl.BlockSpec((tm, d), lambda i, *_: (i, 0)),
        scratch_shapes=[pltpu.VMEM((2, TOP_K * tm, d), F32), pltpu.SemaphoreType.DMA((2,))],
    )
    return pl.pallas_call(
        functools.partial(_combine_kernel, alpha=alpha),
        grid_spec=grid_spec,
        out_shape=jax.ShapeDtypeStruct((t, d), F32),
        compiler_params=_params(("arbitrary",), 56),
        name="moe_combine",
    )(dest, ys, table, x1, gate2, ln_g.reshape(1, d), ln_b.reshape(1, d))


def _route_kernel(lg_ref, b_ref, o_ref, cnt_ref, carry):
    i = pl.program_id(0)

    @pl.when(i == 0)
    def _():
        carry[...] = jnp.zeros_like(carry)

    tm = lg_ref.shape[0]
    x = lg_ref[...] + b_ref[...]
    lane = lax.broadcasted_iota(jnp.int32, (tm, LANES), 1)
    ninf = -jnp.inf

    def top(v):
        vmax = jnp.max(v, axis=1, keepdims=True)
        return vmax, jnp.min(jnp.where(v == vmax, lane, LANES), axis=1, keepdims=True)

    gmask = lane < N_EXPERT_GROUPS
    gmax, g_sel = top(jnp.where(gmask, x, ninf))
    p_group = 1.0 / jnp.sum(jnp.where(gmask, jnp.exp(x - gmax), 0.0), axis=1, keepdims=True)

    lo = N_EXPERT_GROUPS + g_sel * EXPERTS_PER_GROUP
    cur = jnp.where(jnp.logical_and(lane >= lo, lane < lo + EXPERTS_PER_GROUP), x, ninf)
    vals, idxs = [], []
    for _ in range(TOP_K):
        v, ix = top(cur)
        vals.append(v)
        idxs.append(ix)
        cur = jnp.where(lane == ix, ninf, cur)
    exps = [jnp.exp(v - vals[0]) for v in vals]
    den = functools.reduce(lambda a, c: a + c, exps)

    member = functools.reduce(jnp.logical_or, [lane == ix for ix in idxs])
    mf = jnp.where(member, 1.0, 0.0)
    r_i = lax.broadcasted_iota(jnp.int32, (tm, tm), 0)
    c_i = lax.broadcasted_iota(jnp.int32, (tm, tm), 1)
    before = _dot((c_i < r_i).astype(BF16), mf.astype(BF16)) + carry[...]
    carry[...] = carry[...] + jnp.sum(mf, axis=0, keepdims=True)
    cnt_ref[...] = carry[...]

    out = jnp.zeros((tm, LANES), F32)
    for k in range(TOP_K):
        rank = jnp.sum(jnp.where(lane == idxs[k], before, 0.0), axis=1, keepdims=True)
        out = jnp.where(lane == ROUTE_EID + k, (idxs[k] - N_EXPERT_GROUPS).astype(F32), out)
        out = jnp.where(lane == ROUTE_RANK + k, rank, out)
        out = jnp.where(lane == ROUTE_W + k, p_group * (exps[k] / den), out)
    o_ref[...] = out


def route(logits, b_rg, b_re, n_rows, tm_combine):
    n_tok = logits.shape[0]
    n_experts = N_EXPERT_GROUPS * EXPERTS_PER_GROUP
    bias = jnp.zeros((1, LANES), F32).at[0, :N_EXPERT_GROUPS].set(b_rg)
    bias = bias.at[0, N_EXPERT_GROUPS:N_EXPERT_GROUPS + n_experts].set(b_re)
    tm = 512
    table, cnt = pl.pallas_call(
        _route_kernel,
        grid=(n_tok // tm,),
        in_specs=[pl.BlockSpec((tm, LANES), lambda i: (i, 0)),
                  pl.BlockSpec((1, LANES), lambda i: (0, 0))],
        out_specs=[pl.BlockSpec((tm, LANES), lambda i: (i, 0)),
                   pl.BlockSpec((1, LANES), lambda i: (0, 0))],
        out_shape=[jax.ShapeDtypeStruct((n_tok, LANES), F32), jax.ShapeDtypeStruct((1, LANES), F32)],
        scratch_shapes=[pltpu.VMEM((1, LANES), F32)],
        compiler_params=_params(("arbitrary",), 16),
        name="route",
    )(logits, bias)

    eid = table[:, ROUTE_EID:ROUTE_EID + TOP_K].astype(jnp.int32)
    rank = table[:, ROUTE_RANK:ROUTE_RANK + TOP_K].astype(jnp.int32)
    counts = cnt[0, N_EXPERT_GROUPS:N_EXPERT_GROUPS + n_experts].astype(jnp.int32)
    padded = ((counts + MOE_ROWS - 1) // MOE_ROWS) * MOE_ROWS
    pends = jnp.cumsum(padded)
    pstarts = pends - padded
    onehot = eid[:, :, None] == jnp.arange(n_experts, dtype=jnp.int32)
    dest = jnp.sum(jnp.where(onehot, pstarts, 0), axis=-1) + rank
    tok = jnp.broadcast_to(jnp.arange(n_tok, dtype=jnp.int32)[:, None], (n_tok, TOP_K))
    tok_buf = jnp.zeros((n_rows,), jnp.int32).at[dest.reshape(-1)].set(tok.reshape(-1), unique_indices=True)
    n_blocks = n_rows // MOE_ROWS
    block_expert = jnp.clip(
        jnp.searchsorted(pends, jnp.arange(n_blocks, dtype=jnp.int32) * MOE_ROWS, side='right'),
        0, n_experts - 1).astype(jnp.int32)
    rows_left = (pstarts + counts)[block_expert] - jnp.arange(n_blocks, dtype=jnp.int32) * MOE_ROWS
    short_blocks = jnp.clip(-(-rows_left // MOE_SUB), 1, MOE_ROWS // MOE_SUB).astype(jnp.int32)
    n_active = (pends[-1] // MOE_ROWS).astype(jnp.int32).reshape(1)
    dest_tiles = dest.reshape(n_tok // tm_combine, tm_combine, TOP_K).transpose(0, 2, 1).reshape(-1)
    return table, tok_buf, block_expert, short_blocks, n_active, dest_tiles.astype(jnp.int32)


def _layer(x, c, w_ada, b_ada, w_in, b_forget, lam_re, lam_im, log_dt, b_re, b_im, c_re, c_im, d_skip,
           w_glu, b_glu, g_attn, g_ssm, w_out, ln1_g, ln1_b, w_rg, b_rg, w_re, b_re_r,
           w_gate, w_up, w_down, ln2_g, ln2_b, alpha):
    bsz, seq, d = x.shape
    n_tok = bsz * seq
    n_heads = b_forget.shape[0]
    d_att = n_heads * HEAD_DIM
    d_ssm = d_skip.shape[0] * d_skip.shape[1]

    mod = ada_mod(c, w_ada, b_ada).reshape(bsz, 1, -1)
    shift1, scale1, gate1, shift2, scale2, gate2 = jnp.split(mod, 6, axis=-1)

    n_qkv = 3 * d_att
    w_main = jnp.concatenate([w_in[:, :n_qkv], w_in[:, n_qkv + n_heads:]], axis=1).astype(BF16)
    w_f = jnp.zeros((d, LANES), BF16).at[:, :n_heads].set(w_in[:, n_qkv:n_qkv + n_heads].astype(BF16))
    qkv, u, f = in_proj(x, scale1, shift1, w_main, w_f, n_qkv)

    q_aug, k_aug = forget_cum(f, b_forget)
    attn, (w_glu_b, w_out_b, w_gate_b, w_up_b, w_down_b) = attention(
        qkv, q_aug, k_aug, n_heads, [w_glu, w_out, w_gate, w_up, w_down])

    w1, ft, al = ssm_prep(lam_re, lam_im, log_dt, b_re, b_im, c_re, c_im)
    y = ssm_scan(u, w1, ft, al, d_skip)
    ssm_n = glu_norm(y.reshape(n_tok, d_ssm), w_glu_b, b_glu, g_ssm).reshape(bsz, seq, d_ssm)

    n_experts = N_EXPERT_GROUPS * EXPERTS_PER_GROUP
    w_r = jnp.zeros((d, LANES), F32).at[:, :N_EXPERT_GROUPS].set(w_rg)
    w_r = w_r.at[:, N_EXPERT_GROUPS:N_EXPERT_GROUPS + n_experts].set(w_re)
    wr_hi = w_r.astype(BF16)
    wr_lo = (w_r - wr_hi.astype(F32)).astype(BF16)
    x1, h2, logits = out_proj(attn, g_attn, ssm_n, w_out_b, x, gate1, ln1_g, ln1_b,
                              scale2, shift2, wr_hi, wr_lo, alpha)

    n_assign = n_tok * TOP_K
    n_blocks = -(-(n_assign + n_experts * (MOE_ROWS - 1)) // MOE_ROWS)
    n_rows = n_blocks * MOE_ROWS
    table, tok_buf, block_expert, short_blocks, n_active, dest = route(
        logits.reshape(n_tok, LANES), b_rg, b_re_r, n_rows, COMBINE_ROWS)
    ys = moe_experts(h2.reshape(n_tok, d // 2), tok_buf, block_expert, short_blocks, n_active,
                     w_gate_b, w_up_b, w_down_b)
    out = moe_combine(ys, dest, table, x1.reshape(n_tok, d), gate2, ln2_g, ln2_b, alpha, seq)
    return out.reshape(bsz, seq, d)


def kernel(x, c, w_ada, b_ada, w_in, b_forget, ssm_lambda_re, ssm_lambda_im, ssm_log_dt, ssm_b_re, ssm_b_im,
           ssm_c_re, ssm_c_im, ssm_d, w_glu, b_glu, g_attn, g_ssm, w_out, ln1_g, ln1_b, w_router_group,
           b_router_group, w_router_expert, b_router_expert, w_gate, w_up, w_down, ln2_g, ln2_b):
    depth = w_ada.shape[0]
    alpha = (2.0 * depth) ** 0.25
    for l in range(depth):
        x = _layer(x, c, w_ada[l], b_ada[l], w_in[l], b_forget[l], ssm_lambda_re[l], ssm_lambda_im[l],
                   ssm_log_dt[l], ssm_b_re[l], ssm_b_im[l], ssm_c_re[l], ssm_c_im[l], ssm_d[l],
                   w_glu[l], b_glu[l], g_attn[l], g_ssm[l], w_out[l], ln1_g[l], ln1_b[l],
                   w_router_group[l], b_router_group[l], w_router_expert[l], b_router_expert[l],
                   w_gate[l], w_up[l], w_down[l], ln2_g[l], ln2_b[l], alpha)
    return x
```
